```python
import math
import jax, jax.numpy as jnp
from jax import lax
import numpy as np

D_MODEL = 1024
BATCH = 1
SEQ = 16384
DEPTH = 1
DEC_BATCH = 32
DEC_SEQ = 16
PAST_LEN = 4096

CHUNK = 64
N_META = 16
D_MIX = 2 * D_MODEL
HEAD_DIM = 64
ATT_HEADS = (D_MIX // 2) // HEAD_DIM
ATT_KV_HEADS = 4
ATT_REP = ATT_HEADS // ATT_KV_HEADS
ATT_DIM = ATT_HEADS * HEAD_DIM
KV_DIM = ATT_KV_HEADS * HEAD_DIM
WINDOW = 128
SSD_HEADS = (D_MIX // 2) // HEAD_DIM
SSD_DIM = SSD_HEADS * HEAD_DIM
SSD_GROUPS = 2
SSD_REP = SSD_HEADS // SSD_GROUPS
D_STATE = 128
CONV_WIDTH = 4
CONV_DIM = SSD_DIM + 2 * SSD_GROUPS * D_STATE
SSD_BLOCK = CHUNK
IN_PROJ_DIM = SSD_DIM + CONV_DIM + SSD_HEADS + ATT_DIM + 2 * KV_DIM
N_EXPERT_GROUPS = 4
EXPERTS_PER_GROUP = 8
N_EXPERTS = N_EXPERT_GROUPS * EXPERTS_PER_GROUP
TOP_K = 2
EXPERT_FF = D_MODEL // 4
EPS = 1e-6

kernel_name = "hymba_ssd_swa_hiermoe_stream_step"


def rmsnorm(x, g):
    xf = x.astype(jnp.float32)
    y = xf * lax.rsqrt(jnp.mean(xf * xf, axis=-1, keepdims=True) + EPS)
    return (y * g.astype(jnp.float32)).astype(x.dtype)


def alibi_slopes():
    return 2.0 ** (-8.0 * jnp.arange(1, ATT_HEADS + 1, dtype=jnp.float32) / ATT_HEADS)


def project(hn, w_in):
    cuts = []
    acc = 0
    for s in (SSD_DIM, CONV_DIM, SSD_HEADS, ATT_DIM, KV_DIM):
        acc += s
        cuts.append(acc)
    return jnp.split(hn @ w_in, cuts, axis=-1)


def split_heads(q, k, v, g_k):
    b, L = q.shape[:2]
    q = q.reshape(b, L, ATT_HEADS, HEAD_DIM)
    k = rmsnorm(k.reshape(b, L, ATT_KV_HEADS, HEAD_DIM), g_k)
    v = v.reshape(b, L, ATT_KV_HEADS, HEAD_DIM)
    return q, k, v


def causal_conv(xbc, tail, w, bias):
    L = xbc.shape[1]
    xp = jnp.concatenate([tail, xbc], axis=1)
    out = bias
    for j in range(CONV_WIDTH):
        out = out + w[j] * xp[:, j:j + L]
    return jax.nn.silu(out), xp[:, L:]


def ssd_scan(x, dt, a, bmat, cmat, d_skip, h0, block, need_y):
    f32 = jnp.float32
    bsz, L = x.shape[:2]
    nc = L // block
    G, R, P, N = SSD_GROUPS, SSD_REP, HEAD_DIM, D_STATE
    xr = x.astype(f32).reshape(bsz, nc, block, G, R, P)
    dtr = dt.reshape(bsz, nc, block, G, R)
    br = bmat.astype(f32).reshape(bsz, nc, block, G, N)
    cs = jnp.cumsum(dtr * a.reshape(G, R), axis=2)
    xdt = xr * dtr[..., None]
    decay_end = jnp.exp(cs[:, :, -1:] - cs)
    states = jnp.einsum('bclgn,bclgr,bclgrp->bcgrpn', br, decay_end, xdt)
    block_decay = jnp.exp(cs[:, :, -1])

    def step(h, inp):
        s_c, d_c = inp
        return d_c[..., None, None] * h + s_c, h

    h_last, h_in = lax.scan(step, h0.astype(f32).reshape(bsz, G, R, P, N),
                            (jnp.moveaxis(states, 1, 0), jnp.moveaxis(block_decay, 1, 0)))
    h_last = h_last.reshape(bsz, SSD_HEADS, P, N).astype(h0.dtype)
    if not need_y:
        return None, h_last
    cr = cmat.astype(f32).reshape(bsz, nc, block, G, N)
    h_in = jnp.moveaxis(h_in, 0, 1)
    causal = jnp.tril(jnp.ones((block, block), bool))[:, :, None, None]
    seg = cs[:, :, :, None] - cs[:, :, None, :]
    decay = jnp.exp(jnp.where(causal, seg, -jnp.inf))
    cb = jnp.einsum('bclgn,bcsgn->bclsg', cr, br)
    y = jnp.einsum('bclsg,bclsgr,bcsgrp->bclgrp', cb, decay, xdt)
    y = y + jnp.einsum('bclgn,bcgrpn,bclgr->bclgrp', cr, h_in, jnp.exp(cs))
    y = y + d_skip.astype(f32).reshape(G, R, 1) * xr
    return y.reshape(bsz, L, SSD_DIM).astype(x.dtype), h_last


def ssd_branch(z, xbc, dt, p, conv_tail, h0, need_y):
    f32 = jnp.float32
    b, L, _ = xbc.shape
    xbc_act, new_tail = causal_conv(xbc, conv_tail.astype(xbc.dtype), p['conv_w'], p['conv_b'])
    xs, bm, cm = jnp.split(xbc_act, [SSD_DIM, SSD_DIM + SSD_GROUPS * D_STATE], axis=-1)
    dt = jax.nn.softplus(dt.astype(f32) + p['dt_bias'].astype(f32))
    a = -jnp.exp(p['a_log'].astype(f32))
    y, h_new = ssd_scan(xs.reshape(b, L, SSD_HEADS, HEAD_DIM), dt, a,
                        bm.reshape(b, L, SSD_GROUPS, D_STATE), cm.reshape(b, L, SSD_GROUPS, D_STATE),
                        p['d_skip'], h0, min(SSD_BLOCK, L), need_y)
    if not need_y:
        return None, new_tail, h_new
    y = rmsnorm(y * jax.nn.silu(z), p['g_ssd_out'])
    return y, new_tail, h_new


def sink_attend(q, k, v, bias, sinks):
    f32 = jnp.float32
    s = jnp.einsum('bnqkrd,bnskd->bnkrqs', q, k).astype(f32) * (HEAD_DIM ** -0.5) + bias
    sink = sinks.astype(f32).reshape(ATT_KV_HEADS, ATT_REP)[None, None, :, :, None, None]
    m = jnp.maximum(jnp.max(s, axis=-1, keepdims=True), sink)
    e = jnp.exp(s - m)
    prob = e / (jnp.sum(e, axis=-1, keepdims=True) + jnp.exp(sink - m))
    return jnp.einsum('bnkrqs,bnskd->bnqkrd', prob.astype(v.dtype), v)


def meta_attention(q, k, v, g_q, sinks):
    q = rmsnorm(q, g_q).reshape(1, 1, N_META, ATT_KV_HEADS, ATT_REP, HEAD_DIM)
    o = sink_attend(q, k[:, None], v[:, None], jnp.zeros((), jnp.float32), sinks)
    return o.reshape(1, N_META, ATT_DIM)


def window_attention(q, k, v, g_q, sinks, slopes, meta_k, meta_v, buf_k, buf_v, buf_valid):
    f32 = jnp.float32
    b, L = q.shape[:2]
    qb = min(CHUNK, L)
    nc = L // qb
    span = WINDOW + qb
    q = rmsnorm(q, g_q).reshape(b, nc, qb, ATT_KV_HEADS, ATT_REP, HEAD_DIM)
    idx = qb * jnp.arange(nc)[:, None] + jnp.arange(span)[None, :]
    kb = jnp.concatenate([buf_k.astype(k.dtype), k], axis=1)[:, idx]
    vb = jnp.concatenate([buf_v.astype(v.dtype), v], axis=1)[:, idx]
    mshape = (b, nc, N_META, ATT_KV_HEADS, HEAD_DIM)
    kb = jnp.concatenate([jnp.broadcast_to(meta_k[:, None].astype(k.dtype), mshape), kb], axis=2)
    vb = jnp.concatenate([jnp.broadcast_to(meta_v[:, None].astype(v.dtype), mshape), vb], axis=2)
    valid = jnp.concatenate([jnp.full((WINDOW,), buf_valid), jnp.ones((L,), bool)])[idx]
    dist = jnp.abs(WINDOW + jnp.arange(qb)[:, None] - jnp.arange(span)[None, :]).astype(f32)
    alibi = -slopes.reshape(ATT_KV_HEADS, ATT_REP)[:, :, None, None] * dist
    bias = jnp.where(valid[:, None, None, None, :], alibi, -jnp.inf)
    bias = jnp.concatenate([jnp.zeros((nc, ATT_KV_HEADS, ATT_REP, qb, N_META), f32), bias], axis=-1)
    o = sink_attend(q, kb, vb, bias, sinks)
    return o.reshape(b, L, ATT_DIM)


def merge(y_ssd, y_att, g_att_out, w_out):
    return jnp.concatenate([y_ssd, rmsnorm(y_att, g_att_out)], axis=-1) @ w_out


def hier_moe(xn, p):
    f32 = jnp.float32
    t = xn.reshape(-1, D_MODEL)
    logit_g = (t @ p['w_route_group'] + p['b_route_group']).astype(f32)
    onehot_g = jax.nn.one_hot(jnp.argmax(logit_g, axis=-1), N_EXPERT_GROUPS, dtype=f32)
    gate_g = jnp.sum(jax.nn.softmax(logit_g, axis=-1) * onehot_g, axis=-1, keepdims=True)
    logit_e = (t @ p['w_route_expert'] + p['b_route_expert']).astype(f32)
    logit_in = jnp.einsum('tge,tg->te', logit_e.reshape(-1, N_EXPERT_GROUPS, EXPERTS_PER_GROUP), onehot_g)
    top_val, top_idx = lax.top_k(logit_in, TOP_K)
    gates = jax.nn.softmax(top_val, axis=-1) * gate_g
    expert_id = jnp.argmax(onehot_g, axis=-1)[:, None] * EXPERTS_PER_GROUP + top_idx
    combine = jnp.einsum('tk,tke->te', gates, jax.nn.one_hot(expert_id, N_EXPERTS, dtype=f32)).astype(t.dtype)
    out = jnp.zeros_like(t)
    for e in range(N_EXPERTS):
        hid = jax.nn.silu(t @ p['w_gate'][e]) * (t @ p['w_up'][e])
        out = out + combine[:, e:e + 1] * (hid @ p['w_down'][e])
    return out.reshape(xn.shape)


def meta_layer(hm, p, advance):
    z, xbc, dt, q, k, v = project(rmsnorm(hm, p['g_mix']), p['w_in'])
    q, k, v = split_heads(q, k, v, p['g_k'])
    tail0 = jnp.zeros((1, CONV_WIDTH - 1, CONV_DIM), xbc.dtype)
    h0 = jnp.zeros((1, SSD_HEADS, HEAD_DIM, D_STATE), jnp.float32)
    y_ssd, tail, state = ssd_branch(z, xbc, dt, p, tail0, h0, advance)
    if advance:
        y_att = meta_attention(q, k, v, p['g_q'], p['attn_sinks'])
        hm = hm + merge(y_ssd, y_att, p['g_att_out'], p['w_out'])
        hm = hm + hier_moe(rmsnorm(hm, p['g_ffn']), p)
    return hm, k, v, tail, state


def layer_segment(h, p, slopes, meta_k, meta_v, conv_tail, ssd_h0, buf_k, buf_v, buf_valid):
    z, xbc, dt, q, k, v = project(rmsnorm(h, p['g_mix']), p['w_in'])
    q, k, v = split_heads(q, k, v, p['g_k'])
    y_ssd, new_tail, new_state = ssd_branch(z, xbc, dt, p, conv_tail, ssd_h0, True)
    y_att = window_attention(q, k, v, p['g_q'], p['attn_sinks'], slopes, meta_k, meta_v,
                             buf_k, buf_v, buf_valid)
    h = h + merge(y_ssd, y_att, p['g_att_out'], p['w_out'])
    h = h + hier_moe(rmsnorm(h, p['g_ffn']), p)
    return h, new_tail, new_state, k, v


def setup_inputs(seed: int = 0) -> dict:
    key = jax.random.key(seed)
    ks = jax.random.split(key, 32)
    f32 = jnp.float32

    def nrm(k, shape, scale):
        return jax.random.normal(k, shape, f32) * scale

    dt0 = jnp.exp(jax.random.uniform(ks[10], (DEPTH, SSD_HEADS), f32, math.log(1e-3), math.log(1e-1)))
    return {
        "x_prompt": nrm(ks[0], (BATCH, SEQ, D_MODEL), 1.0),
        "x_sample": nrm(ks[1], (DEC_BATCH, DEC_SEQ, D_MODEL), 1.0),
        "cache_conv": nrm(ks[2], (DEPTH, DEC_BATCH, CONV_WIDTH - 1, CONV_DIM), 1.0),
        "state_ssd": nrm(ks[3], (DEPTH, DEC_BATCH, SSD_HEADS, HEAD_DIM, D_STATE), 0.3),
        "cache_k": nrm(ks[4], (DEPTH, DEC_BATCH, WINDOW, ATT_KV_HEADS, HEAD_DIM), 1.0),
        "cache_v": nrm(ks[5], (DEPTH, DEC_BATCH, WINDOW, ATT_KV_HEADS, HEAD_DIM), 1.0),
        "meta_tokens": nrm(ks[6], (N_META, D_MODEL), 1.0),
        "g_mix": 1.0 + nrm(ks[7], (DEPTH, D_MODEL), 0.1),
        "w_in": nrm(ks[8], (DEPTH, D_MODEL, IN_PROJ_DIM), D_MODEL ** -0.5),
        "conv_w": nrm(ks[9], (DEPTH, CONV_WIDTH, CONV_DIM), CONV_WIDTH ** -0.5),
        "conv_b": nrm(ks[11], (DEPTH, CONV_DIM), 0.02),
        "dt_bias": dt0 + jnp.log(-jnp.expm1(-dt0)),
        "a_log": jnp.log(jax.random.uniform(ks[12], (DEPTH, SSD_HEADS), f32, 1.0, 16.0)),
        "d_skip": 1.0 + nrm(ks[13], (DEPTH, SSD_HEADS), 0.1),
        "g_ssd_out": 1.0 + nrm(ks[14], (DEPTH, SSD_DIM), 0.1),
        "g_q": 1.0 + nrm(ks[15], (DEPTH, HEAD_DIM), 0.1),
        "g_k": 1.0 + nrm(ks[16], (DEPTH, HEAD_DIM), 0.1),
        "attn_sinks": nrm(ks[17], (DEPTH, ATT_HEADS), 1.0),
        "g_att_out": 1.0 + nrm(ks[18], (DEPTH, ATT_DIM), 0.1),
        "w_out": nrm(ks[19], (DEPTH, D_MIX, D_MODEL), D_MIX ** -0.5),
        "g_ffn": 1.0 + nrm(ks[20], (DEPTH, D_MODEL), 0.1),
        "w_route_group": nrm(ks[21], (DEPTH, D_MODEL, N_EXPERT_GROUPS), D_MODEL ** -0.5),
        "b_route_group": nrm(ks[22], (DEPTH, N_EXPERT_GROUPS), 0.01),
        "w_route_expert": nrm(ks[23], (DEPTH, D_MODEL, N_EXPERTS), D_MODEL ** -0.5),
        "b_route_expert": nrm(ks[24], (DEPTH, N_EXPERTS), 0.01),
        "w_gate": nrm(ks[25], (DEPTH, N_EXPERTS, D_MODEL, EXPERT_FF), D_MODEL ** -0.5),
        "w_up": nrm(ks[26], (DEPTH, N_EXPERTS, D_MODEL, EXPERT_FF), D_MODEL ** -0.5),
        "w_down": nrm(ks[27], (DEPTH, N_EXPERTS, EXPERT_FF, D_MODEL), EXPERT_FF ** -0.5),
    }


def reference(x_prompt, x_sample, cache_conv, state_ssd, cache_k, cache_v, meta_tokens,
              g_mix, w_in, conv_w, conv_b, dt_bias, a_log, d_skip, g_ssd_out, g_q, g_k,
              attn_sinks, g_att_out, w_out, g_ffn, w_route_group, b_route_group,
              w_route_expert, b_route_expert, w_gate, w_up, w_down):
    slopes = alibi_slopes()
    hp = x_prompt
    hs = x_sample
    hm = meta_tokens[None].astype(x_prompt.dtype)
    bsz = hp.shape[0]
    conv_p, ssd_p, k_p, v_p = [], [], [], []
    conv_s, ssd_s, k_s, v_s = [], [], [], []
    for l in range(DEPTH):
        p = {'g_mix': g_mix[l], 'w_in': w_in[l], 'conv_w': conv_w[l], 'conv_b': conv_b[l],
             'dt_bias': dt_bias[l], 'a_log': a_log[l], 'd_skip': d_skip[l], 'g_ssd_out': g_ssd_out[l],
             'g_q': g_q[l], 'g_k': g_k[l], 'attn_sinks': attn_sinks[l], 'g_att_out': g_att_out[l],
             'w_out': w_out[l], 'g_ffn': g_ffn[l], 'w_route_group': w_route_group[l],
             'b_route_group': b_route_group[l], 'w_route_expert': w_route_expert[l],
             'b_route_expert': b_route_expert[l], 'w_gate': w_gate[l], 'w_up': w_up[l],
             'w_down': w_down[l]}
        hm_next, mk, mv, m_tail, m_state = meta_layer(hm, p, l < DEPTH - 1)
        tail0 = jnp.broadcast_to(m_tail, (bsz,) + m_tail.shape[1:])
        state0 = jnp.broadcast_to(m_state, (bsz,) + m_state.shape[1:])
        buf0 = jnp.zeros((bsz, WINDOW, ATT_KV_HEADS, HEAD_DIM), hp.dtype)
        hp, tail_p, st_p, kp, vp = layer_segment(hp, p, slopes, mk, mv, tail0, state0, buf0, buf0, False)
        conv_p.append(tail_p)
        ssd_p.append(st_p)
        k_p.append(kp[:, -WINDOW:])
        v_p.append(vp[:, -WINDOW:])
        hs, tail_s, st_s, ksm, vsm = layer_segment(hs, p, slopes, mk, mv, cache_conv[l], state_ssd[l],
                                                  cache_k[l], cache_v[l], True)
        conv_s.append(tail_s)
        ssd_s.append(st_s)
        k_s.append(ksm)
        v_s.append(vsm)
        hm = hm_next
    y_prompt = hp
    y_sample = hs
    new_conv_prompt = jnp.stack(conv_p, 0)
    new_ssd_prompt = jnp.stack(ssd_p, 0)
    new_k_prompt = jnp.stack(k_p, 0)
    new_v_prompt = jnp.stack(v_p, 0)
    new_conv_sample = jnp.stack(conv_s, 0)
    new_ssd_sample = jnp.stack(ssd_s, 0)
    new_k_sample = jnp.stack(k_s, 0)
    new_v_sample = jnp.stack(v_s, 0)
    return (y_prompt, y_sample, new_conv_prompt, new_ssd_prompt, new_k_prompt, new_v_prompt,
            new_conv_sample, new_ssd_sample, new_k_sample, new_v_sample)
```

```python
import functools
import math

import numpy as np
import jax
import jax.numpy as jnp
from jax import lax
from jax.experimental import pallas as pl
from jax.experimental.pallas import tpu as pltpu

D_MODEL = 1024
CHUNK = 64
N_META = 16
HEAD_DIM = 64
ATT_HEADS = 16
ATT_KV_HEADS = 4
ATT_REP = ATT_HEADS // ATT_KV_HEADS
ATT_DIM = ATT_HEADS * HEAD_DIM
KV_DIM = ATT_KV_HEADS * HEAD_DIM
WINDOW = 128
SSD_HEADS = 16
SSD_DIM = SSD_HEADS * HEAD_DIM
SSD_GROUPS = 2
GROUP_DIM = SSD_DIM // SSD_GROUPS
D_STATE = 128
CONV_WIDTH = 4
CONV_DIM = SSD_DIM + 2 * SSD_GROUPS * D_STATE
N_EXPERT_GROUPS = 4
EXPERTS_PER_GROUP = 8
N_EXPERTS = N_EXPERT_GROUPS * EXPERTS_PER_GROUP
EXPERT_FF = D_MODEL // 4
EPS = 1e-6

LANES = 128
SUBLANES = 8
KEY_SPAN = 256
VMEM_LIMIT = 48 * 1024 * 1024

F32 = jnp.float32
BF16 = jnp.bfloat16
NEG_INF = float("-inf")


def _dot(a, b):
    return jnp.dot(a, b, preferred_element_type=F32)


def _dot_nt(a, b):
    return lax.dot_general(a, b, (((1,), (1,)), ((), ())), preferred_element_type=F32)


def _dot_tn(a, b):
    return lax.dot_general(a, b, (((0,), (0,)), ((), ())), preferred_element_type=F32)


def _split3(x):
    hi = x.astype(BF16)
    r = x - hi.astype(F32)
    mid = r.astype(BF16)
    lo = (r - mid.astype(F32)).astype(BF16)
    return hi, mid, lo


def _dot_sel(x, sel):
    hi, mid, lo = _split3(x)
    return _dot(hi, sel) + _dot(mid, sel) + _dot(lo, sel)


def _sel_dot(sel, x):
    hi, mid, lo = _split3(x)
    return _dot(sel, hi) + _dot(sel, mid) + _dot(sel, lo)


def _silu(x):
    return x * (1.0 / (1.0 + jnp.exp(-x)))


def _rms(x, g):
    ms = jnp.mean(x * x, axis=-1, keepdims=True)
    return x * lax.rsqrt(ms + EPS) * g


def _const_spec(shape):
    n = len(shape)
    return pl.BlockSpec(shape, lambda *_: (0,) * n)


def _head_norm(x, bd, g):
    sq = x * x
    hi = sq.astype(BF16)
    lo = (sq - hi.astype(F32)).astype(BF16)
    ms = (_dot(hi, bd) + _dot(lo, bd)) * (1.0 / HEAD_DIM)
    return x * lax.rsqrt(ms + EPS) * g


def _proj_kernel(x_ref, gmix_ref, wz_ref, wxbc_ref, wq_ref, wk_ref, wv_ref, wdt_ref,
                 gq_ref, gk_ref, bd_ref, z_ref, xbc_ref, q_ref, k_ref, v_ref, dt_ref):
    xn = _rms(x_ref[...], gmix_ref[...]).astype(BF16)
    z_ref[...] = _dot(xn, wz_ref[...])
    xbc_ref[...] = _dot(xn, wxbc_ref[...])
    v_ref[...] = _dot(xn, wv_ref[...])
    dt_ref[...] = _dot(xn, wdt_ref[...])
    bd = bd_ref[...]
    k_ref[...] = _head_norm(_dot(xn, wk_ref[...]), bd, gk_ref[...])
    scale = HEAD_DIM ** -0.5
    for j in range(ATT_DIM // KV_DIM):
        sl = slice(j * KV_DIM, (j + 1) * KV_DIM)
        qj = _head_norm(_dot(xn, wq_ref[:, sl]), bd, gq_ref[...])
        q_ref[:, sl] = (qj * scale).astype(BF16)


def _project(x, w, tm):
    t = x.shape[0]
    assert t % tm == 0
    row = lambda n: pl.BlockSpec((tm, n), lambda i: (i, 0))
    ins = [x, w["g_mix"], w["w_z"], w["w_xbc"], w["w_q"], w["w_k"], w["w_v"], w["w_dt"],
           w["g_q"], w["g_k"], w["bd"]]
    in_specs = [row(D_MODEL)] + [_const_spec(a.shape) for a in ins[1:]]
    out_dims = (SSD_DIM, CONV_DIM, ATT_DIM, KV_DIM, KV_DIM, LANES)
    out_dtypes = (F32, F32, BF16, F32, F32, F32)
    return pl.pallas_call(
        _proj_kernel,
        grid=(t // tm,),
        in_specs=in_specs,
        out_specs=[row(n) for n in out_dims],
        out_shape=[jax.ShapeDtypeStruct((t, n), d) for n, d in zip(out_dims, out_dtypes)],
        compiler_params=pltpu.CompilerParams(
            dimension_semantics=("arbitrary",), vmem_limit_bytes=VMEM_LIMIT),
        name="projection",
    )(*ins)


def _ssd_kernel(xbc_ref, z_ref, dt_ref, tail_ref, h0_ref, convw_ref, convb_ref, dtb_ref,
                alog_ref, dskip_ref, gout_ref, ehead_ref, epos_ref, tri_ref,
                y_ref, hout_ref, buf_ref, act_ref, dts_ref, ys_ref, st_ref, *, L, TB):
    hp = LANES // L
    n_tiles = SSD_HEADS // hp
    tile_w = hp * HEAD_DIM
    b = pl.program_id(1)
    pad = SUBLANES - (CONV_WIDTH - 1)

    @pl.when(b == 0)
    def _():
        buf_ref[pad:SUBLANES, :] = tail_ref[...]
        st_ref[...] = h0_ref[...]

    buf_ref[SUBLANES:SUBLANES + TB, :] = xbc_ref[...]
    acc = convb_ref[...] + convw_ref[0:1, :] * buf_ref[pad:pad + TB, :]
    for j in range(1, CONV_WIDTH):
        acc = acc + convw_ref[j:j + 1, :] * buf_ref[pad + j:pad + j + TB, :]
    act_ref[...] = _silu(acc)
    buf_ref[pad:SUBLANES, :] = buf_ref[TB + pad:TB + SUBLANES, :]

    dtx = dt_ref[...] + dtb_ref[...]
    dts_ref[...] = jnp.maximum(dtx, 0.0) + jnp.log(1.0 + jnp.exp(-jnp.abs(dtx)))

    a_row = -jnp.exp(alog_ref[...])
    ehead = ehead_ref[...]
    epos = epos_ref[...]
    tri = tri_ref[...]
    dskip = dskip_ref[...]
    row_i = lax.broadcasted_iota(jnp.int32, (L, n_tiles * LANES), 0)
    col_i = lax.broadcasted_iota(jnp.int32, (L, n_tiles * LANES), 1)
    pos_i = col_i % L
    diag_mask = row_i == pos_i
    causal_mask = row_i >= pos_i
    colhead = lax.broadcasted_iota(jnp.int32, (L, tile_w), 1) // HEAD_DIM

    def chunk(c, carry):
        r0 = pl.multiple_of(c * L, L)
        dtc = dts_ref[pl.ds(r0, L), :]
        cs = _sel_dot(tri, dtc * a_row)
        cs_last = cs[L - 1:L, :]
        dec_end = jnp.exp(cs_last - cs)
        dt_b = _dot_sel(dtc, ehead)
        w_b = _dot_sel(dtc * dec_end, ehead)
        ecs_b = _dot_sel(jnp.exp(cs), ehead)
        bdec_b = _dot_sel(jnp.broadcast_to(jnp.exp(cs_last), (2 * SUBLANES, LANES)), ehead)[0:1, :]
        cs_col = _dot_sel(cs, epos)
        cs_row = jnp.sum(jnp.where(diag_mask, cs_col, 0.0), axis=0, keepdims=True)
        lmat = jnp.exp(jnp.where(causal_mask, cs_col - cs_row, NEG_INF))

        xs = act_ref[pl.ds(r0, L), 0:SSD_DIM]
        xdt = (xs * dt_b).astype(BF16)
        wx = (xs * w_b).astype(BF16)
        st = st_ref[...]
        st16 = st.astype(BF16)

        y_diag = []
        y_off = []
        s_new = []
        cb2 = [None] * SSD_GROUPS
        for g in range(SSD_GROUPS):
            bg = act_ref[pl.ds(r0, L), SSD_DIM + g * D_STATE:SSD_DIM + (g + 1) * D_STATE].astype(BF16)
            cg = act_ref[pl.ds(r0, L), SSD_DIM + (SSD_GROUPS + g) * D_STATE:
                         SSD_DIM + (SSD_GROUPS + g + 1) * D_STATE].astype(BF16)
            cb2[g] = _dot_nt(cg, jnp.concatenate([bg] * hp, axis=0))
            gsl = slice(g * GROUP_DIM, (g + 1) * GROUP_DIM)
            y_off.append(_dot(cg, st16[:, gsl]))
            s_new.append(_dot_tn(bg, wx[:, gsl]))
        for t in range(n_tiles):
            g = (t * hp * HEAD_DIM) // GROUP_DIM
            gmat = (lmat[:, t * LANES:(t + 1) * LANES] * cb2[g]).astype(BF16)
            xt = xdt[:, t * tile_w:(t + 1) * tile_w]
            rhs = jnp.concatenate(
                [jnp.where(colhead == hh, xt, jnp.zeros_like(xt)) for hh in range(hp)], axis=0)
            y_diag.append(_dot(gmat, rhs))
        y = (jnp.concatenate(y_diag, axis=1) + jnp.concatenate(y_off, axis=1) * ecs_b
             + dskip * xs)
        ys_ref[pl.ds(r0, L), :] = y
        st_ref[...] = bdec_b * st + jnp.concatenate(s_new, axis=1)
        return carry

    lax.fori_loop(0, TB // L, chunk, 0)

    yg = ys_ref[...] * _silu(z_ref[...])
    y_ref[...] = _rms(yg, gout_ref[...])
    hout_ref[...] = st_ref[...]


def _ssd(xbc, z, dt, tail, h0t, w, L, TB):
    S, Ls, _ = xbc.shape
    assert Ls % TB == 0 and TB % L == 0 and LANES % L == 0
    nb = Ls // TB
    hp = LANES // L
    n_tiles = SSD_HEADS // hp
    per_stream = lambda a: (lambda s, b: (s, 0, 0)) if a.shape[0] == S and S > 1 else (lambda s, b: (0, 0, 0))
    seq = lambda n: pl.BlockSpec((None, TB, n), lambda s, b: (s, b, 0))
    ehead = np.zeros((LANES, SSD_DIM), np.float32)
    epos = np.zeros((LANES, n_tiles * LANES), np.float32)
    for h in range(SSD_HEADS):
        ehead[h, h * HEAD_DIM:(h + 1) * HEAD_DIM] = 1.0
        epos[h, h * L:(h + 1) * L] = 1.0
    tri = np.tril(np.ones((L, L), np.float32))
    consts = [w["conv_w"], w["conv_b"], w["dt_bias"], w["a_log"], w["d_skip"], w["g_ssd_out"],
              jnp.asarray(ehead, BF16), jnp.asarray(epos, BF16), jnp.asarray(tri, BF16)]
    in_specs = [seq(CONV_DIM), seq(SSD_DIM), seq(LANES),
                pl.BlockSpec((None, CONV_WIDTH - 1, CONV_DIM), per_stream(tail)),
                pl.BlockSpec((None, D_STATE, SSD_DIM), per_stream(h0t))]
    in_specs += [_const_spec(c.shape) for c in consts]
    return pl.pallas_call(
        functools.partial(_ssd_kernel, L=L, TB=TB),
        grid=(S, nb),
        in_specs=in_specs,
        out_specs=[seq(SSD_DIM), pl.BlockSpec((None, D_STATE, SSD_DIM), lambda s, b: (s, 0, 0))],
        out_shape=[jax.ShapeDtypeStruct((S, Ls, SSD_DIM), F32),
                   jax.ShapeDtypeStruct((S, D_STATE, SSD_DIM), F32)],
        scratch_shapes=[pltpu.VMEM((SUBLANES + TB, CONV_DIM), F32),
                        pltpu.VMEM((TB, CONV_DIM), F32),
                        pltpu.VMEM((TB, LANES), F32),
                        pltpu.VMEM((TB, SSD_DIM), F32),
                        pltpu.VMEM((D_STATE, SSD_DIM), F32)],
        compiler_params=pltpu.CompilerParams(
            dimension_semantics=("arbitrary", "arbitrary"), vmem_limit_bytes=VMEM_LIMIT),
        name="ssd",
    )(xbc, z, dt, tail, h0t, *consts)


def _attn_kernel(sink_ref, q_ref, k_ref, v_ref, kp_ref, vp_ref, mk_ref, mv_ref, bias_ref, g_ref,
                 o_ref, kbuf, vbuf, ybuf, *, L, TB, mask_start):
    b = pl.program_id(1)
    kbuf[0:WINDOW, :] = kp_ref[...].astype(BF16)
    vbuf[0:WINDOW, :] = vp_ref[...].astype(BF16)
    kbuf[WINDOW:WINDOW + TB, :] = k_ref[...].astype(BF16)
    vbuf[WINDOW:WINDOW + TB, :] = v_ref[...].astype(BF16)
    n_pad = KEY_SPAN - WINDOW - L - N_META
    mk_tail = jnp.concatenate([mk_ref[...].astype(BF16), jnp.zeros((n_pad, KV_DIM), BF16)], axis=0)
    mv_tail = jnp.concatenate([mv_ref[...].astype(BF16), jnp.zeros((n_pad, KV_DIM), BF16)], axis=0)
    row_head = lax.broadcasted_iota(jnp.int32, (ATT_REP * L, 1), 0) // L
    lane = lax.broadcasted_iota(jnp.int32, (ATT_REP * L, KEY_SPAN), 1)

    def chunk(c, carry):
        r0 = pl.multiple_of(c * L, L)
        kc = jnp.concatenate([kbuf[pl.ds(r0, WINDOW), :], kbuf[pl.ds(r0 + WINDOW, L), :], mk_tail], axis=0)
        vc = jnp.concatenate([vbuf[pl.ds(r0, WINDOW), :], vbuf[pl.ds(r0 + WINDOW, L), :], mv_tail], axis=0)
        qc = q_ref[pl.ds(r0, L), :]
        if mask_start:
            n_invalid = WINDOW - (b * TB + c * L)
        for g in range(ATT_KV_HEADS):
            q4 = jnp.concatenate(
                [qc[:, (ATT_REP * g + r) * HEAD_DIM:(ATT_REP * g + r + 1) * HEAD_DIM]
                 for r in range(ATT_REP)], axis=0)
            s = _dot_nt(q4, kc[:, g * HEAD_DIM:(g + 1) * HEAD_DIM]) + bias_ref[g]
            if mask_start:
                s = jnp.where(lane < n_invalid, NEG_INF, s)
            sink = jnp.zeros((ATT_REP * L, 1), F32)
            for r in range(ATT_REP):
                sink = jnp.where(row_head == r, sink_ref[ATT_REP * g + r], sink)
            m = jnp.maximum(jnp.max(s, axis=-1, keepdims=True), sink)
            e = jnp.exp(s - m)
            den = jnp.sum(e, axis=-1, keepdims=True) + jnp.exp(sink - m)
            o4 = _dot(e.astype(BF16), vc[:, g * HEAD_DIM:(g + 1) * HEAD_DIM]) * (1.0 / den)
            og = jnp.concatenate([o4[r * L:(r + 1) * L, :] for r in range(ATT_REP)], axis=1)
            ybuf[pl.ds(r0, L), g * KV_DIM:(g + 1) * KV_DIM] = og
        return carry

    lax.fori_loop(0, TB // L, chunk, 0)
    o_ref[...] = _rms(ybuf[...], g_ref[...])


def _attention(q, k, v, kprev, vprev, mk, mv, w, L, TB, from_cache):
    S, Ls, _ = q.shape
    assert Ls % TB == 0 and TB % L == 0 and TB % WINDOW == 0 or from_cache
    nb = Ls // TB
    seq = lambda n: pl.BlockSpec((None, TB, n), lambda s, b: (s, b, 0))
    if from_cache:
        prev_map = lambda s, b: (s, 0, 0)
    else:
        prev_map = lambda s, b: (s, jnp.maximum(b * (TB // WINDOW) - 1, 0), 0)
    prev = pl.BlockSpec((None, WINDOW, KV_DIM), prev_map)
    slopes = 2.0 ** (-8.0 * np.arange(1, ATT_HEADS + 1, dtype=np.float64) / ATT_HEADS)
    dist = np.abs(WINDOW + np.arange(L)[:, None] - np.arange(WINDOW + L)[None, :])
    bias = np.full((ATT_HEADS, L, KEY_SPAN), NEG_INF, np.float32)
    bias[:, :, :WINDOW + L] = -slopes[:, None, None] * dist[None]
    bias[:, :, WINDOW + L:WINDOW + L + N_META] = 0.0
    bias = jnp.asarray(bias.reshape(ATT_KV_HEADS, ATT_REP * L, KEY_SPAN))
    in_specs = [pl.BlockSpec(memory_space=pltpu.SMEM),
                seq(ATT_DIM), seq(KV_DIM), seq(KV_DIM), prev, prev,
                _const_spec(mk.shape), _const_spec(mv.shape), _const_spec(bias.shape),
                _const_spec(w["g_att_out"].shape)]
    return pl.pallas_call(
        functools.partial(_attn_kernel, L=L, TB=TB, mask_start=not from_cache),
        grid=(S, nb),
        in_specs=in_specs,
        out_specs=seq(ATT_DIM),
        out_shape=jax.ShapeDtypeStruct((S, Ls, ATT_DIM), F32),
        scratch_shapes=[pltpu.VMEM((WINDOW + TB, KV_DIM), BF16),
                        pltpu.VMEM((WINDOW + TB, KV_DIM), BF16),
                        pltpu.VMEM((TB, ATT_DIM), F32)],
        compiler_params=pltpu.CompilerParams(
            dimension_semantics=("arbitrary", "arbitrary"), vmem_limit_bytes=VMEM_LIMIT),
        name="attention",
    )(w["attn_sinks"], q, k, v, kprev, vprev, mk, mv, bias, w["g_att_out"])


def _merge_kernel(h_ref, ys_ref, ya_ref, wo_ref, gffn_ref, wr_ref, br_ref,
                  h1_ref, xn_ref, route_ref):
    ycat = jnp.concatenate([ys_ref[...].astype(BF16), ya_ref[...].astype(BF16)], axis=1)
    h1 = h_ref[...] + _dot(ycat, wo_ref[...])
    h1_ref[...] = h1
    xn = _rms(h1, gffn_ref[...]).astype(BF16)
    xn_ref[...] = xn
    logits = _dot(xn, wr_ref[...]) + br_ref[...]
    lane = lax.broadcasted_iota(jnp.int32, logits.shape, 1)
    big = jnp.int32(LANES)

    def top1(mask):
        mval = jnp.max(jnp.where(mask, logits, NEG_INF), axis=-1, keepdims=True)
        idx = jnp.min(jnp.where(mask & (logits == mval), lane, big), axis=-1, keepdims=True)
        return mval, idx

    gmask = (lane >= N_EXPERTS) & (lane < N_EXPERTS + N_EXPERT_GROUPS)
    gmax, gidx = top1(gmask)
    gate_g = 1.0 / jnp.sum(jnp.where(gmask, jnp.exp(logits - gmax), 0.0), axis=-1, keepdims=True)
    grp = gidx - N_EXPERTS
    emask = (lane // EXPERTS_PER_GROUP) == grp
    v1, i1 = top1(emask)
    v2, i2 = top1(emask & (lane != i1))
    e2 = jnp.exp(v2 - v1)
    g1 = gate_g / (1.0 + e2)
    g2 = gate_g * e2 / (1.0 + e2)
    route = jnp.where(lane == i1, g1, 0.0) + jnp.where(lane == i2, g2, 0.0)
    route = jnp.where(lane == N_EXPERTS, i1.astype(F32), route)
    route = jnp.where(lane == N_EXPERTS + 1, i2.astype(F32), route)
    route = jnp.where(lane == N_EXPERTS + 2, g1, route)
    route = jnp.where(lane == N_EXPERTS + 3, g2, route)
    route_ref[...] = route


def _merge(h, y_ssd, y_att, w, tm):
    t = h.shape[0]
    assert t % tm == 0
    row = lambda n: pl.BlockSpec((tm, n), lambda i: (i, 0))
    consts = [w["w_out"], w["g_ffn"], w["w_route"], w["b_route"]]
    return pl.pallas_call(
        _merge_kernel,
        grid=(t // tm,),
        in_specs=[row(D_MODEL), row(SSD_DIM), row(ATT_DIM)] + [_const_spec(c.shape) for c in consts],
        out_specs=[row(D_MODEL), row(D_MODEL), row(LANES)],
        out_shape=[jax.ShapeDtypeStruct((t, D_MODEL), F32),
                   jax.ShapeDtypeStruct((t, D_MODEL), BF16),
                   jax.ShapeDtypeStruct((t, LANES), F32)],
        compiler_params=pltpu.CompilerParams(
            dimension_semantics=("arbitrary",), vmem_limit_bytes=VMEM_LIMIT),
        name="merge_route",
    )(h, y_ssd, y_att, *consts)


def _expert_kernel(h1_ref, xn_ref, route_ref, wg_ref, wu_ref, wd_ref, o_ref):
    e = pl.program_id(1)

    @pl.when(e == 0)
    def _():
        o_ref[...] = h1_ref[...]

    x = xn_ref[...]
    lane = lax.broadcasted_iota(jnp.int32, route_ref.shape, 1)
    cw = jnp.sum(jnp.where(lane == e, route_ref[...], 0.0), axis=-1, keepdims=True)
    hid = _silu(_dot(x, wg_ref[...])) * _dot(x, wu_ref[...])
    o_ref[...] += cw * _dot(hid.astype(BF16), wd_ref[...])


def _experts(h1, xn, route, w, tm):
    t = h1.shape[0]
    assert t % tm == 0
    row = lambda n: pl.BlockSpec((tm, n), lambda i, e: (i, 0))
    return pl.pallas_call(
        _expert_kernel,
        grid=(t // tm, N_EXPERTS),
        in_specs=[row(D_MODEL), row(D_MODEL), row(LANES),
                  pl.BlockSpec((None, D_MODEL, EXPERT_FF), lambda i, e: (e, 0, 0)),
                  pl.BlockSpec((None, D_MODEL, EXPERT_FF), lambda i, e: (e, 0, 0)),
                  pl.BlockSpec((None, EXPERT_FF, D_MODEL), lambda i, e: (e, 0, 0))],
        out_specs=row(D_MODEL),
        out_shape=jax.ShapeDtypeStruct((t, D_MODEL), F32),
        compiler_params=pltpu.CompilerParams(
            dimension_semantics=("arbitrary", "arbitrary"), vmem_limit_bytes=VMEM_LIMIT),
        name="experts",
    )(h1, xn, route, w["w_gate"], w["w_up"], w["w_down"])


def _prepare_weights(g_mix, w_in, conv_w, conv_b, dt_bias, a_log, d_skip, g_ssd_out, g_q, g_k,
                     attn_sinks, g_att_out, w_out, g_ffn, w_route_group, b_route_group,
                     w_route_expert, b_route_expert, w_gate, w_up, w_down):
    cuts = np.cumsum([0, SSD_DIM, CONV_DIM, SSD_HEADS, ATT_DIM, KV_DIM, KV_DIM])
    wi = w_in.astype(BF16)
    seg = lambda i: wi[:, cuts[i]:cuts[i + 1]]
    pad_lanes = lambda a: jnp.pad(a, ((0, 0), (0, LANES - a.shape[1])))
    bd = np.kron(np.eye(KV_DIM // HEAD_DIM, dtype=np.float32), np.ones((HEAD_DIM, HEAD_DIM), np.float32))
    n_route = N_EXPERTS + N_EXPERT_GROUPS
    return {
        "g_mix": g_mix.reshape(1, D_MODEL),
        "w_z": seg(0), "w_xbc": seg(1), "w_dt": pad_lanes(seg(2)), "w_q": seg(3), "w_k": seg(4), "w_v": seg(5),
        "g_q": jnp.tile(g_q, KV_DIM // HEAD_DIM).reshape(1, KV_DIM),
        "g_k": jnp.tile(g_k, KV_DIM // HEAD_DIM).reshape(1, KV_DIM),
        "bd": jnp.asarray(bd, BF16),
        "conv_w": conv_w, "conv_b": conv_b.reshape(1, CONV_DIM),
        "dt_bias": pad_lanes(dt_bias.reshape(1, SSD_HEADS)),
        "a_log": pad_lanes(a_log.reshape(1, SSD_HEADS)),
        "d_skip": jnp.repeat(d_skip, HEAD_DIM).reshape(1, SSD_DIM),
        "g_ssd_out": g_ssd_out.reshape(1, SSD_DIM),
        "attn_sinks": attn_sinks,
        "g_att_out": g_att_out.reshape(1, ATT_DIM),
        "w_out": w_out.astype(BF16),
        "g_ffn": g_ffn.reshape(1, D_MODEL),
        "w_route": pad_lanes(jnp.concatenate([w_route_expert, w_route_group], axis=1)).astype(BF16),
        "b_route": pad_lanes(jnp.concatenate([b_route_expert, b_route_group]).reshape(1, n_route)),
        "w_gate": w_gate.astype(BF16), "w_up": w_up.astype(BF16), "w_down": w_down.astype(BF16),
    }


def _segment(x3, w, mk, mv, tail, h0t, kprev, vprev, L, tb_ssd, tb_att, tm, tm_moe, from_cache):
    S, Ls, _ = x3.shape
    x = x3.reshape(S * Ls, D_MODEL)
    z, xbc, q, k, v, dt = _project(x, w, tm)
    r3 = lambda a: a.reshape(S, Ls, a.shape[-1])
    xbc3, k3, v3 = r3(xbc), r3(k), r3(v)
    y_ssd, h_new = _ssd(xbc3, r3(z), r3(dt), tail, h0t, w, L, tb_ssd)
    if not from_cache:
        kprev, vprev = k3, v3
    y_att = _attention(r3(q), k3, v3, kprev, vprev, mk, mv, w, L, tb_att, from_cache)
    h1, xn, route = _merge(x, y_ssd.reshape(S * Ls, SSD_DIM), y_att.reshape(S * Ls, ATT_DIM), w, tm)
    y = _experts(h1, xn, route, w, tm_moe)
    return y.reshape(S, Ls, D_MODEL), xbc3, h_new, k3, v3


def _state_to_kernel(h):
    return jnp.transpose(h.reshape(h.shape[0], SSD_DIM, D_STATE), (0, 2, 1))


def _state_from_kernel(ht):
    return jnp.transpose(ht, (0, 2, 1)).reshape(ht.shape[0], SSD_HEADS, HEAD_DIM, D_STATE)


def kernel(x_prompt, x_sample, cache_conv, state_ssd, cache_k, cache_v, meta_tokens, g_mix, w_in, conv_w, conv_b, dt_bias, a_log, d_skip, g_ssd_out, g_q, g_k, attn_sinks, g_att_out, w_out, g_ffn, w_route_group, b_route_group, w_route_expert, b_route_expert, w_gate, w_up, w_down):
    w = _prepare_weights(g_mix[0], w_in[0], conv_w[0], conv_b[0], dt_bias[0], a_log[0], d_skip[0],
                         g_ssd_out[0], g_q[0], g_k[0], attn_sinks[0], g_att_out[0], w_out[0], g_ffn[0],
                         w_route_group[0], b_route_group[0], w_route_expert[0], b_route_expert[0],
                         w_gate[0], w_up[0], w_down[0])
    n_b = x_sample.shape[0]
    n_dec = x_sample.shape[1]

    _, m_xbc, _, mk, mv, m_dt = _project(meta_tokens, w, N_META)
    zero_tail = jnp.zeros((1, CONV_WIDTH - 1, CONV_DIM), F32)
    zero_state = jnp.zeros((1, D_STATE, SSD_DIM), F32)
    m_xbc3 = m_xbc.reshape(1, N_META, CONV_DIM)
    _, m_state = _ssd(m_xbc3, jnp.zeros((1, N_META, SSD_DIM), F32), m_dt.reshape(1, N_META, LANES),
                      zero_tail, zero_state, w, N_META, N_META)
    m_tail = m_xbc3[:, N_META - (CONV_WIDTH - 1):]

    yp, xbc_p, st_p, k_p, v_p = _segment(
        x_prompt, w, mk, mv, m_tail, m_state, None, None,
        L=CHUNK, tb_ssd=256, tb_att=256, tm=256, tm_moe=1024, from_cache=False)
    ys, xbc_s, st_s, k_s, v_s = _segment(
        x_sample, w, mk, mv, cache_conv[0], _state_to_kernel(state_ssd[0]),
        cache_k[0].reshape(n_b, WINDOW, KV_DIM), cache_v[0].reshape(n_b, WINDOW, KV_DIM),
        L=n_dec, tb_ssd=n_dec, tb_att=n_dec, tm=256, tm_moe=512, from_cache=True)

    bp = x_prompt.shape[0]
    heads = lambda a, rows: a.reshape(a.shape[0], rows, ATT_KV_HEADS, HEAD_DIM)[None]
    return (yp, ys,
            xbc_p[:, -(CONV_WIDTH - 1):][None],
            _state_from_kernel(st_p)[None],
            heads(k_p[:, -WINDOW:], WINDOW), heads(v_p[:, -WINDOW:], WINDOW),
            xbc_s[:, -(CONV_WIDTH - 1):][None],
            _state_from_kernel(st_s)[None],
            heads(k_s, n_dec), heads(v_s, n_dec))
```

```python
import functools
import math

import numpy as np
import jax
import jax.numpy as jnp
from jax import lax
from jax.experimental import pallas as pl
from jax.experimental.pallas import tpu as pltpu

D_MODEL = 1024
CHUNK = 64
N_META = 16
HEAD_DIM = 64
ATT_HEADS = 16
ATT_KV_HEADS = 4
ATT_REP = ATT_HEADS // ATT_KV_HEADS
ATT_DIM = ATT_HEADS * HEAD_DIM
KV_DIM = ATT_KV_HEADS * HEAD_DIM
WINDOW = 128
SSD_HEADS = 16
SSD_DIM = SSD_HEADS * HEAD_DIM
SSD_GROUPS = 2
GROUP_DIM = SSD_DIM // SSD_GROUPS
D_STATE = 128
CONV_WIDTH = 4
CONV_DIM = SSD_DIM + 2 * SSD_GROUPS * D_STATE
N_EXPERT_GROUPS = 4
EXPERTS_PER_GROUP = 8
N_EXPERTS = N_EXPERT_GROUPS * EXPERTS_PER_GROUP
EXPERT_FF = D_MODEL // 4
EPS = 1e-6

LANES = 128
SUBLANES = 8
KEY_SPAN = 256
VMEM_LIMIT = 48 * 1024 * 1024

F32 = jnp.float32
BF16 = jnp.bfloat16
NEG_INF = float("-inf")


def _dot(a, b):
    return jnp.dot(a, b, preferred_element_type=F32)


def _dot_nt(a, b):
    return lax.dot_general(a, b, (((1,), (1,)), ((), ())), preferred_element_type=F32)


def _dot_tn(a, b):
    return lax.dot_general(a, b, (((0,), (0,)), ((), ())), preferred_element_type=F32)


def _split3(x):
    hi = x.astype(BF16)
    r = x - hi.astype(F32)
    mid = r.astype(BF16)
    lo = (r - mid.astype(F32)).astype(BF16)
    return hi, mid, lo


def _dot_sel(x, sel):
    hi, mid, lo = _split3(x)
    return _dot(hi, sel) + _dot(mid, sel) + _dot(lo, sel)


def _sel_dot(sel, x):
    hi, mid, lo = _split3(x)
    return _dot(sel, hi) + _dot(sel, mid) + _dot(sel, lo)


def _silu(x):
    return x * (1.0 / (1.0 + jnp.exp(-x)))


def _rms(x, g):
    ms = jnp.mean(x * x, axis=-1, keepdims=True)
    return x * lax.rsqrt(ms + EPS) * g


def _const_spec(shape):
    n = len(shape)
    return pl.BlockSpec(shape, lambda *_: (0,) * n)


def _head_norm(x, bd, g):
    sq = x * x
    hi = sq.astype(BF16)
    lo = (sq - hi.astype(F32)).astype(BF16)
    ms = (_dot(hi, bd) + _dot(lo, bd)) * (1.0 / HEAD_DIM)
    return x * lax.rsqrt(ms + EPS) * g


def _proj_kernel(x_ref, gmix_ref, wz_ref, wxbc_ref, wq_ref, wk_ref, wv_ref, wdt_ref,
                 gq_ref, gk_ref, bd_ref, z_ref, xbc_ref, q_ref, k_ref, v_ref, dt_ref):
    xn = _rms(x_ref[...], gmix_ref[...]).astype(BF16)
    z_ref[...] = _dot(xn, wz_ref[...])
    xbc_ref[...] = _dot(xn, wxbc_ref[...])
    v_ref[...] = _dot(xn, wv_ref[...])
    dt_ref[...] = _dot(xn, wdt_ref[...])
    bd = bd_ref[...]
    k_ref[...] = _head_norm(_dot(xn, wk_ref[...]), bd, gk_ref[...])
    scale = HEAD_DIM ** -0.5
    for j in range(ATT_DIM // KV_DIM):
        sl = slice(j * KV_DIM, (j + 1) * KV_DIM)
        qj = _head_norm(_dot(xn, wq_ref[:, sl]), bd, gq_ref[...])
        q_ref[:, sl] = (qj * scale).astype(BF16)


def _project(x, w, tm):
    t = x.shape[0]
    assert t % tm == 0
    row = lambda n: pl.BlockSpec((tm, n), lambda i: (i, 0))
    ins = [x, w["g_mix"], w["w_z"], w["w_xbc"], w["w_q"], w["w_k"], w["w_v"], w["w_dt"],
           w["g_q"], w["g_k"], w["bd"]]
    in_specs = [row(D_MODEL)] + [_const_spec(a.shape) for a in ins[1:]]
    out_dims = (SSD_DIM, CONV_DIM, ATT_DIM, KV_DIM, KV_DIM, LANES)
    out_dtypes = (F32, F32, BF16, F32, F32, F32)
    return pl.pallas_call(
        _proj_kernel,
        grid=(t // tm,),
        in_specs=in_specs,
        out_specs=[row(n) for n in out_dims],
        out_shape=[jax.ShapeDtypeStruct((t, n), d) for n, d in zip(out_dims, out_dtypes)],
        compiler_params=pltpu.CompilerParams(
            dimension_semantics=("arbitrary",), vmem_limit_bytes=VMEM_LIMIT),
        name="projection",
    )(*ins)


def _ssd_kernel(xbc_ref, z_ref, dt_ref, tail_ref, h0_ref, convw_ref, convb_ref, dtb_ref,
                alog_ref, dskip_ref, gout_ref, ehead_ref, epos_ref, tri_ref,
                y_ref, hout_ref, buf_ref, act_ref, dts_ref, ys_ref, st_ref, *, L, TB):
    hp = LANES // L
    n_tiles = SSD_HEADS // hp
    tile_w = hp * HEAD_DIM
    b = pl.program_id(1)
    pad = SUBLANES - (CONV_WIDTH - 1)

    @pl.when(b == 0)
    def _():
        buf_ref[pad:SUBLANES, :] = tail_ref[...]
        st_ref[...] = h0_ref[...]

    buf_ref[SUBLANES:SUBLANES + TB, :] = xbc_ref[...]
    acc = convb_ref[...] + convw_ref[0:1, :] * buf_ref[pad:pad + TB, :]
    for j in range(1, CONV_WIDTH):
        acc = acc + convw_ref[j:j + 1, :] * buf_ref[pad + j:pad + j + TB, :]
    act_ref[...] = _silu(acc)
    buf_ref[pad:SUBLANES, :] = buf_ref[TB + pad:TB + SUBLANES, :]

    dtx = dt_ref[...] + dtb_ref[...]
    dts_ref[...] = jnp.maximum(dtx, 0.0) + jnp.log(1.0 + jnp.exp(-jnp.abs(dtx)))

    a_row = -jnp.exp(alog_ref[...])
    ehead = ehead_ref[...]
    epos = epos_ref[...]
    tri = tri_ref[...]
    dskip = dskip_ref[...]
    row_i = lax.broadcasted_iota(jnp.int32, (L, n_tiles * LANES), 0)
    col_i = lax.broadcasted_iota(jnp.int32, (L, n_tiles * LANES), 1)
    pos_i = col_i % L
    diag_mask = row_i == pos_i
    causal_mask = row_i >= pos_i
    colhead = lax.broadcasted_iota(jnp.int32, (L, tile_w), 1) // HEAD_DIM

    def chunk(c, carry):
        r0 = pl.multiple_of(c * L, L)
        dtc = dts_ref[pl.ds(r0, L), :]
        cs = _sel_dot(tri, dtc * a_row)
        cs_last = cs[L - 1:L, :]
        dec_end = jnp.exp(cs_last - cs)
        dt_b = _dot_sel(dtc, ehead)
        w_b = _dot_sel(dtc * dec_end, ehead)
        ecs_b = _dot_sel(jnp.exp(cs), ehead)
        bdec_b = _dot_sel(jnp.broadcast_to(jnp.exp(cs_last), (2 * SUBLANES, LANES)), ehead)[0:1, :]
        cs_col = _dot_sel(cs, epos)
        cs_row = jnp.sum(jnp.where(diag_mask, cs_col, 0.0), axis=0, keepdims=True)
        lmat = jnp.exp(jnp.where(causal_mask, cs_col - cs_row, NEG_INF))

        xs = act_ref[pl.ds(r0, L), 0:SSD_DIM]
        xdt = (xs * dt_b).astype(BF16)
        wx = (xs * w_b).astype(BF16)
        st = st_ref[...]
        st16 = st.astype(BF16)

        y_diag = []
        y_off = []
        s_new = []
        cb2 = [None] * SSD_GROUPS
        for g in range(SSD_GROUPS):
            bg = act_ref[pl.ds(r0, L), SSD_DIM + g * D_STATE:SSD_DIM + (g + 1) * D_STATE].astype(BF16)
            cg = act_ref[pl.ds(r0, L), SSD_DIM + (SSD_GROUPS + g) * D_STATE:
                         SSD_DIM + (SSD_GROUPS + g + 1) * D_STATE].astype(BF16)
            cb2[g] = _dot_nt(cg, jnp.concatenate([bg] * hp, axis=0))
            gsl = slice(g * GROUP_DIM, (g + 1) * GROUP_DIM)
            y_off.append(_dot(cg, st16[:, gsl]))
            s_new.append(_dot_tn(bg, wx[:, gsl]))
        for t in range(n_tiles):
            g = (t * hp * HEAD_DIM) // GROUP_DIM
            gmat = (lmat[:, t * LANES:(t + 1) * LANES] * cb2[g]).astype(BF16)
            xt = xdt[:, t * tile_w:(t + 1) * tile_w]
            rhs = jnp.concatenate(
                [jnp.where(colhead == hh, xt, jnp.zeros_like(xt)) for hh in range(hp)], axis=0)
            y_diag.append(_dot(gmat, rhs))
        y = (jnp.concatenate(y_diag, axis=1) + jnp.concatenate(y_off, axis=1) * ecs_b
             + dskip * xs)
        ys_ref[pl.ds(r0, L), :] = y
        st_ref[...] = bdec_b * st + jnp.concatenate(s_new, axis=1)
        return carry

    lax.fori_loop(0, TB // L, chunk, 0)

    yg = ys_ref[...] * _silu(z_ref[...])
    y_ref[...] = _rms(yg, gout_ref[...])
    hout_ref[...] = st_ref[...]


def _ssd(xbc, z, dt, tail, h0t, w, L, TB):
    S, Ls, _ = xbc.shape
    assert Ls % TB == 0 and TB % L == 0 and LANES % L == 0
    nb = Ls // TB
    hp = LANES // L
    n_tiles = SSD_HEADS // hp
    per_stream = lambda a: (lambda s, b: (s, 0, 0)) if a.shape[0] == S and S > 1 else (lambda s, b: (0, 0, 0))
    seq = lambda n: pl.BlockSpec((None, TB, n), lambda s, b: (s, b, 0))
    ehead = np.zeros((LANES, SSD_DIM), np.float32)
    epos = np.zeros((LANES, n_tiles * LANES), np.float32)
    for h in range(SSD_HEADS):
        ehead[h, h * HEAD_DIM:(h + 1) * HEAD_DIM] = 1.0
        epos[h, h * L:(h + 1) * L] = 1.0
    tri = np.tril(np.ones((L, L), np.float32))
    consts = [w["conv_w"], w["conv_b"], w["dt_bias"], w["a_log"], w["d_skip"], w["g_ssd_out"],
              jnp.asarray(ehead, BF16), jnp.asarray(epos, BF16), jnp.asarray(tri, BF16)]
    in_specs = [seq(CONV_DIM), seq(SSD_DIM), seq(LANES),
                pl.BlockSpec((None, CONV_WIDTH - 1, CONV_DIM), per_stream(tail)),
                pl.BlockSpec((None, D_STATE, SSD_DIM), per_stream(h0t))]
    in_specs += [_const_spec(c.shape) for c in consts]
    return pl.pallas_call(
        functools.partial(_ssd_kernel, L=L, TB=TB),
        grid=(S, nb),
        in_specs=in_specs,
        out_specs=[seq(SSD_DIM), pl.BlockSpec((None, D_STATE, SSD_DIM), lambda s, b: (s, 0, 0))],
        out_shape=[jax.ShapeDtypeStruct((S, Ls, SSD_DIM), F32),
                   jax.ShapeDtypeStruct((S, D_STATE, SSD_DIM), F32)],
        scratch_shapes=[pltpu.VMEM((SUBLANES + TB, CONV_DIM), F32),
                        pltpu.VMEM((TB, CONV_DIM), F32),
                        pltpu.VMEM((TB, LANES), F32),
                        pltpu.VMEM((TB, SSD_DIM), F32),
                        pltpu.VMEM((D_STATE, SSD_DIM), F32)],
        compiler_params=pltpu.CompilerParams(
            dimension_semantics=("arbitrary", "arbitrary"), vmem_limit_bytes=VMEM_LIMIT),
        name="ssd",
    )(xbc, z, dt, tail, h0t, *consts)


def _attn_kernel(sink_ref, q_ref, k_ref, v_ref, kp_ref, vp_ref, mk_ref, mv_ref, bias_ref, g_ref,
                 o_ref, kbuf, vbuf, ybuf, *, L, TB, mask_start):
    b = pl.program_id(1)
    kbuf[0:WINDOW, :] = kp_ref[...].astype(BF16)
    vbuf[0:WINDOW, :] = vp_ref[...].astype(BF16)
    kbuf[WINDOW:WINDOW + TB, :] = k_ref[...].astype(BF16)
    vbuf[WINDOW:WINDOW + TB, :] = v_ref[...].astype(BF16)
    n_pad = KEY_SPAN - WINDOW - L - N_META
    mk_tail = jnp.concatenate([mk_ref[...].astype(BF16), jnp.zeros((n_pad, KV_DIM), BF16)], axis=0)
    mv_tail = jnp.concatenate([mv_ref[...].astype(BF16), jnp.zeros((n_pad, KV_DIM), BF16)], axis=0)
    row_head = lax.broadcasted_iota(jnp.int32, (ATT_REP * L, 1), 0) // L
    lane = lax.broadcasted_iota(jnp.int32, (ATT_REP * L, KEY_SPAN), 1)

    def chunk(c, carry):
        r0 = pl.multiple_of(c * L, L)
        kc = jnp.concatenate([kbuf[pl.ds(r0, WINDOW), :], kbuf[pl.ds(r0 + WINDOW, L), :], mk_tail], axis=0)
        vc = jnp.concatenate([vbuf[pl.ds(r0, WINDOW), :], vbuf[pl.ds(r0 + WINDOW, L), :], mv_tail], axis=0)
        qc = q_ref[pl.ds(r0, L), :]
        if mask_start:
            n_invalid = WINDOW - (b * TB + c * L)
        for g in range(ATT_KV_HEADS):
            q4 = jnp.concatenate(
                [qc[:, (ATT_REP * g + r) * HEAD_DIM:(ATT_REP * g + r + 1) * HEAD_DIM]
                 for r in range(ATT_REP)], axis=0)
            s = _dot_nt(q4, kc[:, g * HEAD_DIM:(g + 1) * HEAD_DIM]) + bias_ref[g]
            if mask_start:
                s = jnp.where(lane < n_invalid, NEG_INF, s)
            sink = jnp.zeros((ATT_REP * L, 1), F32)
            for r in range(ATT_REP):
                sink = jnp.where(row_head == r, sink_ref[ATT_REP * g + r], sink)
            m = jnp.maximum(jnp.max(s, axis=-1, keepdims=True), sink)
            e = jnp.exp(s - m)
            den = jnp.sum(e, axis=-1, keepdims=True) + jnp.exp(sink - m)
            o4 = _dot(e.astype(BF16), vc[:, g * HEAD_DIM:(g + 1) * HEAD_DIM]) * (1.0 / den)
            og = jnp.concatenate([o4[r * L:(r + 1) * L, :] for r in range(ATT_REP)], axis=1)
            ybuf[pl.ds(r0, L), g * KV_DIM:(g + 1) * KV_DIM] = og
        return carry

    lax.fori_loop(0, TB // L, chunk, 0)
    o_ref[...] = _rms(ybuf[...], g_ref[...])


def _attention(q, k, v, kprev, vprev, mk, mv, w, L, TB, from_cache):
    S, Ls, _ = q.shape
    assert Ls % TB == 0 and TB % L == 0 and TB % WINDOW == 0 or from_cache
    nb = Ls // TB
    seq = lambda n: pl.BlockSpec((None, TB, n), lambda s, b: (s, b, 0))
    if from_cache:
        prev_map = lambda s, b: (s, 0, 0)
    else:
        prev_map = lambda s, b: (s, jnp.maximum(b * (TB // WINDOW) - 1, 0), 0)
    prev = pl.BlockSpec((None, WINDOW, KV_DIM), prev_map)
    slopes = 2.0 ** (-8.0 * np.arange(1, ATT_HEADS + 1, dtype=np.float64) / ATT_HEADS)
    dist = np.abs(WINDOW + np.arange(L)[:, None] - np.arange(WINDOW + L)[None, :])
    bias = np.full((ATT_HEADS, L, KEY_SPAN), NEG_INF, np.float32)
    bias[:, :, :WINDOW + L] = -slopes[:, None, None] * dist[None]
    bias[:, :, WINDOW + L:WINDOW + L + N_META] = 0.0
    bias = jnp.asarray(bias.reshape(ATT_KV_HEADS, ATT_REP * L, KEY_SPAN))
    in_specs = [pl.BlockSpec(memory_space=pltpu.SMEM),
                seq(ATT_DIM), seq(KV_DIM), seq(KV_DIM), prev, prev,
                _const_spec(mk.shape), _const_spec(mv.shape), _const_spec(bias.shape),
                _const_spec(w["g_att_out"].shape)]
    return pl.pallas_call(
        functools.partial(_attn_kernel, L=L, TB=TB, mask_start=not from_cache),
        grid=(S, nb),
        in_specs=in_specs,
        out_specs=seq(ATT_DIM),
        out_shape=jax.ShapeDtypeStruct((S, Ls, ATT_DIM), F32),
        scratch_shapes=[pltpu.VMEM((WINDOW + TB, KV_DIM), BF16),
                        pltpu.VMEM((WINDOW + TB, KV_DIM), BF16),
                        pltpu.VMEM((TB, ATT_DIM), F32)],
        compiler_params=pltpu.CompilerParams(
            dimension_semantics=("arbitrary", "arbitrary"), vmem_limit_bytes=VMEM_LIMIT),
        name="attention",
    )(w["attn_sinks"], q, k, v, kprev, vprev, mk, mv, bias, w["g_att_out"])


TOK_TILE = 256
SEG_ALIGN = SUBLANES
SORT_ROWS = 2 * TOK_TILE + N_EXPERTS * SEG_ALIGN
XS_COLS = D_MODEL + LANES
META_P1, META_P2, META_G1, META_G2 = N_EXPERTS, N_EXPERTS + 1, N_EXPERTS + 2, N_EXPERTS + 3


def _merge_kernel(h_ref, ys_ref, ya_ref, wo_ref, gffn_ref, wr_ref, br_ref, tri_ref, upper_ref,
                  h1_ref, xn_ref, meta_ref, cnt_ref):
    ycat = jnp.concatenate([ys_ref[...].astype(BF16), ya_ref[...].astype(BF16)], axis=1)
    h1 = h_ref[...] + _dot(ycat, wo_ref[...])
    h1_ref[...] = h1
    xn = _rms(h1, gffn_ref[...]).astype(BF16)
    xn_ref[...] = xn
    logits = _dot(xn, wr_ref[...]) + br_ref[...]
    lane = lax.broadcasted_iota(jnp.int32, logits.shape, 1)
    big = jnp.int32(LANES)

    def top1(mask):
        mval = jnp.max(jnp.where(mask, logits, NEG_INF), axis=-1, keepdims=True)
        idx = jnp.min(jnp.where(mask & (logits == mval), lane, big), axis=-1, keepdims=True)
        return mval, idx

    gmask = (lane >= N_EXPERTS) & (lane < N_EXPERTS + N_EXPERT_GROUPS)
    gmax, gidx = top1(gmask)
    gate_g = 1.0 / jnp.sum(jnp.where(gmask, jnp.exp(logits - gmax), 0.0), axis=-1, keepdims=True)
    grp = gidx - N_EXPERTS
    emask = (lane // EXPERTS_PER_GROUP) == grp
    v1, i1 = top1(emask)
    v2, i2 = top1(emask & (lane != i1))
    e2 = jnp.exp(v2 - v1)
    g1 = gate_g / (1.0 + e2)
    g2 = gate_g * e2 / (1.0 + e2)
    oh1 = jnp.where(lane == i1, 1.0, 0.0)
    oh2 = jnp.where(lane == i2, 1.0, 0.0)
    oh = oh1 + oh2
    earlier = _dot(tri_ref[...], oh.astype(BF16))
    cnt = jnp.sum(oh, axis=0, keepdims=True)
    units = jnp.floor((cnt + (SEG_ALIGN - 1)) * (1.0 / SEG_ALIGN))
    units = jnp.broadcast_to(units, (2 * SUBLANES, LANES)).astype(BF16)
    slot = _dot(units, upper_ref[...])[0:1, :] * SEG_ALIGN + earlier
    p1 = jnp.sum(oh1 * slot, axis=-1, keepdims=True)
    p2 = jnp.sum(oh2 * slot, axis=-1, keepdims=True)
    meta = jnp.where(lane == META_P1, p1, 0.0)
    meta = jnp.where(lane == META_P2, p2, meta)
    meta = jnp.where(lane == META_G1, g1, meta)
    meta = jnp.where(lane == META_G2, g2, meta)
    meta_ref[...] = meta
    cnt_ref[...] = jnp.broadcast_to(cnt, (SUBLANES, LANES))


def _merge(h, y_ssd, y_att, w):
    t = h.shape[0]
    tm = TOK_TILE
    assert t % tm == 0
    row = lambda n: pl.BlockSpec((tm, n), lambda i: (i, 0))
    tri = jnp.asarray(np.tril(np.ones((tm, tm), np.float32), -1), BF16)
    upper = jnp.asarray(np.triu(np.ones((LANES, LANES), np.float32), 1), BF16)
    consts = [w["w_out"], w["g_ffn"], w["w_route"], w["b_route"], tri, upper]
    return pl.pallas_call(
        _merge_kernel,
        grid=(t // tm,),
        in_specs=[row(D_MODEL), row(SSD_DIM), row(ATT_DIM)] + [_const_spec(c.shape) for c in consts],
        out_specs=[row(D_MODEL), row(D_MODEL), row(LANES),
                   pl.BlockSpec((None, SUBLANES, LANES), lambda i: (i, 0, 0))],
        out_shape=[jax.ShapeDtypeStruct((t, D_MODEL), F32),
                   jax.ShapeDtypeStruct((t, D_MODEL), BF16),
                   jax.ShapeDtypeStruct((t, LANES), F32),
                   jax.ShapeDtypeStruct((t // tm, SUBLANES, LANES), F32)],
        compiler_params=pltpu.CompilerParams(
            dimension_semantics=("arbitrary",), vmem_limit_bytes=VMEM_LIMIT),
        name="merge_route",
    )(h, y_ssd, y_att, *consts)


def _meta_col(meta, j):
    lane = lax.broadcasted_iota(jnp.int32, meta.shape, 1)
    return jnp.sum(jnp.where(lane == j, meta, 0.0), axis=-1, keepdims=True)


def _pair_selectors(meta):
    rows = lax.broadcasted_iota(jnp.int32, (meta.shape[0], SORT_ROWS), 1)
    s1 = jnp.where(rows == _meta_col(meta, META_P1).astype(jnp.int32), 1.0, 0.0).astype(BF16)
    s2 = jnp.where(rows == _meta_col(meta, META_P2).astype(jnp.int32), 1.0, 0.0).astype(BF16)
    return s1, s2


def _segment_dmas(off_ref, cnt_ref, row_ref, b, tile_buf, hbm, sem, to_hbm, wait):
    def body(e, carry):
        k = b * N_EXPERTS + e
        n = pl.multiple_of(cnt_ref[k], SEG_ALIGN)

        @pl.when(n > 0)
        def _():
            v = tile_buf.at[pl.ds(pl.multiple_of(off_ref[k], SEG_ALIGN), n)]
            h = hbm.at[pl.ds(pl.multiple_of(row_ref[k], SEG_ALIGN), n)]
            cp = pltpu.make_async_copy(v, h, sem) if to_hbm else pltpu.make_async_copy(h, v, sem)
            if wait:
                cp.wait()
            else:
                cp.start()
        return carry

    lax.fori_loop(0, N_EXPERTS, body, 0)


def _gate_lanes(g):
    hi, mid, lo = _split3(g)
    lane = lax.broadcasted_iota(jnp.int32, (g.shape[0], LANES), 1)
    out = jnp.where(lane == 0, hi.astype(F32), 0.0)
    out = jnp.where(lane == 1, mid.astype(F32), out)
    out = jnp.where(lane == 2, lo.astype(F32), out)
    return out.astype(BF16)


def _dispatch_kernel(off_ref, cnt_ref, row_ref, tail_row_ref, tail_cnt_ref, xn_ref, meta_ref,
                     xs_ref, sort_buf, zero_buf, sem, tail_sem):
    b = pl.program_id(0)
    nb = pl.num_programs(0)
    slot = b % 2
    seg = functools.partial(_segment_dmas, off_ref, cnt_ref, row_ref, hbm=xs_ref, to_hbm=True)

    def tails(wait):
        def body(e, carry):
            n = pl.multiple_of(tail_cnt_ref[e], SEG_ALIGN)

            @pl.when(n > 0)
            def _():
                cp = pltpu.make_async_copy(
                    zero_buf.at[pl.ds(0, n)],
                    xs_ref.at[pl.ds(pl.multiple_of(tail_row_ref[e], SEG_ALIGN), n)], tail_sem.at[0])
                if wait:
                    cp.wait()
                else:
                    cp.start()
            return carry

        lax.fori_loop(0, N_EXPERTS, body, 0)

    @pl.when(b >= 2)
    def _():
        seg(b - 2, sort_buf.at[slot], sem=sem.at[slot], wait=True)

    @pl.when(b == 0)
    def _():
        zero_buf[...] = jnp.zeros(zero_buf.shape, F32)
        tails(wait=False)

    meta = meta_ref[...]
    s1, s2 = _pair_selectors(meta)
    sort_buf[slot, :, 0:D_MODEL] = _dot_tn(s1 + s2, xn_ref[...])
    sort_buf[slot, :, D_MODEL:XS_COLS] = (_dot_tn(s1, _gate_lanes(_meta_col(meta, META_G1)))
                                          + _dot_tn(s2, _gate_lanes(_meta_col(meta, META_G2))))
    seg(b, sort_buf.at[slot], sem=sem.at[slot], wait=False)

    @pl.when(b == nb - 1)
    def _():
        @pl.when(b >= 1)
        def _():
            seg(b - 1, sort_buf.at[1 - slot], sem=sem.at[1 - slot], wait=True)

        seg(b, sort_buf.at[slot], sem=sem.at[slot], wait=True)
        tails(wait=True)


def _expert_kernel(tile_expert_ref, n_active_ref, xs_ref, wg_ref, wu_ref, wd_ref, y_ref):
    @pl.when(pl.program_id(0) < n_active_ref[0])
    def _():
        x = xs_ref[:, 0:D_MODEL].astype(BF16)
        gate = jnp.sum(xs_ref[:, D_MODEL:XS_COLS], axis=-1, keepdims=True)
        hid = _silu(_dot(x, wg_ref[...].astype(BF16))) * _dot(x, wu_ref[...].astype(BF16))
        y_ref[...] = gate * _dot(hid.astype(BF16), wd_ref[...].astype(BF16))


def _combine_kernel(off_ref, cnt_ref, row_ref, h1_ref, meta_ref, y_ref, o_ref, y_buf, sem):
    b = pl.program_id(0)
    nb = pl.num_programs(0)
    slot = b % 2
    seg = functools.partial(_segment_dmas, off_ref, cnt_ref, row_ref, hbm=y_ref, to_hbm=False)

    @pl.when(b == 0)
    def _():
        y_buf[...] = jnp.zeros(y_buf.shape, F32)
        seg(b, y_buf.at[slot], sem=sem.at[slot], wait=False)

    @pl.when(b + 1 < nb)
    def _():
        seg(b + 1, y_buf.at[1 - slot], sem=sem.at[1 - slot], wait=False)

    seg(b, y_buf.at[slot], sem=sem.at[slot], wait=True)
    s1, s2 = _pair_selectors(meta_ref[...])
    sel = s1 + s2
    y = y_buf[slot]
    hi = y.astype(BF16)
    lo = (y - hi.astype(F32)).astype(BF16)
    o_ref[...] = h1_ref[...] + (_dot(sel, hi) + _dot(sel, lo))


def _moe(h1, xn, meta, cnt, w, tm):
    t = h1.shape[0]
    nb = t // TOK_TILE
    i32 = jnp.int32
    cnt = cnt[:, 0, :N_EXPERTS].astype(i32)
    cnt_al = (cnt + (SEG_ALIGN - 1)) // SEG_ALIGN * SEG_ALIGN
    off = jnp.cumsum(cnt_al, axis=1) - cnt_al
    tot = jnp.sum(cnt_al, axis=0)
    tot_tm = (tot + (tm - 1)) // tm * tm
    start = jnp.cumsum(tot_tm) - tot_tm
    row = start[None, :] + jnp.cumsum(cnt_al, axis=0) - cnt_al
    tile_ends = jnp.cumsum(tot_tm // tm)
    max_rows = 2 * t + (SEG_ALIGN - 1) * min(N_EXPERTS * nb, 2 * t) + N_EXPERTS * (tm - SEG_ALIGN)
    nt = -(-max_rows // tm)
    tile_expert = jnp.sum((jnp.arange(nt, dtype=i32)[:, None] >= tile_ends[None, :]).astype(i32), axis=1)
    tile_expert = jnp.minimum(tile_expert, N_EXPERTS - 1)
    n_active = tile_ends[-1:].astype(i32)
    tables = [off.reshape(-1), cnt_al.reshape(-1), row.reshape(-1).astype(i32)]

    tok = lambda n: pl.BlockSpec((TOK_TILE, n), lambda i, *_: (i, 0))
    hbm = pl.BlockSpec(memory_space=pl.ANY)
    xs = pl.pallas_call(
        _dispatch_kernel,
        grid_spec=pltpu.PrefetchScalarGridSpec(
            num_scalar_prefetch=5, grid=(nb,),
            in_specs=[tok(D_MODEL), tok(LANES)],
            out_specs=hbm,
            scratch_shapes=[pltpu.VMEM((2, SORT_ROWS, XS_COLS), F32),
                            pltpu.VMEM((tm, XS_COLS), F32),
                            pltpu.SemaphoreType.DMA((2,)),
                            pltpu.SemaphoreType.DMA((1,))]),
        out_shape=jax.ShapeDtypeStruct((nt * tm, XS_COLS), F32),
        compiler_params=pltpu.CompilerParams(
            dimension_semantics=("arbitrary",), vmem_limit_bytes=VMEM_LIMIT),
        name="dispatch",
    )(*tables, (start + tot).astype(i32), (tot_tm - tot).astype(i32), xn, meta)

    act = lambda i, te, na: jnp.minimum(i, na[0] - 1)
    y = pl.pallas_call(
        _expert_kernel,
        grid_spec=pltpu.PrefetchScalarGridSpec(
            num_scalar_prefetch=2, grid=(nt,),
            in_specs=[pl.BlockSpec((tm, XS_COLS), lambda i, te, na: (act(i, te, na), 0)),
                      pl.BlockSpec((None, D_MODEL, EXPERT_FF), lambda i, te, na: (te[act(i, te, na)], 0, 0)),
                      pl.BlockSpec((None, D_MODEL, EXPERT_FF), lambda i, te, na: (te[act(i, te, na)], 0, 0)),
                      pl.BlockSpec((None, EXPERT_FF, D_MODEL), lambda i, te, na: (te[act(i, te, na)], 0, 0))],
            out_specs=pl.BlockSpec((tm, D_MODEL), lambda i, te, na: (act(i, te, na), 0))),
        out_shape=jax.ShapeDtypeStruct((nt * tm, D_MODEL), F32),
        compiler_params=pltpu.CompilerParams(
            dimension_semantics=("arbitrary",), vmem_limit_bytes=VMEM_LIMIT),
        name="experts",
    )(tile_expert, n_active, xs, w["w_gate"], w["w_up"], w["w_down"])

    return pl.pallas_call(
        _combine_kernel,
        grid_spec=pltpu.PrefetchScalarGridSpec(
            num_scalar_prefetch=3, grid=(nb,),
            in_specs=[tok(D_MODEL), tok(LANES), hbm],
            out_specs=tok(D_MODEL),
            scratch_shapes=[pltpu.VMEM((2, SORT_ROWS, D_MODEL), F32),
                            pltpu.SemaphoreType.DMA((2,))]),
        out_shape=jax.ShapeDtypeStruct((t, D_MODEL), F32),
        compiler_params=pltpu.CompilerParams(
            dimension_semantics=("arbitrary",), vmem_limit_bytes=VMEM_LIMIT),
        name="combine",
    )(*tables, h1, meta, y)


def _prepare_weights(g_mix, w_in, conv_w, conv_b, dt_bias, a_log, d_skip, g_ssd_out, g_q, g_k,
                     attn_sinks, g_att_out, w_out, g_ffn, w_route_group, b_route_group,
                     w_route_expert, b_route_expert, w_gate, w_up, w_down):
    cuts = np.cumsum([0, SSD_DIM, CONV_DIM, SSD_HEADS, ATT_DIM, KV_DIM, KV_DIM])
    wi = w_in.astype(BF16)
    seg = lambda i: wi[:, cuts[i]:cuts[i + 1]]
    pad_lanes = lambda a: jnp.pad(a, ((0, 0), (0, LANES - a.shape[1])))
    bd = np.kron(np.eye(KV_DIM // HEAD_DIM, dtype=np.float32), np.ones((HEAD_DIM, HEAD_DIM), np.float32))
    n_route = N_EXPERTS + N_EXPERT_GROUPS
    return {
        "g_mix": g_mix.reshape(1, D_MODEL),
        "w_z": seg(0), "w_xbc": seg(1), "w_dt": pad_lanes(seg(2)), "w_q": seg(3), "w_k": seg(4), "w_v": seg(5),
        "g_q": jnp.tile(g_q, KV_DIM // HEAD_DIM).reshape(1, KV_DIM),
        "g_k": jnp.tile(g_k, KV_DIM // HEAD_DIM).reshape(1, KV_DIM),
        "bd": jnp.asarray(bd, BF16),
        "conv_w": conv_w, "conv_b": conv_b.reshape(1, CONV_DIM),
        "dt_bias": pad_lanes(dt_bias.reshape(1, SSD_HEADS)),
        "a_log": pad_lanes(a_log.reshape(1, SSD_HEADS)),
        "d_skip": jnp.repeat(d_skip, HEAD_DIM).reshape(1, SSD_DIM),
        "g_ssd_out": g_ssd_out.reshape(1, SSD_DIM),
        "attn_sinks": attn_sinks,
        "g_att_out": g_att_out.reshape(1, ATT_DIM),
        "w_out": w_out.astype(BF16),
        "g_ffn": g_ffn.reshape(1, D_MODEL),
        "w_route": pad_lanes(jnp.concatenate([w_route_expert, w_route_group], axis=1)).astype(BF16),
        "b_route": pad_lanes(jnp.concatenate([b_route_expert, b_route_group]).reshape(1, n_route)),
        "w_gate": w_gate, "w_up": w_up, "w_down": w_down,
    }


def _segment(x3, w, mk, mv, tail, h0t, kprev, vprev, L, tb_ssd, tb_att, tm, tm_moe, from_cache):
    S, Ls, _ = x3.shape
    x = x3.reshape(S * Ls, D_MODEL)
    z, xbc, q, k, v, dt = _project(x, w, tm)
    r3 = lambda a: a.reshape(S, Ls, a.shape[-1])
    xbc3, k3, v3 = r3(xbc), r3(k), r3(v)
    y_ssd, h_new = _ssd(xbc3, r3(z), r3(dt), tail, h0t, w, L, tb_ssd)
    if not from_cache:
        kprev, vprev = k3, v3
    y_att = _attention(r3(q), k3, v3, kprev, vprev, mk, mv, w, L, tb_att, from_cache)
    h1, xn, meta, cnt = _merge(x, y_ssd.reshape(S * Ls, SSD_DIM), y_att.reshape(S * Ls, ATT_DIM), w)
    y = _moe(h1, xn, meta, cnt, w, tm_moe)
    return y.reshape(S, Ls, D_MODEL), xbc3, h_new, k3, v3


def _state_to_kernel(h):
    return jnp.transpose(h.reshape(h.shape[0], SSD_DIM, D_STATE), (0, 2, 1))


def _state_from_kernel(ht):
    return jnp.transpose(ht, (0, 2, 1)).reshape(ht.shape[0], SSD_HEADS, HEAD_DIM, D_STATE)


def kernel(x_prompt, x_sample, cache_conv, state_ssd, cache_k, cache_v, meta_tokens, g_mix, w_in, conv_w, conv_b, dt_bias, a_log, d_skip, g_ssd_out, g_q, g_k, attn_sinks, g_att_out, w_out, g_ffn, w_route_group, b_route_group, w_route_expert, b_route_expert, w_gate, w_up, w_down):
    w = _prepare_weights(g_mix[0], w_in[0], conv_w[0], conv_b[0], dt_bias[0], a_log[0], d_skip[0],
                         g_ssd_out[0], g_q[0], g_k[0], attn_sinks[0], g_att_out[0], w_out[0], g_ffn[0],
                         w_route_group[0], b_route_group[0], w_route_expert[0], b_route_expert[0],
                         w_gate[0], w_up[0], w_down[0])
    n_b = x_sample.shape[0]
    n_dec = x_sample.shape[1]

    _, m_xbc, _, mk, mv, m_dt = _project(meta_tokens, w, N_META)
    zero_tail = jnp.zeros((1, CONV_WIDTH - 1, CONV_DIM), F32)
    zero_state = jnp.zeros((1, D_STATE, SSD_DIM), F32)
    m_xbc3 = m_xbc.reshape(1, N_META, CONV_DIM)
    _, m_state = _ssd(m_xbc3, jnp.zeros((1, N_META, SSD_DIM), F32), m_dt.reshape(1, N_META, LANES),
                      zero_tail, zero_state, w, N_META, N_META)
    m_tail = m_xbc3[:, N_META - (CONV_WIDTH - 1):]

    yp, xbc_p, st_p, k_p, v_p = _segment(
        x_prompt, w, mk, mv, m_tail, m_state, None, None,
        L=CHUNK, tb_ssd=256, tb_att=256, tm=256, tm_moe=256, from_cache=False)
    ys, xbc_s, st_s, k_s, v_s = _segment(
        x_sample, w, mk, mv, cache_conv[0], _state_to_kernel(state_ssd[0]),
        cache_k[0].reshape(n_b, WINDOW, KV_DIM), cache_v[0].reshape(n_b, WINDOW, KV_DIM),
        L=n_dec, tb_ssd=n_dec, tb_att=n_dec, tm=256, tm_moe=128, from_cache=True)

    bp = x_prompt.shape[0]
    heads = lambda a, rows: a.reshape(a.shape[0], rows, ATT_KV_HEADS, HEAD_DIM)[None]
    return (yp, ys,
            xbc_p[:, -(CONV_WIDTH - 1):][None],
            _state_from_kernel(st_p)[None],
            heads(k_p[:, -WINDOW:], WINDOW), heads(v_p[:, -WINDOW:], WINDOW),
            xbc_s[:, -(CONV_WIDTH - 1):][None],
            _state_from_kernel(st_s)[None],
            heads(k_s, n_dec), heads(v_s, n_dec))
```

```python
import functools
import math

import numpy as np
import jax
import jax.numpy as jnp
from jax import lax
from jax.experimental import pallas as pl
from jax.experimental.pallas import tpu as pltpu

D_MODEL = 1024
CHUNK = 64
N_META = 16
HEAD_DIM = 64
ATT_HEADS = 16
ATT_KV_HEADS = 4
ATT_REP = ATT_HEADS // ATT_KV_HEADS
ATT_DIM = ATT_HEADS * HEAD_DIM
KV_DIM = ATT_KV_HEADS * HEAD_DIM
WINDOW = 128
SSD_HEADS = 16
SSD_DIM = SSD_HEADS * HEAD_DIM
SSD_GROUPS = 2
GROUP_DIM = SSD_DIM // SSD_GROUPS
D_STATE = 128
CONV_WIDTH = 4
CONV_DIM = SSD_DIM + 2 * SSD_GROUPS * D_STATE
N_EXPERT_GROUPS = 4
EXPERTS_PER_GROUP = 8
N_EXPERTS = N_EXPERT_GROUPS * EXPERTS_PER_GROUP
EXPERT_FF = D_MODEL // 4
EPS = 1e-6

LANES = 128
SUBLANES = 8
MXU_DIM = 256
DT_COPIES = 3
KEY_SPAN = 256
VMEM_LIMIT = 48 * 1024 * 1024

F32 = jnp.float32
BF16 = jnp.bfloat16
NEG_INF = float("-inf")


def _dot(a, b):
    return jnp.dot(a, b, preferred_element_type=F32)


def _dot_nt(a, b):
    return lax.dot_general(a, b, (((1,), (1,)), ((), ())), preferred_element_type=F32)


def _dot_tn(a, b):
    return lax.dot_general(a, b, (((0,), (0,)), ((), ())), preferred_element_type=F32)


def _split3(x):
    hi = x.astype(BF16)
    r = x - hi.astype(F32)
    mid = r.astype(BF16)
    lo = (r - mid.astype(F32)).astype(BF16)
    return hi, mid, lo


def _pack3(x):
    hi, mid, lo = _split3(x)
    lane = lax.broadcasted_iota(jnp.int32, x.shape, 1)
    packed = jnp.where(lane < SSD_HEADS, hi.astype(F32),
                       jnp.where(lane < 2 * SSD_HEADS, mid.astype(F32), lo.astype(F32)))
    return packed.astype(BF16)


def _sel_dot(sel, x):
    hi, mid, lo = _split3(x)
    return _dot(sel, hi) + _dot(sel, mid) + _dot(sel, lo)


def _silu(x):
    return x * (1.0 / (1.0 + jnp.exp(-x)))


def _rms(x, g):
    ms = jnp.mean(x * x, axis=-1, keepdims=True)
    return x * lax.rsqrt(ms + EPS) * g


def _const_spec(shape):
    n = len(shape)
    return pl.BlockSpec(shape, lambda *_: (0,) * n, pipeline_mode=pl.Buffered(1))


def _head_norm(x, bd, g):
    sq = x * x
    hi = sq.astype(BF16)
    lo = (sq - hi.astype(F32)).astype(BF16)
    ms = (_dot(hi, bd) + _dot(lo, bd)) * (1.0 / HEAD_DIM)
    return x * lax.rsqrt(ms + EPS) * g


def _proj_kernel(x_ref, gmix_ref, wz_ref, wxbc_ref, wq_ref, wk_ref, wv_ref, wdt_ref,
                 gq_ref, gk_ref, bd_ref, z_ref, xbc_ref, q_ref, k_ref, v_ref, dt_ref):
    xn = _rms(x_ref[...], gmix_ref[...]).astype(BF16)
    z_ref[...] = _dot(xn, wz_ref[...])
    xbc_ref[...] = _dot(xn, wxbc_ref[...])
    v_ref[...] = _dot(xn, wv_ref[...])
    dt_ref[...] = _dot(xn, wdt_ref[...])
    bd = bd_ref[...]
    k_ref[...] = _head_norm(_dot(xn, wk_ref[...]), bd, gk_ref[...])
    scale = HEAD_DIM ** -0.5
    for j in range(ATT_DIM // KV_DIM):
        sl = slice(j * KV_DIM, (j + 1) * KV_DIM)
        qj = _head_norm(_dot(xn, wq_ref[:, sl]), bd, gq_ref[...])
        q_ref[:, sl] = (qj * scale).astype(BF16)


def _project(x, w, tm):
    t = x.shape[0]
    assert t % tm == 0
    row = lambda n: pl.BlockSpec((tm, n), lambda i: (i, 0))
    ins = [x, w["g_mix"], w["w_z"], w["w_xbc"], w["w_q"], w["w_k"], w["w_v"], w["w_dt"],
           w["g_q"], w["g_k"], w["bd"]]
    in_specs = [row(D_MODEL)] + [_const_spec(a.shape) for a in ins[1:]]
    out_dims = (SSD_DIM, CONV_DIM, ATT_DIM, KV_DIM, KV_DIM, LANES)
    out_dtypes = (F32, F32, BF16, F32, F32, F32)
    return pl.pallas_call(
        _proj_kernel,
        grid=(t // tm,),
        in_specs=in_specs,
        out_specs=[row(n) for n in out_dims],
        out_shape=[jax.ShapeDtypeStruct((t, n), d) for n, d in zip(out_dims, out_dtypes)],
        compiler_params=pltpu.CompilerParams(
            dimension_semantics=("arbitrary",), vmem_limit_bytes=VMEM_LIMIT),
        name="projection",
    )(*ins)


def _ssd_kernel(xbc_ref, z_ref, dt_ref, tail_ref, h0_ref, convw_ref, convb_ref, dtb_ref,
                alog_ref, dskip_ref, gout_ref, ehead_ref, epos_ref, tri_ref,
                y_ref, hout_ref, buf_ref, act_ref, dts_ref, ys_ref, st_ref, *, L, TB):
    hp = LANES // L
    n_lane_tiles = SSD_HEADS // hp
    hq = MXU_DIM // L
    n_tiles = SSD_HEADS // hq
    tile_w = hq * HEAD_DIM
    b = pl.program_id(1)
    pad = SUBLANES - (CONV_WIDTH - 1)

    @pl.when(b == 0)
    def _():
        buf_ref[pad:SUBLANES, :] = tail_ref[...]
        st_ref[...] = h0_ref[...]

    buf_ref[SUBLANES:SUBLANES + TB, :] = xbc_ref[...]
    acc = convb_ref[...] + convw_ref[0:1, :] * buf_ref[pad:pad + TB, :]
    for j in range(1, CONV_WIDTH):
        acc = acc + convw_ref[j:j + 1, :] * buf_ref[pad + j:pad + j + TB, :]
    act_ref[...] = _silu(acc)
    buf_ref[pad:SUBLANES, :] = buf_ref[TB + pad:TB + SUBLANES, :]

    dtx = dt_ref[...] + dtb_ref[...]
    dts_ref[...] = jnp.maximum(dtx, 0.0) + jnp.log(1.0 + jnp.exp(-jnp.abs(dtx)))

    a_row = -jnp.exp(alog_ref[...])
    ehead = ehead_ref[...]
    epos = epos_ref[...]
    tri = tri_ref[...]
    dskip = dskip_ref[...]
    row_i = lax.broadcasted_iota(jnp.int32, (L, n_lane_tiles * LANES), 0)
    col_i = lax.broadcasted_iota(jnp.int32, (L, n_lane_tiles * LANES), 1)
    pos_i = col_i % L
    diag_mask = row_i == pos_i
    causal_mask = row_i >= pos_i
    colhead = lax.broadcasted_iota(jnp.int32, (L, tile_w), 1) // HEAD_DIM

    def chunk(c):
        r0 = pl.multiple_of(c * L, L)
        dtc = dts_ref[pl.ds(r0, L), :]
        cs = _sel_dot(tri, dtc * a_row)
        cs16 = _pack3(cs)
        dt_b = _dot(_pack3(dtc), ehead)
        cs_b = _dot(cs16, ehead)
        cs_last_b = cs_b[L - 1:L, :]
        ecs_b = jnp.exp(cs_b)
        dec_end_b = jnp.exp(cs_last_b - cs_b)
        bdec_b = jnp.exp(cs_last_b)
        cs_col = cs_b if L == HEAD_DIM else _dot(cs16, epos)
        cs_row = jnp.sum(jnp.where(diag_mask, cs_col, 0.0), axis=0, keepdims=True)
        lmat = jnp.exp(jnp.where(causal_mask, cs_col - cs_row, NEG_INF))

        xs = act_ref[pl.ds(r0, L), 0:SSD_DIM]
        xdt_f = xs * dt_b
        xdt = xdt_f.astype(BF16)
        wx = (xdt_f * dec_end_b).astype(BF16)
        st = st_ref[...]
        st16 = st.astype(BF16)

        y_diag = []
        y_off = []
        s_new = []
        cb = [None] * SSD_GROUPS
        for g in range(SSD_GROUPS):
            bg = act_ref[pl.ds(r0, L), SSD_DIM + g * D_STATE:SSD_DIM + (g + 1) * D_STATE].astype(BF16)
            cg = act_ref[pl.ds(r0, L), SSD_DIM + (SSD_GROUPS + g) * D_STATE:
                         SSD_DIM + (SSD_GROUPS + g + 1) * D_STATE].astype(BF16)
            cb[g] = _dot_nt(cg, jnp.concatenate([bg] * hp, axis=0))
            gsl = slice(g * GROUP_DIM, (g + 1) * GROUP_DIM)
            y_off.append(_dot(cg, st16[:, gsl]))
            s_new.append(_dot_tn(bg, wx[:, gsl]))
        lane_tiles = MXU_DIM // LANES
        for t in range(n_tiles):
            cbt = jnp.concatenate(
                [cb[((t * lane_tiles + j) * hp * HEAD_DIM) // GROUP_DIM] for j in range(lane_tiles)], axis=1)
            gmat = (lmat[:, t * MXU_DIM:(t + 1) * MXU_DIM] * cbt).astype(BF16)
            xt = xdt[:, t * tile_w:(t + 1) * tile_w]
            rhs = jnp.concatenate(
                [jnp.where(colhead == hh, xt, jnp.zeros_like(xt)) for hh in range(hq)], axis=0)
            y_diag.append(_dot(gmat, rhs))
        y = (jnp.concatenate(y_diag, axis=1) + jnp.concatenate(y_off, axis=1) * ecs_b
             + dskip * xs)
        ys_ref[pl.ds(r0, L), :] = y
        st_ref[...] = bdec_b * st + jnp.concatenate(s_new, axis=1)

    n_chunks = TB // L
    unroll = 4 if n_chunks % 4 == 0 else 1

    def chunks(i, carry):
        for u in range(unroll):
            chunk(i * unroll + u)
        return carry

    lax.fori_loop(0, n_chunks // unroll, chunks, 0)

    yg = ys_ref[...] * _silu(z_ref[...])
    y_ref[...] = _rms(yg, gout_ref[...])
    hout_ref[...] = st_ref[...]


def _ssd(xbc, z, dt, tail, h0t, w, L, TB):
    S, Ls, _ = xbc.shape
    assert Ls % TB == 0 and TB % L == 0 and LANES % L == 0
    nb = Ls // TB
    hp = LANES // L
    n_tiles = SSD_HEADS // hp
    per_stream = lambda a: (lambda s, b: (s, 0, 0)) if a.shape[0] == S and S > 1 else (lambda s, b: (0, 0, 0))
    seq = lambda n: pl.BlockSpec((None, TB, n), lambda s, b: (s, b, 0))
    ehead = np.zeros((LANES, SSD_DIM), np.float32)
    epos = np.zeros((LANES, n_tiles * LANES), np.float32)
    for part in range(DT_COPIES):
        for h in range(SSD_HEADS):
            ehead[part * SSD_HEADS + h, h * HEAD_DIM:(h + 1) * HEAD_DIM] = 1.0
            epos[part * SSD_HEADS + h, h * L:(h + 1) * L] = 1.0
    tri = np.tril(np.ones((L, L), np.float32))
    consts = [w["conv_w"], w["conv_b"], w["dt_bias"], w["a_log"], w["d_skip"], w["g_ssd_out"],
              jnp.asarray(ehead, BF16), jnp.asarray(epos, BF16), jnp.asarray(tri, BF16)]
    in_specs = [seq(CONV_DIM), seq(SSD_DIM), seq(LANES),
                pl.BlockSpec((None, CONV_WIDTH - 1, CONV_DIM), per_stream(tail)),
                pl.BlockSpec((None, D_STATE, SSD_DIM), per_stream(h0t))]
    in_specs += [_const_spec(c.shape) for c in consts]
    return pl.pallas_call(
        functools.partial(_ssd_kernel, L=L, TB=TB),
        grid=(S, nb),
        in_specs=in_specs,
        out_specs=[seq(SSD_DIM), pl.BlockSpec((None, D_STATE, SSD_DIM), lambda s, b: (s, 0, 0))],
        out_shape=[jax.ShapeDtypeStruct((S, Ls, SSD_DIM), F32),
                   jax.ShapeDtypeStruct((S, D_STATE, SSD_DIM), F32)],
        scratch_shapes=[pltpu.VMEM((SUBLANES + TB, CONV_DIM), F32),
                        pltpu.VMEM((TB, CONV_DIM), F32),
                        pltpu.VMEM((TB, LANES), F32),
                        pltpu.VMEM((TB, SSD_DIM), F32),
                        pltpu.VMEM((D_STATE, SSD_DIM), F32)],
        compiler_params=pltpu.CompilerParams(
            dimension_semantics=("arbitrary", "arbitrary"), vmem_limit_bytes=VMEM_LIMIT),
        name="ssd",
    )(xbc, z, dt, tail, h0t, *consts)


def _split_heads(x):
    lane = lax.broadcasted_iota(jnp.int32, (x.shape[0], LANES), 1)
    low = lane < HEAD_DIM
    lo, hi = [], []
    for t in range(KV_DIM // LANES):
        tile = x[:, t * LANES:(t + 1) * LANES]
        swapped = pltpu.roll(tile, HEAD_DIM, axis=1)
        lo += [jnp.where(low, tile, 0.0), jnp.where(low, swapped, 0.0)]
        hi += [jnp.where(low, 0.0, swapped), jnp.where(low, 0.0, tile)]
    return jnp.concatenate(lo, axis=1).astype(BF16), jnp.concatenate(hi, axis=1).astype(BF16)


def _attn_kernel(sink_ref, q_ref, k_ref, v_ref, kp_ref, vp_ref, mk_ref, mv_ref, bias_ref, g_ref,
                 o_ref, k_lo, k_hi, v_lo, v_hi, ybuf, s_buf, e_buf, sink_buf, *, L, TB, from_cache):
    b = pl.program_id(1)
    bufs = (k_lo, k_hi, v_lo, v_hi)
    if from_cache:
        window = _split_heads(kp_ref[...]) + _split_heads(vp_ref[...])
        for buf, val in zip(bufs, window):
            buf[0:WINDOW, :] = val
    else:
        @pl.when(b == 0)
        def _():
            for buf in bufs:
                buf[0:WINDOW, :] = jnp.zeros((WINDOW, buf.shape[1]), BF16)

        @pl.when(b > 0)
        def _():
            for buf in bufs:
                buf[0:WINDOW, :] = buf[TB:TB + WINDOW, :]

    for buf, val in zip(bufs, _split_heads(k_ref[...]) + _split_heads(v_ref[...])):
        buf[WINDOW:WINDOW + TB, :] = val
    n_pad = KEY_SPAN - WINDOW - L - N_META
    pad_rows = jnp.zeros((n_pad, 2 * KV_DIM), BF16)
    tails = [jnp.concatenate([t, pad_rows], axis=0)
             for t in _split_heads(mk_ref[...]) + _split_heads(mv_ref[...])]
    lane = lax.broadcasted_iota(jnp.int32, (2 * L, 2 * KEY_SPAN), 1) % KEY_SPAN
    lane_h = lax.broadcasted_iota(jnp.int32, (L, LANES), 1)
    lane_v = lax.broadcasted_iota(jnp.int32, (KEY_SPAN, LANES), 1)
    ones_lo = jnp.where(lane_v < HEAD_DIM, 1.0, 0.0).astype(BF16)
    ones_hi = jnp.where(lane_v < HEAD_DIM, 0.0, 1.0).astype(BF16)
    n_chunks = TB // L
    pairs_per_group = ATT_REP // 2

    def chunk_keys(c):
        r0 = c * L
        return [jnp.concatenate([buf[r0:r0 + WINDOW, :], buf[r0 + WINDOW:r0 + WINDOW + L, :], tail], axis=0)
                for buf, tail in zip(bufs, tails)]

    for c in range(n_chunks):
        r0 = c * L
        keys = chunk_keys(c)
        for g in range(ATT_KV_HEADS):
            gl = slice(g * LANES, (g + 1) * LANES)
            q4 = jnp.concatenate([q_ref[r0:r0 + L, p * LANES:(p + 1) * LANES]
                                  for p in range(pairs_per_group * g, pairs_per_group * (g + 1))], axis=0)
            s = _dot_nt(q4, jnp.concatenate([keys[0][:, gl], keys[1][:, gl]], axis=0)) + bias_ref[g]
            if not from_cache and r0 < WINDOW:
                n_invalid = WINDOW - (b * TB + r0)
                s = jnp.where(lane < n_invalid, NEG_INF, s)
            s_buf[c * ATT_KV_HEADS + g] = s

    for c in range(n_chunks):
        for pair in range(ATT_HEADS // 2):
            slot = c * ATT_KV_HEADS + pair // pairs_per_group
            rows = slice((pair % pairs_per_group) * L, (pair % pairs_per_group + 1) * L)
            sink_terms = []
            for half in range(2):
                cols = slice(half * KEY_SPAN, (half + 1) * KEY_SPAN)
                s = s_buf[slot, rows, cols]
                sink = sink_ref[2 * pair + half]
                m = jnp.maximum(jnp.max(s, axis=-1, keepdims=True), sink)
                e_buf[slot, rows, cols] = jnp.exp(s - m).astype(BF16)
                sink_terms.append(jnp.exp(sink - m))
            sink_buf[c * (ATT_HEADS // 2) + pair] = jnp.where(lane_h < HEAD_DIM, sink_terms[0], sink_terms[1])

    for c in range(n_chunks):
        r0 = c * L
        keys = chunk_keys(c)
        for g in range(ATT_KV_HEADS):
            gl = slice(g * LANES, (g + 1) * LANES)
            vcat = jnp.concatenate([jnp.concatenate([keys[2][:, gl], ones_lo], axis=1),
                                    jnp.concatenate([keys[3][:, gl], ones_hi], axis=1)], axis=0)
            ov = _dot(e_buf[c * ATT_KV_HEADS + g], vcat)
            for j in range(pairs_per_group):
                pair = pairs_per_group * g + j
                den = ov[j * L:(j + 1) * L, LANES:2 * LANES] + sink_buf[c * (ATT_HEADS // 2) + pair]
                ybuf[r0:r0 + L, pair * LANES:(pair + 1) * LANES] = ov[j * L:(j + 1) * L, 0:LANES] * (1.0 / den)
    o_ref[...] = _rms(ybuf[...], g_ref[...])


def _attention(q, k, v, kprev, vprev, mk, mv, w, L, TB, from_cache):
    S, Ls, _ = q.shape
    assert Ls % TB == 0 and TB % L == 0 and (from_cache or TB >= WINDOW)
    nb = Ls // TB
    seq = lambda n: pl.BlockSpec((None, TB, n), lambda s, b: (s, b, 0))
    if not from_cache:
        kprev = vprev = jnp.zeros((1, WINDOW, KV_DIM), F32)
    prev = pl.BlockSpec((None, WINDOW, KV_DIM),
                        (lambda s, b: (s, 0, 0)) if from_cache else (lambda s, b: (0, 0, 0)))
    slopes = 2.0 ** (-8.0 * np.arange(1, ATT_HEADS + 1, dtype=np.float64) / ATT_HEADS)
    dist = np.abs(WINDOW + np.arange(L)[:, None] - np.arange(WINDOW + L)[None, :])
    bias = np.full((ATT_HEADS, L, KEY_SPAN), NEG_INF, np.float32)
    bias[:, :, :WINDOW + L] = -slopes[:, None, None] * dist[None]
    bias[:, :, WINDOW + L:WINDOW + L + N_META] = 0.0
    bias = bias.reshape(ATT_KV_HEADS, ATT_REP // 2, 2, L, KEY_SPAN).transpose(0, 1, 3, 2, 4)
    bias = jnp.asarray(bias.reshape(ATT_KV_HEADS, ATT_REP // 2 * L, 2 * KEY_SPAN))
    n_chunks = TB // L
    in_specs = [pl.BlockSpec(memory_space=pltpu.SMEM),
                seq(ATT_DIM), seq(KV_DIM), seq(KV_DIM), prev, prev,
                _const_spec(mk.shape), _const_spec(mv.shape), _const_spec(bias.shape),
                _const_spec(w["g_att_out"].shape)]
    return pl.pallas_call(
        functools.partial(_attn_kernel, L=L, TB=TB, from_cache=from_cache),
        grid=(S, nb),
        in_specs=in_specs,
        out_specs=seq(ATT_DIM),
        out_shape=jax.ShapeDtypeStruct((S, Ls, ATT_DIM), F32),
        scratch_shapes=[pltpu.VMEM((WINDOW + TB, 2 * KV_DIM), BF16) for _ in range(4)]
                       + [pltpu.VMEM((TB, ATT_DIM), F32),
                          pltpu.VMEM((n_chunks * ATT_KV_HEADS, 2 * L, 2 * KEY_SPAN), F32),
                          pltpu.VMEM((n_chunks * ATT_KV_HEADS, 2 * L, 2 * KEY_SPAN), BF16),
                          pltpu.VMEM((n_chunks * ATT_HEADS // 2, L, LANES), F32)],
        compiler_params=pltpu.CompilerParams(
            dimension_semantics=("arbitrary", "arbitrary"), vmem_limit_bytes=VMEM_LIMIT),
        name="attention",
    )(w["attn_sinks"], q, k, v, kprev, vprev, mk, mv, bias, w["g_att_out"])


TOK_TILE = 256
SEG_ALIGN = SUBLANES
SORT_ROWS = 2 * TOK_TILE + N_EXPERTS * SEG_ALIGN
XS_COLS = D_MODEL + LANES
META_P1, META_P2, META_G1, META_G2 = N_EXPERTS, N_EXPERTS + 1, N_EXPERTS + 2, N_EXPERTS + 3


def _merge_kernel(h_ref, ys_ref, ya_ref, wo_ref, gffn_ref, wr_ref, br_ref, tri_ref, upper_ref,
                  h1_ref, xn_ref, meta_ref, cnt_ref):
    ycat = jnp.concatenate([ys_ref[...].astype(BF16), ya_ref[...].astype(BF16)], axis=1)
    h1 = h_ref[...] + _dot(ycat, wo_ref[...])
    h1_ref[...] = h1
    xn = _rms(h1, gffn_ref[...]).astype(BF16)
    xn_ref[...] = xn
    all_logits = _dot(xn, wr_ref[...]) + br_ref[...]
    tri = tri_ref[...]
    upper = upper_ref[...]
    lane = lax.broadcasted_iota(jnp.int32, (TOK_TILE, LANES), 1)
    big = jnp.int32(LANES)
    gmask = (lane >= N_EXPERTS) & (lane < N_EXPERTS + N_EXPERT_GROUPS)
    for s in range(all_logits.shape[0] // TOK_TILE):
        rows = slice(s * TOK_TILE, (s + 1) * TOK_TILE)
        logits = all_logits[rows]

        def top1(mask):
            mval = jnp.max(jnp.where(mask, logits, NEG_INF), axis=-1, keepdims=True)
            idx = jnp.min(jnp.where(mask & (logits == mval), lane, big), axis=-1, keepdims=True)
            return mval, idx

        gmax, gidx = top1(gmask)
        gate_g = 1.0 / jnp.sum(jnp.where(gmask, jnp.exp(logits - gmax), 0.0), axis=-1, keepdims=True)
        grp = gidx - N_EXPERTS
        emask = (lane // EXPERTS_PER_GROUP) == grp
        v1, i1 = top1(emask)
        v2, i2 = top1(emask & (lane != i1))
        e2 = jnp.exp(v2 - v1)
        g1 = gate_g / (1.0 + e2)
        g2 = gate_g * e2 / (1.0 + e2)
        oh1 = jnp.where(lane == i1, 1.0, 0.0)
        oh2 = jnp.where(lane == i2, 1.0, 0.0)
        oh = oh1 + oh2
        earlier = _dot(tri, oh.astype(BF16))
        cnt = jnp.sum(oh, axis=0, keepdims=True)
        units = jnp.floor((cnt + (SEG_ALIGN - 1)) * (1.0 / SEG_ALIGN))
        units = jnp.broadcast_to(units, (2 * SUBLANES, LANES)).astype(BF16)
        slot = _dot(units, upper)[0:1, :] * SEG_ALIGN + earlier
        p1 = jnp.sum(oh1 * slot, axis=-1, keepdims=True)
        p2 = jnp.sum(oh2 * slot, axis=-1, keepdims=True)
        meta = jnp.where(lane == META_P1, p1, 0.0)
        meta = jnp.where(lane == META_P2, p2, meta)
        meta = jnp.where(lane == META_G1, g1, meta)
        meta = jnp.where(lane == META_G2, g2, meta)
        meta_ref[rows, :] = meta
        cnt_ref[s] = jnp.broadcast_to(cnt, (SUBLANES, LANES))


def _merge(h, y_ssd, y_att, w, tm):
    t = h.shape[0]
    assert t % tm == 0 and tm % TOK_TILE == 0
    sub = tm // TOK_TILE
    row = lambda n: pl.BlockSpec((tm, n), lambda i: (i, 0))
    tri = jnp.asarray(np.tril(np.ones((TOK_TILE, TOK_TILE), np.float32), -1), BF16)
    upper = jnp.asarray(np.triu(np.ones((LANES, LANES), np.float32), 1), BF16)
    consts = [w["w_out"], w["g_ffn"], w["w_route"], w["b_route"], tri, upper]
    return pl.pallas_call(
        _merge_kernel,
        grid=(t // tm,),
        in_specs=[row(D_MODEL), row(SSD_DIM), row(ATT_DIM)] + [_const_spec(c.shape) for c in consts],
        out_specs=[row(D_MODEL), row(D_MODEL), row(LANES),
                   pl.BlockSpec((sub, SUBLANES, LANES), lambda i: (i, 0, 0))],
        out_shape=[jax.ShapeDtypeStruct((t, D_MODEL), F32),
                   jax.ShapeDtypeStruct((t, D_MODEL), BF16),
                   jax.ShapeDtypeStruct((t, LANES), F32),
                   jax.ShapeDtypeStruct((t // TOK_TILE, SUBLANES, LANES), F32)],
        compiler_params=pltpu.CompilerParams(
            dimension_semantics=("arbitrary",), vmem_limit_bytes=VMEM_LIMIT),
        name="merge_route",
    )(h, y_ssd, y_att, *consts)


def _meta_col(meta, j):
    lane = lax.broadcasted_iota(jnp.int32, meta.shape, 1)
    return jnp.sum(jnp.where(lane == j, meta, 0.0), axis=-1, keepdims=True)


def _pair_selectors(meta):
    rows = lax.broadcasted_iota(jnp.int32, (meta.shape[0], SORT_ROWS), 1)
    s1 = jnp.where(rows == _meta_col(meta, META_P1).astype(jnp.int32), 1.0, 0.0).astype(BF16)
    s2 = jnp.where(rows == _meta_col(meta, META_P2).astype(jnp.int32), 1.0, 0.0).astype(BF16)
    return s1, s2


def _segment_dmas(off_ref, cnt_ref, row_ref, b, tile_buf, hbm, sem, to_hbm, wait):
    def body(e, carry):
        k = b * N_EXPERTS + e
        n = pl.multiple_of(cnt_ref[k], SEG_ALIGN)

        @pl.when(n > 0)
        def _():
            v = tile_buf.at[pl.ds(pl.multiple_of(off_ref[k], SEG_ALIGN), n)]
            h = hbm.at[pl.ds(pl.multiple_of(row_ref[k], SEG_ALIGN), n)]
            cp = pltpu.make_async_copy(v, h, sem) if to_hbm else pltpu.make_async_copy(h, v, sem)
            if wait:
                cp.wait()
            else:
                cp.start()
        return carry

    lax.fori_loop(0, N_EXPERTS, body, 0)


def _gate_lanes(g):
    hi, mid, lo = _split3(g)
    lane = lax.broadcasted_iota(jnp.int32, (g.shape[0], LANES), 1)
    out = jnp.where(lane == 0, hi.astype(F32), 0.0)
    out = jnp.where(lane == 1, mid.astype(F32), out)
    out = jnp.where(lane == 2, lo.astype(F32), out)
    return out.astype(BF16)


def _dispatch_kernel(off_ref, cnt_ref, row_ref, tail_row_ref, tail_cnt_ref, xn_ref, meta_ref,
                     xs_ref, sort_buf, zero_buf, sem, tail_sem):
    b = pl.program_id(0)
    nb = pl.num_programs(0)
    slot = b % 2
    seg = functools.partial(_segment_dmas, off_ref, cnt_ref, row_ref, hbm=xs_ref, to_hbm=True)

    def tails(wait):
        def body(e, carry):
            n = pl.multiple_of(tail_cnt_ref[e], SEG_ALIGN)

            @pl.when(n > 0)
            def _():
                cp = pltpu.make_async_copy(
                    zero_buf.at[pl.ds(0, n)],
                    xs_ref.at[pl.ds(pl.multiple_of(tail_row_ref[e], SEG_ALIGN), n)], tail_sem.at[0])
                if wait:
                    cp.wait()
                else:
                    cp.start()
            return carry

        lax.fori_loop(0, N_EXPERTS, body, 0)

    @pl.when(b >= 2)
    def _():
        seg(b - 2, sort_buf.at[slot], sem=sem.at[slot], wait=True)

    @pl.when(b == 0)
    def _():
        zero_buf[...] = jnp.zeros(zero_buf.shape, F32)
        tails(wait=False)

    meta = meta_ref[...]
    s1, s2 = _pair_selectors(meta)
    sort_buf[slot, :, 0:D_MODEL] = _dot_tn(s1 + s2, xn_ref[...])
    sort_buf[slot, :, D_MODEL:XS_COLS] = (_dot_tn(s1, _gate_lanes(_meta_col(meta, META_G1)))
                                          + _dot_tn(s2, _gate_lanes(_meta_col(meta, META_G2))))
    seg(b, sort_buf.at[slot], sem=sem.at[slot], wait=False)

    @pl.when(b == nb - 1)
    def _():
        @pl.when(b >= 1)
        def _():
            seg(b - 1, sort_buf.at[1 - slot], sem=sem.at[1 - slot], wait=True)

        seg(b, sort_buf.at[slot], sem=sem.at[slot], wait=True)
        tails(wait=True)


def _expert_kernel(tile_expert_ref, n_active_ref, xs_ref, wg_ref, wu_ref, wd_ref, y_ref):
    @pl.when(pl.program_id(0) < n_active_ref[0])
    def _():
        x = xs_ref[:, 0:D_MODEL].astype(BF16)
        gate = jnp.sum(xs_ref[:, D_MODEL:XS_COLS], axis=-1, keepdims=True)
        hid = _silu(_dot(x, wg_ref[...].astype(BF16))) * _dot(x, wu_ref[...].astype(BF16))
        y_ref[...] = gate * _dot(hid.astype(BF16), wd_ref[...].astype(BF16))


def _combine_kernel(off_ref, cnt_ref, row_ref, h1_ref, meta_ref, y_ref, o_ref, y_buf, sem):
    b = pl.program_id(0)
    nb = pl.num_programs(0)
    slot = b % 2
    seg = functools.partial(_segment_dmas, off_ref, cnt_ref, row_ref, hbm=y_ref, to_hbm=False)

    @pl.when(b == 0)
    def _():
        y_buf[...] = jnp.zeros(y_buf.shape, F32)
        seg(b, y_buf.at[slot], sem=sem.at[slot], wait=False)

    @pl.when(b + 1 < nb)
    def _():
        seg(b + 1, y_buf.at[1 - slot], sem=sem.at[1 - slot], wait=False)

    seg(b, y_buf.at[slot], sem=sem.at[slot], wait=True)
    s1, s2 = _pair_selectors(meta_ref[...])
    sel = s1 + s2
    y = y_buf[slot]
    hi = y.astype(BF16)
    lo = (y - hi.astype(F32)).astype(BF16)
    o_ref[...] = h1_ref[...] + (_dot(sel, hi) + _dot(sel, lo))


def _moe(h1, xn, meta, cnt, w, tm):
    t = h1.shape[0]
    nb = t // TOK_TILE
    i32 = jnp.int32
    cnt = cnt[:, 0, :N_EXPERTS].astype(i32)
    cnt_al = (cnt + (SEG_ALIGN - 1)) // SEG_ALIGN * SEG_ALIGN
    off = jnp.cumsum(cnt_al, axis=1) - cnt_al
    tot = jnp.sum(cnt_al, axis=0)
    tot_tm = (tot + (tm - 1)) // tm * tm
    start = jnp.cumsum(tot_tm) - tot_tm
    row = start[None, :] + jnp.cumsum(cnt_al, axis=0) - cnt_al
    tile_ends = jnp.cumsum(tot_tm // tm)
    max_rows = 2 * t + (SEG_ALIGN - 1) * min(N_EXPERTS * nb, 2 * t) + N_EXPERTS * (tm - SEG_ALIGN)
    nt = -(-max_rows // tm)
    tile_expert = jnp.sum((jnp.arange(nt, dtype=i32)[:, None] >= tile_ends[None, :]).astype(i32), axis=1)
    tile_expert = jnp.minimum(tile_expert, N_EXPERTS - 1)
    n_active = tile_ends[-1:].astype(i32)
    tables = [off.reshape(-1), cnt_al.reshape(-1), row.reshape(-1).astype(i32)]

    tok = lambda n: pl.BlockSpec((TOK_TILE, n), lambda i, *_: (i, 0))
    hbm = pl.BlockSpec(memory_space=pl.ANY)
    xs = pl.pallas_call(
        _dispatch_kernel,
        grid_spec=pltpu.PrefetchScalarGridSpec(
            num_scalar_prefetch=5, grid=(nb,),
            in_specs=[tok(D_MODEL), tok(LANES)],
            out_specs=hbm,
            scratch_shapes=[pltpu.VMEM((2, SORT_ROWS, XS_COLS), F32),
                            pltpu.VMEM((tm, XS_COLS), F32),
                            pltpu.SemaphoreType.DMA((2,)),
                            pltpu.SemaphoreType.DMA((1,))]),
        out_shape=jax.ShapeDtypeStruct((nt * tm, XS_COLS), F32),
        compiler_params=pltpu.CompilerParams(
            dimension_semantics=("arbitrary",), vmem_limit_bytes=VMEM_LIMIT),
        name="dispatch",
    )(*tables, (start + tot).astype(i32), (tot_tm - tot).astype(i32), xn, meta)

    act = lambda i, te, na: jnp.minimum(i, na[0] - 1)
    y = pl.pallas_call(
        _expert_kernel,
        grid_spec=pltpu.PrefetchScalarGridSpec(
            num_scalar_prefetch=2, grid=(nt,),
            in_specs=[pl.BlockSpec((tm, XS_COLS), lambda i, te, na: (act(i, te, na), 0)),
                      pl.BlockSpec((None, D_MODEL, EXPERT_FF), lambda i, te, na: (te[act(i, te, na)], 0, 0)),
                      pl.BlockSpec((None, D_MODEL, EXPERT_FF), lambda i, te, na: (te[act(i, te, na)], 0, 0)),
                      pl.BlockSpec((None, EXPERT_FF, D_MODEL), lambda i, te, na: (te[act(i, te, na)], 0, 0))],
            out_specs=pl.BlockSpec((tm, D_MODEL), lambda i, te, na: (act(i, te, na), 0))),
        out_shape=jax.ShapeDtypeStruct((nt * tm, D_MODEL), F32),
        compiler_params=pltpu.CompilerParams(
            dimension_semantics=("arbitrary",), vmem_limit_bytes=VMEM_LIMIT),
        name="experts",
    )(tile_expert, n_active, xs, w["w_gate"], w["w_up"], w["w_down"])

    return pl.pallas_call(
        _combine_kernel,
        grid_spec=pltpu.PrefetchScalarGridSpec(
            num_scalar_prefetch=3, grid=(nb,),
            in_specs=[tok(D_MODEL), tok(LANES), hbm],
            out_specs=tok(D_MODEL),
            scratch_shapes=[pltpu.VMEM((2, SORT_ROWS, D_MODEL), F32),
                            pltpu.SemaphoreType.DMA((2,))]),
        out_shape=jax.ShapeDtypeStruct((t, D_MODEL), F32),
        compiler_params=pltpu.CompilerParams(
            dimension_semantics=("arbitrary",), vmem_limit_bytes=VMEM_LIMIT),
        name="combine",
    )(*tables, h1, meta, y)


def _prepare_weights(g_mix, w_in, conv_w, conv_b, dt_bias, a_log, d_skip, g_ssd_out, g_q, g_k,
                     attn_sinks, g_att_out, w_out, g_ffn, w_route_group, b_route_group,
                     w_route_expert, b_route_expert, w_gate, w_up, w_down):
    cuts = np.cumsum([0, SSD_DIM, CONV_DIM, SSD_HEADS, ATT_DIM, KV_DIM, KV_DIM])
    wi = w_in.astype(BF16)
    seg = lambda i: wi[:, cuts[i]:cuts[i + 1]]
    pad_lanes = lambda a: jnp.pad(a, ((0, 0), (0, LANES - a.shape[1])))
    bd = np.kron(np.eye(KV_DIM // HEAD_DIM, dtype=np.float32), np.ones((HEAD_DIM, HEAD_DIM), np.float32))
    n_route = N_EXPERTS + N_EXPERT_GROUPS
    return {
        "g_mix": g_mix.reshape(1, D_MODEL),
        "w_z": seg(0), "w_xbc": seg(1), "w_dt": pad_lanes(jnp.tile(seg(2), (1, DT_COPIES))),
        "w_q": seg(3), "w_k": seg(4), "w_v": seg(5),
        "g_q": jnp.tile(g_q, KV_DIM // HEAD_DIM).reshape(1, KV_DIM),
        "g_k": jnp.tile(g_k, KV_DIM // HEAD_DIM).reshape(1, KV_DIM),
        "bd": jnp.asarray(bd, BF16),
        "conv_w": conv_w, "conv_b": conv_b.reshape(1, CONV_DIM),
        "dt_bias": pad_lanes(jnp.tile(dt_bias, DT_COPIES).reshape(1, DT_COPIES * SSD_HEADS)),
        "a_log": pad_lanes(jnp.tile(a_log, DT_COPIES).reshape(1, DT_COPIES * SSD_HEADS)),
        "d_skip": jnp.repeat(d_skip, HEAD_DIM).reshape(1, SSD_DIM),
        "g_ssd_out": g_ssd_out.reshape(1, SSD_DIM),
        "attn_sinks": attn_sinks,
        "g_att_out": g_att_out.reshape(1, ATT_DIM),
        "w_out": w_out.astype(BF16),
        "g_ffn": g_ffn.reshape(1, D_MODEL),
        "w_route": pad_lanes(jnp.concatenate([w_route_expert, w_route_group], axis=1)).astype(BF16),
        "b_route": pad_lanes(jnp.concatenate([b_route_expert, b_route_group]).reshape(1, n_route)),
        "w_gate": w_gate, "w_up": w_up, "w_down": w_down,
    }


def _segment(x3, w, mk, mv, tail, h0t, kprev, vprev, L, tb_ssd, tb_att, tm, tm_moe, from_cache):
    S, Ls, _ = x3.shape
    x = x3.reshape(S * Ls, D_MODEL)
    z, xbc, q, k, v, dt = _project(x, w, tm)
    r3 = lambda a: a.reshape(S, Ls, a.shape[-1])
    xbc3, k3, v3 = r3(xbc), r3(k), r3(v)
    y_ssd, h_new = _ssd(xbc3, r3(z), r3(dt), tail, h0t, w, L, tb_ssd)
    if not from_cache:
        kprev, vprev = k3, v3
    y_att = _attention(r3(q), k3, v3, kprev, vprev, mk, mv, w, L, tb_att, from_cache)
    h1, xn, meta, cnt = _merge(x, y_ssd.reshape(S * Ls, SSD_DIM), y_att.reshape(S * Ls, ATT_DIM), w, tm)
    y = _moe(h1, xn, meta, cnt, w, tm_moe)
    return y.reshape(S, Ls, D_MODEL), xbc3, h_new, k3, v3


def _state_to_kernel(h):
    return jnp.transpose(h.reshape(h.shape[0], SSD_DIM, D_STATE), (0, 2, 1))


def _state_from_kernel(ht):
    return jnp.transpose(ht, (0, 2, 1)).reshape(ht.shape[0], SSD_HEADS, HEAD_DIM, D_STATE)


def kernel(x_prompt, x_sample, cache_conv, state_ssd, cache_k, cache_v, meta_tokens, g_mix, w_in, conv_w, conv_b, dt_bias, a_log, d_skip, g_ssd_out, g_q, g_k, attn_sinks, g_att_out, w_out, g_ffn, w_route_group, b_route_group, w_route_expert, b_route_expert, w_gate, w_up, w_down):
    w = _prepare_weights(g_mix[0], w_in[0], conv_w[0], conv_b[0], dt_bias[0], a_log[0], d_skip[0],
                         g_ssd_out[0], g_q[0], g_k[0], attn_sinks[0], g_att_out[0], w_out[0], g_ffn[0],
                         w_route_group[0], b_route_group[0], w_route_expert[0], b_route_expert[0],
                         w_gate[0], w_up[0], w_down[0])
    n_b = x_sample.shape[0]
    n_dec = x_sample.shape[1]

    _, m_xbc, _, mk, mv, m_dt = _project(meta_tokens, w, N_META)
    zero_tail = jnp.zeros((1, CONV_WIDTH - 1, CONV_DIM), F32)
    zero_state = jnp.zeros((1, D_STATE, SSD_DIM), F32)
    m_xbc3 = m_xbc.reshape(1, N_META, CONV_DIM)
    _, m_state = _ssd(m_xbc3, jnp.zeros((1, N_META, SSD_DIM), F32), m_dt.reshape(1, N_META, LANES),
                      zero_tail, zero_state, w, N_META, N_META)
    m_tail = m_xbc3[:, N_META - (CONV_WIDTH - 1):]

    yp, xbc_p, st_p, k_p, v_p = _segment(
        x_prompt, w, mk, mv, m_tail, m_state, None, None,
        L=CHUNK, tb_ssd=256, tb_att=256, tm=512, tm_moe=256, from_cache=False)
    ys, xbc_s, st_s, k_s, v_s = _segment(
        x_sample, w, mk, mv, cache_conv[0], _state_to_kernel(state_ssd[0]),
        cache_k[0].reshape(n_b, WINDOW, KV_DIM), cache_v[0].reshape(n_b, WINDOW, KV_DIM),
        L=n_dec, tb_ssd=n_dec, tb_att=n_dec, tm=256, tm_moe=128, from_cache=True)

    bp = x_prompt.shape[0]
    heads = lambda a, rows: a.reshape(a.shape[0], rows, ATT_KV_HEADS, HEAD_DIM)[None]
    return (yp, ys,
            xbc_p[:, -(CONV_WIDTH - 1):][None],
            _state_from_kernel(st_p)[None],
            heads(k_p[:, -WINDOW:], WINDOW), heads(v_p[:, -WINDOW:], WINDOW),
            xbc_s[:, -(CONV_WIDTH - 1):][None],
            _state_from_kernel(st_s)[None],
            heads(k_s, n_dec), heads(v_s, n_dec))
```

```python
import functools
import math

import numpy as np
import jax
import jax.numpy as jnp
from jax import lax
from jax.experimental import pallas as pl
from jax.experimental.pallas import tpu as pltpu

D_MODEL = 1024
CHUNK = 64
N_META = 16
HEAD_DIM = 64
ATT_HEADS = 16
ATT_KV_HEADS = 4
ATT_REP = ATT_HEADS // ATT_KV_HEADS
ATT_DIM = ATT_HEADS * HEAD_DIM
KV_DIM = ATT_KV_HEADS * HEAD_DIM
WINDOW = 128
SSD_HEADS = 16
SSD_DIM = SSD_HEADS * HEAD_DIM
SSD_GROUPS = 2
GROUP_DIM = SSD_DIM // SSD_GROUPS
D_STATE = 128
CONV_WIDTH = 4
CONV_DIM = SSD_DIM + 2 * SSD_GROUPS * D_STATE
N_EXPERT_GROUPS = 4
EXPERTS_PER_GROUP = 8
N_EXPERTS = N_EXPERT_GROUPS * EXPERTS_PER_GROUP
EXPERT_FF = D_MODEL // 4
EPS = 1e-6

LANES = 128
SUBLANES = 8
MXU_DIM = 256
DT_COPIES = 3
KEY_SPAN = 256
VMEM_LIMIT = 48 * 1024 * 1024

F32 = jnp.float32
BF16 = jnp.bfloat16
NEG_INF = float("-inf")


def _dot(a, b):
    return jnp.dot(a, b, preferred_element_type=F32)


def _dot_nt(a, b):
    return lax.dot_general(a, b, (((1,), (1,)), ((), ())), preferred_element_type=F32)


def _dot_tn(a, b):
    return lax.dot_general(a, b, (((0,), (0,)), ((), ())), preferred_element_type=F32)


def _split3(x):
    hi = x.astype(BF16)
    r = x - hi.astype(F32)
    mid = r.astype(BF16)
    lo = (r - mid.astype(F32)).astype(BF16)
    return hi, mid, lo


def _pack3(x):
    hi, mid, lo = _split3(x)
    lane = lax.broadcasted_iota(jnp.int32, x.shape, 1)
    packed = jnp.where(lane < SSD_HEADS, hi.astype(F32),
                       jnp.where(lane < 2 * SSD_HEADS, mid.astype(F32), lo.astype(F32)))
    return packed.astype(BF16)


def _sel_dot(sel, x):
    hi, mid, lo = _split3(x)
    return _dot(sel, hi) + _dot(sel, mid) + _dot(sel, lo)


def _silu(x):
    return x * (1.0 / (1.0 + jnp.exp2(x * (-math.log2(math.e)))))


def _rms(x, g):
    ms = jnp.mean(x * x, axis=-1, keepdims=True)
    return x * lax.rsqrt(ms + EPS) * g


def _const_spec(shape):
    n = len(shape)
    return pl.BlockSpec(shape, lambda *_: (0,) * n, pipeline_mode=pl.Buffered(1))


def _proj_kernel(x_ref, gmix_ref, wqk_ref, wrest_ref, gqk_ref, bd_ref,
                 z_ref, xbc_ref, q_ref, k_ref, v_ref, dt_ref):
    tm = x_ref.shape[0]
    xn = _rms(x_ref[...], gmix_ref[...]).astype(BF16)
    qk = _dot(xn, wqk_ref[...])
    n_slices = (ATT_DIM + KV_DIM) // KV_DIM
    sq = jnp.concatenate([qk[:, j * KV_DIM:(j + 1) * KV_DIM] for j in range(n_slices)], axis=0)
    sq = sq * sq
    hi = sq.astype(BF16)
    lo = (sq - hi.astype(F32)).astype(BF16)
    ms = _dot(jnp.concatenate([hi, lo], axis=1), bd_ref[...]) * (1.0 / HEAD_DIM)
    inv = lax.rsqrt(ms + EPS)
    for j in range(n_slices):
        sl = slice(j * KV_DIM, (j + 1) * KV_DIM)
        normed = qk[:, sl] * inv[j * tm:(j + 1) * tm, :] * gqk_ref[:, sl]
        if j < ATT_DIM // KV_DIM:
            q_ref[:, sl] = (normed * HEAD_DIM ** -0.5).astype(BF16)
        else:
            k_ref[...] = normed
    rest = _dot(xn, wrest_ref[...])
    cuts = np.cumsum([0, SSD_DIM, CONV_DIM, KV_DIM, LANES])
    for ref, lo_c, hi_c in zip((z_ref, xbc_ref, v_ref, dt_ref), cuts[:-1], cuts[1:]):
        ref[...] = rest[:, lo_c:hi_c]


def _project(x, w, tm):
    t = x.shape[0]
    assert t % tm == 0
    row = lambda n: pl.BlockSpec((tm, n), lambda i: (i, 0))
    ins = [x, w["g_mix"], w["w_qk"], w["w_rest"], w["g_qk"], w["bd"]]
    in_specs = [row(D_MODEL)] + [_const_spec(a.shape) for a in ins[1:]]
    out_dims = (SSD_DIM, CONV_DIM, ATT_DIM, KV_DIM, KV_DIM, LANES)
    out_dtypes = (F32, F32, BF16, F32, F32, F32)
    return pl.pallas_call(
        _proj_kernel,
        grid=(t // tm,),
        in_specs=in_specs,
        out_specs=[row(n) for n in out_dims],
        out_shape=[jax.ShapeDtypeStruct((t, n), d) for n, d in zip(out_dims, out_dtypes)],
        compiler_params=pltpu.CompilerParams(
            dimension_semantics=("arbitrary",), vmem_limit_bytes=VMEM_LIMIT),
        name="projection",
    )(*ins)


def _ssd_kernel(xbc_ref, z_ref, dt_ref, tail_ref, h0_ref, convw_ref, convb_ref, dtb_ref,
                alog_ref, dskip_ref, gout_ref, ehead_ref, epos_ref, tri_ref,
                y_ref, hout_ref, buf_ref, act_ref, dts_ref, ys_ref, st_ref, *, L, TB):
    hp = LANES // L
    n_lane_tiles = SSD_HEADS // hp
    hq = MXU_DIM // L
    n_tiles = SSD_HEADS // hq
    tile_w = hq * HEAD_DIM
    b = pl.program_id(1)
    pad = SUBLANES - (CONV_WIDTH - 1)

    @pl.when(b == 0)
    def _():
        buf_ref[pad:SUBLANES, :] = tail_ref[...]
        st_ref[...] = h0_ref[...].T

    buf_ref[SUBLANES:SUBLANES + TB, :] = xbc_ref[...]
    acc = convb_ref[...] + convw_ref[0:1, :] * buf_ref[pad:pad + TB, :]
    for j in range(1, CONV_WIDTH):
        acc = acc + convw_ref[j:j + 1, :] * buf_ref[pad + j:pad + j + TB, :]
    act_ref[...] = _silu(acc)
    buf_ref[pad:SUBLANES, :] = buf_ref[TB + pad:TB + SUBLANES, :]

    dtx = dt_ref[...] + dtb_ref[...]
    dts_ref[...] = jnp.maximum(dtx, 0.0) + jnp.log(1.0 + jnp.exp(-jnp.abs(dtx)))

    a_row = -jnp.exp(alog_ref[...])
    ehead = ehead_ref[...]
    epos = epos_ref[...]
    tri = tri_ref[...]
    dskip = dskip_ref[...]
    row_i = lax.broadcasted_iota(jnp.int32, (L, n_lane_tiles * LANES), 0)
    col_i = lax.broadcasted_iota(jnp.int32, (L, n_lane_tiles * LANES), 1)
    pos_i = col_i % L
    diag_mask = row_i == pos_i
    causal_mask = row_i >= pos_i
    colhead = lax.broadcasted_iota(jnp.int32, (L, tile_w), 1) // HEAD_DIM

    def chunk(c):
        r0 = pl.multiple_of(c * L, L)
        dtc = dts_ref[pl.ds(r0, L), :]
        cs = _sel_dot(tri, dtc * a_row)
        cs16 = _pack3(cs)
        dt_b = _dot(_pack3(dtc), ehead)
        cs_b = _dot(cs16, ehead)
        cs_last_b = cs_b[L - 1:L, :]
        ecs_b = jnp.exp(cs_b)
        dec_end_b = jnp.exp(cs_last_b - cs_b)
        bdec_b = jnp.exp(cs_last_b)
        cs_col = cs_b if L == HEAD_DIM else _dot(cs16, epos)
        cs_row = jnp.sum(jnp.where(diag_mask, cs_col, 0.0), axis=0, keepdims=True)
        lmat = jnp.exp(jnp.where(causal_mask, cs_col - cs_row, NEG_INF))

        xs = act_ref[pl.ds(r0, L), 0:SSD_DIM]
        xdt_f = xs * dt_b
        xdt = xdt_f.astype(BF16)
        wx = (xdt_f * dec_end_b).astype(BF16)
        st = st_ref[...]
        st16 = st.astype(BF16)

        y_diag = []
        y_off = []
        s_new = []
        cb = [None] * SSD_GROUPS
        for g in range(SSD_GROUPS):
            bg = act_ref[pl.ds(r0, L), SSD_DIM + g * D_STATE:SSD_DIM + (g + 1) * D_STATE].astype(BF16)
            cg = act_ref[pl.ds(r0, L), SSD_DIM + (SSD_GROUPS + g) * D_STATE:
                         SSD_DIM + (SSD_GROUPS + g + 1) * D_STATE].astype(BF16)
            cb[g] = _dot_nt(cg, jnp.concatenate([bg] * hp, axis=0))
            gsl = slice(g * GROUP_DIM, (g + 1) * GROUP_DIM)
            y_off.append(_dot(cg, st16[:, gsl]))
            s_new.append(_dot_tn(bg, wx[:, gsl]))
        lane_tiles = MXU_DIM // LANES
        for t in range(n_tiles):
            cbt = jnp.concatenate(
                [cb[((t * lane_tiles + j) * hp * HEAD_DIM) // GROUP_DIM] for j in range(lane_tiles)], axis=1)
            gmat = (lmat[:, t * MXU_DIM:(t + 1) * MXU_DIM] * cbt).astype(BF16)
            xt = xdt[:, t * tile_w:(t + 1) * tile_w]
            rhs = jnp.concatenate(
                [jnp.where(colhead == hh, xt, jnp.zeros_like(xt)) for hh in range(hq)], axis=0)
            y_diag.append(_dot(gmat, rhs))
        y = (jnp.concatenate(y_diag, axis=1) + jnp.concatenate(y_off, axis=1) * ecs_b
             + dskip * xs)
        ys_ref[pl.ds(r0, L), :] = y
        st_ref[...] = bdec_b * st + jnp.concatenate(s_new, axis=1)

    n_chunks = TB // L
    unroll = 4 if n_chunks % 4 == 0 else 1

    def chunks(i, carry):
        for u in range(unroll):
            chunk(i * unroll + u)
        return carry

    lax.fori_loop(0, n_chunks // unroll, chunks, 0)

    yg = ys_ref[...] * _silu(z_ref[...])
    y_ref[...] = _rms(yg, gout_ref[...])

    @pl.when(b == pl.num_programs(1) - 1)
    def _():
        hout_ref[...] = st_ref[...].T


def _ssd(xbc, z, dt, tail, h0t, w, L, TB):
    S, Ls, _ = xbc.shape
    assert Ls % TB == 0 and TB % L == 0 and LANES % L == 0
    nb = Ls // TB
    hp = LANES // L
    n_tiles = SSD_HEADS // hp
    per_stream = lambda a: (lambda s, b: (s, 0, 0)) if a.shape[0] == S and S > 1 else (lambda s, b: (0, 0, 0))
    seq = lambda n: pl.BlockSpec((None, TB, n), lambda s, b: (s, b, 0))
    ehead = np.zeros((LANES, SSD_DIM), np.float32)
    epos = np.zeros((LANES, n_tiles * LANES), np.float32)
    for part in range(DT_COPIES):
        for h in range(SSD_HEADS):
            ehead[part * SSD_HEADS + h, h * HEAD_DIM:(h + 1) * HEAD_DIM] = 1.0
            epos[part * SSD_HEADS + h, h * L:(h + 1) * L] = 1.0
    tri = np.tril(np.ones((L, L), np.float32))
    consts = [w["conv_w"], w["conv_b"], w["dt_bias"], w["a_log"], w["d_skip"], w["g_ssd_out"],
              jnp.asarray(ehead, BF16), jnp.asarray(epos, BF16), jnp.asarray(tri, BF16)]
    in_specs = [seq(CONV_DIM), seq(SSD_DIM), seq(LANES),
                pl.BlockSpec((None, CONV_WIDTH - 1, CONV_DIM), per_stream(tail)),
                pl.BlockSpec((None, SSD_DIM, D_STATE), per_stream(h0t))]
    in_specs += [_const_spec(c.shape) for c in consts]
    return pl.pallas_call(
        functools.partial(_ssd_kernel, L=L, TB=TB),
        grid=(S, nb),
        in_specs=in_specs,
        out_specs=[seq(SSD_DIM), pl.BlockSpec((None, SSD_DIM, D_STATE), lambda s, b: (s, 0, 0))],
        out_shape=[jax.ShapeDtypeStruct((S, Ls, SSD_DIM), F32),
                   jax.ShapeDtypeStruct((S, SSD_DIM, D_STATE), F32)],
        scratch_shapes=[pltpu.VMEM((SUBLANES + TB, CONV_DIM), F32),
                        pltpu.VMEM((TB, CONV_DIM), F32),
                        pltpu.VMEM((TB, LANES), F32),
                        pltpu.VMEM((TB, SSD_DIM), F32),
                        pltpu.VMEM((D_STATE, SSD_DIM), F32)],
        compiler_params=pltpu.CompilerParams(
            dimension_semantics=("arbitrary", "arbitrary"), vmem_limit_bytes=VMEM_LIMIT),
        name="ssd",
    )(xbc, z, dt, tail, h0t, *consts)


def _split_heads(x):
    lane = lax.broadcasted_iota(jnp.int32, (x.shape[0], LANES), 1)
    low = lane < HEAD_DIM
    lo, hi = [], []
    for t in range(KV_DIM // LANES):
        tile = x[:, t * LANES:(t + 1) * LANES]
        swapped = pltpu.roll(tile, HEAD_DIM, axis=1)
        lo += [jnp.where(low, tile, 0.0), jnp.where(low, swapped, 0.0)]
        hi += [jnp.where(low, 0.0, swapped), jnp.where(low, 0.0, tile)]
    return jnp.concatenate(lo, axis=1).astype(BF16), jnp.concatenate(hi, axis=1).astype(BF16)


def _attn_kernel(sink_ref, q_ref, k_ref, v_ref, kp_ref, vp_ref, mk_ref, mv_ref, bias_ref, g_ref,
                 o_ref, k_lo, k_hi, v_lo, v_hi, ybuf, s_buf, e_buf, sink_buf, *, L, TB, from_cache):
    b = pl.program_id(1)
    bufs = (k_lo, k_hi, v_lo, v_hi)
    if from_cache:
        window = _split_heads(kp_ref[...]) + _split_heads(vp_ref[...])
        for buf, val in zip(bufs, window):
            buf[0:WINDOW, :] = val
    else:
        @pl.when(b == 0)
        def _():
            for buf in bufs:
                buf[0:WINDOW, :] = jnp.zeros((WINDOW, buf.shape[1]), BF16)

        @pl.when(b > 0)
        def _():
            for buf in bufs:
                buf[0:WINDOW, :] = buf[TB:TB + WINDOW, :]

    for buf, val in zip(bufs, _split_heads(k_ref[...]) + _split_heads(v_ref[...])):
        buf[WINDOW:WINDOW + TB, :] = val
    n_pad = KEY_SPAN - WINDOW - L - N_META
    pad_rows = jnp.zeros((n_pad, 2 * KV_DIM), BF16)
    tails = [jnp.concatenate([t, pad_rows], axis=0)
             for t in _split_heads(mk_ref[...]) + _split_heads(mv_ref[...])]
    lane = lax.broadcasted_iota(jnp.int32, (2 * L, 2 * KEY_SPAN), 1) % KEY_SPAN
    lane_h = lax.broadcasted_iota(jnp.int32, (L, LANES), 1)
    lane_v = lax.broadcasted_iota(jnp.int32, (KEY_SPAN, LANES), 1)
    ones_lo = jnp.where(lane_v < HEAD_DIM, 1.0, 0.0).astype(BF16)
    ones_hi = jnp.where(lane_v < HEAD_DIM, 0.0, 1.0).astype(BF16)
    n_chunks = TB // L
    pairs_per_group = ATT_REP // 2

    def chunk_keys(c):
        r0 = c * L
        return [jnp.concatenate([buf[r0:r0 + WINDOW, :], buf[r0 + WINDOW:r0 + WINDOW + L, :], tail], axis=0)
                for buf, tail in zip(bufs, tails)]

    for c in range(n_chunks):
        r0 = c * L
        keys = chunk_keys(c)
        for g in range(ATT_KV_HEADS):
            gl = slice(g * LANES, (g + 1) * LANES)
            q4 = jnp.concatenate([q_ref[r0:r0 + L, p * LANES:(p + 1) * LANES]
                                  for p in range(pairs_per_group * g, pairs_per_group * (g + 1))], axis=0)
            s = _dot_nt(q4, jnp.concatenate([keys[0][:, gl], keys[1][:, gl]], axis=0)) + bias_ref[g]
            if not from_cache and r0 < WINDOW:
                n_invalid = WINDOW - (b * TB + r0)
                s = jnp.where(lane < n_invalid, NEG_INF, s)
            s_buf[c * ATT_KV_HEADS + g] = s

    for c in range(n_chunks):
        for pair in range(ATT_HEADS // 2):
            slot = c * ATT_KV_HEADS + pair // pairs_per_group
            rows = slice((pair % pairs_per_group) * L, (pair % pairs_per_group + 1) * L)
            sink_terms = []
            for half in range(2):
                cols = slice(half * KEY_SPAN, (half + 1) * KEY_SPAN)
                s = s_buf[slot, rows, cols]
                sink = sink_ref[2 * pair + half]
                m = jnp.maximum(jnp.max(s, axis=-1, keepdims=True), sink)
                e_buf[slot, rows, cols] = jnp.exp(s - m).astype(BF16)
                sink_terms.append(jnp.exp(sink - m))
            sink_buf[c * (ATT_HEADS // 2) + pair] = jnp.where(lane_h < HEAD_DIM, sink_terms[0], sink_terms[1])

    for c in range(n_chunks):
        r0 = c * L
        keys = chunk_keys(c)
        for g in range(ATT_KV_HEADS):
            gl = slice(g * LANES, (g + 1) * LANES)
            vcat = jnp.concatenate([jnp.concatenate([keys[2][:, gl], ones_lo], axis=1),
                                    jnp.concatenate([keys[3][:, gl], ones_hi], axis=1)], axis=0)
            ov = _dot(e_buf[c * ATT_KV_HEADS + g], vcat)
            for j in range(pairs_per_group):
                pair = pairs_per_group * g + j
                den = ov[j * L:(j + 1) * L, LANES:2 * LANES] + sink_buf[c * (ATT_HEADS // 2) + pair]
                ybuf[r0:r0 + L, pair * LANES:(pair + 1) * LANES] = ov[j * L:(j + 1) * L, 0:LANES] * (1.0 / den)
    o_ref[...] = _rms(ybuf[...], g_ref[...])


def _attention(q, k, v, kprev, vprev, mk, mv, w, L, TB, from_cache):
    S, Ls, _ = q.shape
    assert Ls % TB == 0 and TB % L == 0 and (from_cache or TB >= WINDOW)
    nb = Ls // TB
    seq = lambda n: pl.BlockSpec((None, TB, n), lambda s, b: (s, b, 0))
    if not from_cache:
        kprev = vprev = jnp.zeros((1, WINDOW, KV_DIM), F32)
    prev = pl.BlockSpec((None, WINDOW, KV_DIM),
                        (lambda s, b: (s, 0, 0)) if from_cache else (lambda s, b: (0, 0, 0)))
    slopes = 2.0 ** (-8.0 * np.arange(1, ATT_HEADS + 1, dtype=np.float64) / ATT_HEADS)
    dist = np.abs(WINDOW + np.arange(L)[:, None] - np.arange(WINDOW + L)[None, :])
    bias = np.full((ATT_HEADS, L, KEY_SPAN), NEG_INF, np.float32)
    bias[:, :, :WINDOW + L] = -slopes[:, None, None] * dist[None]
    bias[:, :, WINDOW + L:WINDOW + L + N_META] = 0.0
    bias = bias.reshape(ATT_KV_HEADS, ATT_REP // 2, 2, L, KEY_SPAN).transpose(0, 1, 3, 2, 4)
    bias = jnp.asarray(bias.reshape(ATT_KV_HEADS, ATT_REP // 2 * L, 2 * KEY_SPAN))
    n_chunks = TB // L
    in_specs = [pl.BlockSpec(memory_space=pltpu.SMEM),
                seq(ATT_DIM), seq(KV_DIM), seq(KV_DIM), prev, prev,
                _const_spec(mk.shape), _const_spec(mv.shape), _const_spec(bias.shape),
                _const_spec(w["g_att_out"].shape)]
    return pl.pallas_call(
        functools.partial(_attn_kernel, L=L, TB=TB, from_cache=from_cache),
        grid=(S, nb),
        in_specs=in_specs,
        out_specs=seq(ATT_DIM),
        out_shape=jax.ShapeDtypeStruct((S, Ls, ATT_DIM), F32),
        scratch_shapes=[pltpu.VMEM((WINDOW + TB, 2 * KV_DIM), BF16) for _ in range(4)]
                       + [pltpu.VMEM((TB, ATT_DIM), F32),
                          pltpu.VMEM((n_chunks * ATT_KV_HEADS, 2 * L, 2 * KEY_SPAN), F32),
                          pltpu.VMEM((n_chunks * ATT_KV_HEADS, 2 * L, 2 * KEY_SPAN), BF16),
                          pltpu.VMEM((n_chunks * ATT_HEADS // 2, L, LANES), F32)],
        compiler_params=pltpu.CompilerParams(
            dimension_semantics=("arbitrary", "arbitrary"), vmem_limit_bytes=VMEM_LIMIT),
        name="attention",
    )(w["attn_sinks"], q, k, v, kprev, vprev, mk, mv, bias, w["g_att_out"])


TOK_TILE = 256
SEG_ALIGN = SUBLANES
SORT_ROWS = 2 * TOK_TILE + N_EXPERTS * SEG_ALIGN
XS_COLS = D_MODEL + LANES
META_P1, META_P2, META_G1, META_G2 = N_EXPERTS, N_EXPERTS + 1, N_EXPERTS + 2, N_EXPERTS + 3


def _merge_kernel(h_ref, ys_ref, ya_ref, wo_ref, gffn_ref, wr_ref, br_ref, tri_ref, upper_ref,
                  h1_ref, xn_ref, meta_ref, cnt_ref):
    ycat = jnp.concatenate([ys_ref[...].astype(BF16), ya_ref[...].astype(BF16)], axis=1)
    h1 = h_ref[...] + _dot(ycat, wo_ref[...])
    h1_ref[...] = h1
    xn = _rms(h1, gffn_ref[...]).astype(BF16)
    xn_ref[...] = xn
    all_logits = _dot(xn, wr_ref[...]) + br_ref[...]
    tri = tri_ref[...]
    upper = upper_ref[...]
    lane = lax.broadcasted_iota(jnp.int32, (TOK_TILE, LANES), 1)
    big = jnp.int32(LANES)
    gmask = (lane >= N_EXPERTS) & (lane < N_EXPERTS + N_EXPERT_GROUPS)
    for s in range(all_logits.shape[0] // TOK_TILE):
        rows = slice(s * TOK_TILE, (s + 1) * TOK_TILE)
        logits = all_logits[rows]

        def top1(mask):
            mval = jnp.max(jnp.where(mask, logits, NEG_INF), axis=-1, keepdims=True)
            idx = jnp.min(jnp.where(mask & (logits == mval), lane, big), axis=-1, keepdims=True)
            return mval, idx

        gmax, gidx = top1(gmask)
        gate_g = 1.0 / jnp.sum(jnp.where(gmask, jnp.exp(logits - gmax), 0.0), axis=-1, keepdims=True)
        grp = gidx - N_EXPERTS
        emask = (lane // EXPERTS_PER_GROUP) == grp
        v1, i1 = top1(emask)
        v2, i2 = top1(emask & (lane != i1))
        e2 = jnp.exp(v2 - v1)
        g1 = gate_g / (1.0 + e2)
        g2 = gate_g * e2 / (1.0 + e2)
        oh1 = jnp.where(lane == i1, 1.0, 0.0)
        oh2 = jnp.where(lane == i2, 1.0, 0.0)
        oh = oh1 + oh2
        earlier = _dot(tri, oh.astype(BF16))
        cnt = jnp.sum(oh, axis=0, keepdims=True)
        units = jnp.floor((cnt + (SEG_ALIGN - 1)) * (1.0 / SEG_ALIGN))
        units = jnp.broadcast_to(units, (2 * SUBLANES, LANES)).astype(BF16)
        slot = _dot(units, upper)[0:1, :] * SEG_ALIGN + earlier
        p1 = jnp.sum(oh1 * slot, axis=-1, keepdims=True)
        p2 = jnp.sum(oh2 * slot, axis=-1, keepdims=True)
        meta = jnp.where(lane == META_P1, p1, 0.0)
        meta = jnp.where(lane == META_P2, p2, meta)
        meta = jnp.where(lane == META_G1, g1, meta)
        meta = jnp.where(lane == META_G2, g2, meta)
        meta_ref[rows, :] = meta
        cnt_ref[s] = jnp.broadcast_to(cnt, (SUBLANES, LANES))


def _merge(h, y_ssd, y_att, w, tm):
    t = h.shape[0]
    assert t % tm == 0 and tm % TOK_TILE == 0
    sub = tm // TOK_TILE
    row = lambda n: pl.BlockSpec((tm, n), lambda i: (i, 0))
    tri = jnp.asarray(np.tril(np.ones((TOK_TILE, TOK_TILE), np.float32), -1), BF16)
    upper = jnp.asarray(np.triu(np.ones((LANES, LANES), np.float32), 1), BF16)
    consts = [w["w_out"], w["g_ffn"], w["w_route"], w["b_route"], tri, upper]
    return pl.pallas_call(
        _merge_kernel,
        grid=(t // tm,),
        in_specs=[row(D_MODEL), row(SSD_DIM), row(ATT_DIM)] + [_const_spec(c.shape) for c in consts],
        out_specs=[row(D_MODEL), row(D_MODEL), row(LANES),
                   pl.BlockSpec((sub, SUBLANES, LANES), lambda i: (i, 0, 0))],
        out_shape=[jax.ShapeDtypeStruct((t, D_MODEL), F32),
                   jax.ShapeDtypeStruct((t, D_MODEL), BF16),
                   jax.ShapeDtypeStruct((t, LANES), F32),
                   jax.ShapeDtypeStruct((t // TOK_TILE, SUBLANES, LANES), F32)],
        compiler_params=pltpu.CompilerParams(
            dimension_semantics=("arbitrary",), vmem_limit_bytes=VMEM_LIMIT),
        name="merge_route",
    )(h, y_ssd, y_att, *consts)


def _meta_col(meta, j):
    lane = lax.broadcasted_iota(jnp.int32, meta.shape, 1)
    return jnp.sum(jnp.where(lane == j, meta, 0.0), axis=-1, keepdims=True)


def _pair_selectors(meta):
    rows = lax.broadcasted_iota(jnp.int32, (meta.shape[0], SORT_ROWS), 1)
    s1 = jnp.where(rows == _meta_col(meta, META_P1).astype(jnp.int32), 1.0, 0.0).astype(BF16)
    s2 = jnp.where(rows == _meta_col(meta, META_P2).astype(jnp.int32), 1.0, 0.0).astype(BF16)
    return s1, s2


def _segment_dmas(off_ref, cnt_ref, row_ref, b, tile_buf, hbm, sem, to_hbm, wait):
    def body(e, carry):
        k = b * N_EXPERTS + e
        n = pl.multiple_of(cnt_ref[k], SEG_ALIGN)

        @pl.when(n > 0)
        def _():
            v = tile_buf.at[pl.ds(pl.multiple_of(off_ref[k], SEG_ALIGN), n)]
            h = hbm.at[pl.ds(pl.multiple_of(row_ref[k], SEG_ALIGN), n)]
            cp = pltpu.make_async_copy(v, h, sem) if to_hbm else pltpu.make_async_copy(h, v, sem)
            if wait:
                cp.wait()
            else:
                cp.start()
        return carry

    lax.fori_loop(0, N_EXPERTS, body, 0)


def _gate_lanes(g):
    hi, mid, lo = _split3(g)
    lane = lax.broadcasted_iota(jnp.int32, (g.shape[0], LANES), 1)
    out = jnp.where(lane == 0, hi.astype(F32), 0.0)
    out = jnp.where(lane == 1, mid.astype(F32), out)
    out = jnp.where(lane == 2, lo.astype(F32), out)
    return out.astype(BF16)


def _dispatch_kernel(off_ref, cnt_ref, row_ref, tail_row_ref, tail_cnt_ref, xn_ref, meta_ref, *rest,
                     tile0, first):
    xs_ref, sort_buf, zero_buf, sem, tail_sem = rest if first else rest[1:]
    b = pl.program_id(0)
    nb = pl.num_programs(0)
    slot = b % 2
    seg = functools.partial(_segment_dmas, off_ref, cnt_ref, row_ref, hbm=xs_ref, to_hbm=True)

    def tails(wait):
        def body(e, carry):
            n = pl.multiple_of(tail_cnt_ref[e], SEG_ALIGN)

            @pl.when(n > 0)
            def _():
                cp = pltpu.make_async_copy(
                    zero_buf.at[pl.ds(0, n)],
                    xs_ref.at[pl.ds(pl.multiple_of(tail_row_ref[e], SEG_ALIGN), n)], tail_sem.at[0])
                if wait:
                    cp.wait()
                else:
                    cp.start()
            return carry

        lax.fori_loop(0, N_EXPERTS, body, 0)

    @pl.when(b >= 2)
    def _():
        seg(tile0 + b - 2, sort_buf.at[slot], sem=sem.at[slot], wait=True)

    if first:
        @pl.when(b == 0)
        def _():
            zero_buf[...] = jnp.zeros(zero_buf.shape, F32)
            tails(wait=False)

    meta = meta_ref[...]
    s1, s2 = _pair_selectors(meta)
    sort_buf[slot, :, 0:D_MODEL] = _dot_tn(s1 + s2, xn_ref[...])
    sort_buf[slot, :, D_MODEL:XS_COLS] = (_dot_tn(s1, _gate_lanes(_meta_col(meta, META_G1)))
                                          + _dot_tn(s2, _gate_lanes(_meta_col(meta, META_G2))))
    seg(tile0 + b, sort_buf.at[slot], sem=sem.at[slot], wait=False)

    @pl.when(b == nb - 1)
    def _():
        @pl.when(b >= 1)
        def _():
            seg(tile0 + b - 1, sort_buf.at[1 - slot], sem=sem.at[1 - slot], wait=True)

        seg(tile0 + b, sort_buf.at[slot], sem=sem.at[slot], wait=True)
        if first:
            tails(wait=True)


def _expert_kernel(tile_expert_ref, n_active_ref, xs_ref, wg_ref, wu_ref, wd_ref, y_ref):
    @pl.when(pl.program_id(0) < n_active_ref[0])
    def _():
        x = xs_ref[:, 0:D_MODEL].astype(BF16)
        gate = jnp.sum(xs_ref[:, D_MODEL:XS_COLS], axis=-1, keepdims=True)
        gu = _dot(x, jnp.concatenate([wg_ref[...].astype(BF16), wu_ref[...].astype(BF16)], axis=1))
        hid = _silu(gu[:, 0:EXPERT_FF]) * gu[:, EXPERT_FF:2 * EXPERT_FF]
        y_ref[...] = gate * _dot(hid.astype(BF16), wd_ref[...].astype(BF16))


def _combine_kernel(off_ref, cnt_ref, row_ref, h1_ref, meta_ref, y_ref, o_ref, y_buf, sem, *, tile0):
    b = pl.program_id(0)
    nb = pl.num_programs(0)
    slot = b % 2
    seg = functools.partial(_segment_dmas, off_ref, cnt_ref, row_ref, hbm=y_ref, to_hbm=False)

    @pl.when(b == 0)
    def _():
        y_buf[...] = jnp.zeros(y_buf.shape, F32)
        seg(tile0 + b, y_buf.at[slot], sem=sem.at[slot], wait=False)

    @pl.when(b + 1 < nb)
    def _():
        seg(tile0 + b + 1, y_buf.at[1 - slot], sem=sem.at[1 - slot], wait=False)

    seg(tile0 + b, y_buf.at[slot], sem=sem.at[slot], wait=True)
    s1, s2 = _pair_selectors(meta_ref[...])
    sel = s1 + s2
    y = y_buf[slot]
    hi = y.astype(BF16)
    lo = (y - hi.astype(F32)).astype(BF16)
    o_ref[...] = h1_ref[...] + (_dot(sel, hi) + _dot(sel, lo))


def _moe(parts, w, tm):
    i32 = jnp.int32
    n_tiles = [p[0].shape[0] // TOK_TILE for p in parts]
    nb = sum(n_tiles)
    t = nb * TOK_TILE
    cnt = jnp.concatenate([p[3][:, 0, :N_EXPERTS] for p in parts], axis=0).astype(i32)
    cnt_al = (cnt + (SEG_ALIGN - 1)) // SEG_ALIGN * SEG_ALIGN
    off = jnp.cumsum(cnt_al, axis=1) - cnt_al
    tot = jnp.sum(cnt_al, axis=0)
    tot_tm = (tot + (tm - 1)) // tm * tm
    start = jnp.cumsum(tot_tm) - tot_tm
    row = start[None, :] + jnp.cumsum(cnt_al, axis=0) - cnt_al
    tile_ends = jnp.cumsum(tot_tm // tm)
    max_rows = 2 * t + (SEG_ALIGN - 1) * min(N_EXPERTS * nb, 2 * t) + N_EXPERTS * (tm - SEG_ALIGN)
    nt = -(-max_rows // tm)
    tile_expert = jnp.sum((jnp.arange(nt, dtype=i32)[:, None] >= tile_ends[None, :]).astype(i32), axis=1)
    tile_expert = jnp.minimum(tile_expert, N_EXPERTS - 1)
    n_active = tile_ends[-1:].astype(i32)
    tables = [off.reshape(-1), cnt_al.reshape(-1), row.reshape(-1).astype(i32)]
    tail_tables = [(start + tot).astype(i32), (tot_tm - tot).astype(i32)]

    tok = lambda n: pl.BlockSpec((TOK_TILE, n), lambda i, *_: (i, 0))
    hbm = pl.BlockSpec(memory_space=pl.ANY)
    n_prefetch = len(tables) + len(tail_tables)
    xs = None
    tile0 = 0
    for (_, xn, meta, _), n in zip(parts, n_tiles):
        first = xs is None
        xs = pl.pallas_call(
            functools.partial(_dispatch_kernel, tile0=tile0, first=first),
            grid_spec=pltpu.PrefetchScalarGridSpec(
                num_scalar_prefetch=n_prefetch, grid=(n,),
                in_specs=[tok(D_MODEL), tok(LANES)] + ([] if first else [hbm]),
                out_specs=hbm,
                scratch_shapes=[pltpu.VMEM((2, SORT_ROWS, XS_COLS), F32),
                                pltpu.VMEM((tm, XS_COLS), F32),
                                pltpu.SemaphoreType.DMA((2,)),
                                pltpu.SemaphoreType.DMA((1,))]),
            out_shape=jax.ShapeDtypeStruct((nt * tm, XS_COLS), F32),
            input_output_aliases={} if first else {n_prefetch + 2: 0},
            compiler_params=pltpu.CompilerParams(
                dimension_semantics=("arbitrary",), vmem_limit_bytes=VMEM_LIMIT),
            name="dispatch",
        )(*tables, *tail_tables, xn, meta, *([] if first else [xs]))
        tile0 += n

    act = lambda i, te, na: jnp.minimum(i, na[0] - 1)
    y = pl.pallas_call(
        _expert_kernel,
        grid_spec=pltpu.PrefetchScalarGridSpec(
            num_scalar_prefetch=2, grid=(nt,),
            in_specs=[pl.BlockSpec((tm, XS_COLS), lambda i, te, na: (act(i, te, na), 0)),
                      pl.BlockSpec((None, D_MODEL, EXPERT_FF), lambda i, te, na: (te[act(i, te, na)], 0, 0)),
                      pl.BlockSpec((None, D_MODEL, EXPERT_FF), lambda i, te, na: (te[act(i, te, na)], 0, 0)),
                      pl.BlockSpec((None, EXPERT_FF, D_MODEL), lambda i, te, na: (te[act(i, te, na)], 0, 0))],
            out_specs=pl.BlockSpec((tm, D_MODEL), lambda i, te, na: (act(i, te, na), 0))),
        out_shape=jax.ShapeDtypeStruct((nt * tm, D_MODEL), F32),
        compiler_params=pltpu.CompilerParams(
            dimension_semantics=("arbitrary",), vmem_limit_bytes=VMEM_LIMIT),
        name="experts",
    )(tile_expert, n_active, xs, w["w_gate"], w["w_up"], w["w_down"])

    outs = []
    tile0 = 0
    for (h1, _, meta, _), n in zip(parts, n_tiles):
        outs.append(pl.pallas_call(
            functools.partial(_combine_kernel, tile0=tile0),
            grid_spec=pltpu.PrefetchScalarGridSpec(
                num_scalar_prefetch=len(tables), grid=(n,),
                in_specs=[tok(D_MODEL), tok(LANES), hbm],
                out_specs=tok(D_MODEL),
                scratch_shapes=[pltpu.VMEM((2, SORT_ROWS, D_MODEL), F32),
                                pltpu.SemaphoreType.DMA((2,))]),
            out_shape=jax.ShapeDtypeStruct((n * TOK_TILE, D_MODEL), F32),
            compiler_params=pltpu.CompilerParams(
                dimension_semantics=("arbitrary",), vmem_limit_bytes=VMEM_LIMIT),
            name="combine",
        )(*tables, h1, meta, y))
        tile0 += n
    return outs


def _prepare_weights(g_mix, w_in, conv_w, conv_b, dt_bias, a_log, d_skip, g_ssd_out, g_q, g_k,
                     attn_sinks, g_att_out, w_out, g_ffn, w_route_group, b_route_group,
                     w_route_expert, b_route_expert, w_gate, w_up, w_down):
    cuts = np.cumsum([0, SSD_DIM, CONV_DIM, SSD_HEADS, ATT_DIM, KV_DIM, KV_DIM])
    wi = w_in.astype(BF16)
    seg = lambda i: wi[:, cuts[i]:cuts[i + 1]]
    pad_lanes = lambda a: jnp.pad(a, ((0, 0), (0, LANES - a.shape[1])))
    bd = np.kron(np.eye(KV_DIM // HEAD_DIM, dtype=np.float32), np.ones((HEAD_DIM, HEAD_DIM), np.float32))
    n_route = N_EXPERTS + N_EXPERT_GROUPS
    return {
        "g_mix": g_mix.reshape(1, D_MODEL),
        "w_qk": jnp.concatenate([seg(3), seg(4)], axis=1),
        "w_rest": jnp.concatenate([seg(0), seg(1), seg(5), pad_lanes(jnp.tile(seg(2), (1, DT_COPIES)))], axis=1),
        "g_qk": jnp.concatenate([jnp.tile(g_q, ATT_HEADS), jnp.tile(g_k, ATT_KV_HEADS)]).reshape(1, ATT_DIM + KV_DIM),
        "bd": jnp.asarray(np.concatenate([bd, bd], axis=0), BF16),
        "conv_w": conv_w, "conv_b": conv_b.reshape(1, CONV_DIM),
        "dt_bias": pad_lanes(jnp.tile(dt_bias, DT_COPIES).reshape(1, DT_COPIES * SSD_HEADS)),
        "a_log": pad_lanes(jnp.tile(a_log, DT_COPIES).reshape(1, DT_COPIES * SSD_HEADS)),
        "d_skip": jnp.repeat(d_skip, HEAD_DIM).reshape(1, SSD_DIM),
        "g_ssd_out": g_ssd_out.reshape(1, SSD_DIM),
        "attn_sinks": attn_sinks,
        "g_att_out": g_att_out.reshape(1, ATT_DIM),
        "w_out": w_out.astype(BF16),
        "g_ffn": g_ffn.reshape(1, D_MODEL),
        "w_route": pad_lanes(jnp.concatenate([w_route_expert, w_route_group], axis=1)).astype(BF16),
        "b_route": pad_lanes(jnp.concatenate([b_route_expert, b_route_group]).reshape(1, n_route)),
        "w_gate": w_gate, "w_up": w_up, "w_down": w_down,
    }


def _segment(x3, w, mk, mv, tail, h0t, kprev, vprev, L, tb_ssd, tb_att, tm, from_cache):
    S, Ls, _ = x3.shape
    x = x3.reshape(S * Ls, D_MODEL)
    z, xbc, q, k, v, dt = _project(x, w, tm)
    r3 = lambda a: a.reshape(S, Ls, a.shape[-1])
    xbc3, k3, v3 = r3(xbc), r3(k), r3(v)
    y_ssd, h_new = _ssd(xbc3, r3(z), r3(dt), tail, h0t, w, L, tb_ssd)
    if not from_cache:
        kprev, vprev = k3, v3
    y_att = _attention(r3(q), k3, v3, kprev, vprev, mk, mv, w, L, tb_att, from_cache)
    part = _merge(x, y_ssd.reshape(S * Ls, SSD_DIM), y_att.reshape(S * Ls, ATT_DIM), w, tm)
    return part, xbc3, h_new, k3, v3


def _state_to_kernel(h):
    return h.reshape(h.shape[0], SSD_DIM, D_STATE)


def _state_from_kernel(hk):
    return hk.reshape(hk.shape[0], SSD_HEADS, HEAD_DIM, D_STATE)


def kernel(x_prompt, x_sample, cache_conv, state_ssd, cache_k, cache_v, meta_tokens, g_mix, w_in, conv_w, conv_b, dt_bias, a_log, d_skip, g_ssd_out, g_q, g_k, attn_sinks, g_att_out, w_out, g_ffn, w_route_group, b_route_group, w_route_expert, b_route_expert, w_gate, w_up, w_down):
    w = _prepare_weights(g_mix[0], w_in[0], conv_w[0], conv_b[0], dt_bias[0], a_log[0], d_skip[0],
                         g_ssd_out[0], g_q[0], g_k[0], attn_sinks[0], g_att_out[0], w_out[0], g_ffn[0],
                         w_route_group[0], b_route_group[0], w_route_expert[0], b_route_expert[0],
                         w_gate[0], w_up[0], w_down[0])
    n_b = x_sample.shape[0]
    n_dec = x_sample.shape[1]

    _, m_xbc, _, mk, mv, m_dt = _project(meta_tokens, w, N_META)
    zero_tail = jnp.zeros((1, CONV_WIDTH - 1, CONV_DIM), F32)
    zero_state = jnp.zeros((1, SSD_DIM, D_STATE), F32)
    m_xbc3 = m_xbc.reshape(1, N_META, CONV_DIM)
    _, m_state = _ssd(m_xbc3, jnp.zeros((1, N_META, SSD_DIM), F32), m_dt.reshape(1, N_META, LANES),
                      zero_tail, zero_state, w, N_META, N_META)
    m_tail = m_xbc3[:, N_META - (CONV_WIDTH - 1):]

    part_p, xbc_p, st_p, k_p, v_p = _segment(
        x_prompt, w, mk, mv, m_tail, m_state, None, None,
        L=CHUNK, tb_ssd=256, tb_att=256, tm=512, from_cache=False)
    part_s, xbc_s, st_s, k_s, v_s = _segment(
        x_sample, w, mk, mv, cache_conv[0], _state_to_kernel(state_ssd[0]),
        cache_k[0].reshape(n_b, WINDOW, KV_DIM), cache_v[0].reshape(n_b, WINDOW, KV_DIM),
        L=n_dec, tb_ssd=n_dec, tb_att=n_dec, tm=256, from_cache=True)
    yp, ys = _moe([part_p, part_s], w, tm=256)
    yp = yp.reshape(x_prompt.shape)
    ys = ys.reshape(x_sample.shape)

    heads = lambda a, rows: a.reshape(a.shape[0], rows, ATT_KV_HEADS, HEAD_DIM)[None]
    return (yp, ys,
            xbc_p[:, -(CONV_WIDTH - 1):][None],
            _state_from_kernel(st_p)[None],
            heads(k_p[:, -WINDOW:], WINDOW), heads(v_p[:, -WINDOW:], WINDOW),
            xbc_s[:, -(CONV_WIDTH - 1):][None],
            _state_from_kernel(st_s)[None],
            heads(k_s, n_dec), heads(v_s, n_dec))
```

```python
import functools
import math

import numpy as np
import jax
import jax.numpy as jnp
from jax import lax
from jax.experimental import pallas as pl
from jax.experimental.pallas import tpu as pltpu

D_MODEL = 1024
CHUNK = 64
N_META = 16
HEAD_DIM = 64
ATT_HEADS = 16
ATT_KV_HEADS = 4
ATT_REP = ATT_HEADS // ATT_KV_HEADS
ATT_DIM = ATT_HEADS * HEAD_DIM
KV_DIM = ATT_KV_HEADS * HEAD_DIM
WINDOW = 128
SSD_HEADS = 16
SSD_DIM = SSD_HEADS * HEAD_DIM
SSD_GROUPS = 2
GROUP_DIM = SSD_DIM // SSD_GROUPS
D_STATE = 128
CONV_WIDTH = 4
CONV_DIM = SSD_DIM + 2 * SSD_GROUPS * D_STATE
N_EXPERT_GROUPS = 4
EXPERTS_PER_GROUP = 8
N_EXPERTS = N_EXPERT_GROUPS * EXPERTS_PER_GROUP
EXPERT_FF = D_MODEL // 4
EPS = 1e-6

LANES = 128
SUBLANES = 8
MXU_DIM = 256
DT_COPIES = 3
KEY_SPAN = 256
VMEM_LIMIT = 48 * 1024 * 1024

F32 = jnp.float32
BF16 = jnp.bfloat16
NEG_INF = float("-inf")


def _dot(a, b):
    return jnp.dot(a, b, preferred_element_type=F32)


def _dot_nt(a, b):
    return lax.dot_general(a, b, (((1,), (1,)), ((), ())), preferred_element_type=F32)


def _dot_tn(a, b):
    return lax.dot_general(a, b, (((0,), (0,)), ((), ())), preferred_element_type=F32)


def _split3(x):
    hi = x.astype(BF16)
    r = x - hi.astype(F32)
    mid = r.astype(BF16)
    lo = (r - mid.astype(F32)).astype(BF16)
    return hi, mid, lo


def _pack3(x):
    hi, mid, lo = _split3(x)
    lane = lax.broadcasted_iota(jnp.int32, x.shape, 1)
    packed = jnp.where(lane < SSD_HEADS, hi.astype(F32),
                       jnp.where(lane < 2 * SSD_HEADS, mid.astype(F32), lo.astype(F32)))
    return packed.astype(BF16)


def _sel_dot(sel, x):
    hi, mid, lo = _split3(x)
    return _dot(sel, hi) + _dot(sel, mid) + _dot(sel, lo)


def _silu(x):
    return x * (1.0 / (1.0 + jnp.exp2(x * (-math.log2(math.e)))))


def _rms(x, g):
    ms = jnp.mean(x * x, axis=-1, keepdims=True)
    return x * lax.rsqrt(ms + EPS) * g


def _const_spec(shape):
    n = len(shape)
    return pl.BlockSpec(shape, lambda *_: (0,) * n, pipeline_mode=pl.Buffered(1))


def _proj_kernel(x_ref, gmix_ref, wqk_ref, wrest_ref, gqk_ref, bd_ref,
                 z_ref, xbc_ref, q_ref, k_ref, v_ref, dt_ref):
    tm = x_ref.shape[0]
    xn = _rms(x_ref[...], gmix_ref[...]).astype(BF16)
    qk = _dot(xn, wqk_ref[...])
    n_slices = (ATT_DIM + KV_DIM) // KV_DIM
    sq = jnp.concatenate([qk[:, j * KV_DIM:(j + 1) * KV_DIM] for j in range(n_slices)], axis=0)
    sq = sq * sq
    hi = sq.astype(BF16)
    lo = (sq - hi.astype(F32)).astype(BF16)
    ms = _dot(jnp.concatenate([hi, lo], axis=1), bd_ref[...]) * (1.0 / HEAD_DIM)
    inv = lax.rsqrt(ms + EPS)
    for j in range(n_slices):
        sl = slice(j * KV_DIM, (j + 1) * KV_DIM)
        normed = qk[:, sl] * inv[j * tm:(j + 1) * tm, :] * gqk_ref[:, sl]
        if j < ATT_DIM // KV_DIM:
            q_ref[:, sl] = (normed * HEAD_DIM ** -0.5).astype(BF16)
        else:
            k_ref[...] = normed
    rest = _dot(xn, wrest_ref[...])
    cuts = np.cumsum([0, SSD_DIM, CONV_DIM, KV_DIM, LANES])
    for ref, lo_c, hi_c in zip((z_ref, xbc_ref, v_ref, dt_ref), cuts[:-1], cuts[1:]):
        ref[...] = rest[:, lo_c:hi_c]


def _project(x, w, tm):
    t = x.shape[0]
    assert t % tm == 0
    row = lambda n: pl.BlockSpec((tm, n), lambda i: (i, 0))
    ins = [x, w["g_mix"], w["w_qk"], w["w_rest"], w["g_qk"], w["bd"]]
    in_specs = [row(D_MODEL)] + [_const_spec(a.shape) for a in ins[1:]]
    out_dims = (SSD_DIM, CONV_DIM, ATT_DIM, KV_DIM, KV_DIM, LANES)
    out_dtypes = (F32, F32, BF16, F32, F32, F32)
    return pl.pallas_call(
        _proj_kernel,
        grid=(t // tm,),
        in_specs=in_specs,
        out_specs=[row(n) for n in out_dims],
        out_shape=[jax.ShapeDtypeStruct((t, n), d) for n, d in zip(out_dims, out_dtypes)],
        compiler_params=pltpu.CompilerParams(
            dimension_semantics=("arbitrary",), vmem_limit_bytes=VMEM_LIMIT),
        name="projection",
    )(*ins)


def _ssd_kernel(xbc_ref, z_ref, dt_ref, tail_ref, h0_ref, convw_ref, convb_ref, dtb_ref,
                alog_ref, dskip_ref, gout_ref, ehead_ref, epos_ref, tri_ref,
                y_ref, hout_ref, buf_ref, act_ref, dts_ref, ys_ref, st_ref, *, L, TB):
    hp = LANES // L
    n_lane_tiles = SSD_HEADS // hp
    hq = MXU_DIM // L
    n_tiles = SSD_HEADS // hq
    tile_w = hq * HEAD_DIM
    b = pl.program_id(1)
    pad = SUBLANES - (CONV_WIDTH - 1)

    @pl.when(b == 0)
    def _():
        buf_ref[pad:SUBLANES, :] = tail_ref[...]
        st_ref[...] = h0_ref[...].T

    buf_ref[SUBLANES:SUBLANES + TB, :] = xbc_ref[...]
    acc = convb_ref[...] + convw_ref[0:1, :] * buf_ref[pad:pad + TB, :]
    for j in range(1, CONV_WIDTH):
        acc = acc + convw_ref[j:j + 1, :] * buf_ref[pad + j:pad + j + TB, :]
    act_ref[...] = _silu(acc)
    buf_ref[pad:SUBLANES, :] = buf_ref[TB + pad:TB + SUBLANES, :]

    dtx = dt_ref[...] + dtb_ref[...]
    dts_ref[...] = jnp.maximum(dtx, 0.0) + jnp.log(1.0 + jnp.exp(-jnp.abs(dtx)))

    a_row = -jnp.exp(alog_ref[...])
    ehead = ehead_ref[...]
    epos = epos_ref[...]
    tri = tri_ref[...]
    dskip = dskip_ref[...]
    row_i = lax.broadcasted_iota(jnp.int32, (L, n_lane_tiles * LANES), 0)
    col_i = lax.broadcasted_iota(jnp.int32, (L, n_lane_tiles * LANES), 1)
    pos_i = col_i % L
    diag_mask = row_i == pos_i
    causal_mask = row_i >= pos_i
    colhead = lax.broadcasted_iota(jnp.int32, (L, tile_w), 1) // HEAD_DIM

    def chunk(c):
        r0 = pl.multiple_of(c * L, L)
        dtc = dts_ref[pl.ds(r0, L), :]
        cs = _sel_dot(tri, dtc * a_row)
        cs16 = _pack3(cs)
        dt_b = _dot(_pack3(dtc), ehead)
        cs_b = _dot(cs16, ehead)
        cs_last_b = cs_b[L - 1:L, :]
        ecs_b = jnp.exp(cs_b)
        dec_end_b = jnp.exp(cs_last_b - cs_b)
        bdec_b = jnp.exp(cs_last_b)
        cs_col = cs_b if L == HEAD_DIM else _dot(cs16, epos)
        cs_row = jnp.sum(jnp.where(diag_mask, cs_col, 0.0), axis=0, keepdims=True)
        lmat = jnp.exp(jnp.where(causal_mask, cs_col - cs_row, NEG_INF))

        xs = act_ref[pl.ds(r0, L), 0:SSD_DIM]
        xdt_f = xs * dt_b
        xdt = xdt_f.astype(BF16)
        wx = (xdt_f * dec_end_b).astype(BF16)
        st = st_ref[...]
        st16 = st.astype(BF16)

        y_diag = []
        y_off = []
        s_new = []
        cb = [None] * SSD_GROUPS
        for g in range(SSD_GROUPS):
            bg = act_ref[pl.ds(r0, L), SSD_DIM + g * D_STATE:SSD_DIM + (g + 1) * D_STATE].astype(BF16)
            cg = act_ref[pl.ds(r0, L), SSD_DIM + (SSD_GROUPS + g) * D_STATE:
                         SSD_DIM + (SSD_GROUPS + g + 1) * D_STATE].astype(BF16)
            cb[g] = _dot_nt(cg, jnp.concatenate([bg] * hp, axis=0))
            gsl = slice(g * GROUP_DIM, (g + 1) * GROUP_DIM)
            y_off.append(_dot(cg, st16[:, gsl]))
            s_new.append(_dot_tn(bg, wx[:, gsl]))
        lane_tiles = MXU_DIM // LANES
        for t in range(n_tiles):
            cbt = jnp.concatenate(
                [cb[((t * lane_tiles + j) * hp * HEAD_DIM) // GROUP_DIM] for j in range(lane_tiles)], axis=1)
            gmat = (lmat[:, t * MXU_DIM:(t + 1) * MXU_DIM] * cbt).astype(BF16)
            xt = xdt[:, t * tile_w:(t + 1) * tile_w]
            rhs = jnp.concatenate(
                [jnp.where(colhead == hh, xt, jnp.zeros_like(xt)) for hh in range(hq)], axis=0)
            y_diag.append(_dot(gmat, rhs))
        y = (jnp.concatenate(y_diag, axis=1) + jnp.concatenate(y_off, axis=1) * ecs_b
             + dskip * xs)
        ys_ref[pl.ds(r0, L), :] = y
        st_ref[...] = bdec_b * st + jnp.concatenate(s_new, axis=1)

    n_chunks = TB // L
    unroll = 4 if n_chunks % 4 == 0 else 1

    def chunks(i, carry):
        for u in range(unroll):
            chunk(i * unroll + u)
        return carry

    lax.fori_loop(0, n_chunks // unroll, chunks, 0)

    yg = ys_ref[...] * _silu(z_ref[...])
    y_ref[...] = _rms(yg, gout_ref[...])

    @pl.when(b == pl.num_programs(1) - 1)
    def _():
        hout_ref[...] = st_ref[...].T


def _ssd(xbc, z, dt, tail, h0t, w, L, TB):
    S, Ls, _ = xbc.shape
    assert Ls % TB == 0 and TB % L == 0 and LANES % L == 0
    nb = Ls // TB
    hp = LANES // L
    n_tiles = SSD_HEADS // hp
    per_stream = lambda a: (lambda s, b: (s, 0, 0)) if a.shape[0] == S and S > 1 else (lambda s, b: (0, 0, 0))
    seq = lambda n: pl.BlockSpec((None, TB, n), lambda s, b: (s, b, 0))
    ehead = np.zeros((LANES, SSD_DIM), np.float32)
    epos = np.zeros((LANES, n_tiles * LANES), np.float32)
    for part in range(DT_COPIES):
        for h in range(SSD_HEADS):
            ehead[part * SSD_HEADS + h, h * HEAD_DIM:(h + 1) * HEAD_DIM] = 1.0
            epos[part * SSD_HEADS + h, h * L:(h + 1) * L] = 1.0
    tri = np.tril(np.ones((L, L), np.float32))
    consts = [w["conv_w"], w["conv_b"], w["dt_bias"], w["a_log"], w["d_skip"], w["g_ssd_out"],
              jnp.asarray(ehead, BF16), jnp.asarray(epos, BF16), jnp.asarray(tri, BF16)]
    in_specs = [seq(CONV_DIM), seq(SSD_DIM), seq(LANES),
                pl.BlockSpec((None, CONV_WIDTH - 1, CONV_DIM), per_stream(tail)),
                pl.BlockSpec((None, SSD_DIM, D_STATE), per_stream(h0t))]
    in_specs += [_const_spec(c.shape) for c in consts]
    return pl.pallas_call(
        functools.partial(_ssd_kernel, L=L, TB=TB),
        grid=(S, nb),
        in_specs=in_specs,
        out_specs=[seq(SSD_DIM), pl.BlockSpec((None, SSD_DIM, D_STATE), lambda s, b: (s, 0, 0))],
        out_shape=[jax.ShapeDtypeStruct((S, Ls, SSD_DIM), F32),
                   jax.ShapeDtypeStruct((S, SSD_DIM, D_STATE), F32)],
        scratch_shapes=[pltpu.VMEM((SUBLANES + TB, CONV_DIM), F32),
                        pltpu.VMEM((TB, CONV_DIM), F32),
                        pltpu.VMEM((TB, LANES), F32),
                        pltpu.VMEM((TB, SSD_DIM), F32),
                        pltpu.VMEM((D_STATE, SSD_DIM), F32)],
        compiler_params=pltpu.CompilerParams(
            dimension_semantics=("arbitrary", "arbitrary"), vmem_limit_bytes=VMEM_LIMIT),
        name="ssd",
    )(xbc, z, dt, tail, h0t, *consts)


def _split_heads(x):
    lane = lax.broadcasted_iota(jnp.int32, (x.shape[0], LANES), 1)
    low = lane < HEAD_DIM
    lo, hi = [], []
    for t in range(KV_DIM // LANES):
        tile = x[:, t * LANES:(t + 1) * LANES]
        swapped = pltpu.roll(tile, HEAD_DIM, axis=1)
        lo += [jnp.where(low, tile, 0.0), jnp.where(low, swapped, 0.0)]
        hi += [jnp.where(low, 0.0, swapped), jnp.where(low, 0.0, tile)]
    return jnp.concatenate(lo, axis=1).astype(BF16), jnp.concatenate(hi, axis=1).astype(BF16)


def _attn_kernel(sink_ref, q_ref, k_ref, v_ref, kp_ref, vp_ref, mk_ref, mv_ref, bias_ref, g_ref,
                 o_ref, k_lo, k_hi, v_lo, v_hi, ybuf, s_buf, e_buf, sink_buf, *, L, TB, from_cache):
    b = pl.program_id(1)
    bufs = (k_lo, k_hi, v_lo, v_hi)
    if from_cache:
        window = _split_heads(kp_ref[...]) + _split_heads(vp_ref[...])
        for buf, val in zip(bufs, window):
            buf[0:WINDOW, :] = val
    else:
        @pl.when(b == 0)
        def _():
            for buf in bufs:
                buf[0:WINDOW, :] = jnp.zeros((WINDOW, buf.shape[1]), BF16)

        @pl.when(b > 0)
        def _():
            for buf in bufs:
                buf[0:WINDOW, :] = buf[TB:TB + WINDOW, :]

    for buf, val in zip(bufs, _split_heads(k_ref[...]) + _split_heads(v_ref[...])):
        buf[WINDOW:WINDOW + TB, :] = val
    n_pad = KEY_SPAN - WINDOW - L - N_META
    pad_rows = jnp.zeros((n_pad, 2 * KV_DIM), BF16)
    tails = [jnp.concatenate([t, pad_rows], axis=0)
             for t in _split_heads(mk_ref[...]) + _split_heads(mv_ref[...])]
    lane = lax.broadcasted_iota(jnp.int32, (2 * L, 2 * KEY_SPAN), 1) % KEY_SPAN
    lane_h = lax.broadcasted_iota(jnp.int32, (L, LANES), 1)
    lane_v = lax.broadcasted_iota(jnp.int32, (KEY_SPAN, LANES), 1)
    ones_lo = jnp.where(lane_v < HEAD_DIM, 1.0, 0.0).astype(BF16)
    ones_hi = jnp.where(lane_v < HEAD_DIM, 0.0, 1.0).astype(BF16)
    n_chunks = TB // L
    pairs_per_group = ATT_REP // 2

    def chunk_keys(c):
        r0 = c * L
        return [jnp.concatenate([buf[r0:r0 + WINDOW, :], buf[r0 + WINDOW:r0 + WINDOW + L, :], tail], axis=0)
                for buf, tail in zip(bufs, tails)]

    for c in range(n_chunks):
        r0 = c * L
        keys = chunk_keys(c)
        for g in range(ATT_KV_HEADS):
            gl = slice(g * LANES, (g + 1) * LANES)
            q4 = jnp.concatenate([q_ref[r0:r0 + L, p * LANES:(p + 1) * LANES]
                                  for p in range(pairs_per_group * g, pairs_per_group * (g + 1))], axis=0)
            s = _dot_nt(q4, jnp.concatenate([keys[0][:, gl], keys[1][:, gl]], axis=0)) + bias_ref[g]
            if not from_cache and r0 < WINDOW:
                n_invalid = WINDOW - (b * TB + r0)
                s = jnp.where(lane < n_invalid, NEG_INF, s)
            s_buf[c * ATT_KV_HEADS + g] = s

    for c in range(n_chunks):
        for pair in range(ATT_HEADS // 2):
            slot = c * ATT_KV_HEADS + pair // pairs_per_group
            rows = slice((pair % pairs_per_group) * L, (pair % pairs_per_group + 1) * L)
            sink_terms = []
            for half in range(2):
                cols = slice(half * KEY_SPAN, (half + 1) * KEY_SPAN)
                s = s_buf[slot, rows, cols]
                sink = sink_ref[2 * pair + half]
                m = jnp.maximum(jnp.max(s, axis=-1, keepdims=True), sink)
                e_buf[slot, rows, cols] = jnp.exp(s - m).astype(BF16)
                sink_terms.append(jnp.exp(sink - m))
            sink_buf[c * (ATT_HEADS // 2) + pair] = jnp.where(lane_h < HEAD_DIM, sink_terms[0], sink_terms[1])

    for c in range(n_chunks):
        r0 = c * L
        keys = chunk_keys(c)
        for g in range(ATT_KV_HEADS):
            gl = slice(g * LANES, (g + 1) * LANES)
            vcat = jnp.concatenate([jnp.concatenate([keys[2][:, gl], ones_lo], axis=1),
                                    jnp.concatenate([keys[3][:, gl], ones_hi], axis=1)], axis=0)
            ov = _dot(e_buf[c * ATT_KV_HEADS + g], vcat)
            for j in range(pairs_per_group):
                pair = pairs_per_group * g + j
                den = ov[j * L:(j + 1) * L, LANES:2 * LANES] + sink_buf[c * (ATT_HEADS // 2) + pair]
                ybuf[r0:r0 + L, pair * LANES:(pair + 1) * LANES] = ov[j * L:(j + 1) * L, 0:LANES] * (1.0 / den)
    o_ref[...] = _rms(ybuf[...], g_ref[...])


def _attention(q, k, v, kprev, vprev, mk, mv, w, L, TB, from_cache):
    S, Ls, _ = q.shape
    assert Ls % TB == 0 and TB % L == 0 and (from_cache or TB >= WINDOW)
    nb = Ls // TB
    seq = lambda n: pl.BlockSpec((None, TB, n), lambda s, b: (s, b, 0))
    if not from_cache:
        kprev = vprev = jnp.zeros((1, WINDOW, KV_DIM), F32)
    prev = pl.BlockSpec((None, WINDOW, KV_DIM),
                        (lambda s, b: (s, 0, 0)) if from_cache else (lambda s, b: (0, 0, 0)))
    slopes = 2.0 ** (-8.0 * np.arange(1, ATT_HEADS + 1, dtype=np.float64) / ATT_HEADS)
    dist = np.abs(WINDOW + np.arange(L)[:, None] - np.arange(WINDOW + L)[None, :])
    bias = np.full((ATT_HEADS, L, KEY_SPAN), NEG_INF, np.float32)
    bias[:, :, :WINDOW + L] = -slopes[:, None, None] * dist[None]
    bias[:, :, WINDOW + L:WINDOW + L + N_META] = 0.0
    bias = bias.reshape(ATT_KV_HEADS, ATT_REP // 2, 2, L, KEY_SPAN).transpose(0, 1, 3, 2, 4)
    bias = jnp.asarray(bias.reshape(ATT_KV_HEADS, ATT_REP // 2 * L, 2 * KEY_SPAN))
    n_chunks = TB // L
    in_specs = [pl.BlockSpec(memory_space=pltpu.SMEM),
                seq(ATT_DIM), seq(KV_DIM), seq(KV_DIM), prev, prev,
                _const_spec(mk.shape), _const_spec(mv.shape), _const_spec(bias.shape),
                _const_spec(w["g_att_out"].shape)]
    return pl.pallas_call(
        functools.partial(_attn_kernel, L=L, TB=TB, from_cache=from_cache),
        grid=(S, nb),
        in_specs=in_specs,
        out_specs=seq(ATT_DIM),
        out_shape=jax.ShapeDtypeStruct((S, Ls, ATT_DIM), F32),
        scratch_shapes=[pltpu.VMEM((WINDOW + TB, 2 * KV_DIM), BF16) for _ in range(4)]
                       + [pltpu.VMEM((TB, ATT_DIM), F32),
                          pltpu.VMEM((n_chunks * ATT_KV_HEADS, 2 * L, 2 * KEY_SPAN), F32),
                          pltpu.VMEM((n_chunks * ATT_KV_HEADS, 2 * L, 2 * KEY_SPAN), BF16),
                          pltpu.VMEM((n_chunks * ATT_HEADS // 2, L, LANES), F32)],
        compiler_params=pltpu.CompilerParams(
            dimension_semantics=("arbitrary", "arbitrary"), vmem_limit_bytes=VMEM_LIMIT),
        name="attention",
    )(w["attn_sinks"], q, k, v, kprev, vprev, mk, mv, bias, w["g_att_out"])


TOK_TILE = 256
SEG_ALIGN = SUBLANES
SORT_ROWS = 2 * TOK_TILE + N_EXPERTS * SEG_ALIGN
XS_COLS = D_MODEL + LANES
META_P1, META_P2, META_G1, META_G2 = N_EXPERTS, N_EXPERTS + 1, N_EXPERTS + 2, N_EXPERTS + 3


def _merge_kernel(h_ref, ys_ref, ya_ref, wo_ref, gffn_ref, wr_ref, br_ref, tri_ref, upper_ref,
                  h1_ref, xn_ref, meta_ref, cnt_ref):
    ycat = jnp.concatenate([ys_ref[...].astype(BF16), ya_ref[...].astype(BF16)], axis=1)
    h1 = h_ref[...] + _dot(ycat, wo_ref[...])
    h1_ref[...] = h1
    xn = _rms(h1, gffn_ref[...]).astype(BF16)
    xn_ref[...] = xn
    all_logits = _dot(xn, wr_ref[...]) + br_ref[...]
    tri = tri_ref[...]
    upper = upper_ref[...]
    lane = lax.broadcasted_iota(jnp.int32, (TOK_TILE, LANES), 1)
    big = jnp.int32(LANES)
    gmask = (lane >= N_EXPERTS) & (lane < N_EXPERTS + N_EXPERT_GROUPS)
    for s in range(all_logits.shape[0] // TOK_TILE):
        rows = slice(s * TOK_TILE, (s + 1) * TOK_TILE)
        logits = all_logits[rows]

        def top1(mask):
            mval = jnp.max(jnp.where(mask, logits, NEG_INF), axis=-1, keepdims=True)
            idx = jnp.min(jnp.where(mask & (logits == mval), lane, big), axis=-1, keepdims=True)
            return mval, idx

        gmax, gidx = top1(gmask)
        gate_g = 1.0 / jnp.sum(jnp.where(gmask, jnp.exp(logits - gmax), 0.0), axis=-1, keepdims=True)
        grp = gidx - N_EXPERTS
        emask = (lane // EXPERTS_PER_GROUP) == grp
        v1, i1 = top1(emask)
        v2, i2 = top1(emask & (lane != i1))
        e2 = jnp.exp(v2 - v1)
        g1 = gate_g / (1.0 + e2)
        g2 = gate_g * e2 / (1.0 + e2)
        oh1 = jnp.where(lane == i1, 1.0, 0.0)
        oh2 = jnp.where(lane == i2, 1.0, 0.0)
        oh = oh1 + oh2
        earlier = _dot(tri, oh.astype(BF16))
        cnt = jnp.sum(oh, axis=0, keepdims=True)
        units = jnp.floor((cnt + (SEG_ALIGN - 1)) * (1.0 / SEG_ALIGN))
        units = jnp.broadcast_to(units, (2 * SUBLANES, LANES)).astype(BF16)
        slot = _dot(units, upper)[0:1, :] * SEG_ALIGN + earlier
        p1 = jnp.sum(oh1 * slot, axis=-1, keepdims=True)
        p2 = jnp.sum(oh2 * slot, axis=-1, keepdims=True)
        meta = jnp.where(lane == META_P1, p1, 0.0)
        meta = jnp.where(lane == META_P2, p2, meta)
        meta = jnp.where(lane == META_G1, g1, meta)
        meta = jnp.where(lane == META_G2, g2, meta)
        meta_ref[rows, :] = meta
        cnt_ref[s] = jnp.broadcast_to(cnt, (SUBLANES, LANES))


def _merge(h, y_ssd, y_att, w, tm):
    t = h.shape[0]
    assert t % tm == 0 and tm % TOK_TILE == 0
    sub = tm // TOK_TILE
    row = lambda n: pl.BlockSpec((tm, n), lambda i: (i, 0))
    tri = jnp.asarray(np.tril(np.ones((TOK_TILE, TOK_TILE), np.float32), -1), BF16)
    upper = jnp.asarray(np.triu(np.ones((LANES, LANES), np.float32), 1), BF16)
    consts = [w["w_out"], w["g_ffn"], w["w_route"], w["b_route"], tri, upper]
    return pl.pallas_call(
        _merge_kernel,
        grid=(t // tm,),
        in_specs=[row(D_MODEL), row(SSD_DIM), row(ATT_DIM)] + [_const_spec(c.shape) for c in consts],
        out_specs=[row(D_MODEL), row(D_MODEL), row(LANES),
                   pl.BlockSpec((sub, SUBLANES, LANES), lambda i: (i, 0, 0))],
        out_shape=[jax.ShapeDtypeStruct((t, D_MODEL), F32),
                   jax.ShapeDtypeStruct((t, D_MODEL), BF16),
                   jax.ShapeDtypeStruct((t, LANES), F32),
                   jax.ShapeDtypeStruct((t // TOK_TILE, SUBLANES, LANES), F32)],
        compiler_params=pltpu.CompilerParams(
            dimension_semantics=("arbitrary",), vmem_limit_bytes=VMEM_LIMIT),
        name="merge_route",
    )(h, y_ssd, y_att, *consts)


def _meta_col(meta, j):
    lane = lax.broadcasted_iota(jnp.int32, meta.shape, 1)
    return jnp.sum(jnp.where(lane == j, meta, 0.0), axis=-1, keepdims=True)


def _pair_selectors(meta):
    rows = lax.broadcasted_iota(jnp.int32, (meta.shape[0], SORT_ROWS), 1)
    s1 = jnp.where(rows == _meta_col(meta, META_P1).astype(jnp.int32), 1.0, 0.0).astype(BF16)
    s2 = jnp.where(rows == _meta_col(meta, META_P2).astype(jnp.int32), 1.0, 0.0).astype(BF16)
    return s1, s2


def _segment_dmas(off_ref, cnt_ref, row_ref, b, tile_buf, hbm, sem, to_hbm, wait):
    for e in range(N_EXPERTS):
        k = b * N_EXPERTS + e
        n = pl.multiple_of(cnt_ref[k], SEG_ALIGN)
        v = tile_buf.at[pl.ds(pl.multiple_of(off_ref[k], SEG_ALIGN), n)]
        h = hbm.at[pl.ds(pl.multiple_of(row_ref[k], SEG_ALIGN), n)]
        cp = pltpu.make_async_copy(v, h, sem) if to_hbm else pltpu.make_async_copy(h, v, sem)
        if wait:
            cp.wait()
        else:
            cp.start()


def _gate_lanes(g):
    hi, mid, lo = _split3(g)
    lane = lax.broadcasted_iota(jnp.int32, (g.shape[0], LANES), 1)
    out = jnp.where(lane == 0, hi.astype(F32), 0.0)
    out = jnp.where(lane == 1, mid.astype(F32), out)
    out = jnp.where(lane == 2, lo.astype(F32), out)
    return out.astype(BF16)


def _dispatch_kernel(off_ref, cnt_ref, row_ref, tail_row_ref, tail_cnt_ref, xn_ref, meta_ref, *rest,
                     tile0, first):
    xs_ref, sort_buf, zero_buf, sem, tail_sem = rest if first else rest[1:]
    b = pl.program_id(0)
    nb = pl.num_programs(0)
    slot = b % 2
    seg = functools.partial(_segment_dmas, off_ref, cnt_ref, row_ref, hbm=xs_ref, to_hbm=True)

    def tails(wait):
        def body(e, carry):
            n = pl.multiple_of(tail_cnt_ref[e], SEG_ALIGN)

            @pl.when(n > 0)
            def _():
                cp = pltpu.make_async_copy(
                    zero_buf.at[pl.ds(0, n)],
                    xs_ref.at[pl.ds(pl.multiple_of(tail_row_ref[e], SEG_ALIGN), n)], tail_sem.at[0])
                if wait:
                    cp.wait()
                else:
                    cp.start()
            return carry

        lax.fori_loop(0, N_EXPERTS, body, 0)

    @pl.when(b >= 2)
    def _():
        seg(tile0 + b - 2, sort_buf.at[slot], sem=sem.at[slot], wait=True)

    if first:
        @pl.when(b == 0)
        def _():
            zero_buf[...] = jnp.zeros(zero_buf.shape, F32)
            tails(wait=False)

    meta = meta_ref[...]
    s1, s2 = _pair_selectors(meta)
    sort_buf[slot, :, 0:D_MODEL] = _dot_tn(s1 + s2, xn_ref[...])
    sort_buf[slot, :, D_MODEL:XS_COLS] = (_dot_tn(s1, _gate_lanes(_meta_col(meta, META_G1)))
                                          + _dot_tn(s2, _gate_lanes(_meta_col(meta, META_G2))))
    seg(tile0 + b, sort_buf.at[slot], sem=sem.at[slot], wait=False)

    @pl.when(b == nb - 1)
    def _():
        @pl.when(b >= 1)
        def _():
            seg(tile0 + b - 1, sort_buf.at[1 - slot], sem=sem.at[1 - slot], wait=True)

        seg(tile0 + b, sort_buf.at[slot], sem=sem.at[slot], wait=True)
        if first:
            tails(wait=True)


def _expert_kernel(tile_expert_ref, n_active_ref, xs_ref, wg_ref, wu_ref, wd_ref, y_ref):
    @pl.when(pl.program_id(0) < n_active_ref[0])
    def _():
        x = xs_ref[:, 0:D_MODEL].astype(BF16)
        gate = jnp.sum(xs_ref[:, D_MODEL:XS_COLS], axis=-1, keepdims=True)
        gu = _dot(x, jnp.concatenate([wg_ref[...].astype(BF16), wu_ref[...].astype(BF16)], axis=1))
        hid = _silu(gu[:, 0:EXPERT_FF]) * gu[:, EXPERT_FF:2 * EXPERT_FF]
        y_ref[...] = gate * _dot(hid.astype(BF16), wd_ref[...].astype(BF16))


def _combine_kernel(off_ref, cnt_ref, row_ref, h1_ref, meta_ref, y_ref, o_ref, y_buf, sem, *, tile0):
    b = pl.program_id(0)
    nb = pl.num_programs(0)
    slot = b % 2
    seg = functools.partial(_segment_dmas, off_ref, cnt_ref, row_ref, hbm=y_ref, to_hbm=False)

    @pl.when(b == 0)
    def _():
        y_buf[...] = jnp.zeros(y_buf.shape, F32)
        seg(tile0 + b, y_buf.at[slot], sem=sem.at[slot], wait=False)

    @pl.when(b + 1 < nb)
    def _():
        seg(tile0 + b + 1, y_buf.at[1 - slot], sem=sem.at[1 - slot], wait=False)

    seg(tile0 + b, y_buf.at[slot], sem=sem.at[slot], wait=True)
    s1, s2 = _pair_selectors(meta_ref[...])
    sel = s1 + s2
    y = y_buf[slot, 0:SORT_ROWS, :]
    hi = y.astype(BF16)
    lo = (y - hi.astype(F32)).astype(BF16)
    o_ref[...] = h1_ref[...] + (_dot(sel, hi) + _dot(sel, lo))


def _moe(parts, w, tm):
    i32 = jnp.int32
    n_tiles = [p[0].shape[0] // TOK_TILE for p in parts]
    nb = sum(n_tiles)
    t = nb * TOK_TILE
    cnt = jnp.concatenate([p[3][:, 0, :N_EXPERTS] for p in parts], axis=0).astype(i32)
    cnt_al = (cnt + (SEG_ALIGN - 1)) // SEG_ALIGN * SEG_ALIGN
    off = jnp.cumsum(cnt_al, axis=1) - cnt_al
    tot = jnp.sum(cnt_al, axis=0)
    tot_tm = (tot + (tm - 1)) // tm * tm
    start = jnp.cumsum(tot_tm) - tot_tm
    row = start[None, :] + jnp.cumsum(cnt_al, axis=0) - cnt_al
    tile_ends = jnp.cumsum(tot_tm // tm)
    max_rows = 2 * t + (SEG_ALIGN - 1) * min(N_EXPERTS * nb, 2 * t) + N_EXPERTS * (tm - SEG_ALIGN)
    nt = -(-max_rows // tm)
    tile_expert = jnp.sum((jnp.arange(nt, dtype=i32)[:, None] >= tile_ends[None, :]).astype(i32), axis=1)
    tile_expert = jnp.minimum(tile_expert, N_EXPERTS - 1)
    n_active = tile_ends[-1:].astype(i32)
    empty = cnt_al == 0
    flat = lambda a: a.reshape(-1).astype(i32)
    n_rows = flat(jnp.maximum(cnt_al, SEG_ALIGN))
    e_idx = jnp.arange(N_EXPERTS, dtype=i32)[None, :]
    spare = nt * tm + ((jnp.arange(nb, dtype=i32)[:, None] % 2) * N_EXPERTS + e_idx) * SEG_ALIGN
    dispatch_tables = [flat(off), n_rows, flat(jnp.where(empty, spare, row))]
    combine_tables = [flat(jnp.where(empty, SORT_ROWS + e_idx * SEG_ALIGN, off)), n_rows,
                      flat(jnp.where(empty, 0, row))]
    n_spare = 2 * N_EXPERTS * SEG_ALIGN
    tail_tables = [(start + tot).astype(i32), (tot_tm - tot).astype(i32)]

    tok = lambda n: pl.BlockSpec((TOK_TILE, n), lambda i, *_: (i, 0))
    hbm = pl.BlockSpec(memory_space=pl.ANY)
    n_prefetch = len(dispatch_tables) + len(tail_tables)
    xs = None
    tile0 = 0
    for (_, xn, meta, _), n in zip(parts, n_tiles):
        first = xs is None
        xs = pl.pallas_call(
            functools.partial(_dispatch_kernel, tile0=tile0, first=first),
            grid_spec=pltpu.PrefetchScalarGridSpec(
                num_scalar_prefetch=n_prefetch, grid=(n,),
                in_specs=[tok(D_MODEL), tok(LANES)] + ([] if first else [hbm]),
                out_specs=hbm,
                scratch_shapes=[pltpu.VMEM((2, SORT_ROWS, XS_COLS), F32),
                                pltpu.VMEM((tm, XS_COLS), F32),
                                pltpu.SemaphoreType.DMA((2,)),
                                pltpu.SemaphoreType.DMA((1,))]),
            out_shape=jax.ShapeDtypeStruct((nt * tm + n_spare, XS_COLS), F32),
            input_output_aliases={} if first else {n_prefetch + 2: 0},
            compiler_params=pltpu.CompilerParams(
                dimension_semantics=("arbitrary",), vmem_limit_bytes=VMEM_LIMIT),
            name="dispatch",
        )(*dispatch_tables, *tail_tables, xn, meta, *([] if first else [xs]))
        tile0 += n

    act = lambda i, te, na: jnp.minimum(i, na[0] - 1)
    y = pl.pallas_call(
        _expert_kernel,
        grid_spec=pltpu.PrefetchScalarGridSpec(
            num_scalar_prefetch=2, grid=(nt,),
            in_specs=[pl.BlockSpec((tm, XS_COLS), lambda i, te, na: (act(i, te, na), 0)),
                      pl.BlockSpec((None, D_MODEL, EXPERT_FF), lambda i, te, na: (te[act(i, te, na)], 0, 0)),
                      pl.BlockSpec((None, D_MODEL, EXPERT_FF), lambda i, te, na: (te[act(i, te, na)], 0, 0)),
                      pl.BlockSpec((None, EXPERT_FF, D_MODEL), lambda i, te, na: (te[act(i, te, na)], 0, 0))],
            out_specs=pl.BlockSpec((tm, D_MODEL), lambda i, te, na: (act(i, te, na), 0))),
        out_shape=jax.ShapeDtypeStruct((nt * tm, D_MODEL), F32),
        compiler_params=pltpu.CompilerParams(
            dimension_semantics=("arbitrary",), vmem_limit_bytes=VMEM_LIMIT),
        name="experts",
    )(tile_expert, n_active, xs, w["w_gate"], w["w_up"], w["w_down"])

    outs = []
    tile0 = 0
    for (h1, _, meta, _), n in zip(parts, n_tiles):
        outs.append(pl.pallas_call(
            functools.partial(_combine_kernel, tile0=tile0),
            grid_spec=pltpu.PrefetchScalarGridSpec(
                num_scalar_prefetch=len(combine_tables), grid=(n,),
                in_specs=[tok(D_MODEL), tok(LANES), hbm],
                out_specs=tok(D_MODEL),
                scratch_shapes=[pltpu.VMEM((2, SORT_ROWS + N_EXPERTS * SEG_ALIGN, D_MODEL), F32),
                                pltpu.SemaphoreType.DMA((2,))]),
            out_shape=jax.ShapeDtypeStruct((n * TOK_TILE, D_MODEL), F32),
            compiler_params=pltpu.CompilerParams(
                dimension_semantics=("arbitrary",), vmem_limit_bytes=VMEM_LIMIT),
            name="combine",
        )(*combine_tables, h1, meta, y))
        tile0 += n
    return outs


def _prepare_weights(g_mix, w_in, conv_w, conv_b, dt_bias, a_log, d_skip, g_ssd_out, g_q, g_k,
                     attn_sinks, g_att_out, w_out, g_ffn, w_route_group, b_route_group,
                     w_route_expert, b_route_expert, w_gate, w_up, w_down):
    cuts = np.cumsum([0, SSD_DIM, CONV_DIM, SSD_HEADS, ATT_DIM, KV_DIM, KV_DIM])
    wi = w_in.astype(BF16)
    seg = lambda i: wi[:, cuts[i]:cuts[i + 1]]
    pad_lanes = lambda a: jnp.pad(a, ((0, 0), (0, LANES - a.shape[1])))
    bd = np.kron(np.eye(KV_DIM // HEAD_DIM, dtype=np.float32), np.ones((HEAD_DIM, HEAD_DIM), np.float32))
    n_route = N_EXPERTS + N_EXPERT_GROUPS
    return {
        "g_mix": g_mix.reshape(1, D_MODEL),
        "w_qk": jnp.concatenate([seg(3), seg(4)], axis=1),
        "w_rest": jnp.concatenate([seg(0), seg(1), seg(5), pad_lanes(jnp.tile(seg(2), (1, DT_COPIES)))], axis=1),
        "g_qk": jnp.concatenate([jnp.tile(g_q, ATT_HEADS), jnp.tile(g_k, ATT_KV_HEADS)]).reshape(1, ATT_DIM + KV_DIM),
        "bd": jnp.asarray(np.concatenate([bd, bd], axis=0), BF16),
        "conv_w": conv_w, "conv_b": conv_b.reshape(1, CONV_DIM),
        "dt_bias": pad_lanes(jnp.tile(dt_bias, DT_COPIES).reshape(1, DT_COPIES * SSD_HEADS)),
        "a_log": pad_lanes(jnp.tile(a_log, DT_COPIES).reshape(1, DT_COPIES * SSD_HEADS)),
        "d_skip": jnp.repeat(d_skip, HEAD_DIM).reshape(1, SSD_DIM),
        "g_ssd_out": g_ssd_out.reshape(1, SSD_DIM),
        "attn_sinks": attn_sinks,
        "g_att_out": g_att_out.reshape(1, ATT_DIM),
        "w_out": w_out.astype(BF16),
        "g_ffn": g_ffn.reshape(1, D_MODEL),
        "w_route": pad_lanes(jnp.concatenate([w_route_expert, w_route_group], axis=1)).astype(BF16),
        "b_route": pad_lanes(jnp.concatenate([b_route_expert, b_route_group]).reshape(1, n_route)),
        "w_gate": w_gate, "w_up": w_up, "w_down": w_down,
    }


def _segment(x3, w, mk, mv, tail, h0t, kprev, vprev, L, tb_ssd, tb_att, tm, from_cache):
    S, Ls, _ = x3.shape
    x = x3.reshape(S * Ls, D_MODEL)
    z, xbc, q, k, v, dt = _project(x, w, tm)
    r3 = lambda a: a.reshape(S, Ls, a.shape[-1])
    xbc3, k3, v3 = r3(xbc), r3(k), r3(v)
    y_ssd, h_new = _ssd(xbc3, r3(z), r3(dt), tail, h0t, w, L, tb_ssd)
    if not from_cache:
        kprev, vprev = k3, v3
    y_att = _attention(r3(q), k3, v3, kprev, vprev, mk, mv, w, L, tb_att, from_cache)
    part = _merge(x, y_ssd.reshape(S * Ls, SSD_DIM), y_att.reshape(S * Ls, ATT_DIM), w, tm)
    return part, xbc3, h_new, k3, v3


def _state_to_kernel(h):
    return h.reshape(h.shape[0], SSD_DIM, D_STATE)


def _state_from_kernel(hk):
    return hk.reshape(hk.shape[0], SSD_HEADS, HEAD_DIM, D_STATE)


def kernel(x_prompt, x_sample, cache_conv, state_ssd, cache_k, cache_v, meta_tokens, g_mix, w_in, conv_w, conv_b, dt_bias, a_log, d_skip, g_ssd_out, g_q, g_k, attn_sinks, g_att_out, w_out, g_ffn, w_route_group, b_route_group, w_route_expert, b_route_expert, w_gate, w_up, w_down):
    w = _prepare_weights(g_mix[0], w_in[0], conv_w[0], conv_b[0], dt_bias[0], a_log[0], d_skip[0],
                         g_ssd_out[0], g_q[0], g_k[0], attn_sinks[0], g_att_out[0], w_out[0], g_ffn[0],
                         w_route_group[0], b_route_group[0], w_route_expert[0], b_route_expert[0],
                         w_gate[0], w_up[0], w_down[0])
    n_b = x_sample.shape[0]
    n_dec = x_sample.shape[1]

    _, m_xbc, _, mk, mv, m_dt = _project(meta_tokens, w, N_META)
    zero_tail = jnp.zeros((1, CONV_WIDTH - 1, CONV_DIM), F32)
    zero_state = jnp.zeros((1, SSD_DIM, D_STATE), F32)
    m_xbc3 = m_xbc.reshape(1, N_META, CONV_DIM)
    _, m_state = _ssd(m_xbc3, jnp.zeros((1, N_META, SSD_DIM), F32), m_dt.reshape(1, N_META, LANES),
                      zero_tail, zero_state, w, N_META, N_META)
    m_tail = m_xbc3[:, N_META - (CONV_WIDTH - 1):]

    part_p, xbc_p, st_p, k_p, v_p = _segment(
        x_prompt, w, mk, mv, m_tail, m_state, None, None,
        L=CHUNK, tb_ssd=256, tb_att=256, tm=512, from_cache=False)
    part_s, xbc_s, st_s, k_s, v_s = _segment(
        x_sample, w, mk, mv, cache_conv[0], _state_to_kernel(state_ssd[0]),
        cache_k[0].reshape(n_b, WINDOW, KV_DIM), cache_v[0].reshape(n_b, WINDOW, KV_DIM),
        L=n_dec, tb_ssd=n_dec, tb_att=n_dec, tm=256, from_cache=True)
    yp, ys = _moe([part_p, part_s], w, tm=256)
    yp = yp.reshape(x_prompt.shape)
    ys = ys.reshape(x_sample.shape)

    heads = lambda a, rows: a.reshape(a.shape[0], rows, ATT_KV_HEADS, HEAD_DIM)[None]
    return (yp, ys,
            xbc_p[:, -(CONV_WIDTH - 1):][None],
            _state_from_kernel(st_p)[None],
            heads(k_p[:, -WINDOW:], WINDOW), heads(v_p[:, -WINDOW:], WINDOW),
            xbc_s[:, -(CONV_WIDTH - 1):][None],
            _state_from_kernel(st_s)[None],
            heads(k_s, n_dec), heads(v_s, n_dec))
```

```python
import functools
import math

import numpy as np
import jax
import jax.numpy as jnp
from jax import lax
from jax.experimental import pallas as pl
from jax.experimental.pallas import tpu as pltpu

D_MODEL = 1024
CHUNK = 64
N_META = 16
HEAD_DIM = 64
ATT_HEADS = 16
ATT_KV_HEADS = 4
ATT_REP = ATT_HEADS // ATT_KV_HEADS
ATT_DIM = ATT_HEADS * HEAD_DIM
KV_DIM = ATT_KV_HEADS * HEAD_DIM
WINDOW = 128
SSD_HEADS = 16
SSD_DIM = SSD_HEADS * HEAD_DIM
SSD_GROUPS = 2
GROUP_DIM = SSD_DIM // SSD_GROUPS
D_STATE = 128
CONV_WIDTH = 4
CONV_DIM = SSD_DIM + 2 * SSD_GROUPS * D_STATE
N_EXPERT_GROUPS = 4
EXPERTS_PER_GROUP = 8
N_EXPERTS = N_EXPERT_GROUPS * EXPERTS_PER_GROUP
EXPERT_FF = D_MODEL // 4
EPS = 1e-6

LANES = 128
SUBLANES = 8
MXU_DIM = 256
DT_COPIES = 3
KEY_SPAN = 256
VMEM_LIMIT = 48 * 1024 * 1024

F32 = jnp.float32
BF16 = jnp.bfloat16
NEG_INF = float("-inf")


def _dot(a, b):
    return jnp.dot(a, b, preferred_element_type=F32)


def _dot_nt(a, b):
    return lax.dot_general(a, b, (((1,), (1,)), ((), ())), preferred_element_type=F32)


def _dot_tn(a, b):
    return lax.dot_general(a, b, (((0,), (0,)), ((), ())), preferred_element_type=F32)


def _split3(x):
    hi = x.astype(BF16)
    r = x - hi.astype(F32)
    mid = r.astype(BF16)
    lo = (r - mid.astype(F32)).astype(BF16)
    return hi, mid, lo


def _pack3(x):
    hi, mid, lo = _split3(x)
    lane = lax.broadcasted_iota(jnp.int32, x.shape, 1)
    packed = jnp.where(lane < SSD_HEADS, hi.astype(F32),
                       jnp.where(lane < 2 * SSD_HEADS, mid.astype(F32), lo.astype(F32)))
    return packed.astype(BF16)


def _sel_dot(sel, x):
    hi, mid, lo = _split3(x)
    return _dot(sel, hi) + _dot(sel, mid) + _dot(sel, lo)


def _silu(x):
    return x * (1.0 / (1.0 + jnp.exp2(x * (-math.log2(math.e)))))


def _rms(x, g):
    ms = jnp.mean(x * x, axis=-1, keepdims=True)
    return x * lax.rsqrt(ms + EPS) * g


def _const_spec(shape):
    n = len(shape)
    return pl.BlockSpec(shape, lambda *_: (0,) * n, pipeline_mode=pl.Buffered(1))


def _proj_kernel(x_ref, gmix_ref, wqk_ref, wrest_ref, gqk_ref, bd_ref,
                 z_ref, xbc_ref, q_ref, k_ref, v_ref, dt_ref):
    tm = x_ref.shape[0]
    xn = _rms(x_ref[...], gmix_ref[...]).astype(BF16)
    qk = _dot(xn, wqk_ref[...])
    n_slices = (ATT_DIM + KV_DIM) // KV_DIM
    sq = jnp.concatenate([qk[:, j * KV_DIM:(j + 1) * KV_DIM] for j in range(n_slices)], axis=0)
    sq = sq * sq
    hi = sq.astype(BF16)
    lo = (sq - hi.astype(F32)).astype(BF16)
    ms = _dot(jnp.concatenate([hi, lo], axis=1), bd_ref[...]) * (1.0 / HEAD_DIM)
    inv = lax.rsqrt(ms + EPS)
    for j in range(n_slices):
        sl = slice(j * KV_DIM, (j + 1) * KV_DIM)
        normed = qk[:, sl] * inv[j * tm:(j + 1) * tm, :] * gqk_ref[:, sl]
        if j < ATT_DIM // KV_DIM:
            q_ref[:, sl] = (normed * HEAD_DIM ** -0.5).astype(BF16)
        else:
            k_ref[...] = normed
    rest = _dot(xn, wrest_ref[...])
    cuts = np.cumsum([0, SSD_DIM, CONV_DIM, KV_DIM, LANES])
    for ref, lo_c, hi_c in zip((z_ref, xbc_ref, v_ref, dt_ref), cuts[:-1], cuts[1:]):
        ref[...] = rest[:, lo_c:hi_c]


def _project(x, w, tm):
    t = x.shape[0]
    assert t % tm == 0
    row = lambda n: pl.BlockSpec((tm, n), lambda i: (i, 0))
    ins = [x, w["g_mix"], w["w_qk"], w["w_rest"], w["g_qk"], w["bd"]]
    in_specs = [row(D_MODEL)] + [_const_spec(a.shape) for a in ins[1:]]
    out_dims = (SSD_DIM, CONV_DIM, ATT_DIM, KV_DIM, KV_DIM, LANES)
    out_dtypes = (F32, F32, BF16, F32, F32, F32)
    return pl.pallas_call(
        _proj_kernel,
        grid=(t // tm,),
        in_specs=in_specs,
        out_specs=[row(n) for n in out_dims],
        out_shape=[jax.ShapeDtypeStruct((t, n), d) for n, d in zip(out_dims, out_dtypes)],
        compiler_params=pltpu.CompilerParams(
            dimension_semantics=("arbitrary",), vmem_limit_bytes=VMEM_LIMIT),
        name="projection",
    )(*ins)


def _ssd_kernel(xbc_ref, z_ref, dt_ref, tail_ref, h0_ref, convw_ref, convb_ref, dtb_ref,
                alog_ref, dskip_ref, gout_ref, ehead_ref, epos_ref, tri_ref,
                y_ref, hout_ref, buf_ref, act_ref, dts_ref, ys_ref, st_ref, *, L, TB):
    hp = LANES // L
    n_lane_tiles = SSD_HEADS // hp
    hq = MXU_DIM // L
    n_tiles = SSD_HEADS // hq
    tile_w = hq * HEAD_DIM
    b = pl.program_id(1)
    pad = SUBLANES - (CONV_WIDTH - 1)

    @pl.when(b == 0)
    def _():
        buf_ref[pad:SUBLANES, :] = tail_ref[...]
        st_ref[...] = h0_ref[...].T

    buf_ref[SUBLANES:SUBLANES + TB, :] = xbc_ref[...]
    acc = convb_ref[...] + convw_ref[0:1, :] * buf_ref[pad:pad + TB, :]
    for j in range(1, CONV_WIDTH):
        acc = acc + convw_ref[j:j + 1, :] * buf_ref[pad + j:pad + j + TB, :]
    act_ref[...] = _silu(acc)
    buf_ref[pad:SUBLANES, :] = buf_ref[TB + pad:TB + SUBLANES, :]

    dtx = dt_ref[...] + dtb_ref[...]
    dts_ref[...] = jnp.maximum(dtx, 0.0) + jnp.log(1.0 + jnp.exp(-jnp.abs(dtx)))

    a_row = -jnp.exp(alog_ref[...])
    ehead = ehead_ref[...]
    epos = epos_ref[...]
    tri = tri_ref[...]
    dskip = dskip_ref[...]
    row_i = lax.broadcasted_iota(jnp.int32, (L, n_lane_tiles * LANES), 0)
    col_i = lax.broadcasted_iota(jnp.int32, (L, n_lane_tiles * LANES), 1)
    pos_i = col_i % L
    diag_mask = row_i == pos_i
    causal_mask = row_i >= pos_i
    colhead = lax.broadcasted_iota(jnp.int32, (L, tile_w), 1) // HEAD_DIM

    def chunk(c):
        r0 = pl.multiple_of(c * L, L)
        dtc = dts_ref[pl.ds(r0, L), :]
        cs = _sel_dot(tri, dtc * a_row)
        cs16 = _pack3(cs)
        dt_b = _dot(_pack3(dtc), ehead)
        cs_b = _dot(cs16, ehead)
        cs_last_b = cs_b[L - 1:L, :]
        ecs_b = jnp.exp(cs_b)
        dec_end_b = jnp.exp(cs_last_b - cs_b)
        bdec_b = jnp.exp(cs_last_b)
        cs_col = cs_b if L == HEAD_DIM else _dot(cs16, epos)
        cs_row = jnp.sum(jnp.where(diag_mask, cs_col, 0.0), axis=0, keepdims=True)
        lmat = jnp.exp(jnp.where(causal_mask, cs_col - cs_row, NEG_INF))

        xs = act_ref[pl.ds(r0, L), 0:SSD_DIM]
        xdt_f = xs * dt_b
        xdt = xdt_f.astype(BF16)
        wx = (xdt_f * dec_end_b).astype(BF16)
        st = st_ref[...]
        st16 = st.astype(BF16)

        y_diag = []
        y_off = []
        s_new = []
        cb = [None] * SSD_GROUPS
        for g in range(SSD_GROUPS):
            bg = act_ref[pl.ds(r0, L), SSD_DIM + g * D_STATE:SSD_DIM + (g + 1) * D_STATE].astype(BF16)
            cg = act_ref[pl.ds(r0, L), SSD_DIM + (SSD_GROUPS + g) * D_STATE:
                         SSD_DIM + (SSD_GROUPS + g + 1) * D_STATE].astype(BF16)
            cb[g] = _dot_nt(cg, jnp.concatenate([bg] * hp, axis=0))
            gsl = slice(g * GROUP_DIM, (g + 1) * GROUP_DIM)
            y_off.append(_dot(cg, st16[:, gsl]))
            s_new.append(_dot_tn(bg, wx[:, gsl]))
        lane_tiles = MXU_DIM // LANES
        for t in range(n_tiles):
            cbt = jnp.concatenate(
                [cb[((t * lane_tiles + j) * hp * HEAD_DIM) // GROUP_DIM] for j in range(lane_tiles)], axis=1)
            gmat = (lmat[:, t * MXU_DIM:(t + 1) * MXU_DIM] * cbt).astype(BF16)
            xt = xdt[:, t * tile_w:(t + 1) * tile_w]
            rhs = jnp.concatenate(
                [jnp.where(colhead == hh, xt, jnp.zeros_like(xt)) for hh in range(hq)], axis=0)
            y_diag.append(_dot(gmat, rhs))
        y = (jnp.concatenate(y_diag, axis=1) + jnp.concatenate(y_off, axis=1) * ecs_b
             + dskip * xs)
        ys_ref[pl.ds(r0, L), :] = y
        st_ref[...] = bdec_b * st + jnp.concatenate(s_new, axis=1)

    n_chunks = TB // L
    unroll = 4 if n_chunks % 4 == 0 else 1

    def chunks(i, carry):
        for u in range(unroll):
            chunk(i * unroll + u)
        return carry

    lax.fori_loop(0, n_chunks // unroll, chunks, 0)

    yg = ys_ref[...] * _silu(z_ref[...])
    y_ref[...] = _rms(yg, gout_ref[...])

    @pl.when(b == pl.num_programs(1) - 1)
    def _():
        hout_ref[...] = st_ref[...].T


def _ssd(xbc, z, dt, tail, h0t, w, L, TB):
    S, Ls, _ = xbc.shape
    assert Ls % TB == 0 and TB % L == 0 and LANES % L == 0
    nb = Ls // TB
    hp = LANES // L
    n_tiles = SSD_HEADS // hp
    per_stream = lambda a: (lambda s, b: (s, 0, 0)) if a.shape[0] == S and S > 1 else (lambda s, b: (0, 0, 0))
    seq = lambda n: pl.BlockSpec((None, TB, n), lambda s, b: (s, b, 0))
    ehead = np.zeros((LANES, SSD_DIM), np.float32)
    epos = np.zeros((LANES, n_tiles * LANES), np.float32)
    for part in range(DT_COPIES):
        for h in range(SSD_HEADS):
            ehead[part * SSD_HEADS + h, h * HEAD_DIM:(h + 1) * HEAD_DIM] = 1.0
            epos[part * SSD_HEADS + h, h * L:(h + 1) * L] = 1.0
    tri = np.tril(np.ones((L, L), np.float32))
    consts = [w["conv_w"], w["conv_b"], w["dt_bias"], w["a_log"], w["d_skip"], w["g_ssd_out"],
              jnp.asarray(ehead, BF16), jnp.asarray(epos, BF16), jnp.asarray(tri, BF16)]
    in_specs = [seq(CONV_DIM), seq(SSD_DIM), seq(LANES),
                pl.BlockSpec((None, CONV_WIDTH - 1, CONV_DIM), per_stream(tail)),
                pl.BlockSpec((None, SSD_DIM, D_STATE), per_stream(h0t))]
    in_specs += [_const_spec(c.shape) for c in consts]
    return pl.pallas_call(
        functools.partial(_ssd_kernel, L=L, TB=TB),
        grid=(S, nb),
        in_specs=in_specs,
        out_specs=[seq(SSD_DIM), pl.BlockSpec((None, SSD_DIM, D_STATE), lambda s, b: (s, 0, 0))],
        out_shape=[jax.ShapeDtypeStruct((S, Ls, SSD_DIM), F32),
                   jax.ShapeDtypeStruct((S, SSD_DIM, D_STATE), F32)],
        scratch_shapes=[pltpu.VMEM((SUBLANES + TB, CONV_DIM), F32),
                        pltpu.VMEM((TB, CONV_DIM), F32),
                        pltpu.VMEM((TB, LANES), F32),
                        pltpu.VMEM((TB, SSD_DIM), F32),
                        pltpu.VMEM((D_STATE, SSD_DIM), F32)],
        compiler_params=pltpu.CompilerParams(
            dimension_semantics=("arbitrary", "arbitrary"), vmem_limit_bytes=VMEM_LIMIT),
        name="ssd",
    )(xbc, z, dt, tail, h0t, *consts)


def _split_heads(x):
    lane = lax.broadcasted_iota(jnp.int32, (x.shape[0], LANES), 1)
    low = lane < HEAD_DIM
    lo, hi = [], []
    for t in range(KV_DIM // LANES):
        tile = x[:, t * LANES:(t + 1) * LANES]
        swapped = pltpu.roll(tile, HEAD_DIM, axis=1)
        lo += [jnp.where(low, tile, 0.0), jnp.where(low, swapped, 0.0)]
        hi += [jnp.where(low, 0.0, swapped), jnp.where(low, 0.0, tile)]
    return jnp.concatenate(lo, axis=1).astype(BF16), jnp.concatenate(hi, axis=1).astype(BF16)


def _attn_kernel(sink_ref, q_ref, k_ref, v_ref, kp_ref, vp_ref, mk_ref, mv_ref, bias_ref, g_ref,
                 o_ref, k_lo, k_hi, v_lo, v_hi, ybuf, s_buf, e_buf, sink_buf, *, L, TB, from_cache):
    b = pl.program_id(1)
    bufs = (k_lo, k_hi, v_lo, v_hi)
    if from_cache:
        window = _split_heads(kp_ref[...]) + _split_heads(vp_ref[...])
        for buf, val in zip(bufs, window):
            buf[0:WINDOW, :] = val
    else:
        @pl.when(b == 0)
        def _():
            for buf in bufs:
                buf[0:WINDOW, :] = jnp.zeros((WINDOW, buf.shape[1]), BF16)

        @pl.when(b > 0)
        def _():
            for buf in bufs:
                buf[0:WINDOW, :] = buf[TB:TB + WINDOW, :]

    for buf, val in zip(bufs, _split_heads(k_ref[...]) + _split_heads(v_ref[...])):
        buf[WINDOW:WINDOW + TB, :] = val
    n_pad = KEY_SPAN - WINDOW - L - N_META
    pad_rows = jnp.zeros((n_pad, 2 * KV_DIM), BF16)
    tails = [jnp.concatenate([t, pad_rows], axis=0)
             for t in _split_heads(mk_ref[...]) + _split_heads(mv_ref[...])]
    lane = lax.broadcasted_iota(jnp.int32, (2 * L, 2 * KEY_SPAN), 1) % KEY_SPAN
    lane_h = lax.broadcasted_iota(jnp.int32, (L, LANES), 1)
    lane_v = lax.broadcasted_iota(jnp.int32, (KEY_SPAN, LANES), 1)
    ones_lo = jnp.where(lane_v < HEAD_DIM, 1.0, 0.0).astype(BF16)
    ones_hi = jnp.where(lane_v < HEAD_DIM, 0.0, 1.0).astype(BF16)
    n_chunks = TB // L
    pairs_per_group = ATT_REP // 2

    def chunk_keys(c):
        r0 = c * L
        return [jnp.concatenate([buf[r0:r0 + WINDOW, :], buf[r0 + WINDOW:r0 + WINDOW + L, :], tail], axis=0)
                for buf, tail in zip(bufs, tails)]

    for c in range(n_chunks):
        r0 = c * L
        keys = chunk_keys(c)
        for g in range(ATT_KV_HEADS):
            gl = slice(g * LANES, (g + 1) * LANES)
            q4 = jnp.concatenate([q_ref[r0:r0 + L, p * LANES:(p + 1) * LANES]
                                  for p in range(pairs_per_group * g, pairs_per_group * (g + 1))], axis=0)
            s = _dot_nt(q4, jnp.concatenate([keys[0][:, gl], keys[1][:, gl]], axis=0)) + bias_ref[g]
            if not from_cache and r0 < WINDOW:
                n_invalid = WINDOW - (b * TB + r0)
                s = jnp.where(lane < n_invalid, NEG_INF, s)
            s_buf[c * ATT_KV_HEADS + g] = s

    for c in range(n_chunks):
        for pair in range(ATT_HEADS // 2):
            slot = c * ATT_KV_HEADS + pair // pairs_per_group
            rows = slice((pair % pairs_per_group) * L, (pair % pairs_per_group + 1) * L)
            sink_terms = []
            for half in range(2):
                cols = slice(half * KEY_SPAN, (half + 1) * KEY_SPAN)
                s = s_buf[slot, rows, cols]
                sink = sink_ref[2 * pair + half]
                m = jnp.maximum(jnp.max(s, axis=-1, keepdims=True), sink)
                e_buf[slot, rows, cols] = jnp.exp(s - m).astype(BF16)
                sink_terms.append(jnp.exp(sink - m))
            sink_buf[c * (ATT_HEADS // 2) + pair] = jnp.where(lane_h < HEAD_DIM, sink_terms[0], sink_terms[1])

    for c in range(n_chunks):
        r0 = c * L
        keys = chunk_keys(c)
        for g in range(ATT_KV_HEADS):
            gl = slice(g * LANES, (g + 1) * LANES)
            vcat = jnp.concatenate([jnp.concatenate([keys[2][:, gl], ones_lo], axis=1),
                                    jnp.concatenate([keys[3][:, gl], ones_hi], axis=1)], axis=0)
            ov = _dot(e_buf[c * ATT_KV_HEADS + g], vcat)
            for j in range(pairs_per_group):
                pair = pairs_per_group * g + j
                den = ov[j * L:(j + 1) * L, LANES:2 * LANES] + sink_buf[c * (ATT_HEADS // 2) + pair]
                ybuf[r0:r0 + L, pair * LANES:(pair + 1) * LANES] = ov[j * L:(j + 1) * L, 0:LANES] * (1.0 / den)
    o_ref[...] = _rms(ybuf[...], g_ref[...])


def _attention(q, k, v, kprev, vprev, mk, mv, w, L, TB, from_cache):
    S, Ls, _ = q.shape
    assert Ls % TB == 0 and TB % L == 0 and (from_cache or TB >= WINDOW)
    nb = Ls // TB
    seq = lambda n: pl.BlockSpec((None, TB, n), lambda s, b: (s, b, 0))
    if not from_cache:
        kprev = vprev = jnp.zeros((1, WINDOW, KV_DIM), F32)
    prev = pl.BlockSpec((None, WINDOW, KV_DIM),
                        (lambda s, b: (s, 0, 0)) if from_cache else (lambda s, b: (0, 0, 0)))
    slopes = 2.0 ** (-8.0 * np.arange(1, ATT_HEADS + 1, dtype=np.float64) / ATT_HEADS)
    dist = np.abs(WINDOW + np.arange(L)[:, None] - np.arange(WINDOW + L)[None, :])
    bias = np.full((ATT_HEADS, L, KEY_SPAN), NEG_INF, np.float32)
    bias[:, :, :WINDOW + L] = -slopes[:, None, None] * dist[None]
    bias[:, :, WINDOW + L:WINDOW + L + N_META] = 0.0
    bias = bias.reshape(ATT_KV_HEADS, ATT_REP // 2, 2, L, KEY_SPAN).transpose(0, 1, 3, 2, 4)
    bias = jnp.asarray(bias.reshape(ATT_KV_HEADS, ATT_REP // 2 * L, 2 * KEY_SPAN))
    n_chunks = TB // L
    in_specs = [pl.BlockSpec(memory_space=pltpu.SMEM),
                seq(ATT_DIM), seq(KV_DIM), seq(KV_DIM), prev, prev,
                _const_spec(mk.shape), _const_spec(mv.shape), _const_spec(bias.shape),
                _const_spec(w["g_att_out"].shape)]
    return pl.pallas_call(
        functools.partial(_attn_kernel, L=L, TB=TB, from_cache=from_cache),
        grid=(S, nb),
        in_specs=in_specs,
        out_specs=seq(ATT_DIM),
        out_shape=jax.ShapeDtypeStruct((S, Ls, ATT_DIM), F32),
        scratch_shapes=[pltpu.VMEM((WINDOW + TB, 2 * KV_DIM), BF16) for _ in range(4)]
                       + [pltpu.VMEM((TB, ATT_DIM), F32),
                          pltpu.VMEM((n_chunks * ATT_KV_HEADS, 2 * L, 2 * KEY_SPAN), F32),
                          pltpu.VMEM((n_chunks * ATT_KV_HEADS, 2 * L, 2 * KEY_SPAN), BF16),
                          pltpu.VMEM((n_chunks * ATT_HEADS // 2, L, LANES), F32)],
        compiler_params=pltpu.CompilerParams(
            dimension_semantics=("arbitrary", "arbitrary"), vmem_limit_bytes=VMEM_LIMIT),
        name="attention",
    )(w["attn_sinks"], q, k, v, kprev, vprev, mk, mv, bias, w["g_att_out"])


TOK_TILE = 256
SEG_ALIGN = 2 * SUBLANES
SORT_ROWS = 2 * TOK_TILE + N_EXPERTS * SEG_ALIGN
XS_COLS = D_MODEL + LANES
META_P1, META_P2, META_G1, META_G2 = N_EXPERTS, N_EXPERTS + 1, N_EXPERTS + 2, N_EXPERTS + 3


def _merge_kernel(h_ref, ys_ref, ya_ref, wo_ref, gffn_ref, wr_ref, br_ref, tri_ref, upper_ref,
                  h1_ref, xn_ref, meta_ref, cnt_ref):
    ycat = jnp.concatenate([ys_ref[...].astype(BF16), ya_ref[...].astype(BF16)], axis=1)
    h1 = h_ref[...] + _dot(ycat, wo_ref[...])
    h1_ref[...] = h1
    xn = _rms(h1, gffn_ref[...]).astype(BF16)
    xn_ref[...] = xn
    all_logits = _dot(xn, wr_ref[...]) + br_ref[...]
    tri = tri_ref[...]
    upper = upper_ref[...]
    lane = lax.broadcasted_iota(jnp.int32, (TOK_TILE, LANES), 1)
    big = jnp.int32(LANES)
    gmask = (lane >= N_EXPERTS) & (lane < N_EXPERTS + N_EXPERT_GROUPS)
    for s in range(all_logits.shape[0] // TOK_TILE):
        rows = slice(s * TOK_TILE, (s + 1) * TOK_TILE)
        logits = all_logits[rows]

        def top1(mask):
            mval = jnp.max(jnp.where(mask, logits, NEG_INF), axis=-1, keepdims=True)
            idx = jnp.min(jnp.where(mask & (logits == mval), lane, big), axis=-1, keepdims=True)
            return mval, idx

        gmax, gidx = top1(gmask)
        gate_g = 1.0 / jnp.sum(jnp.where(gmask, jnp.exp(logits - gmax), 0.0), axis=-1, keepdims=True)
        grp = gidx - N_EXPERTS
        emask = (lane // EXPERTS_PER_GROUP) == grp
        v1, i1 = top1(emask)
        v2, i2 = top1(emask & (lane != i1))
        e2 = jnp.exp(v2 - v1)
        g1 = gate_g / (1.0 + e2)
        g2 = gate_g * e2 / (1.0 + e2)
        oh1 = jnp.where(lane == i1, 1.0, 0.0)
        oh2 = jnp.where(lane == i2, 1.0, 0.0)
        oh = oh1 + oh2
        earlier = _dot(tri, oh.astype(BF16))
        cnt = jnp.sum(oh, axis=0, keepdims=True)
        units = jnp.floor((cnt + (SEG_ALIGN - 1)) * (1.0 / SEG_ALIGN))
        units = jnp.broadcast_to(units, (2 * SUBLANES, LANES)).astype(BF16)
        slot = _dot(units, upper)[0:1, :] * SEG_ALIGN + earlier
        p1 = jnp.sum(oh1 * slot, axis=-1, keepdims=True)
        p2 = jnp.sum(oh2 * slot, axis=-1, keepdims=True)
        meta = jnp.where(lane == META_P1, p1, 0.0)
        meta = jnp.where(lane == META_P2, p2, meta)
        meta = jnp.where(lane == META_G1, g1, meta)
        meta = jnp.where(lane == META_G2, g2, meta)
        meta_ref[rows, :] = meta
        cnt_ref[s] = jnp.broadcast_to(cnt, (SUBLANES, LANES))


def _merge(h, y_ssd, y_att, w, tm):
    t = h.shape[0]
    assert t % tm == 0 and tm % TOK_TILE == 0
    sub = tm // TOK_TILE
    row = lambda n: pl.BlockSpec((tm, n), lambda i: (i, 0))
    tri = jnp.asarray(np.tril(np.ones((TOK_TILE, TOK_TILE), np.float32), -1), BF16)
    upper = jnp.asarray(np.triu(np.ones((LANES, LANES), np.float32), 1), BF16)
    consts = [w["w_out"], w["g_ffn"], w["w_route"], w["b_route"], tri, upper]
    return pl.pallas_call(
        _merge_kernel,
        grid=(t // tm,),
        in_specs=[row(D_MODEL), row(SSD_DIM), row(ATT_DIM)] + [_const_spec(c.shape) for c in consts],
        out_specs=[row(D_MODEL), row(D_MODEL), row(LANES),
                   pl.BlockSpec((sub, SUBLANES, LANES), lambda i: (i, 0, 0))],
        out_shape=[jax.ShapeDtypeStruct((t, D_MODEL), F32),
                   jax.ShapeDtypeStruct((t, D_MODEL), BF16),
                   jax.ShapeDtypeStruct((t, LANES), F32),
                   jax.ShapeDtypeStruct((t // TOK_TILE, SUBLANES, LANES), F32)],
        compiler_params=pltpu.CompilerParams(
            dimension_semantics=("arbitrary",), vmem_limit_bytes=VMEM_LIMIT),
        name="merge_route",
    )(h, y_ssd, y_att, *consts)


def _meta_col(meta, j):
    lane = lax.broadcasted_iota(jnp.int32, meta.shape, 1)
    return jnp.sum(jnp.where(lane == j, meta, 0.0), axis=-1, keepdims=True)


def _pair_selectors(meta):
    rows = lax.broadcasted_iota(jnp.int32, (meta.shape[0], SORT_ROWS), 1)
    s1 = jnp.where(rows == _meta_col(meta, META_P1).astype(jnp.int32), 1.0, 0.0).astype(BF16)
    s2 = jnp.where(rows == _meta_col(meta, META_P2).astype(jnp.int32), 1.0, 0.0).astype(BF16)
    return s1, s2


def _segment_dmas(off_ref, cnt_ref, row_ref, b, tile_buf, hbm, sem, to_hbm, wait):
    for e in range(N_EXPERTS):
        k = b * N_EXPERTS + e
        n = pl.multiple_of(cnt_ref[k], SEG_ALIGN)
        v = tile_buf.at[pl.ds(pl.multiple_of(off_ref[k], SEG_ALIGN), n)]
        h = hbm.at[pl.ds(pl.multiple_of(row_ref[k], SEG_ALIGN), n)]
        cp = pltpu.make_async_copy(v, h, sem) if to_hbm else pltpu.make_async_copy(h, v, sem)
        if wait:
            cp.wait()
        else:
            cp.start()


def _gate_lanes(g):
    hi, mid, lo = _split3(g)
    lane = lax.broadcasted_iota(jnp.int32, (g.shape[0], LANES), 1)
    out = jnp.where(lane == 0, hi.astype(F32), 0.0)
    out = jnp.where(lane == 1, mid.astype(F32), out)
    out = jnp.where(lane == 2, lo.astype(F32), out)
    return out.astype(BF16)


def _dispatch_kernel(off_ref, cnt_ref, row_ref, tail_row_ref, tail_cnt_ref, xn_ref, meta_ref, *rest,
                     tile0, first):
    xs_ref, sort_buf, zero_buf, sem, tail_sem = rest if first else rest[1:]
    b = pl.program_id(0)
    nb = pl.num_programs(0)
    slot = b % 2
    seg = functools.partial(_segment_dmas, off_ref, cnt_ref, row_ref, hbm=xs_ref, to_hbm=True)

    def tails(wait):
        def body(e, carry):
            n = pl.multiple_of(tail_cnt_ref[e], SEG_ALIGN)

            @pl.when(n > 0)
            def _():
                cp = pltpu.make_async_copy(
                    zero_buf.at[pl.ds(0, n)],
                    xs_ref.at[pl.ds(pl.multiple_of(tail_row_ref[e], SEG_ALIGN), n)], tail_sem.at[0])
                if wait:
                    cp.wait()
                else:
                    cp.start()
            return carry

        lax.fori_loop(0, N_EXPERTS, body, 0)

    @pl.when(b >= 2)
    def _():
        seg(tile0 + b - 2, sort_buf.at[slot], sem=sem.at[slot], wait=True)

    if first:
        @pl.when(b == 0)
        def _():
            zero_buf[...] = jnp.zeros(zero_buf.shape, BF16)
            tails(wait=False)

    meta = meta_ref[...]
    s1, s2 = _pair_selectors(meta)
    sort_buf[slot, :, 0:D_MODEL] = _dot_tn(s1 + s2, xn_ref[...]).astype(BF16)
    sort_buf[slot, :, D_MODEL:XS_COLS] = (_dot_tn(s1, _gate_lanes(_meta_col(meta, META_G1)))
                                          + _dot_tn(s2, _gate_lanes(_meta_col(meta, META_G2)))).astype(BF16)
    seg(tile0 + b, sort_buf.at[slot], sem=sem.at[slot], wait=False)

    @pl.when(b == nb - 1)
    def _():
        @pl.when(b >= 1)
        def _():
            seg(tile0 + b - 1, sort_buf.at[1 - slot], sem=sem.at[1 - slot], wait=True)

        seg(tile0 + b, sort_buf.at[slot], sem=sem.at[slot], wait=True)
        if first:
            tails(wait=True)


def _expert_kernel(tile_expert_ref, n_active_ref, xs_ref, wg_ref, wu_ref, wd_ref, y_ref):
    @pl.when(pl.program_id(0) < n_active_ref[0])
    def _():
        x = xs_ref[:, 0:D_MODEL]
        gate = jnp.sum(xs_ref[:, D_MODEL:XS_COLS].astype(F32), axis=-1, keepdims=True)
        gu = _dot(x, jnp.concatenate([wg_ref[...].astype(BF16), wu_ref[...].astype(BF16)], axis=1))
        hid = _silu(gu[:, 0:EXPERT_FF]) * gu[:, EXPERT_FF:2 * EXPERT_FF]
        y_ref[...] = (gate * _dot(hid.astype(BF16), wd_ref[...].astype(BF16))).astype(BF16)


def _combine_kernel(off_ref, cnt_ref, row_ref, h1_ref, meta_ref, y_ref, o_ref, y_buf, sem, *, tile0):
    b = pl.program_id(0)
    nb = pl.num_programs(0)
    slot = b % 2
    seg = functools.partial(_segment_dmas, off_ref, cnt_ref, row_ref, hbm=y_ref, to_hbm=False)

    @pl.when(b == 0)
    def _():
        y_buf[...] = jnp.zeros(y_buf.shape, BF16)
        seg(tile0 + b, y_buf.at[slot], sem=sem.at[slot], wait=False)

    @pl.when(b + 1 < nb)
    def _():
        seg(tile0 + b + 1, y_buf.at[1 - slot], sem=sem.at[1 - slot], wait=False)

    seg(tile0 + b, y_buf.at[slot], sem=sem.at[slot], wait=True)
    s1, s2 = _pair_selectors(meta_ref[...])
    sel = s1 + s2
    o_ref[...] = h1_ref[...] + _dot(sel, y_buf[slot, 0:SORT_ROWS, :])


def _moe(parts, w, tm):
    i32 = jnp.int32
    n_tiles = [p[0].shape[0] // TOK_TILE for p in parts]
    nb = sum(n_tiles)
    t = nb * TOK_TILE
    cnt = jnp.concatenate([p[3][:, 0, :N_EXPERTS] for p in parts], axis=0).astype(i32)
    cnt_al = (cnt + (SEG_ALIGN - 1)) // SEG_ALIGN * SEG_ALIGN
    off = jnp.cumsum(cnt_al, axis=1) - cnt_al
    tot = jnp.sum(cnt_al, axis=0)
    tot_tm = (tot + (tm - 1)) // tm * tm
    start = jnp.cumsum(tot_tm) - tot_tm
    row = start[None, :] + jnp.cumsum(cnt_al, axis=0) - cnt_al
    tile_ends = jnp.cumsum(tot_tm // tm)
    max_rows = 2 * t + (SEG_ALIGN - 1) * min(N_EXPERTS * nb, 2 * t) + N_EXPERTS * (tm - SEG_ALIGN)
    nt = -(-max_rows // tm)
    tile_expert = jnp.sum((jnp.arange(nt, dtype=i32)[:, None] >= tile_ends[None, :]).astype(i32), axis=1)
    tile_expert = jnp.minimum(tile_expert, N_EXPERTS - 1)
    n_active = tile_ends[-1:].astype(i32)
    empty = cnt_al == 0
    flat = lambda a: a.reshape(-1).astype(i32)
    n_rows = flat(jnp.maximum(cnt_al, SEG_ALIGN))
    e_idx = jnp.arange(N_EXPERTS, dtype=i32)[None, :]
    spare = nt * tm + ((jnp.arange(nb, dtype=i32)[:, None] % 2) * N_EXPERTS + e_idx) * SEG_ALIGN
    dispatch_tables = [flat(off), n_rows, flat(jnp.where(empty, spare, row))]
    combine_tables = [flat(jnp.where(empty, SORT_ROWS + e_idx * SEG_ALIGN, off)), n_rows,
                      flat(jnp.where(empty, 0, row))]
    n_spare = 2 * N_EXPERTS * SEG_ALIGN
    tail_tables = [(start + tot).astype(i32), (tot_tm - tot).astype(i32)]

    tok = lambda n: pl.BlockSpec((TOK_TILE, n), lambda i, *_: (i, 0))
    hbm = pl.BlockSpec(memory_space=pl.ANY)
    n_prefetch = len(dispatch_tables) + len(tail_tables)
    xs = None
    tile0 = 0
    for (_, xn, meta, _), n in zip(parts, n_tiles):
        first = xs is None
        xs = pl.pallas_call(
            functools.partial(_dispatch_kernel, tile0=tile0, first=first),
            grid_spec=pltpu.PrefetchScalarGridSpec(
                num_scalar_prefetch=n_prefetch, grid=(n,),
                in_specs=[tok(D_MODEL), tok(LANES)] + ([] if first else [hbm]),
                out_specs=hbm,
                scratch_shapes=[pltpu.VMEM((2, SORT_ROWS, XS_COLS), BF16),
                                pltpu.VMEM((tm, XS_COLS), BF16),
                                pltpu.SemaphoreType.DMA((2,)),
                                pltpu.SemaphoreType.DMA((1,))]),
            out_shape=jax.ShapeDtypeStruct((nt * tm + n_spare, XS_COLS), BF16),
            input_output_aliases={} if first else {n_prefetch + 2: 0},
            compiler_params=pltpu.CompilerParams(
                dimension_semantics=("arbitrary",), vmem_limit_bytes=VMEM_LIMIT),
            name="dispatch",
        )(*dispatch_tables, *tail_tables, xn, meta, *([] if first else [xs]))
        tile0 += n

    act = lambda i, te, na: jnp.minimum(i, na[0] - 1)
    y = pl.pallas_call(
        _expert_kernel,
        grid_spec=pltpu.PrefetchScalarGridSpec(
            num_scalar_prefetch=2, grid=(nt,),
            in_specs=[pl.BlockSpec((tm, XS_COLS), lambda i, te, na: (act(i, te, na), 0)),
                      pl.BlockSpec((None, D_MODEL, EXPERT_FF), lambda i, te, na: (te[act(i, te, na)], 0, 0)),
                      pl.BlockSpec((None, D_MODEL, EXPERT_FF), lambda i, te, na: (te[act(i, te, na)], 0, 0)),
                      pl.BlockSpec((None, EXPERT_FF, D_MODEL), lambda i, te, na: (te[act(i, te, na)], 0, 0))],
            out_specs=pl.BlockSpec((tm, D_MODEL), lambda i, te, na: (act(i, te, na), 0))),
        out_shape=jax.ShapeDtypeStruct((nt * tm, D_MODEL), BF16),
        compiler_params=pltpu.CompilerParams(
            dimension_semantics=("arbitrary",), vmem_limit_bytes=VMEM_LIMIT),
        name="experts",
    )(tile_expert, n_active, xs, w["w_gate"], w["w_up"], w["w_down"])

    outs = []
    tile0 = 0
    for (h1, _, meta, _), n in zip(parts, n_tiles):
        outs.append(pl.pallas_call(
            functools.partial(_combine_kernel, tile0=tile0),
            grid_spec=pltpu.PrefetchScalarGridSpec(
                num_scalar_prefetch=len(combine_tables), grid=(n,),
                in_specs=[tok(D_MODEL), tok(LANES), hbm],
                out_specs=tok(D_MODEL),
                scratch_shapes=[pltpu.VMEM((2, SORT_ROWS + N_EXPERTS * SEG_ALIGN, D_MODEL), BF16),
                                pltpu.SemaphoreType.DMA((2,))]),
            out_shape=jax.ShapeDtypeStruct((n * TOK_TILE, D_MODEL), F32),
            compiler_params=pltpu.CompilerParams(
                dimension_semantics=("arbitrary",), vmem_limit_bytes=VMEM_LIMIT),
            name="combine",
        )(*combine_tables, h1, meta, y))
        tile0 += n
    return outs


def _prepare_weights(g_mix, w_in, conv_w, conv_b, dt_bias, a_log, d_skip, g_ssd_out, g_q, g_k,
                     attn_sinks, g_att_out, w_out, g_ffn, w_route_group, b_route_group,
                     w_route_expert, b_route_expert, w_gate, w_up, w_down):
    cuts = np.cumsum([0, SSD_DIM, CONV_DIM, SSD_HEADS, ATT_DIM, KV_DIM, KV_DIM])
    wi = w_in.astype(BF16)
    seg = lambda i: wi[:, cuts[i]:cuts[i + 1]]
    pad_lanes = lambda a: jnp.pad(a, ((0, 0), (0, LANES - a.shape[1])))
    bd = np.kron(np.eye(KV_DIM // HEAD_DIM, dtype=np.float32), np.ones((HEAD_DIM, HEAD_DIM), np.float32))
    n_route = N_EXPERTS + N_EXPERT_GROUPS
    return {
        "g_mix": g_mix.reshape(1, D_MODEL),
        "w_qk": jnp.concatenate([seg(3), seg(4)], axis=1),
        "w_rest": jnp.concatenate([seg(0), seg(1), seg(5), pad_lanes(jnp.tile(seg(2), (1, DT_COPIES)))], axis=1),
        "g_qk": jnp.concatenate([jnp.tile(g_q, ATT_HEADS), jnp.tile(g_k, ATT_KV_HEADS)]).reshape(1, ATT_DIM + KV_DIM),
        "bd": jnp.asarray(np.concatenate([bd, bd], axis=0), BF16),
        "conv_w": conv_w, "conv_b": conv_b.reshape(1, CONV_DIM),
        "dt_bias": pad_lanes(jnp.tile(dt_bias, DT_COPIES).reshape(1, DT_COPIES * SSD_HEADS)),
        "a_log": pad_lanes(jnp.tile(a_log, DT_COPIES).reshape(1, DT_COPIES * SSD_HEADS)),
        "d_skip": jnp.repeat(d_skip, HEAD_DIM).reshape(1, SSD_DIM),
        "g_ssd_out": g_ssd_out.reshape(1, SSD_DIM),
        "attn_sinks": attn_sinks,
        "g_att_out": g_att_out.reshape(1, ATT_DIM),
        "w_out": w_out.astype(BF16),
        "g_ffn": g_ffn.reshape(1, D_MODEL),
        "w_route": pad_lanes(jnp.concatenate([w_route_expert, w_route_group], axis=1)).astype(BF16),
        "b_route": pad_lanes(jnp.concatenate([b_route_expert, b_route_group]).reshape(1, n_route)),
        "w_gate": w_gate, "w_up": w_up, "w_down": w_down,
    }


def _segment(x3, w, mk, mv, tail, h0t, kprev, vprev, L, tb_ssd, tb_att, tm, from_cache):
    S, Ls, _ = x3.shape
    x = x3.reshape(S * Ls, D_MODEL)
    z, xbc, q, k, v, dt = _project(x, w, tm)
    r3 = lambda a: a.reshape(S, Ls, a.shape[-1])
    xbc3, k3, v3 = r3(xbc), r3(k), r3(v)
    y_ssd, h_new = _ssd(xbc3, r3(z), r3(dt), tail, h0t, w, L, tb_ssd)
    if not from_cache:
        kprev, vprev = k3, v3
    y_att = _attention(r3(q), k3, v3, kprev, vprev, mk, mv, w, L, tb_att, from_cache)
    part = _merge(x, y_ssd.reshape(S * Ls, SSD_DIM), y_att.reshape(S * Ls, ATT_DIM), w, tm)
    return part, xbc3, h_new, k3, v3


def _state_to_kernel(h):
    return h.reshape(h.shape[0], SSD_DIM, D_STATE)


def _state_from_kernel(hk):
    return hk.reshape(hk.shape[0], SSD_HEADS, HEAD_DIM, D_STATE)


def kernel(x_prompt, x_sample, cache_conv, state_ssd, cache_k, cache_v, meta_tokens, g_mix, w_in, conv_w, conv_b, dt_bias, a_log, d_skip, g_ssd_out, g_q, g_k, attn_sinks, g_att_out, w_out, g_ffn, w_route_group, b_route_group, w_route_expert, b_route_expert, w_gate, w_up, w_down):
    w = _prepare_weights(g_mix[0], w_in[0], conv_w[0], conv_b[0], dt_bias[0], a_log[0], d_skip[0],
                         g_ssd_out[0], g_q[0], g_k[0], attn_sinks[0], g_att_out[0], w_out[0], g_ffn[0],
                         w_route_group[0], b_route_group[0], w_route_expert[0], b_route_expert[0],
                         w_gate[0], w_up[0], w_down[0])
    n_b = x_sample.shape[0]
    n_dec = x_sample.shape[1]

    _, m_xbc, _, mk, mv, m_dt = _project(meta_tokens, w, N_META)
    zero_tail = jnp.zeros((1, CONV_WIDTH - 1, CONV_DIM), F32)
    zero_state = jnp.zeros((1, SSD_DIM, D_STATE), F32)
    m_xbc3 = m_xbc.reshape(1, N_META, CONV_DIM)
    _, m_state = _ssd(m_xbc3, jnp.zeros((1, N_META, SSD_DIM), F32), m_dt.reshape(1, N_META, LANES),
                      zero_tail, zero_state, w, N_META, N_META)
    m_tail = m_xbc3[:, N_META - (CONV_WIDTH - 1):]

    part_p, xbc_p, st_p, k_p, v_p = _segment(
        x_prompt, w, mk, mv, m_tail, m_state, None, None,
        L=CHUNK, tb_ssd=256, tb_att=256, tm=512, from_cache=False)
    part_s, xbc_s, st_s, k_s, v_s = _segment(
        x_sample, w, mk, mv, cache_conv[0], _state_to_kernel(state_ssd[0]),
        cache_k[0].reshape(n_b, WINDOW, KV_DIM), cache_v[0].reshape(n_b, WINDOW, KV_DIM),
        L=n_dec, tb_ssd=n_dec, tb_att=n_dec, tm=256, from_cache=True)
    yp, ys = _moe([part_p, part_s], w, tm=256)
    yp = yp.reshape(x_prompt.shape)
    ys = ys.reshape(x_sample.shape)

    heads = lambda a, rows: a.reshape(a.shape[0], rows, ATT_KV_HEADS, HEAD_DIM)[None]
    return (yp, ys,
            xbc_p[:, -(CONV_WIDTH - 1):][None],
            _state_from_kernel(st_p)[None],
            heads(k_p[:, -WINDOW:], WINDOW), heads(v_p[:, -WINDOW:], WINDOW),
            xbc_s[:, -(CONV_WIDTH - 1):][None],
            _state_from_kernel(st_s)[None],
            heads(k_s, n_dec), heads(v_s, n_dec))
```

```python
import functools
import math

import numpy as np
import jax
import jax.numpy as jnp
from jax import lax
from jax.experimental import pallas as pl
from jax.experimental.pallas import tpu as pltpu

D_MODEL = 1024
CHUNK = 64
N_META = 16
HEAD_DIM = 64
ATT_HEADS = 16
ATT_KV_HEADS = 4
ATT_REP = ATT_HEADS // ATT_KV_HEADS
ATT_DIM = ATT_HEADS * HEAD_DIM
KV_DIM = ATT_KV_HEADS * HEAD_DIM
WINDOW = 128
SSD_HEADS = 16
SSD_DIM = SSD_HEADS * HEAD_DIM
SSD_GROUPS = 2
GROUP_DIM = SSD_DIM // SSD_GROUPS
D_STATE = 128
CONV_WIDTH = 4
CONV_DIM = SSD_DIM + 2 * SSD_GROUPS * D_STATE
N_EXPERT_GROUPS = 4
EXPERTS_PER_GROUP = 8
N_EXPERTS = N_EXPERT_GROUPS * EXPERTS_PER_GROUP
EXPERT_FF = D_MODEL // 4
EPS = 1e-6

LANES = 128
SUBLANES = 8
MXU_DIM = 256
DT_COPIES = 3
KEY_SPAN = 256
VMEM_LIMIT = 48 * 1024 * 1024

F32 = jnp.float32
BF16 = jnp.bfloat16
NEG_INF = float("-inf")


def _dot(a, b):
    return jnp.dot(a, b, preferred_element_type=F32)


def _dot_nt(a, b):
    return lax.dot_general(a, b, (((1,), (1,)), ((), ())), preferred_element_type=F32)


def _dot_tn(a, b):
    return lax.dot_general(a, b, (((0,), (0,)), ((), ())), preferred_element_type=F32)


def _split3(x):
    hi = x.astype(BF16)
    r = x - hi.astype(F32)
    mid = r.astype(BF16)
    lo = (r - mid.astype(F32)).astype(BF16)
    return hi, mid, lo


def _pack3(x):
    hi, mid, lo = _split3(x)
    lane = lax.broadcasted_iota(jnp.int32, x.shape, 1)
    packed = jnp.where(lane < SSD_HEADS, hi.astype(F32),
                       jnp.where(lane < 2 * SSD_HEADS, mid.astype(F32), lo.astype(F32)))
    return packed.astype(BF16)


def _sel_dot(sel, x):
    hi, mid, lo = _split3(x)
    return _dot(sel, hi) + _dot(sel, mid) + _dot(sel, lo)


def _silu(x):
    return x * (1.0 / (1.0 + jnp.exp2(x * (-math.log2(math.e)))))


def _rms(x, g):
    ms = jnp.mean(x * x, axis=-1, keepdims=True)
    return x * lax.rsqrt(ms + EPS) * g


def _const_spec(shape):
    n = len(shape)
    return pl.BlockSpec(shape, lambda *_: (0,) * n, pipeline_mode=pl.Buffered(1))


def _proj_kernel(x_ref, gmix_ref, wqk_ref, wrest_ref, gqk_ref, bd_ref,
                 z_ref, xbc_ref, q_ref, k_ref, v_ref, dt_ref):
    tm = x_ref.shape[0]
    xn = _rms(x_ref[...], gmix_ref[...]).astype(BF16)
    qk = _dot(xn, wqk_ref[...])
    n_slices = (ATT_DIM + KV_DIM) // KV_DIM
    sq = jnp.concatenate([qk[:, j * KV_DIM:(j + 1) * KV_DIM] for j in range(n_slices)], axis=0)
    sq = sq * sq
    hi = sq.astype(BF16)
    lo = (sq - hi.astype(F32)).astype(BF16)
    ms = _dot(jnp.concatenate([hi, lo], axis=1), bd_ref[...]) * (1.0 / HEAD_DIM)
    inv = lax.rsqrt(ms + EPS)
    for j in range(n_slices):
        sl = slice(j * KV_DIM, (j + 1) * KV_DIM)
        normed = qk[:, sl] * inv[j * tm:(j + 1) * tm, :] * gqk_ref[:, sl]
        if j < ATT_DIM // KV_DIM:
            q_ref[:, sl] = (normed * HEAD_DIM ** -0.5).astype(BF16)
        else:
            k_ref[...] = normed
    rest = _dot(xn, wrest_ref[...])
    cuts = np.cumsum([0, SSD_DIM, CONV_DIM, KV_DIM, LANES])
    for ref, lo_c, hi_c in zip((z_ref, xbc_ref, v_ref, dt_ref), cuts[:-1], cuts[1:]):
        ref[...] = rest[:, lo_c:hi_c]


def _project(x, w, tm):
    t = x.shape[0]
    assert t % tm == 0
    row = lambda n: pl.BlockSpec((tm, n), lambda i: (i, 0))
    ins = [x, w["g_mix"], w["w_qk"], w["w_rest"], w["g_qk"], w["bd"]]
    in_specs = [row(D_MODEL)] + [_const_spec(a.shape) for a in ins[1:]]
    out_dims = (SSD_DIM, CONV_DIM, ATT_DIM, KV_DIM, KV_DIM, LANES)
    out_dtypes = (F32, F32, BF16, F32, F32, F32)
    return pl.pallas_call(
        _proj_kernel,
        grid=(t // tm,),
        in_specs=in_specs,
        out_specs=[row(n) for n in out_dims],
        out_shape=[jax.ShapeDtypeStruct((t, n), d) for n, d in zip(out_dims, out_dtypes)],
        compiler_params=pltpu.CompilerParams(
            dimension_semantics=("arbitrary",), vmem_limit_bytes=VMEM_LIMIT),
        name="projection",
    )(*ins)


def _ssd_kernel(xbc_ref, z_ref, dt_ref, tail_ref, h0_ref, convw_ref, convb_ref, dtb_ref,
                alog_ref, dskip_ref, gout_ref, ehead_ref, epos_ref, tri_ref,
                y_ref, hout_ref, buf_ref, act_ref, dts_ref, ys_ref, st_ref, *, L, TB):
    hp = LANES // L
    n_lane_tiles = SSD_HEADS // hp
    hq = MXU_DIM // L
    n_tiles = SSD_HEADS // hq
    tile_w = hq * HEAD_DIM
    b = pl.program_id(1)
    pad = SUBLANES - (CONV_WIDTH - 1)

    @pl.when(b == 0)
    def _():
        buf_ref[pad:SUBLANES, :] = tail_ref[...]
        st_ref[...] = h0_ref[...].T

    buf_ref[SUBLANES:SUBLANES + TB, :] = xbc_ref[...]
    acc = convb_ref[...] + convw_ref[0:1, :] * buf_ref[pad:pad + TB, :]
    for j in range(1, CONV_WIDTH):
        acc = acc + convw_ref[j:j + 1, :] * buf_ref[pad + j:pad + j + TB, :]
    act_ref[...] = _silu(acc)
    buf_ref[pad:SUBLANES, :] = buf_ref[TB + pad:TB + SUBLANES, :]

    dtx = dt_ref[...] + dtb_ref[...]
    dts_ref[...] = jnp.maximum(dtx, 0.0) + jnp.log(1.0 + jnp.exp(-jnp.abs(dtx)))

    a_row = -jnp.exp(alog_ref[...])
    ehead = ehead_ref[...]
    epos = epos_ref[...]
    tri = tri_ref[...]
    dskip = dskip_ref[...]
    row_i = lax.broadcasted_iota(jnp.int32, (L, n_lane_tiles * LANES), 0)
    col_i = lax.broadcasted_iota(jnp.int32, (L, n_lane_tiles * LANES), 1)
    pos_i = col_i % L
    diag_mask = row_i == pos_i
    causal_mask = row_i >= pos_i
    colhead = lax.broadcasted_iota(jnp.int32, (L, tile_w), 1) // HEAD_DIM

    def chunk(c):
        r0 = pl.multiple_of(c * L, L)
        dtc = dts_ref[pl.ds(r0, L), :]
        cs = _sel_dot(tri, dtc * a_row)
        cs16 = _pack3(cs)
        dt_b = _dot(_pack3(dtc), ehead)
        cs_b = _dot(cs16, ehead)
        cs_last_b = cs_b[L - 1:L, :]
        ecs_b = jnp.exp(cs_b)
        dec_end_b = jnp.exp(cs_last_b - cs_b)
        bdec_b = jnp.exp(cs_last_b)
        cs_col = cs_b if L == HEAD_DIM else _dot(cs16, epos)
        cs_row = jnp.sum(jnp.where(diag_mask, cs_col, 0.0), axis=0, keepdims=True)
        lmat = jnp.exp(jnp.where(causal_mask, cs_col - cs_row, NEG_INF))

        xs = act_ref[pl.ds(r0, L), 0:SSD_DIM]
        xdt_f = xs * dt_b
        xdt = xdt_f.astype(BF16)
        wx = (xdt_f * dec_end_b).astype(BF16)
        st = st_ref[...]
        st16 = st.astype(BF16)

        y_diag = []
        y_off = []
        s_new = []
        cb = [None] * SSD_GROUPS
        for g in range(SSD_GROUPS):
            bg = act_ref[pl.ds(r0, L), SSD_DIM + g * D_STATE:SSD_DIM + (g + 1) * D_STATE].astype(BF16)
            cg = act_ref[pl.ds(r0, L), SSD_DIM + (SSD_GROUPS + g) * D_STATE:
                         SSD_DIM + (SSD_GROUPS + g + 1) * D_STATE].astype(BF16)
            cb[g] = _dot_nt(cg, jnp.concatenate([bg] * hp, axis=0))
            gsl = slice(g * GROUP_DIM, (g + 1) * GROUP_DIM)
            y_off.append(_dot(cg, st16[:, gsl]))
            s_new.append(_dot_tn(bg, wx[:, gsl]))
        lane_tiles = MXU_DIM // LANES
        for t in range(n_tiles):
            cbt = jnp.concatenate(
                [cb[((t * lane_tiles + j) * hp * HEAD_DIM) // GROUP_DIM] for j in range(lane_tiles)], axis=1)
            gmat = (lmat[:, t * MXU_DIM:(t + 1) * MXU_DIM] * cbt).astype(BF16)
            xt = xdt[:, t * tile_w:(t + 1) * tile_w]
            rhs = jnp.concatenate(
                [jnp.where(colhead == hh, xt, jnp.zeros_like(xt)) for hh in range(hq)], axis=0)
            y_diag.append(_dot(gmat, rhs))
        y = (jnp.concatenate(y_diag, axis=1) + jnp.concatenate(y_off, axis=1) * ecs_b
             + dskip * xs)
        ys_ref[pl.ds(r0, L), :] = y
        st_ref[...] = bdec_b * st + jnp.concatenate(s_new, axis=1)

    n_chunks = TB // L
    unroll = 4 if n_chunks % 4 == 0 else 1

    def chunks(i, carry):
        for u in range(unroll):
            chunk(i * unroll + u)
        return carry

    lax.fori_loop(0, n_chunks // unroll, chunks, 0)

    yg = ys_ref[...] * _silu(z_ref[...])
    y_ref[...] = _rms(yg, gout_ref[...])

    @pl.when(b == pl.num_programs(1) - 1)
    def _():
        hout_ref[...] = st_ref[...].T


def _ssd(xbc, z, dt, tail, h0t, w, L, TB):
    S, Ls, _ = xbc.shape
    assert Ls % TB == 0 and TB % L == 0 and LANES % L == 0
    nb = Ls // TB
    hp = LANES // L
    n_tiles = SSD_HEADS // hp
    per_stream = lambda a: (lambda s, b: (s, 0, 0)) if a.shape[0] == S and S > 1 else (lambda s, b: (0, 0, 0))
    seq = lambda n: pl.BlockSpec((None, TB, n), lambda s, b: (s, b, 0))
    ehead = np.zeros((LANES, SSD_DIM), np.float32)
    epos = np.zeros((LANES, n_tiles * LANES), np.float32)
    for part in range(DT_COPIES):
        for h in range(SSD_HEADS):
            ehead[part * SSD_HEADS + h, h * HEAD_DIM:(h + 1) * HEAD_DIM] = 1.0
            epos[part * SSD_HEADS + h, h * L:(h + 1) * L] = 1.0
    tri = np.tril(np.ones((L, L), np.float32))
    consts = [w["conv_w"], w["conv_b"], w["dt_bias"], w["a_log"], w["d_skip"], w["g_ssd_out"],
              jnp.asarray(ehead, BF16), jnp.asarray(epos, BF16), jnp.asarray(tri, BF16)]
    in_specs = [seq(CONV_DIM), seq(SSD_DIM), seq(LANES),
                pl.BlockSpec((None, CONV_WIDTH - 1, CONV_DIM), per_stream(tail)),
                pl.BlockSpec((None, SSD_DIM, D_STATE), per_stream(h0t))]
    in_specs += [_const_spec(c.shape) for c in consts]
    return pl.pallas_call(
        functools.partial(_ssd_kernel, L=L, TB=TB),
        grid=(S, nb),
        in_specs=in_specs,
        out_specs=[seq(SSD_DIM), pl.BlockSpec((None, SSD_DIM, D_STATE), lambda s, b: (s, 0, 0))],
        out_shape=[jax.ShapeDtypeStruct((S, Ls, SSD_DIM), F32),
                   jax.ShapeDtypeStruct((S, SSD_DIM, D_STATE), F32)],
        scratch_shapes=[pltpu.VMEM((SUBLANES + TB, CONV_DIM), F32),
                        pltpu.VMEM((TB, CONV_DIM), F32),
                        pltpu.VMEM((TB, LANES), F32),
                        pltpu.VMEM((TB, SSD_DIM), F32),
                        pltpu.VMEM((D_STATE, SSD_DIM), F32)],
        compiler_params=pltpu.CompilerParams(
            dimension_semantics=("arbitrary", "arbitrary"), vmem_limit_bytes=VMEM_LIMIT),
        name="ssd",
    )(xbc, z, dt, tail, h0t, *consts)


def _split_heads(x):
    lane = lax.broadcasted_iota(jnp.int32, (x.shape[0], LANES), 1)
    low = lane < HEAD_DIM
    lo, hi = [], []
    for t in range(KV_DIM // LANES):
        tile = x[:, t * LANES:(t + 1) * LANES]
        swapped = pltpu.roll(tile, HEAD_DIM, axis=1)
        lo += [jnp.where(low, tile, 0.0), jnp.where(low, swapped, 0.0)]
        hi += [jnp.where(low, 0.0, swapped), jnp.where(low, 0.0, tile)]
    return jnp.concatenate(lo, axis=1).astype(BF16), jnp.concatenate(hi, axis=1).astype(BF16)


def _attn_kernel(sink_ref, q_ref, k_ref, v_ref, kp_ref, vp_ref, mk_ref, mv_ref, bias_ref, g_ref,
                 o_ref, k_lo, k_hi, v_lo, v_hi, ybuf, s_buf, e_buf, sink_buf, *, L, TB, from_cache):
    b = pl.program_id(1)
    bufs = (k_lo, k_hi, v_lo, v_hi)
    if from_cache:
        window = _split_heads(kp_ref[...]) + _split_heads(vp_ref[...])
        for buf, val in zip(bufs, window):
            buf[0:WINDOW, :] = val
    else:
        @pl.when(b == 0)
        def _():
            for buf in bufs:
                buf[0:WINDOW, :] = jnp.zeros((WINDOW, buf.shape[1]), BF16)

        @pl.when(b > 0)
        def _():
            for buf in bufs:
                buf[0:WINDOW, :] = buf[TB:TB + WINDOW, :]

    for buf, val in zip(bufs, _split_heads(k_ref[...]) + _split_heads(v_ref[...])):
        buf[WINDOW:WINDOW + TB, :] = val
    n_pad = KEY_SPAN - WINDOW - L - N_META
    pad_rows = jnp.zeros((n_pad, 2 * KV_DIM), BF16)
    tails = [jnp.concatenate([t, pad_rows], axis=0)
             for t in _split_heads(mk_ref[...]) + _split_heads(mv_ref[...])]
    lane = lax.broadcasted_iota(jnp.int32, (2 * L, 2 * KEY_SPAN), 1) % KEY_SPAN
    lane_h = lax.broadcasted_iota(jnp.int32, (L, LANES), 1)
    lane_v = lax.broadcasted_iota(jnp.int32, (KEY_SPAN, LANES), 1)
    ones_lo = jnp.where(lane_v < HEAD_DIM, 1.0, 0.0).astype(BF16)
    ones_hi = jnp.where(lane_v < HEAD_DIM, 0.0, 1.0).astype(BF16)
    n_chunks = TB // L
    pairs_per_group = ATT_REP // 2

    def chunk_keys(c):
        r0 = c * L
        return [jnp.concatenate([buf[r0:r0 + WINDOW, :], buf[r0 + WINDOW:r0 + WINDOW + L, :], tail], axis=0)
                for buf, tail in zip(bufs, tails)]

    for c in range(n_chunks):
        r0 = c * L
        keys = chunk_keys(c)
        for g in range(ATT_KV_HEADS):
            gl = slice(g * LANES, (g + 1) * LANES)
            q4 = jnp.concatenate([q_ref[r0:r0 + L, p * LANES:(p + 1) * LANES]
                                  for p in range(pairs_per_group * g, pairs_per_group * (g + 1))], axis=0)
            s = _dot_nt(q4, jnp.concatenate([keys[0][:, gl], keys[1][:, gl]], axis=0)) + bias_ref[g]
            if not from_cache and r0 < WINDOW:
                n_invalid = WINDOW - (b * TB + r0)
                s = jnp.where(lane < n_invalid, NEG_INF, s)
            s_buf[c * ATT_KV_HEADS + g] = s

    for c in range(n_chunks):
        for pair in range(ATT_HEADS // 2):
            slot = c * ATT_KV_HEADS + pair // pairs_per_group
            rows = slice((pair % pairs_per_group) * L, (pair % pairs_per_group + 1) * L)
            sink_terms = []
            for half in range(2):
                cols = slice(half * KEY_SPAN, (half + 1) * KEY_SPAN)
                s = s_buf[slot, rows, cols]
                sink = sink_ref[2 * pair + half]
                m = jnp.maximum(jnp.max(s, axis=-1, keepdims=True), sink)
                e_buf[slot, rows, cols] = jnp.exp(s - m).astype(BF16)
                sink_terms.append(jnp.exp(sink - m))
            sink_buf[c * (ATT_HEADS // 2) + pair] = jnp.where(lane_h < HEAD_DIM, sink_terms[0], sink_terms[1])

    for c in range(n_chunks):
        r0 = c * L
        keys = chunk_keys(c)
        for g in range(ATT_KV_HEADS):
            gl = slice(g * LANES, (g + 1) * LANES)
            vcat = jnp.concatenate([jnp.concatenate([keys[2][:, gl], ones_lo], axis=1),
                                    jnp.concatenate([keys[3][:, gl], ones_hi], axis=1)], axis=0)
            ov = _dot(e_buf[c * ATT_KV_HEADS + g], vcat)
            for j in range(pairs_per_group):
                pair = pairs_per_group * g + j
                den = ov[j * L:(j + 1) * L, LANES:2 * LANES] + sink_buf[c * (ATT_HEADS // 2) + pair]
                ybuf[r0:r0 + L, pair * LANES:(pair + 1) * LANES] = ov[j * L:(j + 1) * L, 0:LANES] * (1.0 / den)
    o_ref[...] = _rms(ybuf[...], g_ref[...])


def _attention(q, k, v, kprev, vprev, mk, mv, w, L, TB, from_cache):
    S, Ls, _ = q.shape
    assert Ls % TB == 0 and TB % L == 0 and (from_cache or TB >= WINDOW)
    nb = Ls // TB
    seq = lambda n: pl.BlockSpec((None, TB, n), lambda s, b: (s, b, 0))
    if not from_cache:
        kprev = vprev = jnp.zeros((1, WINDOW, KV_DIM), F32)
    prev = pl.BlockSpec((None, WINDOW, KV_DIM),
                        (lambda s, b: (s, 0, 0)) if from_cache else (lambda s, b: (0, 0, 0)))
    slopes = 2.0 ** (-8.0 * np.arange(1, ATT_HEADS + 1, dtype=np.float64) / ATT_HEADS)
    dist = np.abs(WINDOW + np.arange(L)[:, None] - np.arange(WINDOW + L)[None, :])
    bias = np.full((ATT_HEADS, L, KEY_SPAN), NEG_INF, np.float32)
    bias[:, :, :WINDOW + L] = -slopes[:, None, None] * dist[None]
    bias[:, :, WINDOW + L:WINDOW + L + N_META] = 0.0
    bias = bias.reshape(ATT_KV_HEADS, ATT_REP // 2, 2, L, KEY_SPAN).transpose(0, 1, 3, 2, 4)
    bias = jnp.asarray(bias.reshape(ATT_KV_HEADS, ATT_REP // 2 * L, 2 * KEY_SPAN))
    n_chunks = TB // L
    in_specs = [pl.BlockSpec(memory_space=pltpu.SMEM),
                seq(ATT_DIM), seq(KV_DIM), seq(KV_DIM), prev, prev,
                _const_spec(mk.shape), _const_spec(mv.shape), _const_spec(bias.shape),
                _const_spec(w["g_att_out"].shape)]
    return pl.pallas_call(
        functools.partial(_attn_kernel, L=L, TB=TB, from_cache=from_cache),
        grid=(S, nb),
        in_specs=in_specs,
        out_specs=seq(ATT_DIM),
        out_shape=jax.ShapeDtypeStruct((S, Ls, ATT_DIM), F32),
        scratch_shapes=[pltpu.VMEM((WINDOW + TB, 2 * KV_DIM), BF16) for _ in range(4)]
                       + [pltpu.VMEM((TB, ATT_DIM), F32),
                          pltpu.VMEM((n_chunks * ATT_KV_HEADS, 2 * L, 2 * KEY_SPAN), F32),
                          pltpu.VMEM((n_chunks * ATT_KV_HEADS, 2 * L, 2 * KEY_SPAN), BF16),
                          pltpu.VMEM((n_chunks * ATT_HEADS // 2, L, LANES), F32)],
        compiler_params=pltpu.CompilerParams(
            dimension_semantics=("arbitrary", "arbitrary"), vmem_limit_bytes=VMEM_LIMIT),
        name="attention",
    )(w["attn_sinks"], q, k, v, kprev, vprev, mk, mv, bias, w["g_att_out"])


TOK_TILE = 256
SEG_ALIGN = 2 * SUBLANES
SORT_ROWS = 2 * TOK_TILE + N_EXPERTS * SEG_ALIGN
XS_COLS = D_MODEL + LANES
META_P1, META_P2, META_G1, META_G2 = N_EXPERTS, N_EXPERTS + 1, N_EXPERTS + 2, N_EXPERTS + 3


def _merge_kernel(h_ref, ys_ref, ya_ref, wo_ref, gffn_ref, wr_ref, br_ref, tri_ref, upper_ref,
                  h1_ref, xn_ref, meta_ref, cnt_ref, logit_buf):
    i = pl.program_id(0)

    @pl.when(i == 0)
    def _():
        logit_buf[...] = jnp.zeros(logit_buf.shape, F32)

    ycat = jnp.concatenate([ys_ref[...].astype(BF16), ya_ref[...].astype(BF16)], axis=1)
    h1 = h_ref[...] + _dot(ycat, wo_ref[...])
    h1_ref[...] = h1
    xn = _rms(h1, gffn_ref[...]).astype(BF16)
    xn_ref[...] = xn
    all_logits = logit_buf[(i + 1) % 2]
    logit_buf[i % 2] = _dot(xn, wr_ref[...]) + br_ref[...]
    tri = tri_ref[...]
    upper = upper_ref[...]
    lane = lax.broadcasted_iota(jnp.int32, (TOK_TILE, LANES), 1)
    big = jnp.int32(LANES)
    gmask = (lane >= N_EXPERTS) & (lane < N_EXPERTS + N_EXPERT_GROUPS)
    for s in range(all_logits.shape[0] // TOK_TILE):
        rows = slice(s * TOK_TILE, (s + 1) * TOK_TILE)
        logits = all_logits[rows]

        def top1(mask):
            mval = jnp.max(jnp.where(mask, logits, NEG_INF), axis=-1, keepdims=True)
            idx = jnp.min(jnp.where(mask & (logits == mval), lane, big), axis=-1, keepdims=True)
            return mval, idx

        gmax, gidx = top1(gmask)
        gate_g = 1.0 / jnp.sum(jnp.where(gmask, jnp.exp(logits - gmax), 0.0), axis=-1, keepdims=True)
        grp = gidx - N_EXPERTS
        emask = (lane // EXPERTS_PER_GROUP) == grp
        v1, i1 = top1(emask)
        v2, i2 = top1(emask & (lane != i1))
        e2 = jnp.exp(v2 - v1)
        g1 = gate_g / (1.0 + e2)
        g2 = gate_g * e2 / (1.0 + e2)
        oh1 = jnp.where(lane == i1, 1.0, 0.0)
        oh2 = jnp.where(lane == i2, 1.0, 0.0)
        oh = oh1 + oh2
        earlier = _dot(tri, oh.astype(BF16))
        cnt = jnp.sum(oh, axis=0, keepdims=True)
        units = jnp.floor((cnt + (SEG_ALIGN - 1)) * (1.0 / SEG_ALIGN))
        units = jnp.broadcast_to(units, (2 * SUBLANES, LANES)).astype(BF16)
        slot = _dot(units, upper)[0:1, :] * SEG_ALIGN + earlier
        p1 = jnp.sum(oh1 * slot, axis=-1, keepdims=True)
        p2 = jnp.sum(oh2 * slot, axis=-1, keepdims=True)
        meta = jnp.where(lane == META_P1, p1, 0.0)
        meta = jnp.where(lane == META_P2, p2, meta)
        meta = jnp.where(lane == META_G1, g1, meta)
        meta = jnp.where(lane == META_G2, g2, meta)
        meta_ref[rows, :] = meta
        cnt_ref[s] = jnp.broadcast_to(cnt, (SUBLANES, LANES))


def _merge(h, y_ssd, y_att, w, tm):
    t = h.shape[0]
    assert t % tm == 0 and tm % TOK_TILE == 0
    sub = tm // TOK_TILE
    last = t // tm - 1
    row = lambda n: pl.BlockSpec((tm, n), lambda i: (jnp.minimum(i, last), 0))
    routed = lambda i: jnp.maximum(i - 1, 0)
    tri = jnp.asarray(np.tril(np.ones((TOK_TILE, TOK_TILE), np.float32), -1), BF16)
    upper = jnp.asarray(np.triu(np.ones((LANES, LANES), np.float32), 1), BF16)
    consts = [w["w_out"], w["g_ffn"], w["w_route"], w["b_route"], tri, upper]
    return pl.pallas_call(
        _merge_kernel,
        grid=(t // tm + 1,),
        in_specs=[row(D_MODEL), row(SSD_DIM), row(ATT_DIM)] + [_const_spec(c.shape) for c in consts],
        out_specs=[row(D_MODEL), row(D_MODEL),
                   pl.BlockSpec((tm, LANES), lambda i: (routed(i), 0)),
                   pl.BlockSpec((sub, SUBLANES, LANES), lambda i: (routed(i), 0, 0))],
        scratch_shapes=[pltpu.VMEM((2, tm, LANES), F32)],
        out_shape=[jax.ShapeDtypeStruct((t, D_MODEL), F32),
                   jax.ShapeDtypeStruct((t, D_MODEL), BF16),
                   jax.ShapeDtypeStruct((t, LANES), F32),
                   jax.ShapeDtypeStruct((t // TOK_TILE, SUBLANES, LANES), F32)],
        compiler_params=pltpu.CompilerParams(
            dimension_semantics=("arbitrary",), vmem_limit_bytes=VMEM_LIMIT),
        name="merge_route",
    )(h, y_ssd, y_att, *consts)


def _meta_col(meta, j):
    lane = lax.broadcasted_iota(jnp.int32, meta.shape, 1)
    return jnp.sum(jnp.where(lane == j, meta, 0.0), axis=-1, keepdims=True)


def _pair_selectors(meta):
    rows = lax.broadcasted_iota(jnp.int32, (meta.shape[0], SORT_ROWS), 1)
    s1 = jnp.where(rows == _meta_col(meta, META_P1).astype(jnp.int32), 1.0, 0.0).astype(BF16)
    s2 = jnp.where(rows == _meta_col(meta, META_P2).astype(jnp.int32), 1.0, 0.0).astype(BF16)
    return s1, s2


def _segment_dmas(off_ref, cnt_ref, row_ref, b, tile_buf, hbm, sem, to_hbm, wait):
    for e in range(N_EXPERTS):
        k = b * N_EXPERTS + e
        n = pl.multiple_of(cnt_ref[k], SEG_ALIGN)
        v = tile_buf.at[pl.ds(pl.multiple_of(off_ref[k], SEG_ALIGN), n)]
        h = hbm.at[pl.ds(pl.multiple_of(row_ref[k], SEG_ALIGN), n)]
        cp = pltpu.make_async_copy(v, h, sem) if to_hbm else pltpu.make_async_copy(h, v, sem)
        if wait:
            cp.wait()
        else:
            cp.start()


def _gate_lanes(g):
    hi, mid, lo = _split3(g)
    lane = lax.broadcasted_iota(jnp.int32, (g.shape[0], LANES), 1)
    out = jnp.where(lane == 0, hi.astype(F32), 0.0)
    out = jnp.where(lane == 1, mid.astype(F32), out)
    out = jnp.where(lane == 2, lo.astype(F32), out)
    return out.astype(BF16)


def _dispatch_kernel(off_ref, cnt_ref, row_ref, tail_row_ref, tail_cnt_ref, xn_ref, meta_ref, *rest,
                     tile0, first):
    xs_ref, sort_buf, zero_buf, sem, tail_sem = rest if first else rest[1:]
    b = pl.program_id(0)
    nb = pl.num_programs(0)
    slot = b % 2
    seg = functools.partial(_segment_dmas, off_ref, cnt_ref, row_ref, hbm=xs_ref, to_hbm=True)

    def tails(wait):
        def body(e, carry):
            n = pl.multiple_of(tail_cnt_ref[e], SEG_ALIGN)

            @pl.when(n > 0)
            def _():
                cp = pltpu.make_async_copy(
                    zero_buf.at[pl.ds(0, n)],
                    xs_ref.at[pl.ds(pl.multiple_of(tail_row_ref[e], SEG_ALIGN), n)], tail_sem.at[0])
                if wait:
                    cp.wait()
                else:
                    cp.start()
            return carry

        lax.fori_loop(0, N_EXPERTS, body, 0)

    @pl.when(b >= 2)
    def _():
        seg(tile0 + b - 2, sort_buf.at[slot], sem=sem.at[slot], wait=True)

    if first:
        @pl.when(b == 0)
        def _():
            zero_buf[...] = jnp.zeros(zero_buf.shape, BF16)
            tails(wait=False)

    meta = meta_ref[...]
    s1, s2 = _pair_selectors(meta)
    sort_buf[slot, :, 0:D_MODEL] = _dot_tn(s1 + s2, xn_ref[...]).astype(BF16)
    sort_buf[slot, :, D_MODEL:XS_COLS] = (_dot_tn(s1, _gate_lanes(_meta_col(meta, META_G1)))
                                          + _dot_tn(s2, _gate_lanes(_meta_col(meta, META_G2)))).astype(BF16)
    seg(tile0 + b, sort_buf.at[slot], sem=sem.at[slot], wait=False)

    @pl.when(b == nb - 1)
    def _():
        @pl.when(b >= 1)
        def _():
            seg(tile0 + b - 1, sort_buf.at[1 - slot], sem=sem.at[1 - slot], wait=True)

        seg(tile0 + b, sort_buf.at[slot], sem=sem.at[slot], wait=True)
        if first:
            tails(wait=True)


def _expert_kernel(tile_expert_ref, n_active_ref, xs_ref, wg_ref, wu_ref, wd_ref, y_ref):
    @pl.when(pl.program_id(0) < n_active_ref[0])
    def _():
        x = xs_ref[:, 0:D_MODEL]
        gate = jnp.sum(xs_ref[:, D_MODEL:XS_COLS].astype(F32), axis=-1, keepdims=True)
        gu = _dot(x, jnp.concatenate([wg_ref[...].astype(BF16), wu_ref[...].astype(BF16)], axis=1))
        hid = _silu(gu[:, 0:EXPERT_FF]) * gu[:, EXPERT_FF:2 * EXPERT_FF]
        y_ref[...] = (gate * _dot(hid.astype(BF16), wd_ref[...].astype(BF16))).astype(BF16)


def _combine_kernel(off_ref, cnt_ref, row_ref, h1_ref, meta_ref, y_ref, o_ref, y_buf, sem, *, tile0):
    b = pl.program_id(0)
    nb = pl.num_programs(0)
    slot = b % 2
    seg = functools.partial(_segment_dmas, off_ref, cnt_ref, row_ref, hbm=y_ref, to_hbm=False)

    @pl.when(b == 0)
    def _():
        y_buf[...] = jnp.zeros(y_buf.shape, BF16)
        seg(tile0 + b, y_buf.at[slot], sem=sem.at[slot], wait=False)

    @pl.when(b + 1 < nb)
    def _():
        seg(tile0 + b + 1, y_buf.at[1 - slot], sem=sem.at[1 - slot], wait=False)

    seg(tile0 + b, y_buf.at[slot], sem=sem.at[slot], wait=True)
    s1, s2 = _pair_selectors(meta_ref[...])
    sel = s1 + s2
    o_ref[...] = h1_ref[...] + _dot(sel, y_buf[slot, 0:SORT_ROWS, :])


def _moe(parts, w, tm):
    i32 = jnp.int32
    n_tiles = [p[0].shape[0] // TOK_TILE for p in parts]
    nb = sum(n_tiles)
    t = nb * TOK_TILE
    cnt = jnp.concatenate([p[3][:, 0, :N_EXPERTS] for p in parts], axis=0).astype(i32)
    cnt_al = (cnt + (SEG_ALIGN - 1)) // SEG_ALIGN * SEG_ALIGN
    off = jnp.cumsum(cnt_al, axis=1) - cnt_al
    tot = jnp.sum(cnt_al, axis=0)
    tot_tm = (tot + (tm - 1)) // tm * tm
    start = jnp.cumsum(tot_tm) - tot_tm
    row = start[None, :] + jnp.cumsum(cnt_al, axis=0) - cnt_al
    tile_ends = jnp.cumsum(tot_tm // tm)
    max_rows = 2 * t + (SEG_ALIGN - 1) * min(N_EXPERTS * nb, 2 * t) + N_EXPERTS * (tm - SEG_ALIGN)
    nt = -(-max_rows // tm)
    tile_expert = jnp.sum((jnp.arange(nt, dtype=i32)[:, None] >= tile_ends[None, :]).astype(i32), axis=1)
    tile_expert = jnp.minimum(tile_expert, N_EXPERTS - 1)
    n_active = tile_ends[-1:].astype(i32)
    empty = cnt_al == 0
    flat = lambda a: a.reshape(-1).astype(i32)
    n_rows = flat(jnp.maximum(cnt_al, SEG_ALIGN))
    e_idx = jnp.arange(N_EXPERTS, dtype=i32)[None, :]
    spare = nt * tm + ((jnp.arange(nb, dtype=i32)[:, None] % 2) * N_EXPERTS + e_idx) * SEG_ALIGN
    dispatch_tables = [flat(off), n_rows, flat(jnp.where(empty, spare, row))]
    combine_tables = [flat(jnp.where(empty, SORT_ROWS + e_idx * SEG_ALIGN, off)), n_rows,
                      flat(jnp.where(empty, 0, row))]
    n_spare = 2 * N_EXPERTS * SEG_ALIGN
    tail_tables = [(start + tot).astype(i32), (tot_tm - tot).astype(i32)]

    tok = lambda n: pl.BlockSpec((TOK_TILE, n), lambda i, *_: (i, 0))
    hbm = pl.BlockSpec(memory_space=pl.ANY)
    n_prefetch = len(dispatch_tables) + len(tail_tables)
    xs = None
    tile0 = 0
    for (_, xn, meta, _), n in zip(parts, n_tiles):
        first = xs is None
        xs = pl.pallas_call(
            functools.partial(_dispatch_kernel, tile0=tile0, first=first),
            grid_spec=pltpu.PrefetchScalarGridSpec(
                num_scalar_prefetch=n_prefetch, grid=(n,),
                in_specs=[tok(D_MODEL), tok(LANES)] + ([] if first else [hbm]),
                out_specs=hbm,
                scratch_shapes=[pltpu.VMEM((2, SORT_ROWS, XS_COLS), BF16),
                                pltpu.VMEM((tm, XS_COLS), BF16),
                                pltpu.SemaphoreType.DMA((2,)),
                                pltpu.SemaphoreType.DMA((1,))]),
            out_shape=jax.ShapeDtypeStruct((nt * tm + n_spare, XS_COLS), BF16),
            input_output_aliases={} if first else {n_prefetch + 2: 0},
            compiler_params=pltpu.CompilerParams(
                dimension_semantics=("arbitrary",), vmem_limit_bytes=VMEM_LIMIT),
            name="dispatch",
        )(*dispatch_tables, *tail_tables, xn, meta, *([] if first else [xs]))
        tile0 += n

    act = lambda i, te, na: jnp.minimum(i, na[0] - 1)
    y = pl.pallas_call(
        _expert_kernel,
        grid_spec=pltpu.PrefetchScalarGridSpec(
            num_scalar_prefetch=2, grid=(nt,),
            in_specs=[pl.BlockSpec((tm, XS_COLS), lambda i, te, na: (act(i, te, na), 0)),
                      pl.BlockSpec((None, D_MODEL, EXPERT_FF), lambda i, te, na: (te[act(i, te, na)], 0, 0)),
                      pl.BlockSpec((None, D_MODEL, EXPERT_FF), lambda i, te, na: (te[act(i, te, na)], 0, 0)),
                      pl.BlockSpec((None, EXPERT_FF, D_MODEL), lambda i, te, na: (te[act(i, te, na)], 0, 0))],
            out_specs=pl.BlockSpec((tm, D_MODEL), lambda i, te, na: (act(i, te, na), 0))),
        out_shape=jax.ShapeDtypeStruct((nt * tm, D_MODEL), BF16),
        compiler_params=pltpu.CompilerParams(
            dimension_semantics=("arbitrary",), vmem_limit_bytes=VMEM_LIMIT),
        name="experts",
    )(tile_expert, n_active, xs, w["w_gate"], w["w_up"], w["w_down"])

    outs = []
    tile0 = 0
    for (h1, _, meta, _), n in zip(parts, n_tiles):
        outs.append(pl.pallas_call(
            functools.partial(_combine_kernel, tile0=tile0),
            grid_spec=pltpu.PrefetchScalarGridSpec(
                num_scalar_prefetch=len(combine_tables), grid=(n,),
                in_specs=[tok(D_MODEL), tok(LANES), hbm],
                out_specs=tok(D_MODEL),
                scratch_shapes=[pltpu.VMEM((2, SORT_ROWS + N_EXPERTS * SEG_ALIGN, D_MODEL), BF16),
                                pltpu.SemaphoreType.DMA((2,))]),
            out_shape=jax.ShapeDtypeStruct((n * TOK_TILE, D_MODEL), F32),
            compiler_params=pltpu.CompilerParams(
                dimension_semantics=("arbitrary",), vmem_limit_bytes=VMEM_LIMIT),
            name="combine",
        )(*combine_tables, h1, meta, y))
        tile0 += n
    return outs


def _prepare_weights(g_mix, w_in, conv_w, conv_b, dt_bias, a_log, d_skip, g_ssd_out, g_q, g_k,
                     attn_sinks, g_att_out, w_out, g_ffn, w_route_group, b_route_group,
                     w_route_expert, b_route_expert, w_gate, w_up, w_down):
    cuts = np.cumsum([0, SSD_DIM, CONV_DIM, SSD_HEADS, ATT_DIM, KV_DIM, KV_DIM])
    seg = lambda i: w_in[:, cuts[i]:cuts[i + 1]]
    pad_lanes = lambda a: jnp.pad(a, ((0, 0), (0, LANES - a.shape[1])))
    bd = np.kron(np.eye(KV_DIM // HEAD_DIM, dtype=np.float32), np.ones((HEAD_DIM, HEAD_DIM), np.float32))
    n_route = N_EXPERTS + N_EXPERT_GROUPS
    return {
        "g_mix": g_mix.reshape(1, D_MODEL),
        "w_qk": jnp.concatenate([seg(3), seg(4)], axis=1).astype(BF16),
        "w_rest": jnp.concatenate([seg(0), seg(1), seg(5), pad_lanes(jnp.tile(seg(2), (1, DT_COPIES)))],
                                  axis=1).astype(BF16),
        "g_qk": jnp.concatenate([jnp.tile(g_q, ATT_HEADS), jnp.tile(g_k, ATT_KV_HEADS)]).reshape(1, ATT_DIM + KV_DIM),
        "bd": jnp.asarray(np.concatenate([bd, bd], axis=0), BF16),
        "conv_w": conv_w, "conv_b": conv_b.reshape(1, CONV_DIM),
        "dt_bias": pad_lanes(jnp.tile(dt_bias, DT_COPIES).reshape(1, DT_COPIES * SSD_HEADS)),
        "a_log": pad_lanes(jnp.tile(a_log, DT_COPIES).reshape(1, DT_COPIES * SSD_HEADS)),
        "d_skip": jnp.repeat(d_skip, HEAD_DIM).reshape(1, SSD_DIM),
        "g_ssd_out": g_ssd_out.reshape(1, SSD_DIM),
        "attn_sinks": attn_sinks,
        "g_att_out": g_att_out.reshape(1, ATT_DIM),
        "w_out": w_out.astype(BF16),
        "g_ffn": g_ffn.reshape(1, D_MODEL),
        "w_route": pad_lanes(jnp.concatenate([w_route_expert, w_route_group], axis=1)).astype(BF16),
        "b_route": pad_lanes(jnp.concatenate([b_route_expert, b_route_group]).reshape(1, n_route)),
        "w_gate": w_gate, "w_up": w_up, "w_down": w_down,
    }


def _segment(x3, w, mk, mv, tail, h0t, kprev, vprev, L, tb_ssd, tb_att, tm, from_cache):
    S, Ls, _ = x3.shape
    x = x3.reshape(S * Ls, D_MODEL)
    z, xbc, q, k, v, dt = _project(x, w, tm)
    r3 = lambda a: a.reshape(S, Ls, a.shape[-1])
    xbc3, k3, v3 = r3(xbc), r3(k), r3(v)
    y_ssd, h_new = _ssd(xbc3, r3(z), r3(dt), tail, h0t, w, L, tb_ssd)
    if not from_cache:
        kprev, vprev = k3, v3
    y_att = _attention(r3(q), k3, v3, kprev, vprev, mk, mv, w, L, tb_att, from_cache)
    part = _merge(x, y_ssd.reshape(S * Ls, SSD_DIM), y_att.reshape(S * Ls, ATT_DIM), w, tm)
    return part, xbc3, h_new, k3, v3


def _state_to_kernel(h):
    return h.reshape(h.shape[0], SSD_DIM, D_STATE)


def _state_from_kernel(hk):
    return hk.reshape(hk.shape[0], SSD_HEADS, HEAD_DIM, D_STATE)


def kernel(x_prompt, x_sample, cache_conv, state_ssd, cache_k, cache_v, meta_tokens, g_mix, w_in, conv_w, conv_b, dt_bias, a_log, d_skip, g_ssd_out, g_q, g_k, attn_sinks, g_att_out, w_out, g_ffn, w_route_group, b_route_group, w_route_expert, b_route_expert, w_gate, w_up, w_down):
    w = _prepare_weights(g_mix[0], w_in[0], conv_w[0], conv_b[0], dt_bias[0], a_log[0], d_skip[0],
                         g_ssd_out[0], g_q[0], g_k[0], attn_sinks[0], g_att_out[0], w_out[0], g_ffn[0],
                         w_route_group[0], b_route_group[0], w_route_expert[0], b_route_expert[0],
                         w_gate[0], w_up[0], w_down[0])
    n_b = x_sample.shape[0]
    n_dec = x_sample.shape[1]

    _, m_xbc, _, mk, mv, m_dt = _project(meta_tokens, w, N_META)
    zero_tail = jnp.zeros((1, CONV_WIDTH - 1, CONV_DIM), F32)
    zero_state = jnp.zeros((1, SSD_DIM, D_STATE), F32)
    m_xbc3 = m_xbc.reshape(1, N_META, CONV_DIM)
    _, m_state = _ssd(m_xbc3, jnp.zeros((1, N_META, SSD_DIM), F32), m_dt.reshape(1, N_META, LANES),
                      zero_tail, zero_state, w, N_META, N_META)
    m_tail = m_xbc3[:, N_META - (CONV_WIDTH - 1):]

    part_p, xbc_p, st_p, k_p, v_p = _segment(
        x_prompt, w, mk, mv, m_tail, m_state, None, None,
        L=CHUNK, tb_ssd=256, tb_att=512, tm=512, from_cache=False)
    part_s, xbc_s, st_s, k_s, v_s = _segment(
        x_sample, w, mk, mv, cache_conv[0], _state_to_kernel(state_ssd[0]),
        cache_k[0].reshape(n_b, WINDOW, KV_DIM), cache_v[0].reshape(n_b, WINDOW, KV_DIM),
        L=n_dec, tb_ssd=n_dec, tb_att=n_dec, tm=256, from_cache=True)
    yp, ys = _moe([part_p, part_s], w, tm=512)
    yp = yp.reshape(x_prompt.shape)
    ys = ys.reshape(x_sample.shape)

    heads = lambda a, rows: a.reshape(a.shape[0], rows, ATT_KV_HEADS, HEAD_DIM)[None]
    return (yp, ys,
            xbc_p[:, -(CONV_WIDTH - 1):][None],
            _state_from_kernel(st_p)[None],
            heads(k_p[:, -WINDOW:], WINDOW), heads(v_p[:, -WINDOW:], WINDOW),
            xbc_s[:, -(CONV_WIDTH - 1):][None],
            _state_from_kernel(st_s)[None],
            heads(k_s, n_dec), heads(v_s, n_dec))
```

```python
import functools
import math

import numpy as np
import jax
import jax.numpy as jnp
from jax import lax
from jax.experimental import pallas as pl
from jax.experimental.pallas import tpu as pltpu

D_MODEL = 1024
CHUNK = 64
N_META = 16
HEAD_DIM = 64
ATT_HEADS = 16
ATT_KV_HEADS = 4
ATT_REP = ATT_HEADS // ATT_KV_HEADS
ATT_DIM = ATT_HEADS * HEAD_DIM
KV_DIM = ATT_KV_HEADS * HEAD_DIM
WINDOW = 128
SSD_HEADS = 16
SSD_DIM = SSD_HEADS * HEAD_DIM
SSD_GROUPS = 2
GROUP_DIM = SSD_DIM // SSD_GROUPS
D_STATE = 128
CONV_WIDTH = 4
CONV_DIM = SSD_DIM + 2 * SSD_GROUPS * D_STATE
N_EXPERT_GROUPS = 4
EXPERTS_PER_GROUP = 8
N_EXPERTS = N_EXPERT_GROUPS * EXPERTS_PER_GROUP
EXPERT_FF = D_MODEL // 4
EPS = 1e-6

LANES = 128
SUBLANES = 8
MXU_DIM = 256
DT_COPIES = 3
KEY_SPAN = 256
VMEM_LIMIT = 48 * 1024 * 1024

F32 = jnp.float32
BF16 = jnp.bfloat16
NEG_INF = float("-inf")


def _dot(a, b):
    return jnp.dot(a, b, preferred_element_type=F32)


def _dot_nt(a, b):
    return lax.dot_general(a, b, (((1,), (1,)), ((), ())), preferred_element_type=F32)


def _dot_tn(a, b):
    return lax.dot_general(a, b, (((0,), (0,)), ((), ())), preferred_element_type=F32)


def _split3(x):
    hi = x.astype(BF16)
    r = x - hi.astype(F32)
    mid = r.astype(BF16)
    lo = (r - mid.astype(F32)).astype(BF16)
    return hi, mid, lo


def _pack3(x):
    hi, mid, lo = _split3(x)
    lane = lax.broadcasted_iota(jnp.int32, x.shape, 1)
    packed = jnp.where(lane < SSD_HEADS, hi.astype(F32),
                       jnp.where(lane < 2 * SSD_HEADS, mid.astype(F32), lo.astype(F32)))
    return packed.astype(BF16)


def _sel_dot(sel, x):
    hi, mid, lo = _split3(x)
    return _dot(sel, hi) + _dot(sel, mid) + _dot(sel, lo)


def _silu(x):
    return x * (1.0 / (1.0 + jnp.exp2(x * (-math.log2(math.e)))))


def _rms(x, g):
    ms = jnp.mean(x * x, axis=-1, keepdims=True)
    return x * lax.rsqrt(ms + EPS) * g


def _const_spec(shape):
    n = len(shape)
    return pl.BlockSpec(shape, lambda *_: (0,) * n, pipeline_mode=pl.Buffered(1))


def _proj_kernel(x_ref, gmix_ref, wqk_ref, wrest_ref, gqk_ref, bd_ref,
                 z_ref, xbc_ref, q_ref, k_ref, v_ref, dt_ref):
    tm = x_ref.shape[0]
    xn = _rms(x_ref[...], gmix_ref[...]).astype(BF16)
    qk = _dot(xn, wqk_ref[...])
    n_slices = (ATT_DIM + KV_DIM) // KV_DIM
    sq = jnp.concatenate([qk[:, j * KV_DIM:(j + 1) * KV_DIM] for j in range(n_slices)], axis=0)
    sq = sq * sq
    hi = sq.astype(BF16)
    lo = (sq - hi.astype(F32)).astype(BF16)
    ms = _dot(jnp.concatenate([hi, lo], axis=1), bd_ref[...]) * (1.0 / HEAD_DIM)
    inv = lax.rsqrt(ms + EPS)
    for j in range(n_slices):
        sl = slice(j * KV_DIM, (j + 1) * KV_DIM)
        normed = qk[:, sl] * inv[j * tm:(j + 1) * tm, :] * gqk_ref[:, sl]
        if j < ATT_DIM // KV_DIM:
            q_ref[:, sl] = (normed * HEAD_DIM ** -0.5).astype(BF16)
        else:
            k_ref[...] = normed
    rest = _dot(xn, wrest_ref[...])
    cuts = np.cumsum([0, SSD_DIM, CONV_DIM, KV_DIM, LANES])
    for ref, lo_c, hi_c in zip((z_ref, xbc_ref, v_ref, dt_ref), cuts[:-1], cuts[1:]):
        ref[...] = rest[:, lo_c:hi_c]


def _project(x, w, tm):
    t = x.shape[0]
    assert t % tm == 0
    row = lambda n: pl.BlockSpec((tm, n), lambda i: (i, 0))
    ins = [x, w["g_mix"], w["w_qk"], w["w_rest"], w["g_qk"], w["bd"]]
    in_specs = [row(D_MODEL)] + [_const_spec(a.shape) for a in ins[1:]]
    out_dims = (SSD_DIM, CONV_DIM, ATT_DIM, KV_DIM, KV_DIM, LANES)
    out_dtypes = (F32, F32, BF16, F32, F32, F32)
    return pl.pallas_call(
        _proj_kernel,
        grid=(t // tm,),
        in_specs=in_specs,
        out_specs=[row(n) for n in out_dims],
        out_shape=[jax.ShapeDtypeStruct((t, n), d) for n, d in zip(out_dims, out_dtypes)],
        compiler_params=pltpu.CompilerParams(
            dimension_semantics=("arbitrary",), vmem_limit_bytes=VMEM_LIMIT),
        name="projection",
    )(*ins)


N_SSD_STREAM_INPUTS = 5
N_SSD_CONSTS = 9


def _ssd_kernel(*refs, L, TB, SB):
    streamed = lambda k: k < N_SSD_STREAM_INPUTS or k >= N_SSD_STREAM_INPUTS + N_SSD_CONSTS
    for s in range(SB):
        _ssd_stream(*[r.at[s] if streamed(k) else r for k, r in enumerate(refs)], L=L, TB=TB)


def _ssd_stream(xbc_ref, z_ref, dt_ref, tail_ref, h0_ref, convw_ref, convb_ref, dtb_ref,
                alog_ref, dskip_ref, gout_ref, ehead_ref, epos_ref, tri_ref,
                y_ref, hout_ref, buf_ref, act_ref, dts_ref, ys_ref, st_ref, *, L, TB):
    hp = LANES // L
    n_lane_tiles = SSD_HEADS // hp
    hq = MXU_DIM // L
    n_tiles = SSD_HEADS // hq
    tile_w = hq * HEAD_DIM
    b = pl.program_id(1)
    pad = SUBLANES - (CONV_WIDTH - 1)

    @pl.when(b == 0)
    def _():
        buf_ref[0:SUBLANES, :] = jnp.zeros((SUBLANES, CONV_DIM), F32)
        buf_ref[pad:SUBLANES, :] = tail_ref[...]
        st_ref[...] = h0_ref[...].T

    buf_ref[SUBLANES:SUBLANES + TB, :] = xbc_ref[...]
    rows = buf_ref[...]
    acc = convb_ref[...] + convw_ref[CONV_WIDTH - 1:CONV_WIDTH, :] * rows[SUBLANES:, :]
    for j in range(CONV_WIDTH - 1):
        shifted = pltpu.roll(rows, CONV_WIDTH - 1 - j, axis=0)[SUBLANES:, :]
        acc = acc + convw_ref[j:j + 1, :] * shifted
    act_ref[...] = _silu(acc)
    buf_ref[pad:SUBLANES, :] = buf_ref[TB + pad:TB + SUBLANES, :]

    dtx = dt_ref[...] + dtb_ref[...]
    dts_ref[...] = jnp.maximum(dtx, 0.0) + jnp.log(1.0 + jnp.exp(-jnp.abs(dtx)))

    a_row = -jnp.exp(alog_ref[...])
    ehead = ehead_ref[...]
    epos = epos_ref[...]
    tri = tri_ref[...]
    dskip = dskip_ref[...]
    row_i = lax.broadcasted_iota(jnp.int32, (L, n_lane_tiles * LANES), 0)
    col_i = lax.broadcasted_iota(jnp.int32, (L, n_lane_tiles * LANES), 1)
    pos_i = col_i % L
    diag_mask = row_i == pos_i
    causal_mask = row_i >= pos_i
    colhead = lax.broadcasted_iota(jnp.int32, (L, tile_w), 1) // HEAD_DIM

    def chunk(c):
        r0 = pl.multiple_of(c * L, L)
        dtc = dts_ref[pl.ds(r0, L), :]
        cs = _sel_dot(tri, dtc * a_row)
        cs16 = _pack3(cs)
        dt_b = _dot(_pack3(dtc), ehead)
        cs_b = _dot(cs16, ehead)
        cs_last_b = cs_b[L - 1:L, :]
        ecs_b = jnp.exp(cs_b)
        dec_end_b = jnp.exp(cs_last_b - cs_b)
        bdec_b = jnp.exp(cs_last_b)
        cs_col = cs_b if L == HEAD_DIM else _dot(cs16, epos)
        cs_row = jnp.sum(jnp.where(diag_mask, cs_col, 0.0), axis=0, keepdims=True)
        lmat = jnp.exp(jnp.where(causal_mask, cs_col - cs_row, NEG_INF))

        xs = act_ref[pl.ds(r0, L), 0:SSD_DIM]
        xdt_f = xs * dt_b
        xdt = xdt_f.astype(BF16)
        wx = (xdt_f * dec_end_b).astype(BF16)
        st = st_ref[...]
        st16 = st.astype(BF16)

        y_diag = []
        y_off = []
        s_new = []
        cb = [None] * SSD_GROUPS
        for g in range(SSD_GROUPS):
            bg = act_ref[pl.ds(r0, L), SSD_DIM + g * D_STATE:SSD_DIM + (g + 1) * D_STATE].astype(BF16)
            cg = act_ref[pl.ds(r0, L), SSD_DIM + (SSD_GROUPS + g) * D_STATE:
                         SSD_DIM + (SSD_GROUPS + g + 1) * D_STATE].astype(BF16)
            cb[g] = _dot_nt(cg, jnp.concatenate([bg] * hp, axis=0))
            gsl = slice(g * GROUP_DIM, (g + 1) * GROUP_DIM)
            y_off.append(_dot(cg, st16[:, gsl]))
            s_new.append(_dot_tn(bg, wx[:, gsl]))
        lane_tiles = MXU_DIM // LANES
        for t in range(n_tiles):
            cbt = jnp.concatenate(
                [cb[((t * lane_tiles + j) * hp * HEAD_DIM) // GROUP_DIM] for j in range(lane_tiles)], axis=1)
            gmat = (lmat[:, t * MXU_DIM:(t + 1) * MXU_DIM] * cbt).astype(BF16)
            xt = xdt[:, t * tile_w:(t + 1) * tile_w]
            rhs = jnp.concatenate(
                [jnp.where(colhead == hh, xt, jnp.zeros_like(xt)) for hh in range(hq)], axis=0)
            y_diag.append(_dot(gmat, rhs))
        y = (jnp.concatenate(y_diag, axis=1) + jnp.concatenate(y_off, axis=1) * ecs_b
             + dskip * xs)
        ys_ref[pl.ds(r0, L), :] = y
        st_ref[...] = bdec_b * st + jnp.concatenate(s_new, axis=1)

    n_chunks = TB // L
    unroll = 4 if n_chunks % 4 == 0 else 1

    def chunks(i, carry):
        for u in range(unroll):
            chunk(i * unroll + u)
        return carry

    lax.fori_loop(0, n_chunks // unroll, chunks, 0)

    yg = ys_ref[...] * _silu(z_ref[...])
    y_ref[...] = _rms(yg, gout_ref[...])

    @pl.when(b == pl.num_programs(1) - 1)
    def _():
        hout_ref[...] = st_ref[...].T


def _ssd(xbc, z, dt, tail, h0t, w, L, TB, SB=1):
    S, Ls, _ = xbc.shape
    assert Ls % TB == 0 and TB % L == 0 and LANES % L == 0 and S % SB == 0
    nb = Ls // TB
    assert SB == 1 or nb == 1
    hp = LANES // L
    n_tiles = SSD_HEADS // hp
    per_stream = lambda a: (lambda s, b: (s, 0, 0)) if a.shape[0] == S and S > 1 else (lambda s, b: (0, 0, 0))
    seq = lambda n: pl.BlockSpec((SB, TB, n), lambda s, b: (s, b, 0))
    ehead = np.zeros((LANES, SSD_DIM), np.float32)
    epos = np.zeros((LANES, n_tiles * LANES), np.float32)
    for part in range(DT_COPIES):
        for h in range(SSD_HEADS):
            ehead[part * SSD_HEADS + h, h * HEAD_DIM:(h + 1) * HEAD_DIM] = 1.0
            epos[part * SSD_HEADS + h, h * L:(h + 1) * L] = 1.0
    tri = np.tril(np.ones((L, L), np.float32))
    consts = [w["conv_w"], w["conv_b"], w["dt_bias"], w["a_log"], w["d_skip"], w["g_ssd_out"],
              jnp.asarray(ehead, BF16), jnp.asarray(epos, BF16), jnp.asarray(tri, BF16)]
    in_specs = [seq(CONV_DIM), seq(SSD_DIM), seq(LANES),
                pl.BlockSpec((SB, CONV_WIDTH - 1, CONV_DIM), per_stream(tail)),
                pl.BlockSpec((SB, SSD_DIM, D_STATE), per_stream(h0t))]
    assert len(in_specs) == N_SSD_STREAM_INPUTS and len(consts) == N_SSD_CONSTS
    in_specs += [_const_spec(c.shape) for c in consts]
    return pl.pallas_call(
        functools.partial(_ssd_kernel, L=L, TB=TB, SB=SB),
        grid=(S // SB, nb),
        in_specs=in_specs,
        out_specs=[seq(SSD_DIM), pl.BlockSpec((SB, SSD_DIM, D_STATE), lambda s, b: (s, 0, 0))],
        out_shape=[jax.ShapeDtypeStruct((S, Ls, SSD_DIM), F32),
                   jax.ShapeDtypeStruct((S, SSD_DIM, D_STATE), F32)],
        scratch_shapes=[pltpu.VMEM((SB, SUBLANES + TB, CONV_DIM), F32),
                        pltpu.VMEM((SB, TB, CONV_DIM), F32),
                        pltpu.VMEM((SB, TB, LANES), F32),
                        pltpu.VMEM((SB, TB, SSD_DIM), F32),
                        pltpu.VMEM((SB, D_STATE, SSD_DIM), F32)],
        compiler_params=pltpu.CompilerParams(
            dimension_semantics=("arbitrary", "arbitrary"), vmem_limit_bytes=VMEM_LIMIT),
        name="ssd",
    )(xbc, z, dt, tail, h0t, *consts)


def _split_heads(x):
    lane = lax.broadcasted_iota(jnp.int32, (x.shape[0], LANES), 1)
    low = lane < HEAD_DIM
    lo, hi = [], []
    for t in range(KV_DIM // LANES):
        tile = x[:, t * LANES:(t + 1) * LANES]
        swapped = pltpu.roll(tile, HEAD_DIM, axis=1)
        lo += [jnp.where(low, tile, 0.0), jnp.where(low, swapped, 0.0)]
        hi += [jnp.where(low, 0.0, swapped), jnp.where(low, 0.0, tile)]
    return jnp.concatenate(lo, axis=1).astype(BF16), jnp.concatenate(hi, axis=1).astype(BF16)


def _attn_kernel(*refs, L, TB, SB, from_cache):
    for s in range(SB):
        _attn_stream(*[r if k == 0 or 6 <= k <= 9 else r.at[s] for k, r in enumerate(refs)],
                     L=L, TB=TB, from_cache=from_cache)


def _attn_stream(sink_ref, q_ref, k_ref, v_ref, kp_ref, vp_ref, mk_ref, mv_ref, bias_ref, g_ref,
                 o_ref, k_lo, k_hi, v_lo, v_hi, ybuf, s_buf, e_buf, sink_buf, *, L, TB, from_cache):
    b = pl.program_id(1)
    bufs = (k_lo, k_hi, v_lo, v_hi)
    if from_cache:
        window = _split_heads(kp_ref[...]) + _split_heads(vp_ref[...])
        for buf, val in zip(bufs, window):
            buf[0:WINDOW, :] = val
    else:
        @pl.when(b == 0)
        def _():
            for buf in bufs:
                buf[0:WINDOW, :] = jnp.zeros((WINDOW, buf.shape[1]), BF16)

        @pl.when(b > 0)
        def _():
            for buf in bufs:
                buf[0:WINDOW, :] = buf[TB:TB + WINDOW, :]

    for buf, val in zip(bufs, _split_heads(k_ref[...]) + _split_heads(v_ref[...])):
        buf[WINDOW:WINDOW + TB, :] = val
    n_pad = KEY_SPAN - WINDOW - L - N_META
    pad_rows = jnp.zeros((n_pad, 2 * KV_DIM), BF16)
    tails = [jnp.concatenate([t, pad_rows], axis=0)
             for t in _split_heads(mk_ref[...]) + _split_heads(mv_ref[...])]
    lane = lax.broadcasted_iota(jnp.int32, (2 * L, 2 * KEY_SPAN), 1) % KEY_SPAN
    lane_h = lax.broadcasted_iota(jnp.int32, (L, LANES), 1)
    lane_v = lax.broadcasted_iota(jnp.int32, (KEY_SPAN, LANES), 1)
    ones_lo = jnp.where(lane_v < HEAD_DIM, 1.0, 0.0).astype(BF16)
    ones_hi = jnp.where(lane_v < HEAD_DIM, 0.0, 1.0).astype(BF16)
    n_chunks = TB // L
    pairs_per_group = ATT_REP // 2

    def chunk_keys(c):
        r0 = c * L
        return [jnp.concatenate([buf[r0:r0 + WINDOW, :], buf[r0 + WINDOW:r0 + WINDOW + L, :], tail], axis=0)
                for buf, tail in zip(bufs, tails)]

    for c in range(n_chunks):
        r0 = c * L
        keys = chunk_keys(c)
        for g in range(ATT_KV_HEADS):
            gl = slice(g * LANES, (g + 1) * LANES)
            q4 = jnp.concatenate([q_ref[r0:r0 + L, p * LANES:(p + 1) * LANES]
                                  for p in range(pairs_per_group * g, pairs_per_group * (g + 1))], axis=0)
            s = _dot_nt(q4, jnp.concatenate([keys[0][:, gl], keys[1][:, gl]], axis=0)) + bias_ref[g]
            if not from_cache and r0 < WINDOW:
                n_invalid = WINDOW - (b * TB + r0)
                s = jnp.where(lane < n_invalid, NEG_INF, s)
            s_buf[c * ATT_KV_HEADS + g] = s

    for c in range(n_chunks):
        for pair in range(ATT_HEADS // 2):
            slot = c * ATT_KV_HEADS + pair // pairs_per_group
            rows = slice((pair % pairs_per_group) * L, (pair % pairs_per_group + 1) * L)
            sink_terms = []
            for half in range(2):
                cols = slice(half * KEY_SPAN, (half + 1) * KEY_SPAN)
                s = s_buf[slot, rows, cols]
                sink = sink_ref[2 * pair + half]
                m = jnp.maximum(jnp.max(s, axis=-1, keepdims=True), sink)
                e_buf[slot, rows, cols] = jnp.exp(s - m).astype(BF16)
                sink_terms.append(jnp.exp(sink - m))
            sink_buf[c * (ATT_HEADS // 2) + pair] = jnp.where(lane_h < HEAD_DIM, sink_terms[0], sink_terms[1])

    for c in range(n_chunks):
        r0 = c * L
        keys = chunk_keys(c)
        for g in range(ATT_KV_HEADS):
            gl = slice(g * LANES, (g + 1) * LANES)
            vcat = jnp.concatenate([jnp.concatenate([keys[2][:, gl], ones_lo], axis=1),
                                    jnp.concatenate([keys[3][:, gl], ones_hi], axis=1)], axis=0)
            ov = _dot(e_buf[c * ATT_KV_HEADS + g], vcat)
            for j in range(pairs_per_group):
                pair = pairs_per_group * g + j
                den = ov[j * L:(j + 1) * L, LANES:2 * LANES] + sink_buf[c * (ATT_HEADS // 2) + pair]
                ybuf[r0:r0 + L, pair * LANES:(pair + 1) * LANES] = ov[j * L:(j + 1) * L, 0:LANES] * (1.0 / den)
    o_ref[...] = _rms(ybuf[...], g_ref[...])


def _attention(q, k, v, kprev, vprev, mk, mv, w, L, TB, from_cache, SB=1):
    S, Ls, _ = q.shape
    assert Ls % TB == 0 and TB % L == 0 and (from_cache or TB >= WINDOW) and S % SB == 0
    nb = Ls // TB
    assert SB == 1 or (from_cache and nb == 1)
    seq = lambda n: pl.BlockSpec((SB, TB, n), lambda s, b: (s, b, 0))
    if not from_cache:
        kprev = vprev = jnp.zeros((1, WINDOW, KV_DIM), F32)
    prev = pl.BlockSpec((SB, WINDOW, KV_DIM),
                        (lambda s, b: (s, 0, 0)) if from_cache else (lambda s, b: (0, 0, 0)))
    slopes = 2.0 ** (-8.0 * np.arange(1, ATT_HEADS + 1, dtype=np.float64) / ATT_HEADS)
    dist = np.abs(WINDOW + np.arange(L)[:, None] - np.arange(WINDOW + L)[None, :])
    bias = np.full((ATT_HEADS, L, KEY_SPAN), NEG_INF, np.float32)
    bias[:, :, :WINDOW + L] = -slopes[:, None, None] * dist[None]
    bias[:, :, WINDOW + L:WINDOW + L + N_META] = 0.0
    bias = bias.reshape(ATT_KV_HEADS, ATT_REP // 2, 2, L, KEY_SPAN).transpose(0, 1, 3, 2, 4)
    bias = jnp.asarray(bias.reshape(ATT_KV_HEADS, ATT_REP // 2 * L, 2 * KEY_SPAN))
    n_chunks = TB // L
    in_specs = [pl.BlockSpec(memory_space=pltpu.SMEM),
                seq(ATT_DIM), seq(KV_DIM), seq(KV_DIM), prev, prev,
                _const_spec(mk.shape), _const_spec(mv.shape), _const_spec(bias.shape),
                _const_spec(w["g_att_out"].shape)]
    return pl.pallas_call(
        functools.partial(_attn_kernel, L=L, TB=TB, SB=SB, from_cache=from_cache),
        grid=(S // SB, nb),
        in_specs=in_specs,
        out_specs=seq(ATT_DIM),
        out_shape=jax.ShapeDtypeStruct((S, Ls, ATT_DIM), F32),
        scratch_shapes=[pltpu.VMEM((SB, WINDOW + TB, 2 * KV_DIM), BF16) for _ in range(4)]
                       + [pltpu.VMEM((SB, TB, ATT_DIM), F32),
                          pltpu.VMEM((SB, n_chunks * ATT_KV_HEADS, 2 * L, 2 * KEY_SPAN), F32),
                          pltpu.VMEM((SB, n_chunks * ATT_KV_HEADS, 2 * L, 2 * KEY_SPAN), BF16),
                          pltpu.VMEM((SB, n_chunks * ATT_HEADS // 2, L, LANES), F32)],
        compiler_params=pltpu.CompilerParams(
            dimension_semantics=("arbitrary", "arbitrary"), vmem_limit_bytes=VMEM_LIMIT),
        name="attention",
    )(w["attn_sinks"], q, k, v, kprev, vprev, mk, mv, bias, w["g_att_out"])


TOK_TILE = 256
SEG_ALIGN = 2 * SUBLANES
SORT_ROWS = 2 * TOK_TILE + N_EXPERTS * SEG_ALIGN
XS_COLS = D_MODEL + LANES
META_P1, META_P2, META_G1, META_G2 = N_EXPERTS, N_EXPERTS + 1, N_EXPERTS + 2, N_EXPERTS + 3


def _merge_kernel(h_ref, ys_ref, ya_ref, wo_ref, gffn_ref, wr_ref, br_ref, tri_ref, upper_ref,
                  h1_ref, xn_ref, meta_ref, cnt_ref, logit_buf):
    i = pl.program_id(0)

    @pl.when(i == 0)
    def _():
        logit_buf[...] = jnp.zeros(logit_buf.shape, F32)

    ycat = jnp.concatenate([ys_ref[...].astype(BF16), ya_ref[...].astype(BF16)], axis=1)
    h1 = h_ref[...] + _dot(ycat, wo_ref[...])
    h1_ref[...] = h1
    xn = _rms(h1, gffn_ref[...]).astype(BF16)
    xn_ref[...] = xn
    all_logits = logit_buf[(i + 1) % 2]
    logit_buf[i % 2] = _dot(xn, wr_ref[...]) + br_ref[...]
    tri = tri_ref[...]
    upper = upper_ref[...]
    lane = lax.broadcasted_iota(jnp.int32, (TOK_TILE, LANES), 1)
    big = jnp.int32(LANES)
    gmask = (lane >= N_EXPERTS) & (lane < N_EXPERTS + N_EXPERT_GROUPS)
    for s in range(all_logits.shape[0] // TOK_TILE):
        rows = slice(s * TOK_TILE, (s + 1) * TOK_TILE)
        logits = all_logits[rows]

        def top1(mask):
            mval = jnp.max(jnp.where(mask, logits, NEG_INF), axis=-1, keepdims=True)
            idx = jnp.min(jnp.where(mask & (logits == mval), lane, big), axis=-1, keepdims=True)
            return mval, idx

        gmax, gidx = top1(gmask)
        gate_g = 1.0 / jnp.sum(jnp.where(gmask, jnp.exp(logits - gmax), 0.0), axis=-1, keepdims=True)
        grp = gidx - N_EXPERTS
        emask = (lane // EXPERTS_PER_GROUP) == grp
        v1, i1 = top1(emask)
        v2, i2 = top1(emask & (lane != i1))
        e2 = jnp.exp(v2 - v1)
        g1 = gate_g / (1.0 + e2)
        g2 = gate_g * e2 / (1.0 + e2)
        oh1 = jnp.where(lane == i1, 1.0, 0.0)
        oh2 = jnp.where(lane == i2, 1.0, 0.0)
        oh = oh1 + oh2
        earlier = _dot(tri, oh.astype(BF16))
        cnt = jnp.sum(oh, axis=0, keepdims=True)
        units = jnp.floor((cnt + (SEG_ALIGN - 1)) * (1.0 / SEG_ALIGN))
        units = jnp.broadcast_to(units, (2 * SUBLANES, LANES)).astype(BF16)
        slot = _dot(units, upper)[0:1, :] * SEG_ALIGN + earlier
        p1 = jnp.sum(oh1 * slot, axis=-1, keepdims=True)
        p2 = jnp.sum(oh2 * slot, axis=-1, keepdims=True)
        meta = jnp.where(lane == META_P1, p1, 0.0)
        meta = jnp.where(lane == META_P2, p2, meta)
        meta = jnp.where(lane == META_G1, g1, meta)
        meta = jnp.where(lane == META_G2, g2, meta)
        meta_ref[rows, :] = meta
        cnt_ref[s] = jnp.broadcast_to(cnt, (SUBLANES, LANES))


def _merge(h, y_ssd, y_att, w, tm):
    t = h.shape[0]
    assert t % tm == 0 and tm % TOK_TILE == 0
    sub = tm // TOK_TILE
    last = t // tm - 1
    row = lambda n: pl.BlockSpec((tm, n), lambda i: (jnp.minimum(i, last), 0))
    routed = lambda i: jnp.maximum(i - 1, 0)
    tri = jnp.asarray(np.tril(np.ones((TOK_TILE, TOK_TILE), np.float32), -1), BF16)
    upper = jnp.asarray(np.triu(np.ones((LANES, LANES), np.float32), 1), BF16)
    consts = [w["w_out"], w["g_ffn"], w["w_route"], w["b_route"], tri, upper]
    return pl.pallas_call(
        _merge_kernel,
        grid=(t // tm + 1,),
        in_specs=[row(D_MODEL), row(SSD_DIM), row(ATT_DIM)] + [_const_spec(c.shape) for c in consts],
        out_specs=[row(D_MODEL), row(D_MODEL),
                   pl.BlockSpec((tm, LANES), lambda i: (routed(i), 0)),
                   pl.BlockSpec((sub, SUBLANES, LANES), lambda i: (routed(i), 0, 0))],
        scratch_shapes=[pltpu.VMEM((2, tm, LANES), F32)],
        out_shape=[jax.ShapeDtypeStruct((t, D_MODEL), F32),
                   jax.ShapeDtypeStruct((t, D_MODEL), BF16),
                   jax.ShapeDtypeStruct((t, LANES), F32),
                   jax.ShapeDtypeStruct((t // TOK_TILE, SUBLANES, LANES), F32)],
        compiler_params=pltpu.CompilerParams(
            dimension_semantics=("arbitrary",), vmem_limit_bytes=VMEM_LIMIT),
        name="merge_route",
    )(h, y_ssd, y_att, *consts)


def _meta_col(meta, j):
    lane = lax.broadcasted_iota(jnp.int32, meta.shape, 1)
    return jnp.sum(jnp.where(lane == j, meta, 0.0), axis=-1, keepdims=True)


def _pair_selectors(meta):
    rows = lax.broadcasted_iota(jnp.int32, (meta.shape[0], SORT_ROWS), 1)
    s1 = jnp.where(rows == _meta_col(meta, META_P1).astype(jnp.int32), 1.0, 0.0).astype(BF16)
    s2 = jnp.where(rows == _meta_col(meta, META_P2).astype(jnp.int32), 1.0, 0.0).astype(BF16)
    return s1, s2


def _segment_dmas(off_ref, cnt_ref, row_ref, b, tile_buf, hbm, sem, to_hbm, wait):
    for e in range(N_EXPERTS):
        k = b * N_EXPERTS + e
        n = pl.multiple_of(cnt_ref[k], SEG_ALIGN)
        v = tile_buf.at[pl.ds(pl.multiple_of(off_ref[k], SEG_ALIGN), n)]
        h = hbm.at[pl.ds(pl.multiple_of(row_ref[k], SEG_ALIGN), n)]
        cp = pltpu.make_async_copy(v, h, sem) if to_hbm else pltpu.make_async_copy(h, v, sem)
        if wait:
            cp.wait()
        else:
            cp.start()


def _gate_lanes(g):
    hi, mid, lo = _split3(g)
    lane = lax.broadcasted_iota(jnp.int32, (g.shape[0], LANES), 1)
    out = jnp.where(lane == 0, hi.astype(F32), 0.0)
    out = jnp.where(lane == 1, mid.astype(F32), out)
    out = jnp.where(lane == 2, lo.astype(F32), out)
    return out.astype(BF16)


def _dispatch_kernel(off_ref, cnt_ref, row_ref, tail_row_ref, tail_cnt_ref, xn_ref, meta_ref, *rest,
                     tile0, first):
    xs_ref, sort_buf, zero_buf, sem, tail_sem = rest if first else rest[1:]
    b = pl.program_id(0)
    nb = pl.num_programs(0)
    slot = b % 2
    seg = functools.partial(_segment_dmas, off_ref, cnt_ref, row_ref, hbm=xs_ref, to_hbm=True)

    def tails(wait):
        def body(e, carry):
            n = pl.multiple_of(tail_cnt_ref[e], SEG_ALIGN)

            @pl.when(n > 0)
            def _():
                cp = pltpu.make_async_copy(
                    zero_buf.at[pl.ds(0, n)],
                    xs_ref.at[pl.ds(pl.multiple_of(tail_row_ref[e], SEG_ALIGN), n)], tail_sem.at[0])
                if wait:
                    cp.wait()
                else:
                    cp.start()
            return carry

        lax.fori_loop(0, N_EXPERTS, body, 0)

    @pl.when(b >= 2)
    def _():
        seg(tile0 + b - 2, sort_buf.at[slot], sem=sem.at[slot], wait=True)

    if first:
        @pl.when(b == 0)
        def _():
            zero_buf[...] = jnp.zeros(zero_buf.shape, BF16)
            tails(wait=False)

    meta = meta_ref[...]
    s1, s2 = _pair_selectors(meta)
    sort_buf[slot, :, 0:D_MODEL] = _dot_tn(s1 + s2, xn_ref[...]).astype(BF16)
    sort_buf[slot, :, D_MODEL:XS_COLS] = (_dot_tn(s1, _gate_lanes(_meta_col(meta, META_G1)))
                                          + _dot_tn(s2, _gate_lanes(_meta_col(meta, META_G2)))).astype(BF16)
    seg(tile0 + b, sort_buf.at[slot], sem=sem.at[slot], wait=False)

    @pl.when(b == nb - 1)
    def _():
        @pl.when(b >= 1)
        def _():
            seg(tile0 + b - 1, sort_buf.at[1 - slot], sem=sem.at[1 - slot], wait=True)

        seg(tile0 + b, sort_buf.at[slot], sem=sem.at[slot], wait=True)
        if first:
            tails(wait=True)


def _expert_kernel(tile_expert_ref, n_active_ref, xs_ref, wg_ref, wu_ref, wd_ref, y_ref):
    @pl.when(pl.program_id(0) < n_active_ref[0])
    def _():
        x = xs_ref[:, 0:D_MODEL]
        gate = jnp.sum(xs_ref[:, D_MODEL:XS_COLS].astype(F32), axis=-1, keepdims=True)
        gu = _dot(x, jnp.concatenate([wg_ref[...].astype(BF16), wu_ref[...].astype(BF16)], axis=1))
        hid = _silu(gu[:, 0:EXPERT_FF]) * gu[:, EXPERT_FF:2 * EXPERT_FF]
        y_ref[...] = (gate * _dot(hid.astype(BF16), wd_ref[...].astype(BF16))).astype(BF16)


def _combine_kernel(off_ref, cnt_ref, row_ref, h1_ref, meta_ref, y_ref, o_ref, y_buf, sem, *, tile0):
    b = pl.program_id(0)
    nb = pl.num_programs(0)
    slot = b % 2
    seg = functools.partial(_segment_dmas, off_ref, cnt_ref, row_ref, hbm=y_ref, to_hbm=False)

    @pl.when(b == 0)
    def _():
        y_buf[...] = jnp.zeros(y_buf.shape, BF16)
        seg(tile0 + b, y_buf.at[slot], sem=sem.at[slot], wait=False)

    @pl.when(b + 1 < nb)
    def _():
        seg(tile0 + b + 1, y_buf.at[1 - slot], sem=sem.at[1 - slot], wait=False)

    seg(tile0 + b, y_buf.at[slot], sem=sem.at[slot], wait=True)
    s1, s2 = _pair_selectors(meta_ref[...])
    sel = s1 + s2
    o_ref[...] = h1_ref[...] + _dot(sel, y_buf[slot, 0:SORT_ROWS, :])


def _moe(parts, w, tm):
    i32 = jnp.int32
    n_tiles = [p[0].shape[0] // TOK_TILE for p in parts]
    nb = sum(n_tiles)
    t = nb * TOK_TILE
    cnt = jnp.concatenate([p[3][:, 0, :N_EXPERTS] for p in parts], axis=0).astype(i32)
    cnt_al = (cnt + (SEG_ALIGN - 1)) // SEG_ALIGN * SEG_ALIGN
    off = jnp.cumsum(cnt_al, axis=1) - cnt_al
    tot = jnp.sum(cnt_al, axis=0)
    tot_tm = (tot + (tm - 1)) // tm * tm
    start = jnp.cumsum(tot_tm) - tot_tm
    row = start[None, :] + jnp.cumsum(cnt_al, axis=0) - cnt_al
    tile_ends = jnp.cumsum(tot_tm // tm)
    max_rows = 2 * t + (SEG_ALIGN - 1) * min(N_EXPERTS * nb, 2 * t) + N_EXPERTS * (tm - SEG_ALIGN)
    nt = -(-max_rows // tm)
    tile_expert = jnp.sum((jnp.arange(nt, dtype=i32)[:, None] >= tile_ends[None, :]).astype(i32), axis=1)
    tile_expert = jnp.minimum(tile_expert, N_EXPERTS - 1)
    n_active = tile_ends[-1:].astype(i32)
    empty = cnt_al == 0
    flat = lambda a: a.reshape(-1).astype(i32)
    n_rows = flat(jnp.maximum(cnt_al, SEG_ALIGN))
    e_idx = jnp.arange(N_EXPERTS, dtype=i32)[None, :]
    spare = nt * tm + ((jnp.arange(nb, dtype=i32)[:, None] % 2) * N_EXPERTS + e_idx) * SEG_ALIGN
    dispatch_tables = [flat(off), n_rows, flat(jnp.where(empty, spare, row))]
    combine_tables = [flat(jnp.where(empty, SORT_ROWS + e_idx * SEG_ALIGN, off)), n_rows,
                      flat(jnp.where(empty, 0, row))]
    n_spare = 2 * N_EXPERTS * SEG_ALIGN
    tail_tables = [(start + tot).astype(i32), (tot_tm - tot).astype(i32)]

    tok = lambda n: pl.BlockSpec((TOK_TILE, n), lambda i, *_: (i, 0))
    hbm = pl.BlockSpec(memory_space=pl.ANY)
    n_prefetch = len(dispatch_tables) + len(tail_tables)
    xs = None
    tile0 = 0
    for (_, xn, meta, _), n in zip(parts, n_tiles):
        first = xs is None
        xs = pl.pallas_call(
            functools.partial(_dispatch_kernel, tile0=tile0, first=first),
            grid_spec=pltpu.PrefetchScalarGridSpec(
                num_scalar_prefetch=n_prefetch, grid=(n,),
                in_specs=[tok(D_MODEL), tok(LANES)] + ([] if first else [hbm]),
                out_specs=hbm,
                scratch_shapes=[pltpu.VMEM((2, SORT_ROWS, XS_COLS), BF16),
                                pltpu.VMEM((tm, XS_COLS), BF16),
                                pltpu.SemaphoreType.DMA((2,)),
                                pltpu.SemaphoreType.DMA((1,))]),
            out_shape=jax.ShapeDtypeStruct((nt * tm + n_spare, XS_COLS), BF16),
            input_output_aliases={} if first else {n_prefetch + 2: 0},
            compiler_params=pltpu.CompilerParams(
                dimension_semantics=("arbitrary",), vmem_limit_bytes=VMEM_LIMIT),
            name="dispatch",
        )(*dispatch_tables, *tail_tables, xn, meta, *([] if first else [xs]))
        tile0 += n

    act = lambda i, te, na: jnp.minimum(i, na[0] - 1)
    y = pl.pallas_call(
        _expert_kernel,
        grid_spec=pltpu.PrefetchScalarGridSpec(
            num_scalar_prefetch=2, grid=(nt,),
            in_specs=[pl.BlockSpec((tm, XS_COLS), lambda i, te, na: (act(i, te, na), 0)),
                      pl.BlockSpec((None, D_MODEL, EXPERT_FF), lambda i, te, na: (te[act(i, te, na)], 0, 0)),
                      pl.BlockSpec((None, D_MODEL, EXPERT_FF), lambda i, te, na: (te[act(i, te, na)], 0, 0)),
                      pl.BlockSpec((None, EXPERT_FF, D_MODEL), lambda i, te, na: (te[act(i, te, na)], 0, 0))],
            out_specs=pl.BlockSpec((tm, D_MODEL), lambda i, te, na: (act(i, te, na), 0))),
        out_shape=jax.ShapeDtypeStruct((nt * tm, D_MODEL), BF16),
        compiler_params=pltpu.CompilerParams(
            dimension_semantics=("arbitrary",), vmem_limit_bytes=VMEM_LIMIT),
        name="experts",
    )(tile_expert, n_active, xs, w["w_gate"], w["w_up"], w["w_down"])

    outs = []
    tile0 = 0
    for (h1, _, meta, _), n in zip(parts, n_tiles):
        outs.append(pl.pallas_call(
            functools.partial(_combine_kernel, tile0=tile0),
            grid_spec=pltpu.PrefetchScalarGridSpec(
                num_scalar_prefetch=len(combine_tables), grid=(n,),
                in_specs=[tok(D_MODEL), tok(LANES), hbm],
                out_specs=tok(D_MODEL),
                scratch_shapes=[pltpu.VMEM((2, SORT_ROWS + N_EXPERTS * SEG_ALIGN, D_MODEL), BF16),
                                pltpu.SemaphoreType.DMA((2,))]),
            out_shape=jax.ShapeDtypeStruct((n * TOK_TILE, D_MODEL), F32),
            compiler_params=pltpu.CompilerParams(
                dimension_semantics=("arbitrary",), vmem_limit_bytes=VMEM_LIMIT),
            name="combine",
        )(*combine_tables, h1, meta, y))
        tile0 += n
    return outs


def _prepare_weights(g_mix, w_in, conv_w, conv_b, dt_bias, a_log, d_skip, g_ssd_out, g_q, g_k,
                     attn_sinks, g_att_out, w_out, g_ffn, w_route_group, b_route_group,
                     w_route_expert, b_route_expert, w_gate, w_up, w_down):
    cuts = np.cumsum([0, SSD_DIM, CONV_DIM, SSD_HEADS, ATT_DIM, KV_DIM, KV_DIM])
    seg = lambda i: w_in[:, cuts[i]:cuts[i + 1]]
    pad_lanes = lambda a: jnp.pad(a, ((0, 0), (0, LANES - a.shape[1])))
    bd = np.kron(np.eye(KV_DIM // HEAD_DIM, dtype=np.float32), np.ones((HEAD_DIM, HEAD_DIM), np.float32))
    n_route = N_EXPERTS + N_EXPERT_GROUPS
    return {
        "g_mix": g_mix.reshape(1, D_MODEL),
        "w_qk": jnp.concatenate([seg(3), seg(4)], axis=1).astype(BF16),
        "w_rest": jnp.concatenate([seg(0), seg(1), seg(5), pad_lanes(jnp.tile(seg(2), (1, DT_COPIES)))],
                                  axis=1).astype(BF16),
        "g_qk": jnp.concatenate([jnp.tile(g_q, ATT_HEADS), jnp.tile(g_k, ATT_KV_HEADS)]).reshape(1, ATT_DIM + KV_DIM),
        "bd": jnp.asarray(np.concatenate([bd, bd], axis=0), BF16),
        "conv_w": conv_w, "conv_b": conv_b.reshape(1, CONV_DIM),
        "dt_bias": pad_lanes(jnp.tile(dt_bias, DT_COPIES).reshape(1, DT_COPIES * SSD_HEADS)),
        "a_log": pad_lanes(jnp.tile(a_log, DT_COPIES).reshape(1, DT_COPIES * SSD_HEADS)),
        "d_skip": jnp.repeat(d_skip, HEAD_DIM).reshape(1, SSD_DIM),
        "g_ssd_out": g_ssd_out.reshape(1, SSD_DIM),
        "attn_sinks": attn_sinks,
        "g_att_out": g_att_out.reshape(1, ATT_DIM),
        "w_out": w_out.astype(BF16),
        "g_ffn": g_ffn.reshape(1, D_MODEL),
        "w_route": pad_lanes(jnp.concatenate([w_route_expert, w_route_group], axis=1)).astype(BF16),
        "b_route": pad_lanes(jnp.concatenate([b_route_expert, b_route_group]).reshape(1, n_route)),
        "w_gate": w_gate, "w_up": w_up, "w_down": w_down,
    }


def _segment(x3, w, mk, mv, tail, h0t, kprev, vprev, L, tb_ssd, tb_att, tm, from_cache, sb=1):
    S, Ls, _ = x3.shape
    x = x3.reshape(S * Ls, D_MODEL)
    z, xbc, q, k, v, dt = _project(x, w, tm)
    r3 = lambda a: a.reshape(S, Ls, a.shape[-1])
    xbc3, k3, v3 = r3(xbc), r3(k), r3(v)
    y_ssd, h_new = _ssd(xbc3, r3(z), r3(dt), tail, h0t, w, L, tb_ssd, sb)
    if not from_cache:
        kprev, vprev = k3, v3
    y_att = _attention(r3(q), k3, v3, kprev, vprev, mk, mv, w, L, tb_att, from_cache, sb)
    part = _merge(x, y_ssd.reshape(S * Ls, SSD_DIM), y_att.reshape(S * Ls, ATT_DIM), w, tm)
    return part, xbc3, h_new, k3, v3


def _state_to_kernel(h):
    return h.reshape(h.shape[0], SSD_DIM, D_STATE)


def _state_from_kernel(hk):
    return hk.reshape(hk.shape[0], SSD_HEADS, HEAD_DIM, D_STATE)


def kernel(x_prompt, x_sample, cache_conv, state_ssd, cache_k, cache_v, meta_tokens, g_mix, w_in, conv_w, conv_b, dt_bias, a_log, d_skip, g_ssd_out, g_q, g_k, attn_sinks, g_att_out, w_out, g_ffn, w_route_group, b_route_group, w_route_expert, b_route_expert, w_gate, w_up, w_down):
    w = _prepare_weights(g_mix[0], w_in[0], conv_w[0], conv_b[0], dt_bias[0], a_log[0], d_skip[0],
                         g_ssd_out[0], g_q[0], g_k[0], attn_sinks[0], g_att_out[0], w_out[0], g_ffn[0],
                         w_route_group[0], b_route_group[0], w_route_expert[0], b_route_expert[0],
                         w_gate[0], w_up[0], w_down[0])
    n_b = x_sample.shape[0]
    n_dec = x_sample.shape[1]

    _, m_xbc, _, mk, mv, m_dt = _project(meta_tokens, w, N_META)
    zero_tail = jnp.zeros((1, CONV_WIDTH - 1, CONV_DIM), F32)
    zero_state = jnp.zeros((1, SSD_DIM, D_STATE), F32)
    m_xbc3 = m_xbc.reshape(1, N_META, CONV_DIM)
    _, m_state = _ssd(m_xbc3, jnp.zeros((1, N_META, SSD_DIM), F32), m_dt.reshape(1, N_META, LANES),
                      zero_tail, zero_state, w, N_META, N_META)
    m_tail = m_xbc3[:, N_META - (CONV_WIDTH - 1):]

    part_p, xbc_p, st_p, k_p, v_p = _segment(
        x_prompt, w, mk, mv, m_tail, m_state, None, None,
        L=CHUNK, tb_ssd=256, tb_att=512, tm=512, from_cache=False)
    part_s, xbc_s, st_s, k_s, v_s = _segment(
        x_sample, w, mk, mv, cache_conv[0], _state_to_kernel(state_ssd[0]),
        cache_k[0].reshape(n_b, WINDOW, KV_DIM), cache_v[0].reshape(n_b, WINDOW, KV_DIM),
        L=n_dec, tb_ssd=n_dec, tb_att=n_dec, tm=256, from_cache=True, sb=8)
    yp, ys = _moe([part_p, part_s], w, tm=512)
    yp = yp.reshape(x_prompt.shape)
    ys = ys.reshape(x_sample.shape)

    heads = lambda a, rows: a.reshape(a.shape[0], rows, ATT_KV_HEADS, HEAD_DIM)[None]
    return (yp, ys,
            xbc_p[:, -(CONV_WIDTH - 1):][None],
            _state_from_kernel(st_p)[None],
            heads(k_p[:, -WINDOW:], WINDOW), heads(v_p[:, -WINDOW:], WINDOW),
            xbc_s[:, -(CONV_WIDTH - 1):][None],
            _state_from_kernel(st_s)[None],
            heads(k_s, n_dec), heads(v_s, n_dec))
```

```python
import functools
import math

import numpy as np
import jax
import jax.numpy as jnp
from jax import lax
from jax.experimental import pallas as pl
from jax.experimental.pallas import tpu as pltpu

D_MODEL = 1024
CHUNK = 64
N_META = 16
HEAD_DIM = 64
ATT_HEADS = 16
ATT_KV_HEADS = 4
ATT_REP = ATT_HEADS // ATT_KV_HEADS
ATT_DIM = ATT_HEADS * HEAD_DIM
KV_DIM = ATT_KV_HEADS * HEAD_DIM
WINDOW = 128
SSD_HEADS = 16
SSD_DIM = SSD_HEADS * HEAD_DIM
SSD_GROUPS = 2
GROUP_DIM = SSD_DIM // SSD_GROUPS
D_STATE = 128
CONV_WIDTH = 4
CONV_DIM = SSD_DIM + 2 * SSD_GROUPS * D_STATE
N_EXPERT_GROUPS = 4
EXPERTS_PER_GROUP = 8
N_EXPERTS = N_EXPERT_GROUPS * EXPERTS_PER_GROUP
EXPERT_FF = D_MODEL // 4
EPS = 1e-6

LANES = 128
SUBLANES = 8
MXU_DIM = 256
DT_COPIES = 3
KEY_SPAN = 256
VMEM_LIMIT = 48 * 1024 * 1024

F32 = jnp.float32
BF16 = jnp.bfloat16
NEG_INF = float("-inf")


def _dot(a, b):
    return jnp.dot(a, b, preferred_element_type=F32)


def _dot_nt(a, b):
    return lax.dot_general(a, b, (((1,), (1,)), ((), ())), preferred_element_type=F32)


def _dot_tn(a, b):
    return lax.dot_general(a, b, (((0,), (0,)), ((), ())), preferred_element_type=F32)


def _split3(x):
    hi = x.astype(BF16)
    r = x - hi.astype(F32)
    mid = r.astype(BF16)
    lo = (r - mid.astype(F32)).astype(BF16)
    return hi, mid, lo


def _pack3(x):
    hi, mid, lo = _split3(x)
    lane = lax.broadcasted_iota(jnp.int32, x.shape, 1)
    packed = jnp.where(lane < SSD_HEADS, hi.astype(F32),
                       jnp.where(lane < 2 * SSD_HEADS, mid.astype(F32), lo.astype(F32)))
    return packed.astype(BF16)


def _sel_dot(sel, x):
    hi, mid, lo = _split3(x)
    return _dot(sel, hi) + _dot(sel, mid) + _dot(sel, lo)


def _silu(x):
    return x * (1.0 / (1.0 + jnp.exp2(x * (-math.log2(math.e)))))


def _rms(x, g):
    ms = jnp.mean(x * x, axis=-1, keepdims=True)
    return x * lax.rsqrt(ms + EPS) * g


def _const_spec(shape):
    n = len(shape)
    return pl.BlockSpec(shape, lambda *_: (0,) * n, pipeline_mode=pl.Buffered(1))


def _proj_kernel(x_ref, gmix_ref, wqk_ref, wrest_ref, gqk_ref, bd_ref,
                 z_ref, xbc_ref, q_ref, k_ref, v_ref, dt_ref):
    tm = x_ref.shape[0]
    xn = _rms(x_ref[...], gmix_ref[...]).astype(BF16)
    qk = _dot(xn, wqk_ref[...])
    n_slices = (ATT_DIM + KV_DIM) // KV_DIM
    sq = jnp.concatenate([qk[:, j * KV_DIM:(j + 1) * KV_DIM] for j in range(n_slices)], axis=0)
    sq = sq * sq
    hi = sq.astype(BF16)
    lo = (sq - hi.astype(F32)).astype(BF16)
    ms = _dot(jnp.concatenate([hi, lo], axis=1), bd_ref[...]) * (1.0 / HEAD_DIM)
    inv = lax.rsqrt(ms + EPS)
    for j in range(n_slices):
        sl = slice(j * KV_DIM, (j + 1) * KV_DIM)
        normed = qk[:, sl] * inv[j * tm:(j + 1) * tm, :] * gqk_ref[:, sl]
        if j < ATT_DIM // KV_DIM:
            q_ref[:, sl] = (normed * HEAD_DIM ** -0.5).astype(BF16)
        else:
            k_ref[...] = normed
    rest = _dot(xn, wrest_ref[...])
    cuts = np.cumsum([0, SSD_DIM, CONV_DIM, KV_DIM, LANES])
    for ref, lo_c, hi_c in zip((z_ref, xbc_ref, v_ref, dt_ref), cuts[:-1], cuts[1:]):
        ref[...] = rest[:, lo_c:hi_c]


def _project(x, w, tm):
    t = x.shape[0]
    assert t % tm == 0
    row = lambda n: pl.BlockSpec((tm, n), lambda i: (i, 0))
    ins = [x, w["g_mix"], w["w_qk"], w["w_rest"], w["g_qk"], w["bd"]]
    in_specs = [row(D_MODEL)] + [_const_spec(a.shape) for a in ins[1:]]
    out_dims = (SSD_DIM, CONV_DIM, ATT_DIM, KV_DIM, KV_DIM, LANES)
    out_dtypes = (F32, F32, BF16, F32, F32, F32)
    return pl.pallas_call(
        _proj_kernel,
        grid=(t // tm,),
        in_specs=in_specs,
        out_specs=[row(n) for n in out_dims],
        out_shape=[jax.ShapeDtypeStruct((t, n), d) for n, d in zip(out_dims, out_dtypes)],
        compiler_params=pltpu.CompilerParams(
            dimension_semantics=("arbitrary",), vmem_limit_bytes=VMEM_LIMIT),
        name="projection",
    )(*ins)


N_SSD_STREAM_INPUTS = 5
N_SSD_CONSTS = 9


def _ssd_kernel(*refs, L, TB, SB):
    streamed = lambda k: k < N_SSD_STREAM_INPUTS or k >= N_SSD_STREAM_INPUTS + N_SSD_CONSTS
    for s in range(SB):
        _ssd_stream(*[r.at[s] if streamed(k) else r for k, r in enumerate(refs)], L=L, TB=TB)


def _ssd_stream(xbc_ref, z_ref, dt_ref, tail_ref, h0_ref, convw_ref, convb_ref, dtb_ref,
                alog_ref, dskip_ref, gout_ref, ehead_ref, epos_ref, tri_ref,
                y_ref, hout_ref, buf_ref, act_ref, dts_ref, ys_ref, st_ref, *, L, TB):
    hp = LANES // L
    n_lane_tiles = SSD_HEADS // hp
    hq = MXU_DIM // L
    n_tiles = SSD_HEADS // hq
    tile_w = hq * HEAD_DIM
    b = pl.program_id(1)
    pad = SUBLANES - (CONV_WIDTH - 1)

    @pl.when(b == 0)
    def _():
        buf_ref[0:SUBLANES, :] = jnp.zeros((SUBLANES, CONV_DIM), F32)
        buf_ref[pad:SUBLANES, :] = tail_ref[...]
        st_ref[...] = h0_ref[...].T

    buf_ref[SUBLANES:SUBLANES + TB, :] = xbc_ref[...]
    rows = buf_ref[...]
    acc = convb_ref[...] + convw_ref[CONV_WIDTH - 1:CONV_WIDTH, :] * rows[SUBLANES:, :]
    for j in range(CONV_WIDTH - 1):
        shifted = pltpu.roll(rows, CONV_WIDTH - 1 - j, axis=0)[SUBLANES:, :]
        acc = acc + convw_ref[j:j + 1, :] * shifted
    act_ref[...] = _silu(acc)
    buf_ref[pad:SUBLANES, :] = buf_ref[TB + pad:TB + SUBLANES, :]

    dtx = dt_ref[...] + dtb_ref[...]
    dts_ref[...] = jnp.maximum(dtx, 0.0) + jnp.log(1.0 + jnp.exp(-jnp.abs(dtx)))

    a_row = -jnp.exp(alog_ref[...])
    ehead = ehead_ref[...]
    epos = epos_ref[...]
    tri = tri_ref[...]
    dskip = dskip_ref[...]
    row_i = lax.broadcasted_iota(jnp.int32, (L, n_lane_tiles * LANES), 0)
    col_i = lax.broadcasted_iota(jnp.int32, (L, n_lane_tiles * LANES), 1)
    pos_i = col_i % L
    diag_mask = row_i == pos_i
    causal_mask = row_i >= pos_i
    colhead = lax.broadcasted_iota(jnp.int32, (L, tile_w), 1) // HEAD_DIM

    def chunk(c):
        r0 = pl.multiple_of(c * L, L)
        dtc = dts_ref[pl.ds(r0, L), :]
        cs = _sel_dot(tri, dtc * a_row)
        cs16 = _pack3(cs)
        dt_b = _dot(_pack3(dtc), ehead)
        cs_b = _dot(cs16, ehead)
        cs_last_b = cs_b[L - 1:L, :]
        ecs_b = jnp.exp(cs_b)
        dec_end_b = jnp.exp(cs_last_b - cs_b)
        bdec_b = jnp.exp(cs_last_b)
        cs_col = cs_b if L == HEAD_DIM else _dot(cs16, epos)
        cs_row = jnp.sum(jnp.where(diag_mask, cs_col, 0.0), axis=0, keepdims=True)
        lmat = jnp.exp(jnp.where(causal_mask, cs_col - cs_row, NEG_INF))

        xs = act_ref[pl.ds(r0, L), 0:SSD_DIM]
        xdt_f = xs * dt_b
        xdt = xdt_f.astype(BF16)
        wx = (xdt_f * dec_end_b).astype(BF16)
        st = st_ref[...]
        st16 = st.astype(BF16)

        y_diag = []
        y_off = []
        s_new = []
        cb = [None] * SSD_GROUPS
        for g in range(SSD_GROUPS):
            bg = act_ref[pl.ds(r0, L), SSD_DIM + g * D_STATE:SSD_DIM + (g + 1) * D_STATE].astype(BF16)
            cg = act_ref[pl.ds(r0, L), SSD_DIM + (SSD_GROUPS + g) * D_STATE:
                         SSD_DIM + (SSD_GROUPS + g + 1) * D_STATE].astype(BF16)
            cb[g] = _dot_nt(cg, jnp.concatenate([bg] * hp, axis=0))
            gsl = slice(g * GROUP_DIM, (g + 1) * GROUP_DIM)
            y_off.append(_dot(cg, st16[:, gsl]))
            s_new.append(_dot_tn(bg, wx[:, gsl]))
        lane_tiles = MXU_DIM // LANES
        for t in range(n_tiles):
            cbt = jnp.concatenate(
                [cb[((t * lane_tiles + j) * hp * HEAD_DIM) // GROUP_DIM] for j in range(lane_tiles)], axis=1)
            gmat = (lmat[:, t * MXU_DIM:(t + 1) * MXU_DIM] * cbt).astype(BF16)
            xt = xdt[:, t * tile_w:(t + 1) * tile_w]
            rhs = jnp.concatenate(
                [jnp.where(colhead == hh, xt, jnp.zeros_like(xt)) for hh in range(hq)], axis=0)
            y_diag.append(_dot(gmat, rhs))
        y = (jnp.concatenate(y_diag, axis=1) + jnp.concatenate(y_off, axis=1) * ecs_b
             + dskip * xs)
        ys_ref[pl.ds(r0, L), :] = y
        st_ref[...] = bdec_b * st + jnp.concatenate(s_new, axis=1)

    n_chunks = TB // L
    unroll = 4 if n_chunks % 4 == 0 else 1

    def chunks(i, carry):
        for u in range(unroll):
            chunk(i * unroll + u)
        return carry

    lax.fori_loop(0, n_chunks // unroll, chunks, 0)

    yg = ys_ref[...] * _silu(z_ref[...])
    y_ref[...] = _rms(yg, gout_ref[...])

    @pl.when(b == pl.num_programs(1) - 1)
    def _():
        hout_ref[...] = st_ref[...].T


def _ssd(xbc, z, dt, tail, h0t, w, L, TB, SB=1):
    S, Ls, _ = xbc.shape
    assert Ls % TB == 0 and TB % L == 0 and LANES % L == 0 and S % SB == 0
    nb = Ls // TB
    assert SB == 1 or nb == 1
    hp = LANES // L
    n_tiles = SSD_HEADS // hp
    per_stream = lambda a: (lambda s, b: (s, 0, 0)) if a.shape[0] == S and S > 1 else (lambda s, b: (0, 0, 0))
    seq = lambda n: pl.BlockSpec((SB, TB, n), lambda s, b: (s, b, 0))
    ehead = np.zeros((LANES, SSD_DIM), np.float32)
    epos = np.zeros((LANES, n_tiles * LANES), np.float32)
    for part in range(DT_COPIES):
        for h in range(SSD_HEADS):
            ehead[part * SSD_HEADS + h, h * HEAD_DIM:(h + 1) * HEAD_DIM] = 1.0
            epos[part * SSD_HEADS + h, h * L:(h + 1) * L] = 1.0
    tri = np.tril(np.ones((L, L), np.float32))
    consts = [w["conv_w"], w["conv_b"], w["dt_bias"], w["a_log"], w["d_skip"], w["g_ssd_out"],
              jnp.asarray(ehead, BF16), jnp.asarray(epos, BF16), jnp.asarray(tri, BF16)]
    in_specs = [seq(CONV_DIM), seq(SSD_DIM), seq(LANES),
                pl.BlockSpec((SB, CONV_WIDTH - 1, CONV_DIM), per_stream(tail)),
                pl.BlockSpec((SB, SSD_DIM, D_STATE), per_stream(h0t))]
    assert len(in_specs) == N_SSD_STREAM_INPUTS and len(consts) == N_SSD_CONSTS
    in_specs += [_const_spec(c.shape) for c in consts]
    return pl.pallas_call(
        functools.partial(_ssd_kernel, L=L, TB=TB, SB=SB),
        grid=(S // SB, nb),
        in_specs=in_specs,
        out_specs=[seq(SSD_DIM), pl.BlockSpec((SB, SSD_DIM, D_STATE), lambda s, b: (s, 0, 0))],
        out_shape=[jax.ShapeDtypeStruct((S, Ls, SSD_DIM), F32),
                   jax.ShapeDtypeStruct((S, SSD_DIM, D_STATE), F32)],
        scratch_shapes=[pltpu.VMEM((SB, SUBLANES + TB, CONV_DIM), F32),
                        pltpu.VMEM((SB, TB, CONV_DIM), F32),
                        pltpu.VMEM((SB, TB, LANES), F32),
                        pltpu.VMEM((SB, TB, SSD_DIM), F32),
                        pltpu.VMEM((SB, D_STATE, SSD_DIM), F32)],
        compiler_params=pltpu.CompilerParams(
            dimension_semantics=("arbitrary", "arbitrary"), vmem_limit_bytes=VMEM_LIMIT),
        name="ssd",
    )(xbc, z, dt, tail, h0t, *consts)


def _split_heads(x):
    lane = lax.broadcasted_iota(jnp.int32, (x.shape[0], LANES), 1)
    low = lane < HEAD_DIM
    lo, hi = [], []
    for t in range(KV_DIM // LANES):
        tile = x[:, t * LANES:(t + 1) * LANES]
        swapped = pltpu.roll(tile, HEAD_DIM, axis=1)
        lo += [jnp.where(low, tile, 0.0), jnp.where(low, swapped, 0.0)]
        hi += [jnp.where(low, 0.0, swapped), jnp.where(low, 0.0, tile)]
    return jnp.concatenate(lo, axis=1).astype(BF16), jnp.concatenate(hi, axis=1).astype(BF16)


def _attn_kernel(*refs, L, TB, SB, from_cache):
    for s in range(SB):
        _attn_stream(*[r if k == 0 or 6 <= k <= 9 else r.at[s] for k, r in enumerate(refs)],
                     L=L, TB=TB, from_cache=from_cache)


def _attn_stream(sink_ref, q_ref, k_ref, v_ref, kp_ref, vp_ref, mk_ref, mv_ref, bias_ref, g_ref,
                 o_ref, k_lo, k_hi, v_lo, v_hi, ybuf, s_buf, e_buf, sink_buf, *, L, TB, from_cache):
    b = pl.program_id(1)
    bufs = (k_lo, k_hi, v_lo, v_hi)
    if from_cache:
        window = _split_heads(kp_ref[...]) + _split_heads(vp_ref[...])
        for buf, val in zip(bufs, window):
            buf[0:WINDOW, :] = val
    else:
        @pl.when(b == 0)
        def _():
            for buf in bufs:
                buf[0:WINDOW, :] = jnp.zeros((WINDOW, buf.shape[1]), BF16)

        @pl.when(b > 0)
        def _():
            for buf in bufs:
                buf[0:WINDOW, :] = buf[TB:TB + WINDOW, :]

    for buf, val in zip(bufs, _split_heads(k_ref[...]) + _split_heads(v_ref[...])):
        buf[WINDOW:WINDOW + TB, :] = val
    n_pad = KEY_SPAN - WINDOW - L - N_META
    pad_rows = jnp.zeros((n_pad, 2 * KV_DIM), BF16)
    tails = [jnp.concatenate([t, pad_rows], axis=0)
             for t in _split_heads(mk_ref[...]) + _split_heads(mv_ref[...])]
    lane = lax.broadcasted_iota(jnp.int32, (2 * L, 2 * KEY_SPAN), 1) % KEY_SPAN
    lane_h = lax.broadcasted_iota(jnp.int32, (L, LANES), 1)
    lane_v = lax.broadcasted_iota(jnp.int32, (KEY_SPAN, LANES), 1)
    ones_lo = jnp.where(lane_v < HEAD_DIM, 1.0, 0.0).astype(BF16)
    ones_hi = jnp.where(lane_v < HEAD_DIM, 0.0, 1.0).astype(BF16)
    n_chunks = TB // L
    pairs_per_group = ATT_REP // 2

    def chunk_keys(c):
        r0 = c * L
        return [jnp.concatenate([buf[r0:r0 + WINDOW, :], buf[r0 + WINDOW:r0 + WINDOW + L, :], tail], axis=0)
                for buf, tail in zip(bufs, tails)]

    for c in range(n_chunks):
        r0 = c * L
        keys = chunk_keys(c)
        for g in range(ATT_KV_HEADS):
            gl = slice(g * LANES, (g + 1) * LANES)
            q4 = jnp.concatenate([q_ref[r0:r0 + L, p * LANES:(p + 1) * LANES]
                                  for p in range(pairs_per_group * g, pairs_per_group * (g + 1))], axis=0)
            s = _dot_nt(q4, jnp.concatenate([keys[0][:, gl], keys[1][:, gl]], axis=0)) + bias_ref[g]
            if not from_cache and r0 < WINDOW:
                n_invalid = WINDOW - (b * TB + r0)
                s = jnp.where(lane < n_invalid, NEG_INF, s)
            s_buf[c * ATT_KV_HEADS + g] = s

    for c in range(n_chunks):
        for pair in range(ATT_HEADS // 2):
            slot = c * ATT_KV_HEADS + pair // pairs_per_group
            rows = slice((pair % pairs_per_group) * L, (pair % pairs_per_group + 1) * L)
            sink_terms = []
            for half in range(2):
                cols = slice(half * KEY_SPAN, (half + 1) * KEY_SPAN)
                s = s_buf[slot, rows, cols]
                sink = sink_ref[2 * pair + half]
                m = jnp.maximum(jnp.max(s, axis=-1, keepdims=True), sink)
                e_buf[slot, rows, cols] = jnp.exp(s - m).astype(BF16)
                sink_terms.append(jnp.exp(sink - m))
            sink_buf[c * (ATT_HEADS // 2) + pair] = jnp.where(lane_h < HEAD_DIM, sink_terms[0], sink_terms[1])

    for c in range(n_chunks):
        r0 = c * L
        keys = chunk_keys(c)
        for g in range(ATT_KV_HEADS):
            gl = slice(g * LANES, (g + 1) * LANES)
            vcat = jnp.concatenate([jnp.concatenate([keys[2][:, gl], ones_lo], axis=1),
                                    jnp.concatenate([keys[3][:, gl], ones_hi], axis=1)], axis=0)
            ov = _dot(e_buf[c * ATT_KV_HEADS + g], vcat)
            for j in range(pairs_per_group):
                pair = pairs_per_group * g + j
                den = ov[j * L:(j + 1) * L, LANES:2 * LANES] + sink_buf[c * (ATT_HEADS // 2) + pair]
                ybuf[r0:r0 + L, pair * LANES:(pair + 1) * LANES] = ov[j * L:(j + 1) * L, 0:LANES] * (1.0 / den)
    o_ref[...] = _rms(ybuf[...], g_ref[...])


def _attention(q, k, v, kprev, vprev, mk, mv, w, L, TB, from_cache, SB=1):
    S, Ls, _ = q.shape
    assert Ls % TB == 0 and TB % L == 0 and (from_cache or TB >= WINDOW) and S % SB == 0
    nb = Ls // TB
    assert SB == 1 or (from_cache and nb == 1)
    seq = lambda n: pl.BlockSpec((SB, TB, n), lambda s, b: (s, b, 0))
    if not from_cache:
        kprev = vprev = jnp.zeros((1, WINDOW, KV_DIM), F32)
    prev = pl.BlockSpec((SB, WINDOW, KV_DIM),
                        (lambda s, b: (s, 0, 0)) if from_cache else (lambda s, b: (0, 0, 0)))
    slopes = 2.0 ** (-8.0 * np.arange(1, ATT_HEADS + 1, dtype=np.float64) / ATT_HEADS)
    dist = np.abs(WINDOW + np.arange(L)[:, None] - np.arange(WINDOW + L)[None, :])
    bias = np.full((ATT_HEADS, L, KEY_SPAN), NEG_INF, np.float32)
    bias[:, :, :WINDOW + L] = -slopes[:, None, None] * dist[None]
    bias[:, :, WINDOW + L:WINDOW + L + N_META] = 0.0
    bias = bias.reshape(ATT_KV_HEADS, ATT_REP // 2, 2, L, KEY_SPAN).transpose(0, 1, 3, 2, 4)
    bias = jnp.asarray(bias.reshape(ATT_KV_HEADS, ATT_REP // 2 * L, 2 * KEY_SPAN))
    n_chunks = TB // L
    in_specs = [pl.BlockSpec(memory_space=pltpu.SMEM),
                seq(ATT_DIM), seq(KV_DIM), seq(KV_DIM), prev, prev,
                _const_spec(mk.shape), _const_spec(mv.shape), _const_spec(bias.shape),
                _const_spec(w["g_att_out"].shape)]
    return pl.pallas_call(
        functools.partial(_attn_kernel, L=L, TB=TB, SB=SB, from_cache=from_cache),
        grid=(S // SB, nb),
        in_specs=in_specs,
        out_specs=seq(ATT_DIM),
        out_shape=jax.ShapeDtypeStruct((S, Ls, ATT_DIM), F32),
        scratch_shapes=[pltpu.VMEM((SB, WINDOW + TB, 2 * KV_DIM), BF16) for _ in range(4)]
                       + [pltpu.VMEM((SB, TB, ATT_DIM), F32),
                          pltpu.VMEM((SB, n_chunks * ATT_KV_HEADS, 2 * L, 2 * KEY_SPAN), F32),
                          pltpu.VMEM((SB, n_chunks * ATT_KV_HEADS, 2 * L, 2 * KEY_SPAN), BF16),
                          pltpu.VMEM((SB, n_chunks * ATT_HEADS // 2, L, LANES), F32)],
        compiler_params=pltpu.CompilerParams(
            dimension_semantics=("arbitrary", "arbitrary"), vmem_limit_bytes=VMEM_LIMIT),
        name="attention",
    )(w["attn_sinks"], q, k, v, kprev, vprev, mk, mv, bias, w["g_att_out"])


TOK_TILE = 512
SEG_ALIGN = 2 * SUBLANES
SORT_ROWS = 2 * TOK_TILE + N_EXPERTS * SEG_ALIGN
XS_COLS = D_MODEL + LANES
META_P1, META_P2, META_G1, META_G2 = N_EXPERTS, N_EXPERTS + 1, N_EXPERTS + 2, N_EXPERTS + 3


def _merge_kernel(h_ref, ys_ref, ya_ref, wo_ref, gffn_ref, wr_ref, br_ref, tri_ref, upper_ref,
                  h1_ref, xn_ref, meta_ref, cnt_ref, logit_buf):
    i = pl.program_id(0)

    @pl.when(i == 0)
    def _():
        logit_buf[...] = jnp.zeros(logit_buf.shape, F32)

    ycat = jnp.concatenate([ys_ref[...].astype(BF16), ya_ref[...].astype(BF16)], axis=1)
    h1 = h_ref[...] + _dot(ycat, wo_ref[...])
    h1_ref[...] = h1
    xn = _rms(h1, gffn_ref[...]).astype(BF16)
    xn_ref[...] = xn
    all_logits = logit_buf[(i + 1) % 2]
    logit_buf[i % 2] = _dot(xn, wr_ref[...]) + br_ref[...]
    tri = tri_ref[...]
    upper = upper_ref[...]
    lane = lax.broadcasted_iota(jnp.int32, (TOK_TILE, LANES), 1)
    big = jnp.int32(LANES)
    gmask = (lane >= N_EXPERTS) & (lane < N_EXPERTS + N_EXPERT_GROUPS)
    for s in range(all_logits.shape[0] // TOK_TILE):
        rows = slice(s * TOK_TILE, (s + 1) * TOK_TILE)
        logits = all_logits[rows]

        def top1(mask):
            mval = jnp.max(jnp.where(mask, logits, NEG_INF), axis=-1, keepdims=True)
            idx = jnp.min(jnp.where(mask & (logits == mval), lane, big), axis=-1, keepdims=True)
            return mval, idx

        gmax, gidx = top1(gmask)
        gate_g = 1.0 / jnp.sum(jnp.where(gmask, jnp.exp(logits - gmax), 0.0), axis=-1, keepdims=True)
        grp = gidx - N_EXPERTS
        emask = (lane // EXPERTS_PER_GROUP) == grp
        v1, i1 = top1(emask)
        v2, i2 = top1(emask & (lane != i1))
        e2 = jnp.exp(v2 - v1)
        g1 = gate_g / (1.0 + e2)
        g2 = gate_g * e2 / (1.0 + e2)
        oh1 = jnp.where(lane == i1, 1.0, 0.0)
        oh2 = jnp.where(lane == i2, 1.0, 0.0)
        oh = oh1 + oh2
        earlier = _dot(tri, oh.astype(BF16))
        cnt = jnp.sum(oh, axis=0, keepdims=True)
        units = jnp.floor((cnt + (SEG_ALIGN - 1)) * (1.0 / SEG_ALIGN))
        units = jnp.broadcast_to(units, (2 * SUBLANES, LANES)).astype(BF16)
        slot = _dot(units, upper)[0:1, :] * SEG_ALIGN + earlier
        p1 = jnp.sum(oh1 * slot, axis=-1, keepdims=True)
        p2 = jnp.sum(oh2 * slot, axis=-1, keepdims=True)
        meta = jnp.where(lane == META_P1, p1, 0.0)
        meta = jnp.where(lane == META_P2, p2, meta)
        meta = jnp.where(lane == META_G1, g1, meta)
        meta = jnp.where(lane == META_G2, g2, meta)
        meta_ref[rows, :] = meta
        cnt_ref[s] = jnp.broadcast_to(cnt, (SUBLANES, LANES))


def _merge(h, y_ssd, y_att, w, tm):
    t = h.shape[0]
    assert t % tm == 0 and tm % TOK_TILE == 0
    sub = tm // TOK_TILE
    last = t // tm - 1
    row = lambda n: pl.BlockSpec((tm, n), lambda i: (jnp.minimum(i, last), 0))
    routed = lambda i: jnp.maximum(i - 1, 0)
    tri = jnp.asarray(np.tril(np.ones((TOK_TILE, TOK_TILE), np.float32), -1), BF16)
    upper = jnp.asarray(np.triu(np.ones((LANES, LANES), np.float32), 1), BF16)
    consts = [w["w_out"], w["g_ffn"], w["w_route"], w["b_route"], tri, upper]
    return pl.pallas_call(
        _merge_kernel,
        grid=(t // tm + 1,),
        in_specs=[row(D_MODEL), row(SSD_DIM), row(ATT_DIM)] + [_const_spec(c.shape) for c in consts],
        out_specs=[row(D_MODEL), row(D_MODEL),
                   pl.BlockSpec((tm, LANES), lambda i: (routed(i), 0)),
                   pl.BlockSpec((sub, SUBLANES, LANES), lambda i: (routed(i), 0, 0))],
        scratch_shapes=[pltpu.VMEM((2, tm, LANES), F32)],
        out_shape=[jax.ShapeDtypeStruct((t, D_MODEL), F32),
                   jax.ShapeDtypeStruct((t, D_MODEL), BF16),
                   jax.ShapeDtypeStruct((t, LANES), F32),
                   jax.ShapeDtypeStruct((t // TOK_TILE, SUBLANES, LANES), F32)],
        compiler_params=pltpu.CompilerParams(
            dimension_semantics=("arbitrary",), vmem_limit_bytes=VMEM_LIMIT),
        name="merge_route",
    )(h, y_ssd, y_att, *consts)


def _meta_col(meta, j):
    lane = lax.broadcasted_iota(jnp.int32, meta.shape, 1)
    return jnp.sum(jnp.where(lane == j, meta, 0.0), axis=-1, keepdims=True)


def _pair_selectors(meta):
    rows = lax.broadcasted_iota(jnp.int32, (meta.shape[0], SORT_ROWS), 1)
    s1 = jnp.where(rows == _meta_col(meta, META_P1).astype(jnp.int32), 1.0, 0.0).astype(BF16)
    s2 = jnp.where(rows == _meta_col(meta, META_P2).astype(jnp.int32), 1.0, 0.0).astype(BF16)
    return s1, s2


def _segment_dmas(off_ref, cnt_ref, row_ref, b, tile_buf, hbm, sem, to_hbm, wait):
    for e in range(N_EXPERTS):
        k = b * N_EXPERTS + e
        n = pl.multiple_of(cnt_ref[k], SEG_ALIGN)
        v = tile_buf.at[pl.ds(pl.multiple_of(off_ref[k], SEG_ALIGN), n)]
        h = hbm.at[pl.ds(pl.multiple_of(row_ref[k], SEG_ALIGN), n)]
        cp = pltpu.make_async_copy(v, h, sem) if to_hbm else pltpu.make_async_copy(h, v, sem)
        if wait:
            cp.wait()
        else:
            cp.start()


def _gate_lanes(g):
    hi, mid, lo = _split3(g)
    lane = lax.broadcasted_iota(jnp.int32, (g.shape[0], LANES), 1)
    out = jnp.where(lane == 0, hi.astype(F32), 0.0)
    out = jnp.where(lane == 1, mid.astype(F32), out)
    out = jnp.where(lane == 2, lo.astype(F32), out)
    return out.astype(BF16)


def _dispatch_kernel(off_ref, cnt_ref, row_ref, tail_row_ref, tail_cnt_ref, xn_ref, meta_ref, *rest,
                     tile0, first):
    xs_ref, sort_buf, zero_buf, sem, tail_sem = rest if first else rest[1:]
    b = pl.program_id(0)
    nb = pl.num_programs(0)
    slot = b % 2
    seg = functools.partial(_segment_dmas, off_ref, cnt_ref, row_ref, hbm=xs_ref, to_hbm=True)

    def tails(wait):
        def body(e, carry):
            n = pl.multiple_of(tail_cnt_ref[e], SEG_ALIGN)

            @pl.when(n > 0)
            def _():
                cp = pltpu.make_async_copy(
                    zero_buf.at[pl.ds(0, n)],
                    xs_ref.at[pl.ds(pl.multiple_of(tail_row_ref[e], SEG_ALIGN), n)], tail_sem.at[0])
                if wait:
                    cp.wait()
                else:
                    cp.start()
            return carry

        lax.fori_loop(0, N_EXPERTS, body, 0)

    @pl.when(b >= 2)
    def _():
        seg(tile0 + b - 2, sort_buf.at[slot], sem=sem.at[slot], wait=True)

    if first:
        @pl.when(b == 0)
        def _():
            zero_buf[...] = jnp.zeros(zero_buf.shape, BF16)
            tails(wait=False)

    meta = meta_ref[...]
    s1, s2 = _pair_selectors(meta)
    sort_buf[slot, :, 0:D_MODEL] = _dot_tn(s1 + s2, xn_ref[...]).astype(BF16)
    sort_buf[slot, :, D_MODEL:XS_COLS] = (_dot_tn(s1, _gate_lanes(_meta_col(meta, META_G1)))
                                          + _dot_tn(s2, _gate_lanes(_meta_col(meta, META_G2)))).astype(BF16)
    seg(tile0 + b, sort_buf.at[slot], sem=sem.at[slot], wait=False)

    @pl.when(b == nb - 1)
    def _():
        @pl.when(b >= 1)
        def _():
            seg(tile0 + b - 1, sort_buf.at[1 - slot], sem=sem.at[1 - slot], wait=True)

        seg(tile0 + b, sort_buf.at[slot], sem=sem.at[slot], wait=True)
        if first:
            tails(wait=True)


def _expert_kernel(tile_expert_ref, n_active_ref, xs_ref, wg_ref, wu_ref, wd_ref, y_ref):
    @pl.when(pl.program_id(0) < n_active_ref[0])
    def _():
        x = xs_ref[:, 0:D_MODEL]
        gate = jnp.sum(xs_ref[:, D_MODEL:XS_COLS].astype(F32), axis=-1, keepdims=True)
        gu = _dot(x, jnp.concatenate([wg_ref[...].astype(BF16), wu_ref[...].astype(BF16)], axis=1))
        hid = _silu(gu[:, 0:EXPERT_FF]) * gu[:, EXPERT_FF:2 * EXPERT_FF]
        y_ref[...] = (gate * _dot(hid.astype(BF16), wd_ref[...].astype(BF16))).astype(BF16)


def _combine_kernel(off_ref, cnt_ref, row_ref, h1_ref, meta_ref, y_ref, o_ref, y_buf, sem, *, tile0):
    b = pl.program_id(0)
    nb = pl.num_programs(0)
    slot = b % 2
    seg = functools.partial(_segment_dmas, off_ref, cnt_ref, row_ref, hbm=y_ref, to_hbm=False)

    @pl.when(b == 0)
    def _():
        y_buf[...] = jnp.zeros(y_buf.shape, BF16)
        seg(tile0 + b, y_buf.at[slot], sem=sem.at[slot], wait=False)

    @pl.when(b + 1 < nb)
    def _():
        seg(tile0 + b + 1, y_buf.at[1 - slot], sem=sem.at[1 - slot], wait=False)

    seg(tile0 + b, y_buf.at[slot], sem=sem.at[slot], wait=True)
    s1, s2 = _pair_selectors(meta_ref[...])
    sel = s1 + s2
    o_ref[...] = h1_ref[...] + _dot(sel, y_buf[slot, 0:SORT_ROWS, :])


def _moe(parts, w, tm):
    i32 = jnp.int32
    n_tiles = [p[0].shape[0] // TOK_TILE for p in parts]
    nb = sum(n_tiles)
    t = nb * TOK_TILE
    cnt = jnp.concatenate([p[3][:, 0, :N_EXPERTS] for p in parts], axis=0).astype(i32)
    cnt_al = (cnt + (SEG_ALIGN - 1)) // SEG_ALIGN * SEG_ALIGN
    off = jnp.cumsum(cnt_al, axis=1) - cnt_al
    tot = jnp.sum(cnt_al, axis=0)
    tot_tm = (tot + (tm - 1)) // tm * tm
    start = jnp.cumsum(tot_tm) - tot_tm
    row = start[None, :] + jnp.cumsum(cnt_al, axis=0) - cnt_al
    tile_ends = jnp.cumsum(tot_tm // tm)
    max_rows = 2 * t + (SEG_ALIGN - 1) * min(N_EXPERTS * nb, 2 * t) + N_EXPERTS * (tm - SEG_ALIGN)
    nt = -(-max_rows // tm)
    tile_expert = jnp.sum((jnp.arange(nt, dtype=i32)[:, None] >= tile_ends[None, :]).astype(i32), axis=1)
    tile_expert = jnp.minimum(tile_expert, N_EXPERTS - 1)
    n_active = tile_ends[-1:].astype(i32)
    empty = cnt_al == 0
    flat = lambda a: a.reshape(-1).astype(i32)
    n_rows = flat(jnp.maximum(cnt_al, SEG_ALIGN))
    e_idx = jnp.arange(N_EXPERTS, dtype=i32)[None, :]
    spare = nt * tm + ((jnp.arange(nb, dtype=i32)[:, None] % 2) * N_EXPERTS + e_idx) * SEG_ALIGN
    dispatch_tables = [flat(off), n_rows, flat(jnp.where(empty, spare, row))]
    combine_tables = [flat(jnp.where(empty, SORT_ROWS + e_idx * SEG_ALIGN, off)), n_rows,
                      flat(jnp.where(empty, 0, row))]
    n_spare = 2 * N_EXPERTS * SEG_ALIGN
    tail_tables = [(start + tot).astype(i32), (tot_tm - tot).astype(i32)]

    tok = lambda n: pl.BlockSpec((TOK_TILE, n), lambda i, *_: (i, 0))
    hbm = pl.BlockSpec(memory_space=pl.ANY)
    n_prefetch = len(dispatch_tables) + len(tail_tables)
    xs = None
    tile0 = 0
    for (_, xn, meta, _), n in zip(parts, n_tiles):
        first = xs is None
        xs = pl.pallas_call(
            functools.partial(_dispatch_kernel, tile0=tile0, first=first),
            grid_spec=pltpu.PrefetchScalarGridSpec(
                num_scalar_prefetch=n_prefetch, grid=(n,),
                in_specs=[tok(D_MODEL), tok(LANES)] + ([] if first else [hbm]),
                out_specs=hbm,
                scratch_shapes=[pltpu.VMEM((2, SORT_ROWS, XS_COLS), BF16),
                                pltpu.VMEM((tm, XS_COLS), BF16),
                                pltpu.SemaphoreType.DMA((2,)),
                                pltpu.SemaphoreType.DMA((1,))]),
            out_shape=jax.ShapeDtypeStruct((nt * tm + n_spare, XS_COLS), BF16),
            input_output_aliases={} if first else {n_prefetch + 2: 0},
            compiler_params=pltpu.CompilerParams(
                dimension_semantics=("arbitrary",), vmem_limit_bytes=VMEM_LIMIT),
            name="dispatch",
        )(*dispatch_tables, *tail_tables, xn, meta, *([] if first else [xs]))
        tile0 += n

    act = lambda i, te, na: jnp.minimum(i, na[0] - 1)
    y = pl.pallas_call(
        _expert_kernel,
        grid_spec=pltpu.PrefetchScalarGridSpec(
            num_scalar_prefetch=2, grid=(nt,),
            in_specs=[pl.BlockSpec((tm, XS_COLS), lambda i, te, na: (act(i, te, na), 0)),
                      pl.BlockSpec((None, D_MODEL, EXPERT_FF), lambda i, te, na: (te[act(i, te, na)], 0, 0)),
                      pl.BlockSpec((None, D_MODEL, EXPERT_FF), lambda i, te, na: (te[act(i, te, na)], 0, 0)),
                      pl.BlockSpec((None, EXPERT_FF, D_MODEL), lambda i, te, na: (te[act(i, te, na)], 0, 0))],
            out_specs=pl.BlockSpec((tm, D_MODEL), lambda i, te, na: (act(i, te, na), 0))),
        out_shape=jax.ShapeDtypeStruct((nt * tm, D_MODEL), BF16),
        compiler_params=pltpu.CompilerParams(
            dimension_semantics=("arbitrary",), vmem_limit_bytes=VMEM_LIMIT),
        name="experts",
    )(tile_expert, n_active, xs, w["w_gate"], w["w_up"], w["w_down"])

    outs = []
    tile0 = 0
    for (h1, _, meta, _), n in zip(parts, n_tiles):
        outs.append(pl.pallas_call(
            functools.partial(_combine_kernel, tile0=tile0),
            grid_spec=pltpu.PrefetchScalarGridSpec(
                num_scalar_prefetch=len(combine_tables), grid=(n,),
                in_specs=[tok(D_MODEL), tok(LANES), hbm],
                out_specs=tok(D_MODEL),
                scratch_shapes=[pltpu.VMEM((2, SORT_ROWS + N_EXPERTS * SEG_ALIGN, D_MODEL), BF16),
                                pltpu.SemaphoreType.DMA((2,))]),
            out_shape=jax.ShapeDtypeStruct((n * TOK_TILE, D_MODEL), F32),
            compiler_params=pltpu.CompilerParams(
                dimension_semantics=("arbitrary",), vmem_limit_bytes=VMEM_LIMIT),
            name="combine",
        )(*combine_tables, h1, meta, y))
        tile0 += n
    return outs


def _prepare_weights(g_mix, w_in, conv_w, conv_b, dt_bias, a_log, d_skip, g_ssd_out, g_q, g_k,
                     attn_sinks, g_att_out, w_out, g_ffn, w_route_group, b_route_group,
                     w_route_expert, b_route_expert, w_gate, w_up, w_down):
    cuts = np.cumsum([0, SSD_DIM, CONV_DIM, SSD_HEADS, ATT_DIM, KV_DIM, KV_DIM])
    seg = lambda i: w_in[:, cuts[i]:cuts[i + 1]]
    pad_lanes = lambda a: jnp.pad(a, ((0, 0), (0, LANES - a.shape[1])))
    bd = np.kron(np.eye(KV_DIM // HEAD_DIM, dtype=np.float32), np.ones((HEAD_DIM, HEAD_DIM), np.float32))
    n_route = N_EXPERTS + N_EXPERT_GROUPS
    return {
        "g_mix": g_mix.reshape(1, D_MODEL),
        "w_qk": jnp.concatenate([seg(3), seg(4)], axis=1).astype(BF16),
        "w_rest": jnp.concatenate([seg(0), seg(1), seg(5), pad_lanes(jnp.tile(seg(2), (1, DT_COPIES)))],
                                  axis=1).astype(BF16),
        "g_qk": jnp.concatenate([jnp.tile(g_q, ATT_HEADS), jnp.tile(g_k, ATT_KV_HEADS)]).reshape(1, ATT_DIM + KV_DIM),
        "bd": jnp.asarray(np.concatenate([bd, bd], axis=0), BF16),
        "conv_w": conv_w, "conv_b": conv_b.reshape(1, CONV_DIM),
        "dt_bias": pad_lanes(jnp.tile(dt_bias, DT_COPIES).reshape(1, DT_COPIES * SSD_HEADS)),
        "a_log": pad_lanes(jnp.tile(a_log, DT_COPIES).reshape(1, DT_COPIES * SSD_HEADS)),
        "d_skip": jnp.repeat(d_skip, HEAD_DIM).reshape(1, SSD_DIM),
        "g_ssd_out": g_ssd_out.reshape(1, SSD_DIM),
        "attn_sinks": attn_sinks,
        "g_att_out": g_att_out.reshape(1, ATT_DIM),
        "w_out": w_out.astype(BF16),
        "g_ffn": g_ffn.reshape(1, D_MODEL),
        "w_route": pad_lanes(jnp.concatenate([w_route_expert, w_route_group], axis=1)).astype(BF16),
        "b_route": pad_lanes(jnp.concatenate([b_route_expert, b_route_group]).reshape(1, n_route)),
        "w_gate": w_gate, "w_up": w_up, "w_down": w_down,
    }


def _segment(x3, w, mk, mv, tail, h0t, kprev, vprev, L, tb_ssd, tb_att, tm, from_cache, sb=1):
    S, Ls, _ = x3.shape
    x = x3.reshape(S * Ls, D_MODEL)
    z, xbc, q, k, v, dt = _project(x, w, tm)
    r3 = lambda a: a.reshape(S, Ls, a.shape[-1])
    xbc3, k3, v3 = r3(xbc), r3(k), r3(v)
    y_ssd, h_new = _ssd(xbc3, r3(z), r3(dt), tail, h0t, w, L, tb_ssd, sb)
    if not from_cache:
        kprev, vprev = k3, v3
    y_att = _attention(r3(q), k3, v3, kprev, vprev, mk, mv, w, L, tb_att, from_cache, sb)
    part = _merge(x, y_ssd.reshape(S * Ls, SSD_DIM), y_att.reshape(S * Ls, ATT_DIM), w, tm)
    return part, xbc3, h_new, k3, v3


def _state_to_kernel(h):
    return h.reshape(h.shape[0], SSD_DIM, D_STATE)


def _state_from_kernel(hk):
    return hk.reshape(hk.shape[0], SSD_HEADS, HEAD_DIM, D_STATE)


def kernel(x_prompt, x_sample, cache_conv, state_ssd, cache_k, cache_v, meta_tokens, g_mix, w_in, conv_w, conv_b, dt_bias, a_log, d_skip, g_ssd_out, g_q, g_k, attn_sinks, g_att_out, w_out, g_ffn, w_route_group, b_route_group, w_route_expert, b_route_expert, w_gate, w_up, w_down):
    w = _prepare_weights(g_mix[0], w_in[0], conv_w[0], conv_b[0], dt_bias[0], a_log[0], d_skip[0],
                         g_ssd_out[0], g_q[0], g_k[0], attn_sinks[0], g_att_out[0], w_out[0], g_ffn[0],
                         w_route_group[0], b_route_group[0], w_route_expert[0], b_route_expert[0],
                         w_gate[0], w_up[0], w_down[0])
    n_b = x_sample.shape[0]
    n_dec = x_sample.shape[1]

    _, m_xbc, _, mk, mv, m_dt = _project(meta_tokens, w, N_META)
    zero_tail = jnp.zeros((1, CONV_WIDTH - 1, CONV_DIM), F32)
    zero_state = jnp.zeros((1, SSD_DIM, D_STATE), F32)
    m_xbc3 = m_xbc.reshape(1, N_META, CONV_DIM)
    _, m_state = _ssd(m_xbc3, jnp.zeros((1, N_META, SSD_DIM), F32), m_dt.reshape(1, N_META, LANES),
                      zero_tail, zero_state, w, N_META, N_META)
    m_tail = m_xbc3[:, N_META - (CONV_WIDTH - 1):]

    part_p, xbc_p, st_p, k_p, v_p = _segment(
        x_prompt, w, mk, mv, m_tail, m_state, None, None,
        L=CHUNK, tb_ssd=256, tb_att=512, tm=512, from_cache=False)
    part_s, xbc_s, st_s, k_s, v_s = _segment(
        x_sample, w, mk, mv, cache_conv[0], _state_to_kernel(state_ssd[0]),
        cache_k[0].reshape(n_b, WINDOW, KV_DIM), cache_v[0].reshape(n_b, WINDOW, KV_DIM),
        L=n_dec, tb_ssd=n_dec, tb_att=n_dec, tm=512, from_cache=True, sb=8)
    yp, ys = _moe([part_p, part_s], w, tm=512)
    yp = yp.reshape(x_prompt.shape)
    ys = ys.reshape(x_sample.shape)

    heads = lambda a, rows: a.reshape(a.shape[0], rows, ATT_KV_HEADS, HEAD_DIM)[None]
    return (yp, ys,
            xbc_p[:, -(CONV_WIDTH - 1):][None],
            _state_from_kernel(st_p)[None],
            heads(k_p[:, -WINDOW:], WINDOW), heads(v_p[:, -WINDOW:], WINDOW),
            xbc_s[:, -(CONV_WIDTH - 1):][None],
            _state_from_kernel(st_s)[None],
            heads(k_s, n_dec), heads(v_s, n_dec))
```

```python
import functools
import math

import numpy as np
import jax
import jax.numpy as jnp
from jax import lax
from jax.experimental import pallas as pl
from jax.experimental.pallas import tpu as pltpu

D_MODEL = 1024
CHUNK = 64
N_META = 16
HEAD_DIM = 64
ATT_HEADS = 16
ATT_KV_HEADS = 4
ATT_REP = ATT_HEADS // ATT_KV_HEADS
ATT_DIM = ATT_HEADS * HEAD_DIM
KV_DIM = ATT_KV_HEADS * HEAD_DIM
WINDOW = 128
SSD_HEADS = 16
SSD_DIM = SSD_HEADS * HEAD_DIM
SSD_GROUPS = 2
GROUP_DIM = SSD_DIM // SSD_GROUPS
D_STATE = 128
CONV_WIDTH = 4
CONV_DIM = SSD_DIM + 2 * SSD_GROUPS * D_STATE
N_EXPERT_GROUPS = 4
EXPERTS_PER_GROUP = 8
N_EXPERTS = N_EXPERT_GROUPS * EXPERTS_PER_GROUP
EXPERT_FF = D_MODEL // 4
EPS = 1e-6

LANES = 128
SUBLANES = 8
MXU_DIM = 256
DT_COPIES = 3
KEY_SPAN = 256
VMEM_LIMIT = 48 * 1024 * 1024

F32 = jnp.float32
BF16 = jnp.bfloat16
NEG_INF = float("-inf")
LOG2E = math.log2(math.e)


def _dot(a, b):
    return jnp.dot(a, b, preferred_element_type=F32)


def _dot_nt(a, b):
    return lax.dot_general(a, b, (((1,), (1,)), ((), ())), preferred_element_type=F32)


def _dot_tn(a, b):
    return lax.dot_general(a, b, (((0,), (0,)), ((), ())), preferred_element_type=F32)


def _split3(x):
    hi = x.astype(BF16)
    r = x - hi.astype(F32)
    mid = r.astype(BF16)
    lo = (r - mid.astype(F32)).astype(BF16)
    return hi, mid, lo


def _pack3(x):
    hi, mid, lo = _split3(x)
    lane = lax.broadcasted_iota(jnp.int32, x.shape, 1)
    packed = jnp.where(lane < SSD_HEADS, hi.astype(F32),
                       jnp.where(lane < 2 * SSD_HEADS, mid.astype(F32), lo.astype(F32)))
    return packed.astype(BF16)


def _sel_dot(sel, x):
    hi, mid, lo = _split3(x)
    return _dot(sel, hi) + _dot(sel, mid) + _dot(sel, lo)


def _silu(x):
    return x * (1.0 / (1.0 + jnp.exp2(x * (-LOG2E))))


def _rms(x, g):
    ms = jnp.mean(x * x, axis=-1, keepdims=True)
    return x * lax.rsqrt(ms + EPS) * g


def _const_spec(shape):
    n = len(shape)
    return pl.BlockSpec(shape, lambda *_: (0,) * n, pipeline_mode=pl.Buffered(1))


def _proj_kernel(x_ref, gmix_ref, wqk_ref, wrest_ref, gqk_ref, bd_ref,
                 z_ref, xbc_ref, q_ref, k_ref, v_ref, dt_ref):
    tm = x_ref.shape[0]
    xn = _rms(x_ref[...], gmix_ref[...]).astype(BF16)
    qk = _dot(xn, wqk_ref[...])
    n_slices = (ATT_DIM + KV_DIM) // KV_DIM
    sq = jnp.concatenate([qk[:, j * KV_DIM:(j + 1) * KV_DIM] for j in range(n_slices)], axis=0)
    sq = sq * sq
    hi = sq.astype(BF16)
    lo = (sq - hi.astype(F32)).astype(BF16)
    ms = _dot(jnp.concatenate([hi, lo], axis=1), bd_ref[...]) * (1.0 / HEAD_DIM)
    inv = lax.rsqrt(ms + EPS)
    for j in range(n_slices):
        sl = slice(j * KV_DIM, (j + 1) * KV_DIM)
        normed = qk[:, sl] * inv[j * tm:(j + 1) * tm, :] * gqk_ref[:, sl]
        if j < ATT_DIM // KV_DIM:
            q_ref[:, sl] = (normed * (HEAD_DIM ** -0.5 * LOG2E)).astype(BF16)
        else:
            k_ref[...] = normed
    rest = _dot(xn, wrest_ref[...])
    cuts = np.cumsum([0, SSD_DIM, CONV_DIM, KV_DIM, LANES])
    for ref, lo_c, hi_c in zip((z_ref, xbc_ref, v_ref, dt_ref), cuts[:-1], cuts[1:]):
        ref[...] = rest[:, lo_c:hi_c]


def _project(x, w, tm):
    t = x.shape[0]
    assert t % tm == 0
    row = lambda n: pl.BlockSpec((tm, n), lambda i: (i, 0))
    ins = [x, w["g_mix"], w["w_qk"], w["w_rest"], w["g_qk"], w["bd"]]
    in_specs = [row(D_MODEL)] + [_const_spec(a.shape) for a in ins[1:]]
    out_dims = (SSD_DIM, CONV_DIM, ATT_DIM, KV_DIM, KV_DIM, LANES)
    out_dtypes = (F32, F32, BF16, F32, F32, F32)
    return pl.pallas_call(
        _proj_kernel,
        grid=(t // tm,),
        in_specs=in_specs,
        out_specs=[row(n) for n in out_dims],
        out_shape=[jax.ShapeDtypeStruct((t, n), d) for n, d in zip(out_dims, out_dtypes)],
        compiler_params=pltpu.CompilerParams(
            dimension_semantics=("arbitrary",), vmem_limit_bytes=VMEM_LIMIT),
        name="projection",
    )(*ins)


N_SSD_STREAM_INPUTS = 5
N_SSD_CONSTS = 9


def _ssd_kernel(*refs, L, TB, SB):
    streamed = lambda k: k < N_SSD_STREAM_INPUTS or k >= N_SSD_STREAM_INPUTS + N_SSD_CONSTS
    for s in range(SB):
        _ssd_stream(*[r.at[s] if streamed(k) else r for k, r in enumerate(refs)], L=L, TB=TB)


def _ssd_stream(xbc_ref, z_ref, dt_ref, tail_ref, h0_ref, convw_ref, convb_ref, dtb_ref,
                alog_ref, dskip_ref, gout_ref, ehead_ref, epos_ref, tri_ref,
                y_ref, hout_ref, buf_ref, act_ref, dts_ref, ys_ref, st_ref, *, L, TB):
    hp = LANES // L
    n_lane_tiles = SSD_HEADS // hp
    hq = MXU_DIM // L
    n_tiles = SSD_HEADS // hq
    tile_w = hq * HEAD_DIM
    b = pl.program_id(1)
    pad = SUBLANES - (CONV_WIDTH - 1)

    @pl.when(b == 0)
    def _():
        buf_ref[0:SUBLANES, :] = jnp.zeros((SUBLANES, CONV_DIM), F32)
        buf_ref[pad:SUBLANES, :] = tail_ref[...]
        st_ref[...] = h0_ref[...].T

    buf_ref[SUBLANES:SUBLANES + TB, :] = xbc_ref[...]
    rows = buf_ref[...]
    acc = convb_ref[...] + convw_ref[CONV_WIDTH - 1:CONV_WIDTH, :] * rows[SUBLANES:, :]
    for j in range(CONV_WIDTH - 1):
        shifted = pltpu.roll(rows, CONV_WIDTH - 1 - j, axis=0)[SUBLANES:, :]
        acc = acc + convw_ref[j:j + 1, :] * shifted
    act_ref[...] = _silu(acc)
    buf_ref[pad:SUBLANES, :] = buf_ref[TB + pad:TB + SUBLANES, :]

    dtx = dt_ref[...] + dtb_ref[...]
    dts_ref[...] = jnp.maximum(dtx, 0.0) + jnp.log(1.0 + jnp.exp(-jnp.abs(dtx)))

    a_row = -jnp.exp(alog_ref[...]) * LOG2E
    ehead = ehead_ref[...]
    epos = epos_ref[...]
    tri = tri_ref[...]
    dskip = dskip_ref[...]
    row_i = lax.broadcasted_iota(jnp.int32, (L, n_lane_tiles * LANES), 0)
    col_i = lax.broadcasted_iota(jnp.int32, (L, n_lane_tiles * LANES), 1)
    pos_i = col_i % L
    diag_mask = row_i == pos_i
    causal_mask = row_i >= pos_i
    colhead = lax.broadcasted_iota(jnp.int32, (L, tile_w), 1) // HEAD_DIM

    def chunk(c):
        r0 = pl.multiple_of(c * L, L)
        dtc = dts_ref[pl.ds(r0, L), :]
        cs = _sel_dot(tri, dtc * a_row)
        cs16 = _pack3(cs)
        dt_b = _dot(_pack3(dtc), ehead)
        cs_b = _dot(cs16, ehead)
        cs_last_b = cs_b[L - 1:L, :]
        ecs_b = jnp.exp2(cs_b)
        dec_end_b = jnp.exp2(cs_last_b - cs_b)
        bdec_b = jnp.exp2(cs_last_b)
        cs_col = cs_b if L == HEAD_DIM else _dot(cs16, epos)
        cs_row = jnp.sum(jnp.where(diag_mask, cs_col, 0.0), axis=0, keepdims=True)
        lmat = jnp.exp2(jnp.where(causal_mask, cs_col - cs_row, NEG_INF))

        xs = act_ref[pl.ds(r0, L), 0:SSD_DIM]
        xdt_f = xs * dt_b
        xdt = xdt_f.astype(BF16)
        wx = (xdt_f * dec_end_b).astype(BF16)
        st = st_ref[...]
        st16 = st.astype(BF16)

        y_diag = []
        y_off = []
        s_new = []
        cb = [None] * SSD_GROUPS
        for g in range(SSD_GROUPS):
            bg = act_ref[pl.ds(r0, L), SSD_DIM + g * D_STATE:SSD_DIM + (g + 1) * D_STATE].astype(BF16)
            cg = act_ref[pl.ds(r0, L), SSD_DIM + (SSD_GROUPS + g) * D_STATE:
                         SSD_DIM + (SSD_GROUPS + g + 1) * D_STATE].astype(BF16)
            cb[g] = _dot_nt(cg, jnp.concatenate([bg] * hp, axis=0))
            gsl = slice(g * GROUP_DIM, (g + 1) * GROUP_DIM)
            y_off.append(_dot(cg, st16[:, gsl]))
            s_new.append(_dot_tn(bg, wx[:, gsl]))
        lane_tiles = MXU_DIM // LANES
        for t in range(n_tiles):
            cbt = jnp.concatenate(
                [cb[((t * lane_tiles + j) * hp * HEAD_DIM) // GROUP_DIM] for j in range(lane_tiles)], axis=1)
            gmat = (lmat[:, t * MXU_DIM:(t + 1) * MXU_DIM] * cbt).astype(BF16)
            xt = xdt[:, t * tile_w:(t + 1) * tile_w]
            rhs = jnp.concatenate(
                [jnp.where(colhead == hh, xt, jnp.zeros_like(xt)) for hh in range(hq)], axis=0)
            y_diag.append(_dot(gmat, rhs))
        y = (jnp.concatenate(y_diag, axis=1) + jnp.concatenate(y_off, axis=1) * ecs_b
             + dskip * xs)
        ys_ref[pl.ds(r0, L), :] = y
        st_ref[...] = bdec_b * st + jnp.concatenate(s_new, axis=1)

    n_chunks = TB // L
    unroll = 4 if n_chunks % 4 == 0 else 1

    def chunks(i, carry):
        for u in range(unroll):
            chunk(i * unroll + u)
        return carry

    lax.fori_loop(0, n_chunks // unroll, chunks, 0)

    yg = ys_ref[...] * _silu(z_ref[...])
    y_ref[...] = _rms(yg, gout_ref[...])

    @pl.when(b == pl.num_programs(1) - 1)
    def _():
        hout_ref[...] = st_ref[...].T


def _ssd(xbc, z, dt, tail, h0t, w, L, TB, SB=1):
    S, Ls, _ = xbc.shape
    assert Ls % TB == 0 and TB % L == 0 and LANES % L == 0 and S % SB == 0
    nb = Ls // TB
    assert SB == 1 or nb == 1
    hp = LANES // L
    n_tiles = SSD_HEADS // hp
    per_stream = lambda a: (lambda s, b: (s, 0, 0)) if a.shape[0] == S and S > 1 else (lambda s, b: (0, 0, 0))
    seq = lambda n: pl.BlockSpec((SB, TB, n), lambda s, b: (s, b, 0))
    ehead = np.zeros((LANES, SSD_DIM), np.float32)
    epos = np.zeros((LANES, n_tiles * LANES), np.float32)
    for part in range(DT_COPIES):
        for h in range(SSD_HEADS):
            ehead[part * SSD_HEADS + h, h * HEAD_DIM:(h + 1) * HEAD_DIM] = 1.0
            epos[part * SSD_HEADS + h, h * L:(h + 1) * L] = 1.0
    tri = np.tril(np.ones((L, L), np.float32))
    consts = [w["conv_w"], w["conv_b"], w["dt_bias"], w["a_log"], w["d_skip"], w["g_ssd_out"],
              jnp.asarray(ehead, BF16), jnp.asarray(epos, BF16), jnp.asarray(tri, BF16)]
    in_specs = [seq(CONV_DIM), seq(SSD_DIM), seq(LANES),
                pl.BlockSpec((SB, CONV_WIDTH - 1, CONV_DIM), per_stream(tail)),
                pl.BlockSpec((SB, SSD_DIM, D_STATE), per_stream(h0t))]
    assert len(in_specs) == N_SSD_STREAM_INPUTS and len(consts) == N_SSD_CONSTS
    in_specs += [_const_spec(c.shape) for c in consts]
    return pl.pallas_call(
        functools.partial(_ssd_kernel, L=L, TB=TB, SB=SB),
        grid=(S // SB, nb),
        in_specs=in_specs,
        out_specs=[seq(SSD_DIM), pl.BlockSpec((SB, SSD_DIM, D_STATE), lambda s, b: (s, 0, 0))],
        out_shape=[jax.ShapeDtypeStruct((S, Ls, SSD_DIM), F32),
                   jax.ShapeDtypeStruct((S, SSD_DIM, D_STATE), F32)],
        scratch_shapes=[pltpu.VMEM((SB, SUBLANES + TB, CONV_DIM), F32),
                        pltpu.VMEM((SB, TB, CONV_DIM), F32),
                        pltpu.VMEM((SB, TB, LANES), F32),
                        pltpu.VMEM((SB, TB, SSD_DIM), F32),
                        pltpu.VMEM((SB, D_STATE, SSD_DIM), F32)],
        compiler_params=pltpu.CompilerParams(
            dimension_semantics=("arbitrary", "arbitrary"), vmem_limit_bytes=VMEM_LIMIT),
        name="ssd",
    )(xbc, z, dt, tail, h0t, *consts)


def _split_heads(x):
    lane = lax.broadcasted_iota(jnp.int32, (x.shape[0], LANES), 1)
    low = lane < HEAD_DIM
    lo, hi = [], []
    for t in range(KV_DIM // LANES):
        tile = x[:, t * LANES:(t + 1) * LANES]
        swapped = pltpu.roll(tile, HEAD_DIM, axis=1)
        lo += [jnp.where(low, tile, 0.0), jnp.where(low, swapped, 0.0)]
        hi += [jnp.where(low, 0.0, swapped), jnp.where(low, 0.0, tile)]
    return jnp.concatenate(lo, axis=1).astype(BF16), jnp.concatenate(hi, axis=1).astype(BF16)


def _attn_kernel(*refs, L, TB, SB, from_cache):
    for s in range(SB):
        _attn_stream(*[r if k == 0 or 6 <= k <= 9 else r.at[s] for k, r in enumerate(refs)],
                     L=L, TB=TB, from_cache=from_cache)


def _attn_stream(sink_ref, q_ref, k_ref, v_ref, kp_ref, vp_ref, mk_ref, mv_ref, bias_ref, g_ref,
                 o_ref, k_lo, k_hi, v_lo, v_hi, ybuf, s_buf, e_buf, sink_buf, *, L, TB, from_cache):
    b = pl.program_id(1)
    bufs = (k_lo, k_hi, v_lo, v_hi)
    if from_cache:
        window = _split_heads(kp_ref[...]) + _split_heads(vp_ref[...])
        for buf, val in zip(bufs, window):
            buf[0:WINDOW, :] = val
    else:
        @pl.when(b == 0)
        def _():
            for buf in bufs:
                buf[0:WINDOW, :] = jnp.zeros((WINDOW, buf.shape[1]), BF16)

        @pl.when(b > 0)
        def _():
            for buf in bufs:
                buf[0:WINDOW, :] = buf[TB:TB + WINDOW, :]

    for buf, val in zip(bufs, _split_heads(k_ref[...]) + _split_heads(v_ref[...])):
        buf[WINDOW:WINDOW + TB, :] = val
    n_pad = KEY_SPAN - WINDOW - L - N_META
    pad_rows = jnp.zeros((n_pad, 2 * KV_DIM), BF16)
    tails = [jnp.concatenate([t, pad_rows], axis=0)
             for t in _split_heads(mk_ref[...]) + _split_heads(mv_ref[...])]
    lane = lax.broadcasted_iota(jnp.int32, (2 * L, 2 * KEY_SPAN), 1) % KEY_SPAN
    lane_h = lax.broadcasted_iota(jnp.int32, (L, LANES), 1)
    lane_v = lax.broadcasted_iota(jnp.int32, (KEY_SPAN, LANES), 1)
    ones_lo = jnp.where(lane_v < HEAD_DIM, 1.0, 0.0).astype(BF16)
    ones_hi = jnp.where(lane_v < HEAD_DIM, 0.0, 1.0).astype(BF16)
    n_chunks = TB // L
    pairs_per_group = ATT_REP // 2

    def chunk_keys(c):
        r0 = c * L
        return [jnp.concatenate([buf[r0:r0 + WINDOW, :], buf[r0 + WINDOW:r0 + WINDOW + L, :], tail], axis=0)
                for buf, tail in zip(bufs, tails)]

    for c in range(n_chunks):
        r0 = c * L
        keys = chunk_keys(c)
        for g in range(ATT_KV_HEADS):
            gl = slice(g * LANES, (g + 1) * LANES)
            q4 = jnp.concatenate([q_ref[r0:r0 + L, p * LANES:(p + 1) * LANES]
                                  for p in range(pairs_per_group * g, pairs_per_group * (g + 1))], axis=0)
            s = _dot_nt(q4, jnp.concatenate([keys[0][:, gl], keys[1][:, gl]], axis=0)) + bias_ref[g]
            if not from_cache and r0 < WINDOW:
                n_invalid = WINDOW - (b * TB + r0)
                s = jnp.where(lane < n_invalid, NEG_INF, s)
            s_buf[c * ATT_KV_HEADS + g] = s

    for c in range(n_chunks):
        for pair in range(ATT_HEADS // 2):
            slot = c * ATT_KV_HEADS + pair // pairs_per_group
            rows = slice((pair % pairs_per_group) * L, (pair % pairs_per_group + 1) * L)
            sink_terms = []
            for half in range(2):
                cols = slice(half * KEY_SPAN, (half + 1) * KEY_SPAN)
                s = s_buf[slot, rows, cols]
                sink = sink_ref[2 * pair + half] * LOG2E
                m = jnp.maximum(jnp.max(s, axis=-1, keepdims=True), sink)
                e_buf[slot, rows, cols] = jnp.exp2(s - m).astype(BF16)
                sink_terms.append(jnp.exp2(sink - m))
            sink_buf[c * (ATT_HEADS // 2) + pair] = jnp.where(lane_h < HEAD_DIM, sink_terms[0], sink_terms[1])

    for c in range(n_chunks):
        r0 = c * L
        keys = chunk_keys(c)
        for g in range(ATT_KV_HEADS):
            gl = slice(g * LANES, (g + 1) * LANES)
            vcat = jnp.concatenate([jnp.concatenate([keys[2][:, gl], ones_lo], axis=1),
                                    jnp.concatenate([keys[3][:, gl], ones_hi], axis=1)], axis=0)
            ov = _dot(e_buf[c * ATT_KV_HEADS + g], vcat)
            for j in range(pairs_per_group):
                pair = pairs_per_group * g + j
                den = ov[j * L:(j + 1) * L, LANES:2 * LANES] + sink_buf[c * (ATT_HEADS // 2) + pair]
                ybuf[r0:r0 + L, pair * LANES:(pair + 1) * LANES] = ov[j * L:(j + 1) * L, 0:LANES] * (1.0 / den)
    o_ref[...] = _rms(ybuf[...], g_ref[...])


def _attention(q, k, v, kprev, vprev, mk, mv, w, L, TB, from_cache, SB=1):
    S, Ls, _ = q.shape
    assert Ls % TB == 0 and TB % L == 0 and (from_cache or TB >= WINDOW) and S % SB == 0
    nb = Ls // TB
    assert SB == 1 or (from_cache and nb == 1)
    seq = lambda n: pl.BlockSpec((SB, TB, n), lambda s, b: (s, b, 0))
    if not from_cache:
        kprev = vprev = jnp.zeros((1, WINDOW, KV_DIM), F32)
    prev = pl.BlockSpec((SB, WINDOW, KV_DIM),
                        (lambda s, b: (s, 0, 0)) if from_cache else (lambda s, b: (0, 0, 0)))
    slopes = 2.0 ** (-8.0 * np.arange(1, ATT_HEADS + 1, dtype=np.float64) / ATT_HEADS)
    dist = np.abs(WINDOW + np.arange(L)[:, None] - np.arange(WINDOW + L)[None, :])
    bias = np.full((ATT_HEADS, L, KEY_SPAN), NEG_INF, np.float32)
    bias[:, :, :WINDOW + L] = -slopes[:, None, None] * dist[None] * LOG2E
    bias[:, :, WINDOW + L:WINDOW + L + N_META] = 0.0
    bias = bias.reshape(ATT_KV_HEADS, ATT_REP // 2, 2, L, KEY_SPAN).transpose(0, 1, 3, 2, 4)
    bias = jnp.asarray(bias.reshape(ATT_KV_HEADS, ATT_REP // 2 * L, 2 * KEY_SPAN))
    n_chunks = TB // L
    in_specs = [pl.BlockSpec(memory_space=pltpu.SMEM),
                seq(ATT_DIM), seq(KV_DIM), seq(KV_DIM), prev, prev,
                _const_spec(mk.shape), _const_spec(mv.shape), _const_spec(bias.shape),
                _const_spec(w["g_att_out"].shape)]
    return pl.pallas_call(
        functools.partial(_attn_kernel, L=L, TB=TB, SB=SB, from_cache=from_cache),
        grid=(S // SB, nb),
        in_specs=in_specs,
        out_specs=seq(ATT_DIM),
        out_shape=jax.ShapeDtypeStruct((S, Ls, ATT_DIM), F32),
        scratch_shapes=[pltpu.VMEM((SB, WINDOW + TB, 2 * KV_DIM), BF16) for _ in range(4)]
                       + [pltpu.VMEM((SB, TB, ATT_DIM), F32),
                          pltpu.VMEM((SB, n_chunks * ATT_KV_HEADS, 2 * L, 2 * KEY_SPAN), F32),
                          pltpu.VMEM((SB, n_chunks * ATT_KV_HEADS, 2 * L, 2 * KEY_SPAN), BF16),
                          pltpu.VMEM((SB, n_chunks * ATT_HEADS // 2, L, LANES), F32)],
        compiler_params=pltpu.CompilerParams(
            dimension_semantics=("arbitrary", "arbitrary"), vmem_limit_bytes=VMEM_LIMIT),
        name="attention",
    )(w["attn_sinks"], q, k, v, kprev, vprev, mk, mv, bias, w["g_att_out"])


TOK_TILE = 512
SEG_ALIGN = 2 * SUBLANES
SORT_ROWS = 2 * TOK_TILE + N_EXPERTS * SEG_ALIGN
XS_COLS = D_MODEL + LANES
META_P1, META_P2, META_G1, META_G2 = N_EXPERTS, N_EXPERTS + 1, N_EXPERTS + 2, N_EXPERTS + 3


def _merge_kernel(h_ref, ys_ref, ya_ref, wo_ref, gffn_ref, wr_ref, br_ref, tri_ref, upper_ref,
                  h1_ref, xn_ref, meta_ref, cnt_ref, logit_buf):
    i = pl.program_id(0)

    @pl.when(i == 0)
    def _():
        logit_buf[...] = jnp.zeros(logit_buf.shape, F32)

    ycat = jnp.concatenate([ys_ref[...].astype(BF16), ya_ref[...].astype(BF16)], axis=1)
    h1 = h_ref[...] + _dot(ycat, wo_ref[...])
    h1_ref[...] = h1
    xn = _rms(h1, gffn_ref[...]).astype(BF16)
    xn_ref[...] = xn
    all_logits = logit_buf[(i + 1) % 2]
    logit_buf[i % 2] = _dot(xn, wr_ref[...]) + br_ref[...]
    tri = tri_ref[...]
    upper = upper_ref[...]
    lane = lax.broadcasted_iota(jnp.int32, (TOK_TILE, LANES), 1)
    big = jnp.int32(LANES)
    gmask = (lane >= N_EXPERTS) & (lane < N_EXPERTS + N_EXPERT_GROUPS)
    for s in range(all_logits.shape[0] // TOK_TILE):
        rows = slice(s * TOK_TILE, (s + 1) * TOK_TILE)
        logits = all_logits[rows]

        def top1(mask):
            mval = jnp.max(jnp.where(mask, logits, NEG_INF), axis=-1, keepdims=True)
            idx = jnp.min(jnp.where(mask & (logits == mval), lane, big), axis=-1, keepdims=True)
            return mval, idx

        gmax, gidx = top1(gmask)
        gate_g = 1.0 / jnp.sum(jnp.where(gmask, jnp.exp(logits - gmax), 0.0), axis=-1, keepdims=True)
        grp = gidx - N_EXPERTS
        emask = (lane // EXPERTS_PER_GROUP) == grp
        v1, i1 = top1(emask)
        v2, i2 = top1(emask & (lane != i1))
        e2 = jnp.exp(v2 - v1)
        g1 = gate_g / (1.0 + e2)
        g2 = gate_g * e2 / (1.0 + e2)
        oh1 = jnp.where(lane == i1, 1.0, 0.0)
        oh2 = jnp.where(lane == i2, 1.0, 0.0)
        oh = oh1 + oh2
        earlier = _dot(tri, oh.astype(BF16))
        cnt = jnp.sum(oh, axis=0, keepdims=True)
        units = jnp.floor((cnt + (SEG_ALIGN - 1)) * (1.0 / SEG_ALIGN))
        units = jnp.broadcast_to(units, (2 * SUBLANES, LANES)).astype(BF16)
        slot = _dot(units, upper)[0:1, :] * SEG_ALIGN + earlier
        p1 = jnp.sum(oh1 * slot, axis=-1, keepdims=True)
        p2 = jnp.sum(oh2 * slot, axis=-1, keepdims=True)
        meta = jnp.where(lane == META_P1, p1, 0.0)
        meta = jnp.where(lane == META_P2, p2, meta)
        meta = jnp.where(lane == META_G1, g1, meta)
        meta = jnp.where(lane == META_G2, g2, meta)
        meta_ref[rows, :] = meta
        cnt_ref[s] = jnp.broadcast_to(cnt, (SUBLANES, LANES))


def _merge(h, y_ssd, y_att, w, tm):
    t = h.shape[0]
    assert t % tm == 0 and tm % TOK_TILE == 0
    sub = tm // TOK_TILE
    last = t // tm - 1
    row = lambda n: pl.BlockSpec((tm, n), lambda i: (jnp.minimum(i, last), 0))
    routed = lambda i: jnp.maximum(i - 1, 0)
    tri = jnp.asarray(np.tril(np.ones((TOK_TILE, TOK_TILE), np.float32), -1), BF16)
    upper = jnp.asarray(np.triu(np.ones((LANES, LANES), np.float32), 1), BF16)
    consts = [w["w_out"], w["g_ffn"], w["w_route"], w["b_route"], tri, upper]
    return pl.pallas_call(
        _merge_kernel,
        grid=(t // tm + 1,),
        in_specs=[row(D_MODEL), row(SSD_DIM), row(ATT_DIM)] + [_const_spec(c.shape) for c in consts],
        out_specs=[row(D_MODEL), row(D_MODEL),
                   pl.BlockSpec((tm, LANES), lambda i: (routed(i), 0)),
                   pl.BlockSpec((sub, SUBLANES, LANES), lambda i: (routed(i), 0, 0))],
        scratch_shapes=[pltpu.VMEM((2, tm, LANES), F32)],
        out_shape=[jax.ShapeDtypeStruct((t, D_MODEL), F32),
                   jax.ShapeDtypeStruct((t, D_MODEL), BF16),
                   jax.ShapeDtypeStruct((t, LANES), F32),
                   jax.ShapeDtypeStruct((t // TOK_TILE, SUBLANES, LANES), F32)],
        compiler_params=pltpu.CompilerParams(
            dimension_semantics=("arbitrary",), vmem_limit_bytes=VMEM_LIMIT),
        name="merge_route",
    )(h, y_ssd, y_att, *consts)


def _meta_col(meta, j):
    lane = lax.broadcasted_iota(jnp.int32, meta.shape, 1)
    return jnp.sum(jnp.where(lane == j, meta, 0.0), axis=-1, keepdims=True)


def _pair_selectors(meta):
    rows = lax.broadcasted_iota(jnp.int32, (meta.shape[0], SORT_ROWS), 1)
    s1 = jnp.where(rows == _meta_col(meta, META_P1).astype(jnp.int32), 1.0, 0.0).astype(BF16)
    s2 = jnp.where(rows == _meta_col(meta, META_P2).astype(jnp.int32), 1.0, 0.0).astype(BF16)
    return s1, s2


def _segment_dmas(off_ref, cnt_ref, row_ref, b, tile_buf, hbm, sem, to_hbm, wait):
    for e in range(N_EXPERTS):
        k = b * N_EXPERTS + e
        n = pl.multiple_of(cnt_ref[k], SEG_ALIGN)
        v = tile_buf.at[pl.ds(pl.multiple_of(off_ref[k], SEG_ALIGN), n)]
        h = hbm.at[pl.ds(pl.multiple_of(row_ref[k], SEG_ALIGN), n)]
        cp = pltpu.make_async_copy(v, h, sem) if to_hbm else pltpu.make_async_copy(h, v, sem)
        if wait:
            cp.wait()
        else:
            cp.start()


def _gate_lanes(g):
    hi, mid, lo = _split3(g)
    lane = lax.broadcasted_iota(jnp.int32, (g.shape[0], LANES), 1)
    out = jnp.where(lane == 0, hi.astype(F32), 0.0)
    out = jnp.where(lane == 1, mid.astype(F32), out)
    out = jnp.where(lane == 2, lo.astype(F32), out)
    return out.astype(BF16)


def _dispatch_kernel(off_ref, cnt_ref, row_ref, tail_row_ref, tail_cnt_ref, xn_ref, meta_ref, *rest,
                     tile0, first):
    xs_ref, sort_buf, zero_buf, sem, tail_sem = rest if first else rest[1:]
    b = pl.program_id(0)
    nb = pl.num_programs(0)
    slot = b % 2
    seg = functools.partial(_segment_dmas, off_ref, cnt_ref, row_ref, hbm=xs_ref, to_hbm=True)

    def tails(wait):
        def body(e, carry):
            n = pl.multiple_of(tail_cnt_ref[e], SEG_ALIGN)

            @pl.when(n > 0)
            def _():
                cp = pltpu.make_async_copy(
                    zero_buf.at[pl.ds(0, n)],
                    xs_ref.at[pl.ds(pl.multiple_of(tail_row_ref[e], SEG_ALIGN), n)], tail_sem.at[0])
                if wait:
                    cp.wait()
                else:
                    cp.start()
            return carry

        lax.fori_loop(0, N_EXPERTS, body, 0)

    @pl.when(b >= 2)
    def _():
        seg(tile0 + b - 2, sort_buf.at[slot], sem=sem.at[slot], wait=True)

    if first:
        @pl.when(b == 0)
        def _():
            zero_buf[...] = jnp.zeros(zero_buf.shape, BF16)
            tails(wait=False)

    meta = meta_ref[...]
    s1, s2 = _pair_selectors(meta)
    sort_buf[slot, :, 0:D_MODEL] = _dot_tn(s1 + s2, xn_ref[...]).astype(BF16)
    sort_buf[slot, :, D_MODEL:XS_COLS] = (_dot_tn(s1, _gate_lanes(_meta_col(meta, META_G1)))
                                          + _dot_tn(s2, _gate_lanes(_meta_col(meta, META_G2)))).astype(BF16)
    seg(tile0 + b, sort_buf.at[slot], sem=sem.at[slot], wait=False)

    @pl.when(b == nb - 1)
    def _():
        @pl.when(b >= 1)
        def _():
            seg(tile0 + b - 1, sort_buf.at[1 - slot], sem=sem.at[1 - slot], wait=True)

        seg(tile0 + b, sort_buf.at[slot], sem=sem.at[slot], wait=True)
        if first:
            tails(wait=True)


def _expert_kernel(tile_expert_ref, n_active_ref, xs_ref, wg_ref, wu_ref, wd_ref, y_ref):
    @pl.when(pl.program_id(0) < n_active_ref[0])
    def _():
        x = xs_ref[:, 0:D_MODEL]
        gate = jnp.sum(xs_ref[:, D_MODEL:XS_COLS].astype(F32), axis=-1, keepdims=True)
        gu = _dot(x, jnp.concatenate([wg_ref[...].astype(BF16), wu_ref[...].astype(BF16)], axis=1))
        hid = _silu(gu[:, 0:EXPERT_FF]) * gu[:, EXPERT_FF:2 * EXPERT_FF]
        y_ref[...] = (gate * _dot(hid.astype(BF16), wd_ref[...].astype(BF16))).astype(BF16)


def _combine_kernel(off_ref, cnt_ref, row_ref, h1_ref, meta_ref, y_ref, o_ref, y_buf, sem, *, tile0):
    b = pl.program_id(0)
    nb = pl.num_programs(0)
    slot = b % 2
    seg = functools.partial(_segment_dmas, off_ref, cnt_ref, row_ref, hbm=y_ref, to_hbm=False)

    @pl.when(b == 0)
    def _():
        y_buf[...] = jnp.zeros(y_buf.shape, BF16)
        seg(tile0 + b, y_buf.at[slot], sem=sem.at[slot], wait=False)

    @pl.when(b + 1 < nb)
    def _():
        seg(tile0 + b + 1, y_buf.at[1 - slot], sem=sem.at[1 - slot], wait=False)

    seg(tile0 + b, y_buf.at[slot], sem=sem.at[slot], wait=True)
    s1, s2 = _pair_selectors(meta_ref[...])
    sel = s1 + s2
    o_ref[...] = h1_ref[...] + _dot(sel, y_buf[slot, 0:SORT_ROWS, :])


def _moe(parts, w, tm):
    i32 = jnp.int32
    n_tiles = [p[0].shape[0] // TOK_TILE for p in parts]
    nb = sum(n_tiles)
    t = nb * TOK_TILE
    cnt = jnp.concatenate([p[3][:, 0, :N_EXPERTS] for p in parts], axis=0).astype(i32)
    cnt_al = (cnt + (SEG_ALIGN - 1)) // SEG_ALIGN * SEG_ALIGN
    off = jnp.cumsum(cnt_al, axis=1) - cnt_al
    tot = jnp.sum(cnt_al, axis=0)
    tot_tm = (tot + (tm - 1)) // tm * tm
    start = jnp.cumsum(tot_tm) - tot_tm
    row = start[None, :] + jnp.cumsum(cnt_al, axis=0) - cnt_al
    tile_ends = jnp.cumsum(tot_tm // tm)
    max_rows = 2 * t + (SEG_ALIGN - 1) * min(N_EXPERTS * nb, 2 * t) + N_EXPERTS * (tm - SEG_ALIGN)
    nt = -(-max_rows // tm)
    tile_expert = jnp.sum((jnp.arange(nt, dtype=i32)[:, None] >= tile_ends[None, :]).astype(i32), axis=1)
    tile_expert = jnp.minimum(tile_expert, N_EXPERTS - 1)
    n_active = tile_ends[-1:].astype(i32)
    empty = cnt_al == 0
    flat = lambda a: a.reshape(-1).astype(i32)
    n_rows = flat(jnp.maximum(cnt_al, SEG_ALIGN))
    e_idx = jnp.arange(N_EXPERTS, dtype=i32)[None, :]
    spare = nt * tm + ((jnp.arange(nb, dtype=i32)[:, None] % 2) * N_EXPERTS + e_idx) * SEG_ALIGN
    dispatch_tables = [flat(off), n_rows, flat(jnp.where(empty, spare, row))]
    combine_tables = [flat(jnp.where(empty, SORT_ROWS + e_idx * SEG_ALIGN, off)), n_rows,
                      flat(jnp.where(empty, 0, row))]
    n_spare = 2 * N_EXPERTS * SEG_ALIGN
    tail_tables = [(start + tot).astype(i32), (tot_tm - tot).astype(i32)]

    tok = lambda n: pl.BlockSpec((TOK_TILE, n), lambda i, *_: (i, 0))
    hbm = pl.BlockSpec(memory_space=pl.ANY)
    n_prefetch = len(dispatch_tables) + len(tail_tables)
    xs = None
    tile0 = 0
    for (_, xn, meta, _), n in zip(parts, n_tiles):
        first = xs is None
        xs = pl.pallas_call(
            functools.partial(_dispatch_kernel, tile0=tile0, first=first),
            grid_spec=pltpu.PrefetchScalarGridSpec(
                num_scalar_prefetch=n_prefetch, grid=(n,),
                in_specs=[tok(D_MODEL), tok(LANES)] + ([] if first else [hbm]),
                out_specs=hbm,
                scratch_shapes=[pltpu.VMEM((2, SORT_ROWS, XS_COLS), BF16),
                                pltpu.VMEM((tm, XS_COLS), BF16),
                                pltpu.SemaphoreType.DMA((2,)),
                                pltpu.SemaphoreType.DMA((1,))]),
            out_shape=jax.ShapeDtypeStruct((nt * tm + n_spare, XS_COLS), BF16),
            input_output_aliases={} if first else {n_prefetch + 2: 0},
            compiler_params=pltpu.CompilerParams(
                dimension_semantics=("arbitrary",), vmem_limit_bytes=VMEM_LIMIT),
            name="dispatch",
        )(*dispatch_tables, *tail_tables, xn, meta, *([] if first else [xs]))
        tile0 += n

    act = lambda i, te, na: jnp.minimum(i, na[0] - 1)
    y = pl.pallas_call(
        _expert_kernel,
        grid_spec=pltpu.PrefetchScalarGridSpec(
            num_scalar_prefetch=2, grid=(nt,),
            in_specs=[pl.BlockSpec((tm, XS_COLS), lambda i, te, na: (act(i, te, na), 0)),
                      pl.BlockSpec((None, D_MODEL, EXPERT_FF), lambda i, te, na: (te[act(i, te, na)], 0, 0)),
                      pl.BlockSpec((None, D_MODEL, EXPERT_FF), lambda i, te, na: (te[act(i, te, na)], 0, 0)),
                      pl.BlockSpec((None, EXPERT_FF, D_MODEL), lambda i, te, na: (te[act(i, te, na)], 0, 0))],
            out_specs=pl.BlockSpec((tm, D_MODEL), lambda i, te, na: (act(i, te, na), 0))),
        out_shape=jax.ShapeDtypeStruct((nt * tm, D_MODEL), BF16),
        compiler_params=pltpu.CompilerParams(
            dimension_semantics=("arbitrary",), vmem_limit_bytes=VMEM_LIMIT),
        name="experts",
    )(tile_expert, n_active, xs, w["w_gate"], w["w_up"], w["w_down"])

    outs = []
    tile0 = 0
    for (h1, _, meta, _), n in zip(parts, n_tiles):
        outs.append(pl.pallas_call(
            functools.partial(_combine_kernel, tile0=tile0),
            grid_spec=pltpu.PrefetchScalarGridSpec(
                num_scalar_prefetch=len(combine_tables), grid=(n,),
                in_specs=[tok(D_MODEL), tok(LANES), hbm],
                out_specs=tok(D_MODEL),
                scratch_shapes=[pltpu.VMEM((2, SORT_ROWS + N_EXPERTS * SEG_ALIGN, D_MODEL), BF16),
                                pltpu.SemaphoreType.DMA((2,))]),
            out_shape=jax.ShapeDtypeStruct((n * TOK_TILE, D_MODEL), F32),
            compiler_params=pltpu.CompilerParams(
                dimension_semantics=("arbitrary",), vmem_limit_bytes=VMEM_LIMIT),
            name="combine",
        )(*combine_tables, h1, meta, y))
        tile0 += n
    return outs


def _prepare_weights(g_mix, w_in, conv_w, conv_b, dt_bias, a_log, d_skip, g_ssd_out, g_q, g_k,
                     attn_sinks, g_att_out, w_out, g_ffn, w_route_group, b_route_group,
                     w_route_expert, b_route_expert, w_gate, w_up, w_down):
    cuts = np.cumsum([0, SSD_DIM, CONV_DIM, SSD_HEADS, ATT_DIM, KV_DIM, KV_DIM])
    seg = lambda i: w_in[:, cuts[i]:cuts[i + 1]]
    pad_lanes = lambda a: jnp.pad(a, ((0, 0), (0, LANES - a.shape[1])))
    bd = np.kron(np.eye(KV_DIM // HEAD_DIM, dtype=np.float32), np.ones((HEAD_DIM, HEAD_DIM), np.float32))
    n_route = N_EXPERTS + N_EXPERT_GROUPS
    return {
        "g_mix": g_mix.reshape(1, D_MODEL),
        "w_qk": jnp.concatenate([seg(3), seg(4)], axis=1).astype(BF16),
        "w_rest": jnp.concatenate([seg(0), seg(1), seg(5), pad_lanes(jnp.tile(seg(2), (1, DT_COPIES)))],
                                  axis=1).astype(BF16),
        "g_qk": jnp.concatenate([jnp.tile(g_q, ATT_HEADS), jnp.tile(g_k, ATT_KV_HEADS)]).reshape(1, ATT_DIM + KV_DIM),
        "bd": jnp.asarray(np.concatenate([bd, bd], axis=0), BF16),
        "conv_w": conv_w, "conv_b": conv_b.reshape(1, CONV_DIM),
        "dt_bias": pad_lanes(jnp.tile(dt_bias, DT_COPIES).reshape(1, DT_COPIES * SSD_HEADS)),
        "a_log": pad_lanes(jnp.tile(a_log, DT_COPIES).reshape(1, DT_COPIES * SSD_HEADS)),
        "d_skip": jnp.repeat(d_skip, HEAD_DIM).reshape(1, SSD_DIM),
        "g_ssd_out": g_ssd_out.reshape(1, SSD_DIM),
        "attn_sinks": attn_sinks,
        "g_att_out": g_att_out.reshape(1, ATT_DIM),
        "w_out": w_out.astype(BF16),
        "g_ffn": g_ffn.reshape(1, D_MODEL),
        "w_route": pad_lanes(jnp.concatenate([w_route_expert, w_route_group], axis=1)).astype(BF16),
        "b_route": pad_lanes(jnp.concatenate([b_route_expert, b_route_group]).reshape(1, n_route)),
        "w_gate": w_gate, "w_up": w_up, "w_down": w_down,
    }


def _segment(x3, w, mk, mv, tail, h0t, kprev, vprev, L, tb_ssd, tb_att, tm, from_cache, sb=1):
    S, Ls, _ = x3.shape
    x = x3.reshape(S * Ls, D_MODEL)
    z, xbc, q, k, v, dt = _project(x, w, tm)
    r3 = lambda a: a.reshape(S, Ls, a.shape[-1])
    xbc3, k3, v3 = r3(xbc), r3(k), r3(v)
    y_ssd, h_new = _ssd(xbc3, r3(z), r3(dt), tail, h0t, w, L, tb_ssd, sb)
    if not from_cache:
        kprev, vprev = k3, v3
    y_att = _attention(r3(q), k3, v3, kprev, vprev, mk, mv, w, L, tb_att, from_cache, sb)
    part = _merge(x, y_ssd.reshape(S * Ls, SSD_DIM), y_att.reshape(S * Ls, ATT_DIM), w, tm)
    return part, xbc3, h_new, k3, v3


def _state_to_kernel(h):
    return h.reshape(h.shape[0], SSD_DIM, D_STATE)


def _state_from_kernel(hk):
    return hk.reshape(hk.shape[0], SSD_HEADS, HEAD_DIM, D_STATE)


def kernel(x_prompt, x_sample, cache_conv, state_ssd, cache_k, cache_v, meta_tokens, g_mix, w_in, conv_w, conv_b, dt_bias, a_log, d_skip, g_ssd_out, g_q, g_k, attn_sinks, g_att_out, w_out, g_ffn, w_route_group, b_route_group, w_route_expert, b_route_expert, w_gate, w_up, w_down):
    w = _prepare_weights(g_mix[0], w_in[0], conv_w[0], conv_b[0], dt_bias[0], a_log[0], d_skip[0],
                         g_ssd_out[0], g_q[0], g_k[0], attn_sinks[0], g_att_out[0], w_out[0], g_ffn[0],
                         w_route_group[0], b_route_group[0], w_route_expert[0], b_route_expert[0],
                         w_gate[0], w_up[0], w_down[0])
    n_b = x_sample.shape[0]
    n_dec = x_sample.shape[1]

    _, m_xbc, _, mk, mv, m_dt = _project(meta_tokens, w, N_META)
    zero_tail = jnp.zeros((1, CONV_WIDTH - 1, CONV_DIM), F32)
    zero_state = jnp.zeros((1, SSD_DIM, D_STATE), F32)
    m_xbc3 = m_xbc.reshape(1, N_META, CONV_DIM)
    _, m_state = _ssd(m_xbc3, jnp.zeros((1, N_META, SSD_DIM), F32), m_dt.reshape(1, N_META, LANES),
                      zero_tail, zero_state, w, N_META, N_META)
    m_tail = m_xbc3[:, N_META - (CONV_WIDTH - 1):]

    part_p, xbc_p, st_p, k_p, v_p = _segment(
        x_prompt, w, mk, mv, m_tail, m_state, None, None,
        L=CHUNK, tb_ssd=256, tb_att=512, tm=512, from_cache=False)
    part_s, xbc_s, st_s, k_s, v_s = _segment(
        x_sample, w, mk, mv, cache_conv[0], _state_to_kernel(state_ssd[0]),
        cache_k[0].reshape(n_b, WINDOW, KV_DIM), cache_v[0].reshape(n_b, WINDOW, KV_DIM),
        L=n_dec, tb_ssd=n_dec, tb_att=n_dec, tm=512, from_cache=True, sb=8)
    yp, ys = _moe([part_p, part_s], w, tm=512)
    yp = yp.reshape(x_prompt.shape)
    ys = ys.reshape(x_sample.shape)

    heads = lambda a, rows: a.reshape(a.shape[0], rows, ATT_KV_HEADS, HEAD_DIM)[None]
    return (yp, ys,
            xbc_p[:, -(CONV_WIDTH - 1):][None],
            _state_from_kernel(st_p)[None],
            heads(k_p[:, -WINDOW:], WINDOW), heads(v_p[:, -WINDOW:], WINDOW),
            xbc_s[:, -(CONV_WIDTH - 1):][None],
            _state_from_kernel(st_s)[None],
            heads(k_s, n_dec), heads(v_s, n_dec))
```

```python
import functools
import math

import numpy as np
import jax
import jax.numpy as jnp
from jax import lax
from jax.experimental import pallas as pl
from jax.experimental.pallas import tpu as pltpu

D_MODEL = 1024
CHUNK = 64
N_META = 16
HEAD_DIM = 64
ATT_HEADS = 16
ATT_KV_HEADS = 4
ATT_REP = ATT_HEADS // ATT_KV_HEADS
ATT_DIM = ATT_HEADS * HEAD_DIM
KV_DIM = ATT_KV_HEADS * HEAD_DIM
WINDOW = 128
SSD_HEADS = 16
SSD_DIM = SSD_HEADS * HEAD_DIM
SSD_GROUPS = 2
GROUP_DIM = SSD_DIM // SSD_GROUPS
D_STATE = 128
CONV_WIDTH = 4
CONV_DIM = SSD_DIM + 2 * SSD_GROUPS * D_STATE
N_EXPERT_GROUPS = 4
EXPERTS_PER_GROUP = 8
N_EXPERTS = N_EXPERT_GROUPS * EXPERTS_PER_GROUP
EXPERT_FF = D_MODEL // 4
EPS = 1e-6

LANES = 128
SUBLANES = 8
MXU_DIM = 256
DT_COPIES = 3
KEY_SPAN = 256
VMEM_LIMIT = 48 * 1024 * 1024

F32 = jnp.float32
BF16 = jnp.bfloat16
NEG_INF = float("-inf")
LOG2E = math.log2(math.e)


def _dot(a, b):
    return jnp.dot(a, b, preferred_element_type=F32)


def _dot_nt(a, b):
    return lax.dot_general(a, b, (((1,), (1,)), ((), ())), preferred_element_type=F32)


def _dot_tn(a, b):
    return lax.dot_general(a, b, (((0,), (0,)), ((), ())), preferred_element_type=F32)


def _split3(x):
    hi = x.astype(BF16)
    r = x - hi.astype(F32)
    mid = r.astype(BF16)
    lo = (r - mid.astype(F32)).astype(BF16)
    return hi, mid, lo


def _pack3(x):
    hi, mid, lo = _split3(x)
    lane = lax.broadcasted_iota(jnp.int32, x.shape, 1)
    packed = jnp.where(lane < SSD_HEADS, hi.astype(F32),
                       jnp.where(lane < 2 * SSD_HEADS, mid.astype(F32), lo.astype(F32)))
    return packed.astype(BF16)


def _sel_dot(sel, x):
    hi, mid, lo = _split3(x)
    return _dot(sel, hi) + _dot(sel, mid) + _dot(sel, lo)


def _silu(x):
    return x * (1.0 / (1.0 + jnp.exp2(x * (-LOG2E))))


def _rms(x, g):
    ms = jnp.mean(x * x, axis=-1, keepdims=True)
    return x * lax.rsqrt(ms + EPS) * g


def _const_spec(shape):
    n = len(shape)
    return pl.BlockSpec(shape, lambda *_: (0,) * n, pipeline_mode=pl.Buffered(1))


def _proj_kernel(x_ref, gmix_ref, wqk_ref, wrest_ref, gqk_ref, bd_ref,
                 z_ref, xbc_ref, q_ref, k_ref, v_ref, dt_ref):
    tm = x_ref.shape[0]
    xn = _rms(x_ref[...], gmix_ref[...]).astype(BF16)
    qk = _dot(xn, wqk_ref[...])
    n_slices = (ATT_DIM + KV_DIM) // KV_DIM
    sq = jnp.concatenate([qk[:, j * KV_DIM:(j + 1) * KV_DIM] for j in range(n_slices)], axis=0)
    sq = sq * sq
    hi = sq.astype(BF16)
    lo = (sq - hi.astype(F32)).astype(BF16)
    ms = _dot(jnp.concatenate([hi, lo], axis=1), bd_ref[...]) * (1.0 / HEAD_DIM)
    inv = lax.rsqrt(ms + EPS)
    for j in range(n_slices):
        sl = slice(j * KV_DIM, (j + 1) * KV_DIM)
        normed = qk[:, sl] * inv[j * tm:(j + 1) * tm, :] * gqk_ref[:, sl]
        if j < ATT_DIM // KV_DIM:
            q_ref[:, sl] = (normed * (HEAD_DIM ** -0.5 * LOG2E)).astype(BF16)
        else:
            k_ref[...] = normed
    rest = _dot(xn, wrest_ref[...])
    cuts = np.cumsum([0, SSD_DIM, CONV_DIM, KV_DIM, LANES])
    for ref, lo_c, hi_c in zip((z_ref, xbc_ref, v_ref, dt_ref), cuts[:-1], cuts[1:]):
        ref[...] = rest[:, lo_c:hi_c]


def _project(x, w, tm):
    t = x.shape[0]
    assert t % tm == 0
    row = lambda n: pl.BlockSpec((tm, n), lambda i: (i, 0))
    ins = [x, w["g_mix"], w["w_qk"], w["w_rest"], w["g_qk"], w["bd"]]
    in_specs = [row(D_MODEL)] + [_const_spec(a.shape) for a in ins[1:]]
    out_dims = (SSD_DIM, CONV_DIM, ATT_DIM, KV_DIM, KV_DIM, LANES)
    out_dtypes = (F32, F32, BF16, F32, F32, F32)
    return pl.pallas_call(
        _proj_kernel,
        grid=(t // tm,),
        in_specs=in_specs,
        out_specs=[row(n) for n in out_dims],
        out_shape=[jax.ShapeDtypeStruct((t, n), d) for n, d in zip(out_dims, out_dtypes)],
        compiler_params=pltpu.CompilerParams(
            dimension_semantics=("arbitrary",), vmem_limit_bytes=VMEM_LIMIT),
        name="projection",
    )(*ins)


N_SSD_STREAM_INPUTS = 5
N_SSD_CONSTS = 9


def _ssd_kernel(*refs, L, TB, SB):
    streamed = lambda k: k < N_SSD_STREAM_INPUTS or k >= N_SSD_STREAM_INPUTS + N_SSD_CONSTS
    for s in range(SB):
        _ssd_stream(*[r.at[s] if streamed(k) else r for k, r in enumerate(refs)], L=L, TB=TB)


def _ssd_stream(xbc_ref, z_ref, dt_ref, tail_ref, h0_ref, convw_ref, convb_ref, dtb_ref,
                alog_ref, dskip_ref, gout_ref, ehead_ref, epos_ref, tri_ref,
                y_ref, hout_ref, buf_ref, act_ref, dts_ref, ys_ref, st_ref, *, L, TB):
    hp = LANES // L
    n_lane_tiles = SSD_HEADS // hp
    hq = MXU_DIM // L
    n_tiles = SSD_HEADS // hq
    tile_w = hq * HEAD_DIM
    b = pl.program_id(1)
    pad = SUBLANES - (CONV_WIDTH - 1)

    @pl.when(b == 0)
    def _():
        buf_ref[0:SUBLANES, :] = jnp.zeros((SUBLANES, CONV_DIM), F32)
        buf_ref[pad:SUBLANES, :] = tail_ref[...]
        st_ref[...] = h0_ref[...].T

    buf_ref[SUBLANES:SUBLANES + TB, :] = xbc_ref[...]
    rows = buf_ref[...]
    acc = convb_ref[...] + convw_ref[CONV_WIDTH - 1:CONV_WIDTH, :] * rows[SUBLANES:, :]
    for j in range(CONV_WIDTH - 1):
        shifted = pltpu.roll(rows, CONV_WIDTH - 1 - j, axis=0)[SUBLANES:, :]
        acc = acc + convw_ref[j:j + 1, :] * shifted
    act_ref[...] = _silu(acc)
    buf_ref[pad:SUBLANES, :] = buf_ref[TB + pad:TB + SUBLANES, :]

    dtx = dt_ref[...] + dtb_ref[...]
    dts_ref[...] = jnp.maximum(dtx, 0.0) + jnp.log(1.0 + jnp.exp(-jnp.abs(dtx)))

    a_row = -jnp.exp(alog_ref[...]) * LOG2E
    ehead = ehead_ref[...]
    epos = epos_ref[...]
    tri = tri_ref[...]
    dskip = dskip_ref[...]
    row_i = lax.broadcasted_iota(jnp.int32, (L, n_lane_tiles * LANES), 0)
    col_i = lax.broadcasted_iota(jnp.int32, (L, n_lane_tiles * LANES), 1)
    pos_i = col_i % L
    diag_mask = row_i == pos_i
    causal_mask = row_i >= pos_i
    colhead = lax.broadcasted_iota(jnp.int32, (L, tile_w), 1) // HEAD_DIM

    def chunk(c):
        r0 = pl.multiple_of(c * L, L)
        dtc = dts_ref[pl.ds(r0, L), :]
        cs = _sel_dot(tri, dtc * a_row)
        cs16 = _pack3(cs)
        dt_b = _dot(_pack3(dtc), ehead)
        cs_b = _dot(cs16, ehead)
        cs_last_b = cs_b[L - 1:L, :]
        ecs_b = jnp.exp2(cs_b)
        dec_end_b = jnp.exp2(cs_last_b - cs_b)
        bdec_b = jnp.exp2(cs_last_b)
        cs_col = cs_b if L == HEAD_DIM else _dot(cs16, epos)
        cs_row = jnp.sum(jnp.where(diag_mask, cs_col, 0.0), axis=0, keepdims=True)
        lmat = jnp.exp2(jnp.where(causal_mask, cs_col - cs_row, NEG_INF))

        xs = act_ref[pl.ds(r0, L), 0:SSD_DIM]
        xdt_f = xs * dt_b
        xdt = xdt_f.astype(BF16)
        wx = (xdt_f * dec_end_b).astype(BF16)
        st = st_ref[...]
        st16 = st.astype(BF16)

        y_diag = []
        y_off = []
        s_new = []
        cb = [None] * SSD_GROUPS
        for g in range(SSD_GROUPS):
            bg = act_ref[pl.ds(r0, L), SSD_DIM + g * D_STATE:SSD_DIM + (g + 1) * D_STATE].astype(BF16)
            cg = act_ref[pl.ds(r0, L), SSD_DIM + (SSD_GROUPS + g) * D_STATE:
                         SSD_DIM + (SSD_GROUPS + g + 1) * D_STATE].astype(BF16)
            cb[g] = _dot_nt(cg, jnp.concatenate([bg] * hp, axis=0))
            gsl = slice(g * GROUP_DIM, (g + 1) * GROUP_DIM)
            y_off.append(_dot(cg, st16[:, gsl]))
            s_new.append(_dot_tn(bg, wx[:, gsl]))
        lane_tiles = MXU_DIM // LANES
        for t in range(n_tiles):
            cbt = jnp.concatenate(
                [cb[((t * lane_tiles + j) * hp * HEAD_DIM) // GROUP_DIM] for j in range(lane_tiles)], axis=1)
            gmat = (lmat[:, t * MXU_DIM:(t + 1) * MXU_DIM] * cbt).astype(BF16)
            xt = xdt[:, t * tile_w:(t + 1) * tile_w]
            rhs = jnp.concatenate(
                [jnp.where(colhead == hh, xt, jnp.zeros_like(xt)) for hh in range(hq)], axis=0)
            y_diag.append(_dot(gmat, rhs))
        y = (jnp.concatenate(y_diag, axis=1) + jnp.concatenate(y_off, axis=1) * ecs_b
             + dskip * xs)
        ys_ref[pl.ds(r0, L), :] = y
        st_ref[...] = bdec_b * st + jnp.concatenate(s_new, axis=1)

    n_chunks = TB // L
    unroll = n_chunks if n_chunks in (4, 8) else 1

    def chunks(i, carry):
        for u in range(unroll):
            chunk(i * unroll + u)
        return carry

    lax.fori_loop(0, n_chunks // unroll, chunks, 0)

    yg = ys_ref[...] * _silu(z_ref[...])
    y_ref[...] = _rms(yg, gout_ref[...])

    @pl.when(b == pl.num_programs(1) - 1)
    def _():
        hout_ref[...] = st_ref[...].T


def _ssd(xbc, z, dt, tail, h0t, w, L, TB, SB=1):
    S, Ls, _ = xbc.shape
    assert Ls % TB == 0 and TB % L == 0 and LANES % L == 0 and S % SB == 0
    nb = Ls // TB
    assert SB == 1 or nb == 1
    hp = LANES // L
    n_tiles = SSD_HEADS // hp
    per_stream = lambda a: (lambda s, b: (s, 0, 0)) if a.shape[0] == S and S > 1 else (lambda s, b: (0, 0, 0))
    seq = lambda n: pl.BlockSpec((SB, TB, n), lambda s, b: (s, b, 0))
    ehead = np.zeros((LANES, SSD_DIM), np.float32)
    epos = np.zeros((LANES, n_tiles * LANES), np.float32)
    for part in range(DT_COPIES):
        for h in range(SSD_HEADS):
            ehead[part * SSD_HEADS + h, h * HEAD_DIM:(h + 1) * HEAD_DIM] = 1.0
            epos[part * SSD_HEADS + h, h * L:(h + 1) * L] = 1.0
    tri = np.tril(np.ones((L, L), np.float32))
    consts = [w["conv_w"], w["conv_b"], w["dt_bias"], w["a_log"], w["d_skip"], w["g_ssd_out"],
              jnp.asarray(ehead, BF16), jnp.asarray(epos, BF16), jnp.asarray(tri, BF16)]
    in_specs = [seq(CONV_DIM), seq(SSD_DIM), seq(LANES),
                pl.BlockSpec((SB, CONV_WIDTH - 1, CONV_DIM), per_stream(tail)),
                pl.BlockSpec((SB, SSD_DIM, D_STATE), per_stream(h0t))]
    assert len(in_specs) == N_SSD_STREAM_INPUTS and len(consts) == N_SSD_CONSTS
    in_specs += [_const_spec(c.shape) for c in consts]
    return pl.pallas_call(
        functools.partial(_ssd_kernel, L=L, TB=TB, SB=SB),
        grid=(S // SB, nb),
        in_specs=in_specs,
        out_specs=[seq(SSD_DIM), pl.BlockSpec((SB, SSD_DIM, D_STATE), lambda s, b: (s, 0, 0))],
        out_shape=[jax.ShapeDtypeStruct((S, Ls, SSD_DIM), F32),
                   jax.ShapeDtypeStruct((S, SSD_DIM, D_STATE), F32)],
        scratch_shapes=[pltpu.VMEM((SB, SUBLANES + TB, CONV_DIM), F32),
                        pltpu.VMEM((SB, TB, CONV_DIM), F32),
                        pltpu.VMEM((SB, TB, LANES), F32),
                        pltpu.VMEM((SB, TB, SSD_DIM), F32),
                        pltpu.VMEM((SB, D_STATE, SSD_DIM), F32)],
        compiler_params=pltpu.CompilerParams(
            dimension_semantics=("arbitrary", "arbitrary"), vmem_limit_bytes=VMEM_LIMIT),
        name="ssd",
    )(xbc, z, dt, tail, h0t, *consts)


def _split_heads(x):
    lane = lax.broadcasted_iota(jnp.int32, (x.shape[0], LANES), 1)
    low = lane < HEAD_DIM
    lo, hi = [], []
    for t in range(KV_DIM // LANES):
        tile = x[:, t * LANES:(t + 1) * LANES]
        swapped = pltpu.roll(tile, HEAD_DIM, axis=1)
        lo += [jnp.where(low, tile, 0.0), jnp.where(low, swapped, 0.0)]
        hi += [jnp.where(low, 0.0, swapped), jnp.where(low, 0.0, tile)]
    return jnp.concatenate(lo, axis=1).astype(BF16), jnp.concatenate(hi, axis=1).astype(BF16)


def _attn_kernel(*refs, L, TB, SB, from_cache):
    for s in range(SB):
        _attn_stream(*[r if k == 0 or 6 <= k <= 9 else r.at[s] for k, r in enumerate(refs)],
                     L=L, TB=TB, from_cache=from_cache)


def _attn_stream(sink_ref, q_ref, k_ref, v_ref, kp_ref, vp_ref, mk_ref, mv_ref, bias_ref, g_ref,
                 o_ref, k_lo, k_hi, v_lo, v_hi, ybuf, s_buf, e_buf, sink_buf, *, L, TB, from_cache):
    b = pl.program_id(1)
    bufs = (k_lo, k_hi, v_lo, v_hi)
    if from_cache:
        window = _split_heads(kp_ref[...]) + _split_heads(vp_ref[...])
        for buf, val in zip(bufs, window):
            buf[0:WINDOW, :] = val
    else:
        @pl.when(b == 0)
        def _():
            for buf in bufs:
                buf[0:WINDOW, :] = jnp.zeros((WINDOW, buf.shape[1]), BF16)

        @pl.when(b > 0)
        def _():
            for buf in bufs:
                buf[0:WINDOW, :] = buf[TB:TB + WINDOW, :]

    for buf, val in zip(bufs, _split_heads(k_ref[...]) + _split_heads(v_ref[...])):
        buf[WINDOW:WINDOW + TB, :] = val
    n_pad = KEY_SPAN - WINDOW - L - N_META
    pad_rows = jnp.zeros((n_pad, 2 * KV_DIM), BF16)
    tails = [jnp.concatenate([t, pad_rows], axis=0)
             for t in _split_heads(mk_ref[...]) + _split_heads(mv_ref[...])]
    lane = lax.broadcasted_iota(jnp.int32, (2 * L, 2 * KEY_SPAN), 1) % KEY_SPAN
    lane_h = lax.broadcasted_iota(jnp.int32, (L, LANES), 1)
    lane_v = lax.broadcasted_iota(jnp.int32, (KEY_SPAN, LANES), 1)
    ones_lo = jnp.where(lane_v < HEAD_DIM, 1.0, 0.0).astype(BF16)
    ones_hi = jnp.where(lane_v < HEAD_DIM, 0.0, 1.0).astype(BF16)
    n_chunks = TB // L
    pairs_per_group = ATT_REP // 2

    def chunk_keys(c):
        r0 = c * L
        return [jnp.concatenate([buf[r0:r0 + WINDOW, :], buf[r0 + WINDOW:r0 + WINDOW + L, :], tail], axis=0)
                for buf, tail in zip(bufs, tails)]

    for c in range(n_chunks):
        r0 = c * L
        keys = chunk_keys(c)
        for g in range(ATT_KV_HEADS):
            gl = slice(g * LANES, (g + 1) * LANES)
            q4 = jnp.concatenate([q_ref[r0:r0 + L, p * LANES:(p + 1) * LANES]
                                  for p in range(pairs_per_group * g, pairs_per_group * (g + 1))], axis=0)
            s = _dot_nt(q4, jnp.concatenate([keys[0][:, gl], keys[1][:, gl]], axis=0)) + bias_ref[g]
            if not from_cache and r0 < WINDOW:
                n_invalid = WINDOW - (b * TB + r0)
                s = jnp.where(lane < n_invalid, NEG_INF, s)
            s_buf[c * ATT_KV_HEADS + g] = s

    for c in range(n_chunks):
        for pair in range(ATT_HEADS // 2):
            slot = c * ATT_KV_HEADS + pair // pairs_per_group
            rows = slice((pair % pairs_per_group) * L, (pair % pairs_per_group + 1) * L)
            sink_terms = []
            for half in range(2):
                cols = slice(half * KEY_SPAN, (half + 1) * KEY_SPAN)
                s = s_buf[slot, rows, cols]
                sink = sink_ref[2 * pair + half] * LOG2E
                m = jnp.maximum(jnp.max(s, axis=-1, keepdims=True), sink)
                e_buf[slot, rows, cols] = jnp.exp2(s - m).astype(BF16)
                sink_terms.append(jnp.exp2(sink - m))
            sink_buf[c * (ATT_HEADS // 2) + pair] = jnp.where(lane_h < HEAD_DIM, sink_terms[0], sink_terms[1])

    for c in range(n_chunks):
        r0 = c * L
        keys = chunk_keys(c)
        for g in range(ATT_KV_HEADS):
            gl = slice(g * LANES, (g + 1) * LANES)
            vcat = jnp.concatenate([jnp.concatenate([keys[2][:, gl], ones_lo], axis=1),
                                    jnp.concatenate([keys[3][:, gl], ones_hi], axis=1)], axis=0)
            ov = _dot(e_buf[c * ATT_KV_HEADS + g], vcat)
            for j in range(pairs_per_group):
                pair = pairs_per_group * g + j
                den = ov[j * L:(j + 1) * L, LANES:2 * LANES] + sink_buf[c * (ATT_HEADS // 2) + pair]
                ybuf[r0:r0 + L, pair * LANES:(pair + 1) * LANES] = ov[j * L:(j + 1) * L, 0:LANES] * (1.0 / den)
    o_ref[...] = _rms(ybuf[...], g_ref[...])


def _attention(q, k, v, kprev, vprev, mk, mv, w, L, TB, from_cache, SB=1):
    S, Ls, _ = q.shape
    assert Ls % TB == 0 and TB % L == 0 and (from_cache or TB >= WINDOW) and S % SB == 0
    nb = Ls // TB
    assert SB == 1 or (from_cache and nb == 1)
    seq = lambda n: pl.BlockSpec((SB, TB, n), lambda s, b: (s, b, 0))
    if not from_cache:
        kprev = vprev = jnp.zeros((1, WINDOW, KV_DIM), F32)
    prev = pl.BlockSpec((SB, WINDOW, KV_DIM),
                        (lambda s, b: (s, 0, 0)) if from_cache else (lambda s, b: (0, 0, 0)))
    slopes = 2.0 ** (-8.0 * np.arange(1, ATT_HEADS + 1, dtype=np.float64) / ATT_HEADS)
    dist = np.abs(WINDOW + np.arange(L)[:, None] - np.arange(WINDOW + L)[None, :])
    bias = np.full((ATT_HEADS, L, KEY_SPAN), NEG_INF, np.float32)
    bias[:, :, :WINDOW + L] = -slopes[:, None, None] * dist[None] * LOG2E
    bias[:, :, WINDOW + L:WINDOW + L + N_META] = 0.0
    bias = bias.reshape(ATT_KV_HEADS, ATT_REP // 2, 2, L, KEY_SPAN).transpose(0, 1, 3, 2, 4)
    bias = jnp.asarray(bias.reshape(ATT_KV_HEADS, ATT_REP // 2 * L, 2 * KEY_SPAN))
    n_chunks = TB // L
    in_specs = [pl.BlockSpec(memory_space=pltpu.SMEM),
                seq(ATT_DIM), seq(KV_DIM), seq(KV_DIM), prev, prev,
                _const_spec(mk.shape), _const_spec(mv.shape), _const_spec(bias.shape),
                _const_spec(w["g_att_out"].shape)]
    return pl.pallas_call(
        functools.partial(_attn_kernel, L=L, TB=TB, SB=SB, from_cache=from_cache),
        grid=(S // SB, nb),
        in_specs=in_specs,
        out_specs=seq(ATT_DIM),
        out_shape=jax.ShapeDtypeStruct((S, Ls, ATT_DIM), F32),
        scratch_shapes=[pltpu.VMEM((SB, WINDOW + TB, 2 * KV_DIM), BF16) for _ in range(4)]
                       + [pltpu.VMEM((SB, TB, ATT_DIM), F32),
                          pltpu.VMEM((SB, n_chunks * ATT_KV_HEADS, 2 * L, 2 * KEY_SPAN), F32),
                          pltpu.VMEM((SB, n_chunks * ATT_KV_HEADS, 2 * L, 2 * KEY_SPAN), BF16),
                          pltpu.VMEM((SB, n_chunks * ATT_HEADS // 2, L, LANES), F32)],
        compiler_params=pltpu.CompilerParams(
            dimension_semantics=("arbitrary", "arbitrary"), vmem_limit_bytes=VMEM_LIMIT),
        name="attention",
    )(w["attn_sinks"], q, k, v, kprev, vprev, mk, mv, bias, w["g_att_out"])


TOK_TILE = 512
SEG_ALIGN = 2 * SUBLANES
SORT_ROWS = 2 * TOK_TILE + N_EXPERTS * SEG_ALIGN
XS_COLS = D_MODEL + LANES
META_P1, META_P2, META_G1, META_G2 = N_EXPERTS, N_EXPERTS + 1, N_EXPERTS + 2, N_EXPERTS + 3


def _merge_kernel(h_ref, ys_ref, ya_ref, wo_ref, gffn_ref, wr_ref, br_ref, tri_ref, upper_ref,
                  h1_ref, xn_ref, meta_ref, cnt_ref, logit_buf):
    i = pl.program_id(0)

    @pl.when(i == 0)
    def _():
        logit_buf[...] = jnp.zeros(logit_buf.shape, F32)

    ycat = jnp.concatenate([ys_ref[...].astype(BF16), ya_ref[...].astype(BF16)], axis=1)
    h1 = h_ref[...] + _dot(ycat, wo_ref[...])
    h1_ref[...] = h1
    xn = _rms(h1, gffn_ref[...]).astype(BF16)
    xn_ref[...] = xn
    all_logits = logit_buf[(i + 1) % 2]
    logit_buf[i % 2] = _dot(xn, wr_ref[...]) + br_ref[...]
    tri = tri_ref[...]
    upper = upper_ref[...]
    lane = lax.broadcasted_iota(jnp.int32, (TOK_TILE, LANES), 1)
    big = jnp.int32(LANES)
    gmask = (lane >= N_EXPERTS) & (lane < N_EXPERTS + N_EXPERT_GROUPS)
    for s in range(all_logits.shape[0] // TOK_TILE):
        rows = slice(s * TOK_TILE, (s + 1) * TOK_TILE)
        logits = all_logits[rows]

        def top1(mask):
            mval = jnp.max(jnp.where(mask, logits, NEG_INF), axis=-1, keepdims=True)
            idx = jnp.min(jnp.where(mask & (logits == mval), lane, big), axis=-1, keepdims=True)
            return mval, idx

        gmax, gidx = top1(gmask)
        gate_g = 1.0 / jnp.sum(jnp.where(gmask, jnp.exp(logits - gmax), 0.0), axis=-1, keepdims=True)
        grp = gidx - N_EXPERTS
        emask = (lane // EXPERTS_PER_GROUP) == grp
        v1, i1 = top1(emask)
        v2, i2 = top1(emask & (lane != i1))
        e2 = jnp.exp(v2 - v1)
        g1 = gate_g / (1.0 + e2)
        g2 = gate_g * e2 / (1.0 + e2)
        oh1 = jnp.where(lane == i1, 1.0, 0.0)
        oh2 = jnp.where(lane == i2, 1.0, 0.0)
        oh = oh1 + oh2
        earlier = _dot(tri, oh.astype(BF16))
        cnt = jnp.sum(oh, axis=0, keepdims=True)
        units = jnp.floor((cnt + (SEG_ALIGN - 1)) * (1.0 / SEG_ALIGN))
        units = jnp.broadcast_to(units, (2 * SUBLANES, LANES)).astype(BF16)
        slot = _dot(units, upper)[0:1, :] * SEG_ALIGN + earlier
        p1 = jnp.sum(oh1 * slot, axis=-1, keepdims=True)
        p2 = jnp.sum(oh2 * slot, axis=-1, keepdims=True)
        meta = jnp.where(lane == META_P1, p1, 0.0)
        meta = jnp.where(lane == META_P2, p2, meta)
        meta = jnp.where(lane == META_G1, g1, meta)
        meta = jnp.where(lane == META_G2, g2, meta)
        meta_ref[rows, :] = meta
        cnt_ref[s] = jnp.broadcast_to(cnt, (SUBLANES, LANES))


def _merge(h, y_ssd, y_att, w, tm):
    t = h.shape[0]
    assert t % tm == 0 and tm % TOK_TILE == 0
    sub = tm // TOK_TILE
    last = t // tm - 1
    row = lambda n: pl.BlockSpec((tm, n), lambda i: (jnp.minimum(i, last), 0))
    routed = lambda i: jnp.maximum(i - 1, 0)
    tri = jnp.asarray(np.tril(np.ones((TOK_TILE, TOK_TILE), np.float32), -1), BF16)
    upper = jnp.asarray(np.triu(np.ones((LANES, LANES), np.float32), 1), BF16)
    consts = [w["w_out"], w["g_ffn"], w["w_route"], w["b_route"], tri, upper]
    return pl.pallas_call(
        _merge_kernel,
        grid=(t // tm + 1,),
        in_specs=[row(D_MODEL), row(SSD_DIM), row(ATT_DIM)] + [_const_spec(c.shape) for c in consts],
        out_specs=[row(D_MODEL), row(D_MODEL),
                   pl.BlockSpec((tm, LANES), lambda i: (routed(i), 0)),
                   pl.BlockSpec((sub, SUBLANES, LANES), lambda i: (routed(i), 0, 0))],
        scratch_shapes=[pltpu.VMEM((2, tm, LANES), F32)],
        out_shape=[jax.ShapeDtypeStruct((t, D_MODEL), F32),
                   jax.ShapeDtypeStruct((t, D_MODEL), BF16),
                   jax.ShapeDtypeStruct((t, LANES), F32),
                   jax.ShapeDtypeStruct((t // TOK_TILE, SUBLANES, LANES), F32)],
        compiler_params=pltpu.CompilerParams(
            dimension_semantics=("arbitrary",), vmem_limit_bytes=VMEM_LIMIT),
        name="merge_route",
    )(h, y_ssd, y_att, *consts)


def _meta_col(meta, j):
    lane = lax.broadcasted_iota(jnp.int32, meta.shape, 1)
    return jnp.sum(jnp.where(lane == j, meta, 0.0), axis=-1, keepdims=True)


def _pair_selectors(meta):
    rows = lax.broadcasted_iota(jnp.int32, (meta.shape[0], SORT_ROWS), 1)
    s1 = jnp.where(rows == _meta_col(meta, META_P1).astype(jnp.int32), 1.0, 0.0).astype(BF16)
    s2 = jnp.where(rows == _meta_col(meta, META_P2).astype(jnp.int32), 1.0, 0.0).astype(BF16)
    return s1, s2


def _segment_dmas(off_ref, cnt_ref, row_ref, b, tile_buf, hbm, sem, to_hbm, wait):
    for e in range(N_EXPERTS):
        k = b * N_EXPERTS + e
        n = pl.multiple_of(cnt_ref[k], SEG_ALIGN)
        v = tile_buf.at[pl.ds(pl.multiple_of(off_ref[k], SEG_ALIGN), n)]
        h = hbm.at[pl.ds(pl.multiple_of(row_ref[k], SEG_ALIGN), n)]
        cp = pltpu.make_async_copy(v, h, sem) if to_hbm else pltpu.make_async_copy(h, v, sem)
        if wait:
            cp.wait()
        else:
            cp.start()


def _gate_lanes(g):
    hi, mid, lo = _split3(g)
    lane = lax.broadcasted_iota(jnp.int32, (g.shape[0], LANES), 1)
    out = jnp.where(lane == 0, hi.astype(F32), 0.0)
    out = jnp.where(lane == 1, mid.astype(F32), out)
    out = jnp.where(lane == 2, lo.astype(F32), out)
    return out.astype(BF16)


def _dispatch_kernel(off_ref, cnt_ref, row_ref, tail_row_ref, tail_cnt_ref, xn_ref, meta_ref, *rest,
                     tile0, first):
    xs_ref, sort_buf, zero_buf, sem, tail_sem = rest if first else rest[1:]
    b = pl.program_id(0)
    nb = pl.num_programs(0)
    slot = b % 2
    seg = functools.partial(_segment_dmas, off_ref, cnt_ref, row_ref, hbm=xs_ref, to_hbm=True)

    def tails(wait):
        def body(e, carry):
            n = pl.multiple_of(tail_cnt_ref[e], SEG_ALIGN)

            @pl.when(n > 0)
            def _():
                cp = pltpu.make_async_copy(
                    zero_buf.at[pl.ds(0, n)],
                    xs_ref.at[pl.ds(pl.multiple_of(tail_row_ref[e], SEG_ALIGN), n)], tail_sem.at[0])
                if wait:
                    cp.wait()
                else:
                    cp.start()
            return carry

        lax.fori_loop(0, N_EXPERTS, body, 0)

    @pl.when(b >= 2)
    def _():
        seg(tile0 + b - 2, sort_buf.at[slot], sem=sem.at[slot], wait=True)

    if first:
        @pl.when(b == 0)
        def _():
            zero_buf[...] = jnp.zeros(zero_buf.shape, BF16)
            tails(wait=False)

    meta = meta_ref[...]
    s1, s2 = _pair_selectors(meta)
    sort_buf[slot, :, 0:D_MODEL] = _dot_tn(s1 + s2, xn_ref[...]).astype(BF16)
    sort_buf[slot, :, D_MODEL:XS_COLS] = (_dot_tn(s1, _gate_lanes(_meta_col(meta, META_G1)))
                                          + _dot_tn(s2, _gate_lanes(_meta_col(meta, META_G2)))).astype(BF16)
    seg(tile0 + b, sort_buf.at[slot], sem=sem.at[slot], wait=False)

    @pl.when(b == nb - 1)
    def _():
        @pl.when(b >= 1)
        def _():
            seg(tile0 + b - 1, sort_buf.at[1 - slot], sem=sem.at[1 - slot], wait=True)

        seg(tile0 + b, sort_buf.at[slot], sem=sem.at[slot], wait=True)
        if first:
            tails(wait=True)


def _expert_kernel(tile_expert_ref, n_active_ref, xs_ref, wg_ref, wu_ref, wd_ref, y_ref):
    @pl.when(pl.program_id(0) < n_active_ref[0])
    def _():
        x = xs_ref[:, 0:D_MODEL]
        gate = jnp.sum(xs_ref[:, D_MODEL:XS_COLS].astype(F32), axis=-1, keepdims=True)
        gu = _dot(x, jnp.concatenate([wg_ref[...].astype(BF16), wu_ref[...].astype(BF16)], axis=1))
        hid = _silu(gu[:, 0:EXPERT_FF]) * gu[:, EXPERT_FF:2 * EXPERT_FF]
        y_ref[...] = (gate * _dot(hid.astype(BF16), wd_ref[...].astype(BF16))).astype(BF16)


def _combine_kernel(off_ref, cnt_ref, row_ref, h1_ref, meta_ref, y_ref, o_ref, y_buf, sem, *, tile0):
    b = pl.program_id(0)
    nb = pl.num_programs(0)
    slot = b % 2
    seg = functools.partial(_segment_dmas, off_ref, cnt_ref, row_ref, hbm=y_ref, to_hbm=False)

    @pl.when(b == 0)
    def _():
        y_buf[...] = jnp.zeros(y_buf.shape, BF16)
        seg(tile0 + b, y_buf.at[slot], sem=sem.at[slot], wait=False)

    @pl.when(b + 1 < nb)
    def _():
        seg(tile0 + b + 1, y_buf.at[1 - slot], sem=sem.at[1 - slot], wait=False)

    seg(tile0 + b, y_buf.at[slot], sem=sem.at[slot], wait=True)
    s1, s2 = _pair_selectors(meta_ref[...])
    sel = s1 + s2
    o_ref[...] = h1_ref[...] + _dot(sel, y_buf[slot, 0:SORT_ROWS, :])


def _moe(parts, w, tm):
    i32 = jnp.int32
    n_tiles = [p[0].shape[0] // TOK_TILE for p in parts]
    nb = sum(n_tiles)
    t = nb * TOK_TILE
    cnt = jnp.concatenate([p[3][:, 0, :N_EXPERTS] for p in parts], axis=0).astype(i32)
    cnt_al = (cnt + (SEG_ALIGN - 1)) // SEG_ALIGN * SEG_ALIGN
    off = jnp.cumsum(cnt_al, axis=1) - cnt_al
    tot = jnp.sum(cnt_al, axis=0)
    tot_tm = (tot + (tm - 1)) // tm * tm
    start = jnp.cumsum(tot_tm) - tot_tm
    row = start[None, :] + jnp.cumsum(cnt_al, axis=0) - cnt_al
    tile_ends = jnp.cumsum(tot_tm // tm)
    max_rows = 2 * t + (SEG_ALIGN - 1) * min(N_EXPERTS * nb, 2 * t) + N_EXPERTS * (tm - SEG_ALIGN)
    nt = -(-max_rows // tm)
    tile_expert = jnp.sum((jnp.arange(nt, dtype=i32)[:, None] >= tile_ends[None, :]).astype(i32), axis=1)
    tile_expert = jnp.minimum(tile_expert, N_EXPERTS - 1)
    n_active = tile_ends[-1:].astype(i32)
    empty = cnt_al == 0
    flat = lambda a: a.reshape(-1).astype(i32)
    n_rows = flat(jnp.maximum(cnt_al, SEG_ALIGN))
    e_idx = jnp.arange(N_EXPERTS, dtype=i32)[None, :]
    spare = nt * tm + ((jnp.arange(nb, dtype=i32)[:, None] % 2) * N_EXPERTS + e_idx) * SEG_ALIGN
    dispatch_tables = [flat(off), n_rows, flat(jnp.where(empty, spare, row))]
    combine_tables = [flat(jnp.where(empty, SORT_ROWS + e_idx * SEG_ALIGN, off)), n_rows,
                      flat(jnp.where(empty, 0, row))]
    n_spare = 2 * N_EXPERTS * SEG_ALIGN
    tail_tables = [(start + tot).astype(i32), (tot_tm - tot).astype(i32)]

    tok = lambda n: pl.BlockSpec((TOK_TILE, n), lambda i, *_: (i, 0))
    hbm = pl.BlockSpec(memory_space=pl.ANY)
    n_prefetch = len(dispatch_tables) + len(tail_tables)
    xs = None
    tile0 = 0
    for (_, xn, meta, _), n in zip(parts, n_tiles):
        first = xs is None
        xs = pl.pallas_call(
            functools.partial(_dispatch_kernel, tile0=tile0, first=first),
            grid_spec=pltpu.PrefetchScalarGridSpec(
                num_scalar_prefetch=n_prefetch, grid=(n,),
                in_specs=[tok(D_MODEL), tok(LANES)] + ([] if first else [hbm]),
                out_specs=hbm,
                scratch_shapes=[pltpu.VMEM((2, SORT_ROWS, XS_COLS), BF16),
                                pltpu.VMEM((tm, XS_COLS), BF16),
                                pltpu.SemaphoreType.DMA((2,)),
                                pltpu.SemaphoreType.DMA((1,))]),
            out_shape=jax.ShapeDtypeStruct((nt * tm + n_spare, XS_COLS), BF16),
            input_output_aliases={} if first else {n_prefetch + 2: 0},
            compiler_params=pltpu.CompilerParams(
                dimension_semantics=("arbitrary",), vmem_limit_bytes=VMEM_LIMIT),
            name="dispatch",
        )(*dispatch_tables, *tail_tables, xn, meta, *([] if first else [xs]))
        tile0 += n

    act = lambda i, te, na: jnp.minimum(i, na[0] - 1)
    y = pl.pallas_call(
        _expert_kernel,
        grid_spec=pltpu.PrefetchScalarGridSpec(
            num_scalar_prefetch=2, grid=(nt,),
            in_specs=[pl.BlockSpec((tm, XS_COLS), lambda i, te, na: (act(i, te, na), 0)),
                      pl.BlockSpec((None, D_MODEL, EXPERT_FF), lambda i, te, na: (te[act(i, te, na)], 0, 0)),
                      pl.BlockSpec((None, D_MODEL, EXPERT_FF), lambda i, te, na: (te[act(i, te, na)], 0, 0)),
                      pl.BlockSpec((None, EXPERT_FF, D_MODEL), lambda i, te, na: (te[act(i, te, na)], 0, 0))],
            out_specs=pl.BlockSpec((tm, D_MODEL), lambda i, te, na: (act(i, te, na), 0))),
        out_shape=jax.ShapeDtypeStruct((nt * tm, D_MODEL), BF16),
        compiler_params=pltpu.CompilerParams(
            dimension_semantics=("arbitrary",), vmem_limit_bytes=VMEM_LIMIT),
        name="experts",
    )(tile_expert, n_active, xs, w["w_gate"], w["w_up"], w["w_down"])

    outs = []
    tile0 = 0
    for (h1, _, meta, _), n in zip(parts, n_tiles):
        outs.append(pl.pallas_call(
            functools.partial(_combine_kernel, tile0=tile0),
            grid_spec=pltpu.PrefetchScalarGridSpec(
                num_scalar_prefetch=len(combine_tables), grid=(n,),
                in_specs=[tok(D_MODEL), tok(LANES), hbm],
                out_specs=tok(D_MODEL),
                scratch_shapes=[pltpu.VMEM((2, SORT_ROWS + N_EXPERTS * SEG_ALIGN, D_MODEL), BF16),
                                pltpu.SemaphoreType.DMA((2,))]),
            out_shape=jax.ShapeDtypeStruct((n * TOK_TILE, D_MODEL), F32),
            compiler_params=pltpu.CompilerParams(
                dimension_semantics=("arbitrary",), vmem_limit_bytes=VMEM_LIMIT),
            name="combine",
        )(*combine_tables, h1, meta, y))
        tile0 += n
    return outs


def _prepare_weights(g_mix, w_in, conv_w, conv_b, dt_bias, a_log, d_skip, g_ssd_out, g_q, g_k,
                     attn_sinks, g_att_out, w_out, g_ffn, w_route_group, b_route_group,
                     w_route_expert, b_route_expert, w_gate, w_up, w_down):
    cuts = np.cumsum([0, SSD_DIM, CONV_DIM, SSD_HEADS, ATT_DIM, KV_DIM, KV_DIM])
    seg = lambda i: w_in[:, cuts[i]:cuts[i + 1]]
    pad_lanes = lambda a: jnp.pad(a, ((0, 0), (0, LANES - a.shape[1])))
    bd = np.kron(np.eye(KV_DIM // HEAD_DIM, dtype=np.float32), np.ones((HEAD_DIM, HEAD_DIM), np.float32))
    n_route = N_EXPERTS + N_EXPERT_GROUPS
    return {
        "g_mix": g_mix.reshape(1, D_MODEL),
        "w_qk": jnp.concatenate([seg(3), seg(4)], axis=1).astype(BF16),
        "w_rest": jnp.concatenate([seg(0), seg(1), seg(5), pad_lanes(jnp.tile(seg(2), (1, DT_COPIES)))],
                                  axis=1).astype(BF16),
        "g_qk": jnp.concatenate([jnp.tile(g_q, ATT_HEADS), jnp.tile(g_k, ATT_KV_HEADS)]).reshape(1, ATT_DIM + KV_DIM),
        "bd": jnp.asarray(np.concatenate([bd, bd], axis=0), BF16),
        "conv_w": conv_w, "conv_b": conv_b.reshape(1, CONV_DIM),
        "dt_bias": pad_lanes(jnp.tile(dt_bias, DT_COPIES).reshape(1, DT_COPIES * SSD_HEADS)),
        "a_log": pad_lanes(jnp.tile(a_log, DT_COPIES).reshape(1, DT_COPIES * SSD_HEADS)),
        "d_skip": jnp.repeat(d_skip, HEAD_DIM).reshape(1, SSD_DIM),
        "g_ssd_out": g_ssd_out.reshape(1, SSD_DIM),
        "attn_sinks": attn_sinks,
        "g_att_out": g_att_out.reshape(1, ATT_DIM),
        "w_out": w_out.astype(BF16),
        "g_ffn": g_ffn.reshape(1, D_MODEL),
        "w_route": pad_lanes(jnp.concatenate([w_route_expert, w_route_group], axis=1)).astype(BF16),
        "b_route": pad_lanes(jnp.concatenate([b_route_expert, b_route_group]).reshape(1, n_route)),
        "w_gate": w_gate, "w_up": w_up, "w_down": w_down,
    }


def _segment(x3, w, mk, mv, tail, h0t, kprev, vprev, L, tb_ssd, tb_att, tm, from_cache, sb=1):
    S, Ls, _ = x3.shape
    x = x3.reshape(S * Ls, D_MODEL)
    z, xbc, q, k, v, dt = _project(x, w, tm)
    r3 = lambda a: a.reshape(S, Ls, a.shape[-1])
    xbc3, k3, v3 = r3(xbc), r3(k), r3(v)
    y_ssd, h_new = _ssd(xbc3, r3(z), r3(dt), tail, h0t, w, L, tb_ssd, sb)
    if not from_cache:
        kprev, vprev = k3, v3
    y_att = _attention(r3(q), k3, v3, kprev, vprev, mk, mv, w, L, tb_att, from_cache, sb)
    part = _merge(x, y_ssd.reshape(S * Ls, SSD_DIM), y_att.reshape(S * Ls, ATT_DIM), w, tm)
    return part, xbc3, h_new, k3, v3


def _state_to_kernel(h):
    return h.reshape(h.shape[0], SSD_DIM, D_STATE)


def _state_from_kernel(hk):
    return hk.reshape(hk.shape[0], SSD_HEADS, HEAD_DIM, D_STATE)


def kernel(x_prompt, x_sample, cache_conv, state_ssd, cache_k, cache_v, meta_tokens, g_mix, w_in, conv_w, conv_b, dt_bias, a_log, d_skip, g_ssd_out, g_q, g_k, attn_sinks, g_att_out, w_out, g_ffn, w_route_group, b_route_group, w_route_expert, b_route_expert, w_gate, w_up, w_down):
    w = _prepare_weights(g_mix[0], w_in[0], conv_w[0], conv_b[0], dt_bias[0], a_log[0], d_skip[0],
                         g_ssd_out[0], g_q[0], g_k[0], attn_sinks[0], g_att_out[0], w_out[0], g_ffn[0],
                         w_route_group[0], b_route_group[0], w_route_expert[0], b_route_expert[0],
                         w_gate[0], w_up[0], w_down[0])
    n_b = x_sample.shape[0]
    n_dec = x_sample.shape[1]

    _, m_xbc, _, mk, mv, m_dt = _project(meta_tokens, w, N_META)
    zero_tail = jnp.zeros((1, CONV_WIDTH - 1, CONV_DIM), F32)
    zero_state = jnp.zeros((1, SSD_DIM, D_STATE), F32)
    m_xbc3 = m_xbc.reshape(1, N_META, CONV_DIM)
    _, m_state = _ssd(m_xbc3, jnp.zeros((1, N_META, SSD_DIM), F32), m_dt.reshape(1, N_META, LANES),
                      zero_tail, zero_state, w, N_META, N_META)
    m_tail = m_xbc3[:, N_META - (CONV_WIDTH - 1):]

    part_p, xbc_p, st_p, k_p, v_p = _segment(
        x_prompt, w, mk, mv, m_tail, m_state, None, None,
        L=CHUNK, tb_ssd=512, tb_att=512, tm=512, from_cache=False)
    part_s, xbc_s, st_s, k_s, v_s = _segment(
        x_sample, w, mk, mv, cache_conv[0], _state_to_kernel(state_ssd[0]),
        cache_k[0].reshape(n_b, WINDOW, KV_DIM), cache_v[0].reshape(n_b, WINDOW, KV_DIM),
        L=n_dec, tb_ssd=n_dec, tb_att=n_dec, tm=512, from_cache=True, sb=8)
    yp, ys = _moe([part_p, part_s], w, tm=512)
    yp = yp.reshape(x_prompt.shape)
    ys = ys.reshape(x_sample.shape)

    heads = lambda a, rows: a.reshape(a.shape[0], rows, ATT_KV_HEADS, HEAD_DIM)[None]
    return (yp, ys,
            xbc_p[:, -(CONV_WIDTH - 1):][None],
            _state_from_kernel(st_p)[None],
            heads(k_p[:, -WINDOW:], WINDOW), heads(v_p[:, -WINDOW:], WINDOW),
            xbc_s[:, -(CONV_WIDTH - 1):][None],
            _state_from_kernel(st_s)[None],
            heads(k_s, n_dec), heads(v_s, n_dec))
```

```python
import functools
import math

import numpy as np
import jax
import jax.numpy as jnp
from jax import lax
from jax.experimental import pallas as pl
from jax.experimental.pallas import tpu as pltpu

D_MODEL = 1024
CHUNK = 64
N_META = 16
HEAD_DIM = 64
ATT_HEADS = 16
ATT_KV_HEADS = 4
ATT_REP = ATT_HEADS // ATT_KV_HEADS
ATT_DIM = ATT_HEADS * HEAD_DIM
KV_DIM = ATT_KV_HEADS * HEAD_DIM
WINDOW = 128
SSD_HEADS = 16
SSD_DIM = SSD_HEADS * HEAD_DIM
SSD_GROUPS = 2
GROUP_DIM = SSD_DIM // SSD_GROUPS
D_STATE = 128
CONV_WIDTH = 4
CONV_DIM = SSD_DIM + 2 * SSD_GROUPS * D_STATE
N_EXPERT_GROUPS = 4
EXPERTS_PER_GROUP = 8
N_EXPERTS = N_EXPERT_GROUPS * EXPERTS_PER_GROUP
EXPERT_FF = D_MODEL // 4
EPS = 1e-6

LANES = 128
SUBLANES = 8
MXU_DIM = 256
DT_COPIES = 3
KEY_SPAN = 256
VMEM_LIMIT = 48 * 1024 * 1024

F32 = jnp.float32
BF16 = jnp.bfloat16
NEG_INF = float("-inf")
LOG2E = math.log2(math.e)


def _dot(a, b):
    return jnp.dot(a, b, preferred_element_type=F32)


def _dot_nt(a, b):
    return lax.dot_general(a, b, (((1,), (1,)), ((), ())), preferred_element_type=F32)


def _dot_tn(a, b):
    return lax.dot_general(a, b, (((0,), (0,)), ((), ())), preferred_element_type=F32)


def _split3(x):
    hi = x.astype(BF16)
    r = x - hi.astype(F32)
    mid = r.astype(BF16)
    lo = (r - mid.astype(F32)).astype(BF16)
    return hi, mid, lo


def _pack3(x):
    hi, mid, lo = _split3(x)
    lane = lax.broadcasted_iota(jnp.int32, x.shape, 1)
    packed = jnp.where(lane < SSD_HEADS, hi.astype(F32),
                       jnp.where(lane < 2 * SSD_HEADS, mid.astype(F32), lo.astype(F32)))
    return packed.astype(BF16)


def _sel_dot(sel, x):
    hi, mid, lo = _split3(x)
    return _dot(sel, hi) + _dot(sel, mid) + _dot(sel, lo)


def _silu(x):
    return x * (1.0 / (1.0 + jnp.exp2(x * (-LOG2E))))


def _rms(x, g):
    ms = jnp.mean(x * x, axis=-1, keepdims=True)
    return x * lax.rsqrt(ms + EPS) * g


def _const_spec(shape):
    n = len(shape)
    return pl.BlockSpec(shape, lambda *_: (0,) * n, pipeline_mode=pl.Buffered(1))


def _proj_kernel(x_ref, gmix_ref, wqk_ref, wrest_ref, gqk_ref, bd_ref,
                 z_ref, xbc_ref, q_ref, k_ref, v_ref, dt_ref):
    tm = x_ref.shape[0]
    xn = _rms(x_ref[...], gmix_ref[...]).astype(BF16)
    qk = _dot(xn, wqk_ref[...])
    n_slices = (ATT_DIM + KV_DIM) // KV_DIM
    sq = jnp.concatenate([qk[:, j * KV_DIM:(j + 1) * KV_DIM] for j in range(n_slices)], axis=0)
    sq = sq * sq
    hi = sq.astype(BF16)
    lo = (sq - hi.astype(F32)).astype(BF16)
    ms = _dot(jnp.concatenate([hi, lo], axis=1), bd_ref[...]) * (1.0 / HEAD_DIM)
    inv = lax.rsqrt(ms + EPS)
    for j in range(n_slices):
        sl = slice(j * KV_DIM, (j + 1) * KV_DIM)
        normed = qk[:, sl] * inv[j * tm:(j + 1) * tm, :] * gqk_ref[:, sl]
        if j < ATT_DIM // KV_DIM:
            q_ref[:, sl] = (normed * (HEAD_DIM ** -0.5 * LOG2E)).astype(BF16)
        else:
            k_ref[...] = normed
    rest = _dot(xn, wrest_ref[...])
    cuts = np.cumsum([0, SSD_DIM, CONV_DIM, KV_DIM, LANES])
    for ref, lo_c, hi_c in zip((z_ref, xbc_ref, v_ref, dt_ref), cuts[:-1], cuts[1:]):
        ref[...] = rest[:, lo_c:hi_c]


def _project(x, w, tm):
    t = x.shape[0]
    assert t % tm == 0
    row = lambda n: pl.BlockSpec((tm, n), lambda i: (i, 0))
    ins = [x, w["g_mix"], w["w_qk"], w["w_rest"], w["g_qk"], w["bd"]]
    in_specs = [row(D_MODEL)] + [_const_spec(a.shape) for a in ins[1:]]
    out_dims = (SSD_DIM, CONV_DIM, ATT_DIM, KV_DIM, KV_DIM, LANES)
    out_dtypes = (F32, F32, BF16, F32, F32, F32)
    return pl.pallas_call(
        _proj_kernel,
        grid=(t // tm,),
        in_specs=in_specs,
        out_specs=[row(n) for n in out_dims],
        out_shape=[jax.ShapeDtypeStruct((t, n), d) for n, d in zip(out_dims, out_dtypes)],
        compiler_params=pltpu.CompilerParams(
            dimension_semantics=("arbitrary",), vmem_limit_bytes=VMEM_LIMIT),
        name="projection",
    )(*ins)


N_SSD_STREAM_INPUTS = 5
N_SSD_CONSTS = 9


def _ssd_kernel(*refs, L, TB, SB):
    streamed = lambda k: k < N_SSD_STREAM_INPUTS or k >= N_SSD_STREAM_INPUTS + N_SSD_CONSTS
    for s in range(SB):
        _ssd_stream(*[r.at[s] if streamed(k) else r for k, r in enumerate(refs)], L=L, TB=TB)


def _ssd_stream(xbc_ref, z_ref, dt_ref, tail_ref, h0_ref, convw_ref, convb_ref, dtb_ref,
                alog_ref, dskip_ref, gout_ref, ehead_ref, epos_ref, tri_ref,
                y_ref, hout_ref, buf_ref, act_ref, dts_ref, ys_ref, st_ref, *, L, TB):
    hp = LANES // L
    n_lane_tiles = SSD_HEADS // hp
    hq = MXU_DIM // L
    n_tiles = SSD_HEADS // hq
    tile_w = hq * HEAD_DIM
    b = pl.program_id(1)
    pad = SUBLANES - (CONV_WIDTH - 1)

    @pl.when(b == 0)
    def _():
        buf_ref[0:SUBLANES, :] = jnp.zeros((SUBLANES, CONV_DIM), F32)
        buf_ref[pad:SUBLANES, :] = tail_ref[...]
        st_ref[...] = h0_ref[...].T

    buf_ref[SUBLANES:SUBLANES + TB, :] = xbc_ref[...]
    rows = buf_ref[...]
    acc = convb_ref[...] + convw_ref[CONV_WIDTH - 1:CONV_WIDTH, :] * rows[SUBLANES:, :]
    for j in range(CONV_WIDTH - 1):
        shifted = pltpu.roll(rows, CONV_WIDTH - 1 - j, axis=0)[SUBLANES:, :]
        acc = acc + convw_ref[j:j + 1, :] * shifted
    act_ref[...] = _silu(acc)
    buf_ref[pad:SUBLANES, :] = buf_ref[TB + pad:TB + SUBLANES, :]

    dtx = dt_ref[...] + dtb_ref[...]
    dts_ref[...] = jnp.maximum(dtx, 0.0) + jnp.log(1.0 + jnp.exp(-jnp.abs(dtx)))

    a_row = -jnp.exp(alog_ref[...]) * LOG2E
    ehead = ehead_ref[...]
    epos = epos_ref[...]
    tri = tri_ref[...]
    dskip = dskip_ref[...]
    row_i = lax.broadcasted_iota(jnp.int32, (L, n_lane_tiles * LANES), 0)
    col_i = lax.broadcasted_iota(jnp.int32, (L, n_lane_tiles * LANES), 1)
    pos_i = col_i % L
    diag_mask = row_i == pos_i
    causal_mask = row_i >= pos_i
    colhead = lax.broadcasted_iota(jnp.int32, (L, tile_w), 1) // HEAD_DIM

    def chunk(c):
        r0 = pl.multiple_of(c * L, L)
        dtc = dts_ref[pl.ds(r0, L), :]
        cs = _sel_dot(tri, dtc * a_row)
        cs16 = _pack3(cs)
        dt_b = _dot(_pack3(dtc), ehead)
        cs_b = _dot(cs16, ehead)
        cs_last_b = cs_b[L - 1:L, :]
        ecs_b = jnp.exp2(cs_b)
        dec_end_b = jnp.exp2(cs_last_b - cs_b)
        bdec_b = jnp.exp2(cs_last_b)
        cs_col = cs_b if L == HEAD_DIM else _dot(cs16, epos)
        cs_row = jnp.sum(jnp.where(diag_mask, cs_col, 0.0), axis=0, keepdims=True)
        lmat = jnp.exp2(jnp.where(causal_mask, cs_col - cs_row, NEG_INF))

        xs = act_ref[pl.ds(r0, L), 0:SSD_DIM]
        xdt_f = xs * dt_b
        xdt = xdt_f.astype(BF16)
        wx = (xdt_f * dec_end_b).astype(BF16)
        st = st_ref[...]
        st16 = st.astype(BF16)

        y_diag = []
        y_off = []
        s_new = []
        cb = [None] * SSD_GROUPS
        for g in range(SSD_GROUPS):
            bg = act_ref[pl.ds(r0, L), SSD_DIM + g * D_STATE:SSD_DIM + (g + 1) * D_STATE].astype(BF16)
            cg = act_ref[pl.ds(r0, L), SSD_DIM + (SSD_GROUPS + g) * D_STATE:
                         SSD_DIM + (SSD_GROUPS + g + 1) * D_STATE].astype(BF16)
            cb[g] = _dot_nt(cg, jnp.concatenate([bg] * hp, axis=0))
            gsl = slice(g * GROUP_DIM, (g + 1) * GROUP_DIM)
            y_off.append(_dot(cg, st16[:, gsl]))
            s_new.append(_dot_tn(bg, wx[:, gsl]))
        lane_tiles = MXU_DIM // LANES
        for t in range(n_tiles):
            cbt = jnp.concatenate(
                [cb[((t * lane_tiles + j) * hp * HEAD_DIM) // GROUP_DIM] for j in range(lane_tiles)], axis=1)
            gmat = (lmat[:, t * MXU_DIM:(t + 1) * MXU_DIM] * cbt).astype(BF16)
            xt = xdt[:, t * tile_w:(t + 1) * tile_w]
            rhs = jnp.concatenate(
                [jnp.where(colhead == hh, xt, jnp.zeros_like(xt)) for hh in range(hq)], axis=0)
            y_diag.append(_dot(gmat, rhs))
        y = (jnp.concatenate(y_diag, axis=1) + jnp.concatenate(y_off, axis=1) * ecs_b
             + dskip * xs)
        ys_ref[pl.ds(r0, L), :] = y
        st_ref[...] = bdec_b * st + jnp.concatenate(s_new, axis=1)

    n_chunks = TB // L
    unroll = n_chunks if n_chunks in (4, 8) else 1

    def chunks(i, carry):
        for u in range(unroll):
            chunk(i * unroll + u)
        return carry

    lax.fori_loop(0, n_chunks // unroll, chunks, 0)

    yg = ys_ref[...] * _silu(z_ref[...])
    y_ref[...] = _rms(yg, gout_ref[...])

    @pl.when(b == pl.num_programs(1) - 1)
    def _():
        hout_ref[...] = st_ref[...].T


def _ssd(xbc, z, dt, tail, h0t, w, L, TB, SB=1):
    S, Ls, _ = xbc.shape
    assert Ls % TB == 0 and TB % L == 0 and LANES % L == 0 and S % SB == 0
    nb = Ls // TB
    assert SB == 1 or nb == 1
    hp = LANES // L
    n_tiles = SSD_HEADS // hp
    per_stream = lambda a: (lambda s, b: (s, 0, 0)) if a.shape[0] == S and S > 1 else (lambda s, b: (0, 0, 0))
    seq = lambda n: pl.BlockSpec((SB, TB, n), lambda s, b: (s, b, 0))
    ehead = np.zeros((LANES, SSD_DIM), np.float32)
    epos = np.zeros((LANES, n_tiles * LANES), np.float32)
    for part in range(DT_COPIES):
        for h in range(SSD_HEADS):
            ehead[part * SSD_HEADS + h, h * HEAD_DIM:(h + 1) * HEAD_DIM] = 1.0
            epos[part * SSD_HEADS + h, h * L:(h + 1) * L] = 1.0
    tri = np.tril(np.ones((L, L), np.float32))
    consts = [w["conv_w"], w["conv_b"], w["dt_bias"], w["a_log"], w["d_skip"], w["g_ssd_out"],
              jnp.asarray(ehead, BF16), jnp.asarray(epos, BF16), jnp.asarray(tri, BF16)]
    in_specs = [seq(CONV_DIM), seq(SSD_DIM), seq(LANES),
                pl.BlockSpec((SB, CONV_WIDTH - 1, CONV_DIM), per_stream(tail)),
                pl.BlockSpec((SB, SSD_DIM, D_STATE), per_stream(h0t))]
    assert len(in_specs) == N_SSD_STREAM_INPUTS and len(consts) == N_SSD_CONSTS
    in_specs += [_const_spec(c.shape) for c in consts]
    return pl.pallas_call(
        functools.partial(_ssd_kernel, L=L, TB=TB, SB=SB),
        grid=(S // SB, nb),
        in_specs=in_specs,
        out_specs=[seq(SSD_DIM), pl.BlockSpec((SB, SSD_DIM, D_STATE), lambda s, b: (s, 0, 0))],
        out_shape=[jax.ShapeDtypeStruct((S, Ls, SSD_DIM), F32),
                   jax.ShapeDtypeStruct((S, SSD_DIM, D_STATE), F32)],
        scratch_shapes=[pltpu.VMEM((SB, SUBLANES + TB, CONV_DIM), F32),
                        pltpu.VMEM((SB, TB, CONV_DIM), F32),
                        pltpu.VMEM((SB, TB, LANES), F32),
                        pltpu.VMEM((SB, TB, SSD_DIM), F32),
                        pltpu.VMEM((SB, D_STATE, SSD_DIM), F32)],
        compiler_params=pltpu.CompilerParams(
            dimension_semantics=("arbitrary", "arbitrary"), vmem_limit_bytes=VMEM_LIMIT),
        name="ssd",
    )(xbc, z, dt, tail, h0t, *consts)


def _split_heads(x):
    lane = lax.broadcasted_iota(jnp.int32, (x.shape[0], LANES), 1)
    low = lane < HEAD_DIM
    lo, hi = [], []
    for t in range(KV_DIM // LANES):
        tile = x[:, t * LANES:(t + 1) * LANES]
        swapped = pltpu.roll(tile, HEAD_DIM, axis=1)
        lo += [jnp.where(low, tile, 0.0), jnp.where(low, swapped, 0.0)]
        hi += [jnp.where(low, 0.0, swapped), jnp.where(low, 0.0, tile)]
    return jnp.concatenate(lo, axis=1).astype(BF16), jnp.concatenate(hi, axis=1).astype(BF16)


def _attn_kernel(*refs, L, TB, SB, from_cache):
    for s in range(SB):
        _attn_stream(*[r if k == 0 or 6 <= k <= 9 else r.at[s] for k, r in enumerate(refs)],
                     L=L, TB=TB, from_cache=from_cache)


def _attn_stream(sink_ref, q_ref, k_ref, v_ref, kp_ref, vp_ref, mk_ref, mv_ref, bias_ref, g_ref,
                 o_ref, k_lo, k_hi, v_lo, v_hi, ybuf, s_buf, e_buf, sink_buf, *, L, TB, from_cache):
    b = pl.program_id(1)
    bufs = (k_lo, k_hi, v_lo, v_hi)
    if from_cache:
        window = _split_heads(kp_ref[...]) + _split_heads(vp_ref[...])
        for buf, val in zip(bufs, window):
            buf[0:WINDOW, :] = val
    else:
        @pl.when(b == 0)
        def _():
            for buf in bufs:
                buf[0:WINDOW, :] = jnp.zeros((WINDOW, buf.shape[1]), BF16)

        @pl.when(b > 0)
        def _():
            for buf in bufs:
                buf[0:WINDOW, :] = buf[TB:TB + WINDOW, :]

    for buf, val in zip(bufs, _split_heads(k_ref[...]) + _split_heads(v_ref[...])):
        buf[WINDOW:WINDOW + TB, :] = val
    n_pad = KEY_SPAN - WINDOW - L - N_META
    pad_rows = jnp.zeros((n_pad, 2 * KV_DIM), BF16)
    tails = [jnp.concatenate([t, pad_rows], axis=0)
             for t in _split_heads(mk_ref[...]) + _split_heads(mv_ref[...])]
    lane = lax.broadcasted_iota(jnp.int32, (2 * L, 2 * KEY_SPAN), 1) % KEY_SPAN
    lane_h = lax.broadcasted_iota(jnp.int32, (L, LANES), 1)
    lane_v = lax.broadcasted_iota(jnp.int32, (KEY_SPAN, LANES), 1)
    ones_lo = jnp.where(lane_v < HEAD_DIM, 1.0, 0.0).astype(BF16)
    ones_hi = jnp.where(lane_v < HEAD_DIM, 0.0, 1.0).astype(BF16)
    n_chunks = TB // L
    pairs_per_group = ATT_REP // 2

    def chunk_keys(c):
        r0 = c * L
        return [jnp.concatenate([buf[r0:r0 + WINDOW, :], buf[r0 + WINDOW:r0 + WINDOW + L, :], tail], axis=0)
                for buf, tail in zip(bufs, tails)]

    for c in range(n_chunks):
        r0 = c * L
        keys = chunk_keys(c)
        for g in range(ATT_KV_HEADS):
            gl = slice(g * LANES, (g + 1) * LANES)
            q4 = jnp.concatenate([q_ref[r0:r0 + L, p * LANES:(p + 1) * LANES]
                                  for p in range(pairs_per_group * g, pairs_per_group * (g + 1))], axis=0)
            s = _dot_nt(q4, jnp.concatenate([keys[0][:, gl], keys[1][:, gl]], axis=0)) + bias_ref[g]
            if not from_cache and r0 < WINDOW:
                n_invalid = WINDOW - (b * TB + r0)
                s = jnp.where(lane < n_invalid, NEG_INF, s)
            s_buf[c * ATT_KV_HEADS + g] = s

    for c in range(n_chunks):
        for pair in range(ATT_HEADS // 2):
            slot = c * ATT_KV_HEADS + pair // pairs_per_group
            rows = slice((pair % pairs_per_group) * L, (pair % pairs_per_group + 1) * L)
            sink_terms = []
            for half in range(2):
                cols = slice(half * KEY_SPAN, (half + 1) * KEY_SPAN)
                s = s_buf[slot, rows, cols]
                sink = sink_ref[2 * pair + half] * LOG2E
                m = jnp.maximum(jnp.max(s, axis=-1, keepdims=True), sink)
                e_buf[slot, rows, cols] = jnp.exp2(s - m).astype(BF16)
                sink_terms.append(jnp.exp2(sink - m))
            sink_buf[c * (ATT_HEADS // 2) + pair] = jnp.where(lane_h < HEAD_DIM, sink_terms[0], sink_terms[1])

    for c in range(n_chunks):
        r0 = c * L
        keys = chunk_keys(c)
        for g in range(ATT_KV_HEADS):
            gl = slice(g * LANES, (g + 1) * LANES)
            vcat = jnp.concatenate([jnp.concatenate([keys[2][:, gl], ones_lo], axis=1),
                                    jnp.concatenate([keys[3][:, gl], ones_hi], axis=1)], axis=0)
            ov = _dot(e_buf[c * ATT_KV_HEADS + g], vcat)
            for j in range(pairs_per_group):
                pair = pairs_per_group * g + j
                den = ov[j * L:(j + 1) * L, LANES:2 * LANES] + sink_buf[c * (ATT_HEADS // 2) + pair]
                ybuf[r0:r0 + L, pair * LANES:(pair + 1) * LANES] = ov[j * L:(j + 1) * L, 0:LANES] * (1.0 / den)
    o_ref[...] = _rms(ybuf[...], g_ref[...])


def _attention(q, k, v, kprev, vprev, mk, mv, w, L, TB, from_cache, SB=1):
    S, Ls, _ = q.shape
    assert Ls % TB == 0 and TB % L == 0 and (from_cache or TB >= WINDOW) and S % SB == 0
    nb = Ls // TB
    assert SB == 1 or (from_cache and nb == 1)
    seq = lambda n: pl.BlockSpec((SB, TB, n), lambda s, b: (s, b, 0))
    if not from_cache:
        kprev = vprev = jnp.zeros((1, WINDOW, KV_DIM), F32)
    prev = pl.BlockSpec((SB, WINDOW, KV_DIM),
                        (lambda s, b: (s, 0, 0)) if from_cache else (lambda s, b: (0, 0, 0)))
    slopes = 2.0 ** (-8.0 * np.arange(1, ATT_HEADS + 1, dtype=np.float64) / ATT_HEADS)
    dist = np.abs(WINDOW + np.arange(L)[:, None] - np.arange(WINDOW + L)[None, :])
    bias = np.full((ATT_HEADS, L, KEY_SPAN), NEG_INF, np.float32)
    bias[:, :, :WINDOW + L] = -slopes[:, None, None] * dist[None] * LOG2E
    bias[:, :, WINDOW + L:WINDOW + L + N_META] = 0.0
    bias = bias.reshape(ATT_KV_HEADS, ATT_REP // 2, 2, L, KEY_SPAN).transpose(0, 1, 3, 2, 4)
    bias = jnp.asarray(bias.reshape(ATT_KV_HEADS, ATT_REP // 2 * L, 2 * KEY_SPAN))
    n_chunks = TB // L
    in_specs = [pl.BlockSpec(memory_space=pltpu.SMEM),
                seq(ATT_DIM), seq(KV_DIM), seq(KV_DIM), prev, prev,
                _const_spec(mk.shape), _const_spec(mv.shape), _const_spec(bias.shape),
                _const_spec(w["g_att_out"].shape)]
    return pl.pallas_call(
        functools.partial(_attn_kernel, L=L, TB=TB, SB=SB, from_cache=from_cache),
        grid=(S // SB, nb),
        in_specs=in_specs,
        out_specs=seq(ATT_DIM),
        out_shape=jax.ShapeDtypeStruct((S, Ls, ATT_DIM), F32),
        scratch_shapes=[pltpu.VMEM((SB, WINDOW + TB, 2 * KV_DIM), BF16) for _ in range(4)]
                       + [pltpu.VMEM((SB, TB, ATT_DIM), F32),
                          pltpu.VMEM((SB, n_chunks * ATT_KV_HEADS, 2 * L, 2 * KEY_SPAN), F32),
                          pltpu.VMEM((SB, n_chunks * ATT_KV_HEADS, 2 * L, 2 * KEY_SPAN), BF16),
                          pltpu.VMEM((SB, n_chunks * ATT_HEADS // 2, L, LANES), F32)],
        compiler_params=pltpu.CompilerParams(
            dimension_semantics=("arbitrary", "arbitrary"), vmem_limit_bytes=VMEM_LIMIT),
        name="attention",
    )(w["attn_sinks"], q, k, v, kprev, vprev, mk, mv, bias, w["g_att_out"])


TOK_TILE = 512
SEG_ALIGN = 2 * SUBLANES
SORT_ROWS = 2 * TOK_TILE + N_EXPERTS * SEG_ALIGN
XS_COLS = D_MODEL + LANES
META_P1, META_P2, META_G1, META_G2, META_E1 = (N_EXPERTS + j for j in range(5))
PAY_G1, PAY_G2, PAY_E1 = 0, 3, 6


def _merge_kernel(h_ref, ys_ref, ya_ref, wo_ref, gffn_ref, wr_ref, br_ref, tri_ref, upper_ref,
                  h1_ref, xn_ref, meta_ref, cnt_ref, logit_buf):
    i = pl.program_id(0)

    @pl.when(i == 0)
    def _():
        logit_buf[...] = jnp.zeros(logit_buf.shape, F32)

    ycat = jnp.concatenate([ys_ref[...].astype(BF16), ya_ref[...].astype(BF16)], axis=1)
    h1 = h_ref[...] + _dot(ycat, wo_ref[...])
    h1_ref[...] = h1
    xn = _rms(h1, gffn_ref[...]).astype(BF16)
    xn_ref[...] = xn
    all_logits = logit_buf[(i + 1) % 2]
    logit_buf[i % 2] = _dot(xn, wr_ref[...]) + br_ref[...]
    tri = tri_ref[...]
    upper = upper_ref[...]
    lane = lax.broadcasted_iota(jnp.int32, (TOK_TILE, LANES), 1)
    big = jnp.int32(LANES)
    gmask = (lane >= N_EXPERTS) & (lane < N_EXPERTS + N_EXPERT_GROUPS)
    for s in range(all_logits.shape[0] // TOK_TILE):
        rows = slice(s * TOK_TILE, (s + 1) * TOK_TILE)
        logits = all_logits[rows]

        def top1(mask):
            mval = jnp.max(jnp.where(mask, logits, NEG_INF), axis=-1, keepdims=True)
            idx = jnp.min(jnp.where(mask & (logits == mval), lane, big), axis=-1, keepdims=True)
            return mval, idx

        gmax, gidx = top1(gmask)
        gate_g = 1.0 / jnp.sum(jnp.where(gmask, jnp.exp(logits - gmax), 0.0), axis=-1, keepdims=True)
        grp = gidx - N_EXPERTS
        emask = (lane // EXPERTS_PER_GROUP) == grp
        v1, i1 = top1(emask)
        v2, i2 = top1(emask & (lane != i1))
        e2 = jnp.exp(v2 - v1)
        g1 = gate_g / (1.0 + e2)
        g2 = gate_g * e2 / (1.0 + e2)
        oh1 = jnp.where(lane == i1, 1.0, 0.0)
        oh2 = jnp.where(lane == i2, 1.0, 0.0)
        oh = oh1 + oh2
        earlier = _dot(tri, oh.astype(BF16))
        cnt = jnp.sum(oh, axis=0, keepdims=True)
        units = jnp.floor((cnt + (SEG_ALIGN - 1)) * (1.0 / SEG_ALIGN))
        units = jnp.broadcast_to(units, (2 * SUBLANES, LANES)).astype(BF16)
        slot = _dot(units, upper)[0:1, :] * SEG_ALIGN + earlier
        p1 = jnp.sum(oh1 * slot, axis=-1, keepdims=True)
        p2 = jnp.sum(oh2 * slot, axis=-1, keepdims=True)
        meta = jnp.where(lane == META_P1, p1, 0.0)
        meta = jnp.where(lane == META_P2, p2, meta)
        meta = jnp.where(lane == META_G1, g1, meta)
        meta = jnp.where(lane == META_G2, g2, meta)
        meta = jnp.where(lane == META_E1, i1.astype(F32), meta)
        meta_ref[rows, :] = meta
        cnt_ref[s] = jnp.broadcast_to(cnt, (SUBLANES, LANES))


def _merge(h, y_ssd, y_att, w, tm):
    t = h.shape[0]
    assert t % tm == 0 and tm % TOK_TILE == 0
    sub = tm // TOK_TILE
    last = t // tm - 1
    row = lambda n: pl.BlockSpec((tm, n), lambda i: (jnp.minimum(i, last), 0))
    routed = lambda i: jnp.maximum(i - 1, 0)
    tri = jnp.asarray(np.tril(np.ones((TOK_TILE, TOK_TILE), np.float32), -1), BF16)
    upper = jnp.asarray(np.triu(np.ones((LANES, LANES), np.float32), 1), BF16)
    consts = [w["w_out"], w["g_ffn"], w["w_route"], w["b_route"], tri, upper]
    return pl.pallas_call(
        _merge_kernel,
        grid=(t // tm + 1,),
        in_specs=[row(D_MODEL), row(SSD_DIM), row(ATT_DIM)] + [_const_spec(c.shape) for c in consts],
        out_specs=[row(D_MODEL), row(D_MODEL),
                   pl.BlockSpec((tm, LANES), lambda i: (routed(i), 0)),
                   pl.BlockSpec((sub, SUBLANES, LANES), lambda i: (routed(i), 0, 0))],
        scratch_shapes=[pltpu.VMEM((2, tm, LANES), F32)],
        out_shape=[jax.ShapeDtypeStruct((t, D_MODEL), F32),
                   jax.ShapeDtypeStruct((t, D_MODEL), BF16),
                   jax.ShapeDtypeStruct((t, LANES), F32),
                   jax.ShapeDtypeStruct((t // TOK_TILE, SUBLANES, LANES), F32)],
        compiler_params=pltpu.CompilerParams(
            dimension_semantics=("arbitrary",), vmem_limit_bytes=VMEM_LIMIT),
        name="merge_route",
    )(h, y_ssd, y_att, *consts)


def _meta_col(meta, j):
    lane = lax.broadcasted_iota(jnp.int32, meta.shape, 1)
    return jnp.sum(jnp.where(lane == j, meta, 0.0), axis=-1, keepdims=True)


def _pair_selector(meta):
    rows = lax.broadcasted_iota(jnp.int32, (meta.shape[0], SORT_ROWS), 1)
    p1 = _meta_col(meta, META_P1).astype(jnp.int32)
    p2 = _meta_col(meta, META_P2).astype(jnp.int32)
    return jnp.where(rows == p1, 1.0, jnp.where(rows == p2, 1.0, 0.0)).astype(BF16)


def _segment_dmas(off_ref, cnt_ref, row_ref, b, tile_buf, hbm, sem, to_hbm, wait):
    for e in range(N_EXPERTS):
        k = b * N_EXPERTS + e
        n = pl.multiple_of(cnt_ref[k], SEG_ALIGN)
        v = tile_buf.at[pl.ds(pl.multiple_of(off_ref[k], SEG_ALIGN), n)]
        h = hbm.at[pl.ds(pl.multiple_of(row_ref[k], SEG_ALIGN), n)]
        cp = pltpu.make_async_copy(v, h, sem) if to_hbm else pltpu.make_async_copy(h, v, sem)
        if wait:
            cp.wait()
        else:
            cp.start()


def _token_payload(meta):
    cols = {PAY_E1: _meta_col(meta, META_E1)}
    for base, src in ((PAY_G1, META_G1), (PAY_G2, META_G2)):
        for k, part in enumerate(_split3(_meta_col(meta, src))):
            cols[base + k] = part.astype(F32)
    lane = lax.broadcasted_iota(jnp.int32, (meta.shape[0], LANES), 1)
    out = jnp.zeros((meta.shape[0], LANES), F32)
    for k, c in cols.items():
        out = jnp.where(lane == k, c, out)
    return out.astype(BF16)


def _dispatch_kernel(off_ref, cnt_ref, row_ref, tail_row_ref, tail_cnt_ref, xn_ref, meta_ref, *rest,
                     tile0, first):
    xs_ref, sort_buf, zero_buf, sem, tail_sem = rest if first else rest[1:]
    b = pl.program_id(0)
    nb = pl.num_programs(0)
    slot = b % 2
    seg = functools.partial(_segment_dmas, off_ref, cnt_ref, row_ref, hbm=xs_ref, to_hbm=True)

    def tails(wait):
        def body(e, carry):
            n = pl.multiple_of(tail_cnt_ref[e], SEG_ALIGN)

            @pl.when(n > 0)
            def _():
                cp = pltpu.make_async_copy(
                    zero_buf.at[pl.ds(0, n)],
                    xs_ref.at[pl.ds(pl.multiple_of(tail_row_ref[e], SEG_ALIGN), n)], tail_sem.at[0])
                if wait:
                    cp.wait()
                else:
                    cp.start()
            return carry

        lax.fori_loop(0, N_EXPERTS, body, 0)

    @pl.when(b >= 2)
    def _():
        seg(tile0 + b - 2, sort_buf.at[slot], sem=sem.at[slot], wait=True)

    if first:
        @pl.when(b == 0)
        def _():
            zero_buf[...] = jnp.zeros(zero_buf.shape, BF16)
            tails(wait=False)

    meta = meta_ref[...]
    payload = jnp.concatenate([xn_ref[...], _token_payload(meta)], axis=1)
    sort_buf[slot] = _dot_tn(_pair_selector(meta), payload).astype(BF16)
    seg(tile0 + b, sort_buf.at[slot], sem=sem.at[slot], wait=False)

    @pl.when(b == nb - 1)
    def _():
        @pl.when(b >= 1)
        def _():
            seg(tile0 + b - 1, sort_buf.at[1 - slot], sem=sem.at[1 - slot], wait=True)

        seg(tile0 + b, sort_buf.at[slot], sem=sem.at[slot], wait=True)
        if first:
            tails(wait=True)


def _expert_kernel(tile_expert_ref, n_active_ref, xs_ref, wg_ref, wu_ref, wd_ref, y_ref):
    @pl.when(pl.program_id(0) < n_active_ref[0])
    def _():
        x = xs_ref[:, 0:D_MODEL]
        pay = xs_ref[:, D_MODEL:XS_COLS].astype(F32)
        lane = lax.broadcasted_iota(jnp.int32, pay.shape, 1)
        pick = lambda lo, n: jnp.sum(jnp.where((lane >= lo) & (lane < lo + n), pay, 0.0), axis=-1, keepdims=True)
        expert = tile_expert_ref[pl.program_id(0)].astype(F32)
        gate = jnp.where(pick(PAY_E1, 1) == expert, pick(PAY_G1, 3), pick(PAY_G2, 3))
        gu = _dot(x, jnp.concatenate([wg_ref[...].astype(BF16), wu_ref[...].astype(BF16)], axis=1))
        hid = _silu(gu[:, 0:EXPERT_FF]) * gu[:, EXPERT_FF:2 * EXPERT_FF]
        y_ref[...] = (gate * _dot(hid.astype(BF16), wd_ref[...].astype(BF16))).astype(BF16)


def _combine_kernel(off_ref, cnt_ref, row_ref, h1_ref, meta_ref, y_ref, o_ref, y_buf, sem, *, tile0):
    b = pl.program_id(0)
    nb = pl.num_programs(0)
    slot = b % 2
    seg = functools.partial(_segment_dmas, off_ref, cnt_ref, row_ref, hbm=y_ref, to_hbm=False)

    @pl.when(b == 0)
    def _():
        y_buf[...] = jnp.zeros(y_buf.shape, BF16)
        seg(tile0 + b, y_buf.at[slot], sem=sem.at[slot], wait=False)

    @pl.when(b + 1 < nb)
    def _():
        seg(tile0 + b + 1, y_buf.at[1 - slot], sem=sem.at[1 - slot], wait=False)

    seg(tile0 + b, y_buf.at[slot], sem=sem.at[slot], wait=True)
    o_ref[...] = h1_ref[...] + _dot(_pair_selector(meta_ref[...]), y_buf[slot, 0:SORT_ROWS, :])


def _moe(parts, w, tm):
    i32 = jnp.int32
    n_tiles = [p[0].shape[0] // TOK_TILE for p in parts]
    nb = sum(n_tiles)
    t = nb * TOK_TILE
    cnt = jnp.concatenate([p[3][:, 0, :N_EXPERTS] for p in parts], axis=0).astype(i32)
    cnt_al = (cnt + (SEG_ALIGN - 1)) // SEG_ALIGN * SEG_ALIGN
    off = jnp.cumsum(cnt_al, axis=1) - cnt_al
    tot = jnp.sum(cnt_al, axis=0)
    tot_tm = (tot + (tm - 1)) // tm * tm
    start = jnp.cumsum(tot_tm) - tot_tm
    row = start[None, :] + jnp.cumsum(cnt_al, axis=0) - cnt_al
    tile_ends = jnp.cumsum(tot_tm // tm)
    max_rows = 2 * t + (SEG_ALIGN - 1) * min(N_EXPERTS * nb, 2 * t) + N_EXPERTS * (tm - SEG_ALIGN)
    nt = -(-max_rows // tm)
    tile_expert = jnp.sum((jnp.arange(nt, dtype=i32)[:, None] >= tile_ends[None, :]).astype(i32), axis=1)
    tile_expert = jnp.minimum(tile_expert, N_EXPERTS - 1)
    n_active = tile_ends[-1:].astype(i32)
    empty = cnt_al == 0
    flat = lambda a: a.reshape(-1).astype(i32)
    n_rows = flat(jnp.maximum(cnt_al, SEG_ALIGN))
    e_idx = jnp.arange(N_EXPERTS, dtype=i32)[None, :]
    spare = nt * tm + ((jnp.arange(nb, dtype=i32)[:, None] % 2) * N_EXPERTS + e_idx) * SEG_ALIGN
    dispatch_tables = [flat(off), n_rows, flat(jnp.where(empty, spare, row))]
    combine_tables = [flat(jnp.where(empty, SORT_ROWS + e_idx * SEG_ALIGN, off)), n_rows,
                      flat(jnp.where(empty, 0, row))]
    n_spare = 2 * N_EXPERTS * SEG_ALIGN
    tail_tables = [(start + tot).astype(i32), (tot_tm - tot).astype(i32)]

    tok = lambda n: pl.BlockSpec((TOK_TILE, n), lambda i, *_: (i, 0))
    hbm = pl.BlockSpec(memory_space=pl.ANY)
    n_prefetch = len(dispatch_tables) + len(tail_tables)
    xs = None
    tile0 = 0
    for (_, xn, meta, _), n in zip(parts, n_tiles):
        first = xs is None
        xs = pl.pallas_call(
            functools.partial(_dispatch_kernel, tile0=tile0, first=first),
            grid_spec=pltpu.PrefetchScalarGridSpec(
                num_scalar_prefetch=n_prefetch, grid=(n,),
                in_specs=[tok(D_MODEL), tok(LANES)] + ([] if first else [hbm]),
                out_specs=hbm,
                scratch_shapes=[pltpu.VMEM((2, SORT_ROWS, XS_COLS), BF16),
                                pltpu.VMEM((tm, XS_COLS), BF16),
                                pltpu.SemaphoreType.DMA((2,)),
                                pltpu.SemaphoreType.DMA((1,))]),
            out_shape=jax.ShapeDtypeStruct((nt * tm + n_spare, XS_COLS), BF16),
            input_output_aliases={} if first else {n_prefetch + 2: 0},
            compiler_params=pltpu.CompilerParams(
                dimension_semantics=("arbitrary",), vmem_limit_bytes=VMEM_LIMIT),
            name="dispatch",
        )(*dispatch_tables, *tail_tables, xn, meta, *([] if first else [xs]))
        tile0 += n

    act = lambda i, te, na: jnp.minimum(i, na[0] - 1)
    y = pl.pallas_call(
        _expert_kernel,
        grid_spec=pltpu.PrefetchScalarGridSpec(
            num_scalar_prefetch=2, grid=(nt,),
            in_specs=[pl.BlockSpec((tm, XS_COLS), lambda i, te, na: (act(i, te, na), 0)),
                      pl.BlockSpec((None, D_MODEL, EXPERT_FF), lambda i, te, na: (te[act(i, te, na)], 0, 0)),
                      pl.BlockSpec((None, D_MODEL, EXPERT_FF), lambda i, te, na: (te[act(i, te, na)], 0, 0)),
                      pl.BlockSpec((None, EXPERT_FF, D_MODEL), lambda i, te, na: (te[act(i, te, na)], 0, 0))],
            out_specs=pl.BlockSpec((tm, D_MODEL), lambda i, te, na: (act(i, te, na), 0))),
        out_shape=jax.ShapeDtypeStruct((nt * tm, D_MODEL), BF16),
        compiler_params=pltpu.CompilerParams(
            dimension_semantics=("arbitrary",), vmem_limit_bytes=VMEM_LIMIT),
        name="experts",
    )(tile_expert, n_active, xs, w["w_gate"], w["w_up"], w["w_down"])

    outs = []
    tile0 = 0
    for (h1, _, meta, _), n in zip(parts, n_tiles):
        outs.append(pl.pallas_call(
            functools.partial(_combine_kernel, tile0=tile0),
            grid_spec=pltpu.PrefetchScalarGridSpec(
                num_scalar_prefetch=len(combine_tables), grid=(n,),
                in_specs=[tok(D_MODEL), tok(LANES), hbm],
                out_specs=tok(D_MODEL),
                scratch_shapes=[pltpu.VMEM((2, SORT_ROWS + N_EXPERTS * SEG_ALIGN, D_MODEL), BF16),
                                pltpu.SemaphoreType.DMA((2,))]),
            out_shape=jax.ShapeDtypeStruct((n * TOK_TILE, D_MODEL), F32),
            compiler_params=pltpu.CompilerParams(
                dimension_semantics=("arbitrary",), vmem_limit_bytes=VMEM_LIMIT),
            name="combine",
        )(*combine_tables, h1, meta, y))
        tile0 += n
    return outs


def _prepare_weights(g_mix, w_in, conv_w, conv_b, dt_bias, a_log, d_skip, g_ssd_out, g_q, g_k,
                     attn_sinks, g_att_out, w_out, g_ffn, w_route_group, b_route_group,
                     w_route_expert, b_route_expert, w_gate, w_up, w_down):
    cuts = np.cumsum([0, SSD_DIM, CONV_DIM, SSD_HEADS, ATT_DIM, KV_DIM, KV_DIM])
    seg = lambda i: w_in[:, cuts[i]:cuts[i + 1]]
    pad_lanes = lambda a: jnp.pad(a, ((0, 0), (0, LANES - a.shape[1])))
    bd = np.kron(np.eye(KV_DIM // HEAD_DIM, dtype=np.float32), np.ones((HEAD_DIM, HEAD_DIM), np.float32))
    n_route = N_EXPERTS + N_EXPERT_GROUPS
    return {
        "g_mix": g_mix.reshape(1, D_MODEL),
        "w_qk": jnp.concatenate([seg(3), seg(4)], axis=1).astype(BF16),
        "w_rest": jnp.concatenate([seg(0), seg(1), seg(5), pad_lanes(jnp.tile(seg(2), (1, DT_COPIES)))],
                                  axis=1).astype(BF16),
        "g_qk": jnp.concatenate([jnp.tile(g_q, ATT_HEADS), jnp.tile(g_k, ATT_KV_HEADS)]).reshape(1, ATT_DIM + KV_DIM),
        "bd": jnp.asarray(np.concatenate([bd, bd], axis=0), BF16),
        "conv_w": conv_w, "conv_b": conv_b.reshape(1, CONV_DIM),
        "dt_bias": pad_lanes(jnp.tile(dt_bias, DT_COPIES).reshape(1, DT_COPIES * SSD_HEADS)),
        "a_log": pad_lanes(jnp.tile(a_log, DT_COPIES).reshape(1, DT_COPIES * SSD_HEADS)),
        "d_skip": jnp.repeat(d_skip, HEAD_DIM).reshape(1, SSD_DIM),
        "g_ssd_out": g_ssd_out.reshape(1, SSD_DIM),
        "attn_sinks": attn_sinks,
        "g_att_out": g_att_out.reshape(1, ATT_DIM),
        "w_out": w_out.astype(BF16),
        "g_ffn": g_ffn.reshape(1, D_MODEL),
        "w_route": pad_lanes(jnp.concatenate([w_route_expert, w_route_group], axis=1)).astype(BF16),
        "b_route": pad_lanes(jnp.concatenate([b_route_expert, b_route_group]).reshape(1, n_route)),
        "w_gate": w_gate, "w_up": w_up, "w_down": w_down,
    }


def _segment(x3, w, mk, mv, tail, h0t, kprev, vprev, L, tb_ssd, tb_att, tm, from_cache, sb=1):
    S, Ls, _ = x3.shape
    x = x3.reshape(S * Ls, D_MODEL)
    z, xbc, q, k, v, dt = _project(x, w, tm)
    r3 = lambda a: a.reshape(S, Ls, a.shape[-1])
    xbc3, k3, v3 = r3(xbc), r3(k), r3(v)
    y_ssd, h_new = _ssd(xbc3, r3(z), r3(dt), tail, h0t, w, L, tb_ssd, sb)
    if not from_cache:
        kprev, vprev = k3, v3
    y_att = _attention(r3(q), k3, v3, kprev, vprev, mk, mv, w, L, tb_att, from_cache, sb)
    part = _merge(x, y_ssd.reshape(S * Ls, SSD_DIM), y_att.reshape(S * Ls, ATT_DIM), w, tm)
    return part, xbc3, h_new, k3, v3


def _state_to_kernel(h):
    return h.reshape(h.shape[0], SSD_DIM, D_STATE)


def _state_from_kernel(hk):
    return hk.reshape(hk.shape[0], SSD_HEADS, HEAD_DIM, D_STATE)


def kernel(x_prompt, x_sample, cache_conv, state_ssd, cache_k, cache_v, meta_tokens, g_mix, w_in, conv_w, conv_b, dt_bias, a_log, d_skip, g_ssd_out, g_q, g_k, attn_sinks, g_att_out, w_out, g_ffn, w_route_group, b_route_group, w_route_expert, b_route_expert, w_gate, w_up, w_down):
    w = _prepare_weights(g_mix[0], w_in[0], conv_w[0], conv_b[0], dt_bias[0], a_log[0], d_skip[0],
                         g_ssd_out[0], g_q[0], g_k[0], attn_sinks[0], g_att_out[0], w_out[0], g_ffn[0],
                         w_route_group[0], b_route_group[0], w_route_expert[0], b_route_expert[0],
                         w_gate[0], w_up[0], w_down[0])
    n_b = x_sample.shape[0]
    n_dec = x_sample.shape[1]

    _, m_xbc, _, mk, mv, m_dt = _project(meta_tokens, w, N_META)
    zero_tail = jnp.zeros((1, CONV_WIDTH - 1, CONV_DIM), F32)
    zero_state = jnp.zeros((1, SSD_DIM, D_STATE), F32)
    m_xbc3 = m_xbc.reshape(1, N_META, CONV_DIM)
    _, m_state = _ssd(m_xbc3, jnp.zeros((1, N_META, SSD_DIM), F32), m_dt.reshape(1, N_META, LANES),
                      zero_tail, zero_state, w, N_META, N_META)
    m_tail = m_xbc3[:, N_META - (CONV_WIDTH - 1):]

    part_p, xbc_p, st_p, k_p, v_p = _segment(
        x_prompt, w, mk, mv, m_tail, m_state, None, None,
        L=CHUNK, tb_ssd=512, tb_att=512, tm=512, from_cache=False)
    part_s, xbc_s, st_s, k_s, v_s = _segment(
        x_sample, w, mk, mv, cache_conv[0], _state_to_kernel(state_ssd[0]),
        cache_k[0].reshape(n_b, WINDOW, KV_DIM), cache_v[0].reshape(n_b, WINDOW, KV_DIM),
        L=n_dec, tb_ssd=n_dec, tb_att=n_dec, tm=512, from_cache=True, sb=8)
    yp, ys = _moe([part_p, part_s], w, tm=512)
    yp = yp.reshape(x_prompt.shape)
    ys = ys.reshape(x_sample.shape)

    heads = lambda a, rows: a.reshape(a.shape[0], rows, ATT_KV_HEADS, HEAD_DIM)[None]
    return (yp, ys,
            xbc_p[:, -(CONV_WIDTH - 1):][None],
            _state_from_kernel(st_p)[None],
            heads(k_p[:, -WINDOW:], WINDOW), heads(v_p[:, -WINDOW:], WINDOW),
            xbc_s[:, -(CONV_WIDTH - 1):][None],
            _state_from_kernel(st_s)[None],
            heads(k_s, n_dec), heads(v_s, n_dec))
```

```python
import functools
import math

import numpy as np
import jax
import jax.numpy as jnp
from jax import lax
from jax.experimental import pallas as pl
from jax.experimental.pallas import tpu as pltpu

D_MODEL = 1024
CHUNK = 64
N_META = 16
HEAD_DIM = 64
ATT_HEADS = 16
ATT_KV_HEADS = 4
ATT_REP = ATT_HEADS // ATT_KV_HEADS
ATT_DIM = ATT_HEADS * HEAD_DIM
KV_DIM = ATT_KV_HEADS * HEAD_DIM
WINDOW = 128
SSD_HEADS = 16
SSD_DIM = SSD_HEADS * HEAD_DIM
SSD_GROUPS = 2
GROUP_DIM = SSD_DIM // SSD_GROUPS
D_STATE = 128
CONV_WIDTH = 4
CONV_DIM = SSD_DIM + 2 * SSD_GROUPS * D_STATE
N_EXPERT_GROUPS = 4
EXPERTS_PER_GROUP = 8
N_EXPERTS = N_EXPERT_GROUPS * EXPERTS_PER_GROUP
EXPERT_FF = D_MODEL // 4
EPS = 1e-6

LANES = 128
SUBLANES = 8
MXU_DIM = 256
DT_COPIES = 3
KEY_SPAN = 256
VMEM_LIMIT = 48 * 1024 * 1024

F32 = jnp.float32
BF16 = jnp.bfloat16
NEG_INF = float("-inf")
LOG2E = math.log2(math.e)


def _dot(a, b):
    return jnp.dot(a, b, preferred_element_type=F32)


def _dot_nt(a, b):
    return lax.dot_general(a, b, (((1,), (1,)), ((), ())), preferred_element_type=F32)


def _dot_tn(a, b):
    return lax.dot_general(a, b, (((0,), (0,)), ((), ())), preferred_element_type=F32)


def _split3(x):
    hi = x.astype(BF16)
    r = x - hi.astype(F32)
    mid = r.astype(BF16)
    lo = (r - mid.astype(F32)).astype(BF16)
    return hi, mid, lo


def _pack3(x):
    hi, mid, lo = _split3(x)
    lane = lax.broadcasted_iota(jnp.int32, x.shape, 1)
    packed = jnp.where(lane < SSD_HEADS, hi.astype(F32),
                       jnp.where(lane < 2 * SSD_HEADS, mid.astype(F32), lo.astype(F32)))
    return packed.astype(BF16)


def _sel_dot(sel, x):
    hi, mid, lo = _split3(x)
    return _dot(sel, hi) + _dot(sel, mid) + _dot(sel, lo)


def _silu(x):
    return x * (1.0 / (1.0 + jnp.exp2(x * (-LOG2E))))


def _rms(x, g):
    ms = jnp.mean(x * x, axis=-1, keepdims=True)
    return x * lax.rsqrt(ms + EPS) * g


def _const_spec(shape):
    n = len(shape)
    return pl.BlockSpec(shape, lambda *_: (0,) * n, pipeline_mode=pl.Buffered(1))


def _proj_kernel(x_ref, gmix_ref, wqk_ref, wrest_ref, gqk_ref, bd_ref,
                 z_ref, xbc_ref, q_ref, k_ref, v_ref, dt_ref):
    tm = x_ref.shape[0]
    xn = _rms(x_ref[...], gmix_ref[...]).astype(BF16)
    qk = _dot(xn, wqk_ref[...])
    n_slices = (ATT_DIM + KV_DIM) // KV_DIM
    sq = jnp.concatenate([qk[:, j * KV_DIM:(j + 1) * KV_DIM] for j in range(n_slices)], axis=0)
    ms = _dot((sq * sq).astype(BF16), bd_ref[...]) * (1.0 / HEAD_DIM)
    inv = lax.rsqrt(ms + EPS)
    for j in range(n_slices):
        sl = slice(j * KV_DIM, (j + 1) * KV_DIM)
        normed = qk[:, sl] * inv[j * tm:(j + 1) * tm, :] * gqk_ref[:, sl]
        if j < ATT_DIM // KV_DIM:
            q_ref[:, sl] = (normed * (HEAD_DIM ** -0.5 * LOG2E)).astype(BF16)
        else:
            k_ref[...] = normed
    rest = _dot(xn, wrest_ref[...])
    cuts = np.cumsum([0, SSD_DIM, CONV_DIM, KV_DIM, LANES])
    for ref, lo_c, hi_c in zip((z_ref, xbc_ref, v_ref, dt_ref), cuts[:-1], cuts[1:]):
        ref[...] = rest[:, lo_c:hi_c]


def _project(x, w, tm):
    t = x.shape[0]
    assert t % tm == 0
    row = lambda n: pl.BlockSpec((tm, n), lambda i: (i, 0))
    ins = [x, w["g_mix"], w["w_qk"], w["w_rest"], w["g_qk"], w["bd"]]
    in_specs = [row(D_MODEL)] + [_const_spec(a.shape) for a in ins[1:]]
    out_dims = (SSD_DIM, CONV_DIM, ATT_DIM, KV_DIM, KV_DIM, LANES)
    out_dtypes = (F32, F32, BF16, F32, F32, F32)
    return pl.pallas_call(
        _proj_kernel,
        grid=(t // tm,),
        in_specs=in_specs,
        out_specs=[row(n) for n in out_dims],
        out_shape=[jax.ShapeDtypeStruct((t, n), d) for n, d in zip(out_dims, out_dtypes)],
        compiler_params=pltpu.CompilerParams(
            dimension_semantics=("arbitrary",), vmem_limit_bytes=VMEM_LIMIT),
        name="projection",
    )(*ins)


N_SSD_STREAM_INPUTS = 5
N_SSD_CONSTS = 9


def _ssd_kernel(*refs, L, TB, SB):
    streamed = lambda k: k < N_SSD_STREAM_INPUTS or k >= N_SSD_STREAM_INPUTS + N_SSD_CONSTS
    for s in range(SB):
        _ssd_stream(*[r.at[s] if streamed(k) else r for k, r in enumerate(refs)], L=L, TB=TB)


def _ssd_stream(xbc_ref, z_ref, dt_ref, tail_ref, h0_ref, convw_ref, convb_ref, dtb_ref,
                alog_ref, dskip_ref, gout_ref, ehead_ref, epos_ref, tri_ref,
                y_ref, hout_ref, buf_ref, act_ref, dts_ref, ys_ref, st_ref, *, L, TB):
    hp = LANES // L
    n_lane_tiles = SSD_HEADS // hp
    hq = MXU_DIM // L
    n_tiles = SSD_HEADS // hq
    tile_w = hq * HEAD_DIM
    b = pl.program_id(1)
    pad = SUBLANES - (CONV_WIDTH - 1)

    @pl.when(b == 0)
    def _():
        buf_ref[0:SUBLANES, :] = jnp.zeros((SUBLANES, CONV_DIM), F32)
        buf_ref[pad:SUBLANES, :] = tail_ref[...]
        st_ref[...] = h0_ref[...].T

    buf_ref[SUBLANES:SUBLANES + TB, :] = xbc_ref[...]
    rows = buf_ref[...]
    acc = convb_ref[...] + convw_ref[CONV_WIDTH - 1:CONV_WIDTH, :] * rows[SUBLANES:, :]
    for j in range(CONV_WIDTH - 1):
        shifted = pltpu.roll(rows, CONV_WIDTH - 1 - j, axis=0)[SUBLANES:, :]
        acc = acc + convw_ref[j:j + 1, :] * shifted
    act_ref[...] = _silu(acc)
    buf_ref[pad:SUBLANES, :] = buf_ref[TB + pad:TB + SUBLANES, :]

    dtx = dt_ref[...] + dtb_ref[...]
    dts_ref[...] = jnp.maximum(dtx, 0.0) + jnp.log(1.0 + jnp.exp(-jnp.abs(dtx)))

    a_row = -jnp.exp(alog_ref[...]) * LOG2E
    ehead = ehead_ref[...]
    epos = epos_ref[...]
    tri = tri_ref[...]
    dskip = dskip_ref[...]
    row_i = lax.broadcasted_iota(jnp.int32, (L, n_lane_tiles * LANES), 0)
    col_i = lax.broadcasted_iota(jnp.int32, (L, n_lane_tiles * LANES), 1)
    pos_i = col_i % L
    diag_mask = row_i == pos_i
    causal_mask = row_i >= pos_i
    colhead = lax.broadcasted_iota(jnp.int32, (L, tile_w), 1) // HEAD_DIM

    def chunk(c):
        r0 = pl.multiple_of(c * L, L)
        dtc = dts_ref[pl.ds(r0, L), :]
        cs = _sel_dot(tri, dtc * a_row)
        cs16 = _pack3(cs)
        dt_b = _dot(_pack3(dtc), ehead)
        cs_b = _dot(cs16, ehead)
        cs_last_b = cs_b[L - 1:L, :]
        ecs_b = jnp.exp2(cs_b)
        dec_end_b = jnp.exp2(cs_last_b - cs_b)
        bdec_b = jnp.exp2(cs_last_b)
        cs_col = cs_b if L == HEAD_DIM else _dot(cs16, epos)
        cs_row = jnp.sum(jnp.where(diag_mask, cs_col, 0.0), axis=0, keepdims=True)
        lmat = jnp.exp2(jnp.where(causal_mask, cs_col - cs_row, NEG_INF))

        xs = act_ref[pl.ds(r0, L), 0:SSD_DIM]
        xdt_f = xs * dt_b
        xdt = xdt_f.astype(BF16)
        wx = (xdt_f * dec_end_b).astype(BF16)
        st = st_ref[...]
        st16 = st.astype(BF16)

        y_diag = []
        y_off = []
        s_new = []
        cb = [None] * SSD_GROUPS
        for g in range(SSD_GROUPS):
            bg = act_ref[pl.ds(r0, L), SSD_DIM + g * D_STATE:SSD_DIM + (g + 1) * D_STATE].astype(BF16)
            cg = act_ref[pl.ds(r0, L), SSD_DIM + (SSD_GROUPS + g) * D_STATE:
                         SSD_DIM + (SSD_GROUPS + g + 1) * D_STATE].astype(BF16)
            cb[g] = _dot_nt(cg, jnp.concatenate([bg] * hp, axis=0))
            gsl = slice(g * GROUP_DIM, (g + 1) * GROUP_DIM)
            y_off.append(_dot(cg, st16[:, gsl]))
            s_new.append(_dot_tn(bg, wx[:, gsl]))
        lane_tiles = MXU_DIM // LANES
        for t in range(n_tiles):
            cbt = jnp.concatenate(
                [cb[((t * lane_tiles + j) * hp * HEAD_DIM) // GROUP_DIM] for j in range(lane_tiles)], axis=1)
            gmat = (lmat[:, t * MXU_DIM:(t + 1) * MXU_DIM] * cbt).astype(BF16)
            xt = xdt[:, t * tile_w:(t + 1) * tile_w]
            rhs = jnp.concatenate(
                [jnp.where(colhead == hh, xt, jnp.zeros_like(xt)) for hh in range(hq)], axis=0)
            y_diag.append(_dot(gmat, rhs))
        y = (jnp.concatenate(y_diag, axis=1) + jnp.concatenate(y_off, axis=1) * ecs_b
             + dskip * xs)
        ys_ref[pl.ds(r0, L), :] = y
        st_ref[...] = bdec_b * st + jnp.concatenate(s_new, axis=1)

    n_chunks = TB // L
    unroll = n_chunks if n_chunks in (4, 8) else 1

    def chunks(i, carry):
        for u in range(unroll):
            chunk(i * unroll + u)
        return carry

    lax.fori_loop(0, n_chunks // unroll, chunks, 0)

    yg = ys_ref[...] * _silu(z_ref[...])
    y_ref[...] = _rms(yg, gout_ref[...])

    @pl.when(b == pl.num_programs(1) - 1)
    def _():
        hout_ref[...] = st_ref[...].T


def _ssd(xbc, z, dt, tail, h0t, w, L, TB, SB=1):
    S, Ls, _ = xbc.shape
    assert Ls % TB == 0 and TB % L == 0 and LANES % L == 0 and S % SB == 0
    nb = Ls // TB
    assert SB == 1 or nb == 1
    hp = LANES // L
    n_tiles = SSD_HEADS // hp
    per_stream = lambda a: (lambda s, b: (s, 0, 0)) if a.shape[0] == S and S > 1 else (lambda s, b: (0, 0, 0))
    seq = lambda n: pl.BlockSpec((SB, TB, n), lambda s, b: (s, b, 0))
    ehead = np.zeros((LANES, SSD_DIM), np.float32)
    epos = np.zeros((LANES, n_tiles * LANES), np.float32)
    for part in range(DT_COPIES):
        for h in range(SSD_HEADS):
            ehead[part * SSD_HEADS + h, h * HEAD_DIM:(h + 1) * HEAD_DIM] = 1.0
            epos[part * SSD_HEADS + h, h * L:(h + 1) * L] = 1.0
    tri = np.tril(np.ones((L, L), np.float32))
    consts = [w["conv_w"], w["conv_b"], w["dt_bias"], w["a_log"], w["d_skip"], w["g_ssd_out"],
              jnp.asarray(ehead, BF16), jnp.asarray(epos, BF16), jnp.asarray(tri, BF16)]
    in_specs = [seq(CONV_DIM), seq(SSD_DIM), seq(LANES),
                pl.BlockSpec((SB, CONV_WIDTH - 1, CONV_DIM), per_stream(tail)),
                pl.BlockSpec((SB, SSD_DIM, D_STATE), per_stream(h0t))]
    assert len(in_specs) == N_SSD_STREAM_INPUTS and len(consts) == N_SSD_CONSTS
    in_specs += [_const_spec(c.shape) for c in consts]
    return pl.pallas_call(
        functools.partial(_ssd_kernel, L=L, TB=TB, SB=SB),
        grid=(S // SB, nb),
        in_specs=in_specs,
        out_specs=[seq(SSD_DIM), pl.BlockSpec((SB, SSD_DIM, D_STATE), lambda s, b: (s, 0, 0))],
        out_shape=[jax.ShapeDtypeStruct((S, Ls, SSD_DIM), F32),
                   jax.ShapeDtypeStruct((S, SSD_DIM, D_STATE), F32)],
        scratch_shapes=[pltpu.VMEM((SB, SUBLANES + TB, CONV_DIM), F32),
                        pltpu.VMEM((SB, TB, CONV_DIM), F32),
                        pltpu.VMEM((SB, TB, LANES), F32),
                        pltpu.VMEM((SB, TB, SSD_DIM), F32),
                        pltpu.VMEM((SB, D_STATE, SSD_DIM), F32)],
        compiler_params=pltpu.CompilerParams(
            dimension_semantics=("arbitrary", "arbitrary"), vmem_limit_bytes=VMEM_LIMIT),
        name="ssd",
    )(xbc, z, dt, tail, h0t, *consts)


def _split_heads(x):
    lane = lax.broadcasted_iota(jnp.int32, (x.shape[0], LANES), 1)
    low = lane < HEAD_DIM
    lo, hi = [], []
    for t in range(KV_DIM // LANES):
        tile = x[:, t * LANES:(t + 1) * LANES]
        swapped = pltpu.roll(tile, HEAD_DIM, axis=1)
        lo += [jnp.where(low, tile, 0.0), jnp.where(low, swapped, 0.0)]
        hi += [jnp.where(low, 0.0, swapped), jnp.where(low, 0.0, tile)]
    return jnp.concatenate(lo, axis=1).astype(BF16), jnp.concatenate(hi, axis=1).astype(BF16)


def _attn_kernel(*refs, L, TB, SB, from_cache):
    for s in range(SB):
        _attn_stream(*[r if k == 0 or 6 <= k <= 9 else r.at[s] for k, r in enumerate(refs)],
                     L=L, TB=TB, from_cache=from_cache)


def _attn_stream(sink_ref, q_ref, k_ref, v_ref, kp_ref, vp_ref, mk_ref, mv_ref, bias_ref, g_ref,
                 o_ref, k_lo, k_hi, v_lo, v_hi, ybuf, s_buf, e_buf, sink_buf, *, L, TB, from_cache):
    b = pl.program_id(1)
    bufs = (k_lo, k_hi, v_lo, v_hi)
    if from_cache:
        window = _split_heads(kp_ref[...]) + _split_heads(vp_ref[...])
        for buf, val in zip(bufs, window):
            buf[0:WINDOW, :] = val
    else:
        @pl.when(b == 0)
        def _():
            for buf in bufs:
                buf[0:WINDOW, :] = jnp.zeros((WINDOW, buf.shape[1]), BF16)

        @pl.when(b > 0)
        def _():
            for buf in bufs:
                buf[0:WINDOW, :] = buf[TB:TB + WINDOW, :]

    for buf, val in zip(bufs, _split_heads(k_ref[...]) + _split_heads(v_ref[...])):
        buf[WINDOW:WINDOW + TB, :] = val
    n_pad = KEY_SPAN - WINDOW - L - N_META
    pad_rows = jnp.zeros((n_pad, 2 * KV_DIM), BF16)
    tails = [jnp.concatenate([t, pad_rows], axis=0)
             for t in _split_heads(mk_ref[...]) + _split_heads(mv_ref[...])]
    lane = lax.broadcasted_iota(jnp.int32, (2 * L, 2 * KEY_SPAN), 1) % KEY_SPAN
    lane_h = lax.broadcasted_iota(jnp.int32, (L, LANES), 1)
    lane_v = lax.broadcasted_iota(jnp.int32, (KEY_SPAN, LANES), 1)
    ones_lo = jnp.where(lane_v < HEAD_DIM, 1.0, 0.0).astype(BF16)
    ones_hi = jnp.where(lane_v < HEAD_DIM, 0.0, 1.0).astype(BF16)
    n_chunks = TB // L
    pairs_per_group = ATT_REP // 2

    def chunk_keys(c):
        r0 = c * L
        return [jnp.concatenate([buf[r0:r0 + WINDOW, :], buf[r0 + WINDOW:r0 + WINDOW + L, :], tail], axis=0)
                for buf, tail in zip(bufs, tails)]

    for c in range(n_chunks):
        r0 = c * L
        keys = chunk_keys(c)
        for g in range(ATT_KV_HEADS):
            gl = slice(g * LANES, (g + 1) * LANES)
            q4 = jnp.concatenate([q_ref[r0:r0 + L, p * LANES:(p + 1) * LANES]
                                  for p in range(pairs_per_group * g, pairs_per_group * (g + 1))], axis=0)
            s = _dot_nt(q4, jnp.concatenate([keys[0][:, gl], keys[1][:, gl]], axis=0)) + bias_ref[g]
            if not from_cache and r0 < WINDOW:
                n_invalid = WINDOW - (b * TB + r0)
                s = jnp.where(lane < n_invalid, NEG_INF, s)
            s_buf[c * ATT_KV_HEADS + g] = s

    for c in range(n_chunks):
        for pair in range(ATT_HEADS // 2):
            slot = c * ATT_KV_HEADS + pair // pairs_per_group
            rows = slice((pair % pairs_per_group) * L, (pair % pairs_per_group + 1) * L)
            sink_terms = []
            for half in range(2):
                cols = slice(half * KEY_SPAN, (half + 1) * KEY_SPAN)
                s = s_buf[slot, rows, cols]
                sink = sink_ref[2 * pair + half] * LOG2E
                m = jnp.maximum(jnp.max(s, axis=-1, keepdims=True), sink)
                e_buf[slot, rows, cols] = jnp.exp2(s - m).astype(BF16)
                sink_terms.append(jnp.exp2(sink - m))
            sink_buf[c * (ATT_HEADS // 2) + pair] = jnp.where(lane_h < HEAD_DIM, sink_terms[0], sink_terms[1])

    for c in range(n_chunks):
        r0 = c * L
        keys = chunk_keys(c)
        for g in range(ATT_KV_HEADS):
            gl = slice(g * LANES, (g + 1) * LANES)
            vcat = jnp.concatenate([jnp.concatenate([keys[2][:, gl], ones_lo], axis=1),
                                    jnp.concatenate([keys[3][:, gl], ones_hi], axis=1)], axis=0)
            ov = _dot(e_buf[c * ATT_KV_HEADS + g], vcat)
            for j in range(pairs_per_group):
                pair = pairs_per_group * g + j
                den = ov[j * L:(j + 1) * L, LANES:2 * LANES] + sink_buf[c * (ATT_HEADS // 2) + pair]
                ybuf[r0:r0 + L, pair * LANES:(pair + 1) * LANES] = ov[j * L:(j + 1) * L, 0:LANES] * (1.0 / den)
    o_ref[...] = _rms(ybuf[...], g_ref[...])


def _attention(q, k, v, kprev, vprev, mk, mv, w, L, TB, from_cache, SB=1):
    S, Ls, _ = q.shape
    assert Ls % TB == 0 and TB % L == 0 and (from_cache or TB >= WINDOW) and S % SB == 0
    nb = Ls // TB
    assert SB == 1 or (from_cache and nb == 1)
    seq = lambda n: pl.BlockSpec((SB, TB, n), lambda s, b: (s, b, 0))
    if not from_cache:
        kprev = vprev = jnp.zeros((1, WINDOW, KV_DIM), F32)
    prev = pl.BlockSpec((SB, WINDOW, KV_DIM),
                        (lambda s, b: (s, 0, 0)) if from_cache else (lambda s, b: (0, 0, 0)))
    slopes = 2.0 ** (-8.0 * np.arange(1, ATT_HEADS + 1, dtype=np.float64) / ATT_HEADS)
    dist = np.abs(WINDOW + np.arange(L)[:, None] - np.arange(WINDOW + L)[None, :])
    bias = np.full((ATT_HEADS, L, KEY_SPAN), NEG_INF, np.float32)
    bias[:, :, :WINDOW + L] = -slopes[:, None, None] * dist[None] * LOG2E
    bias[:, :, WINDOW + L:WINDOW + L + N_META] = 0.0
    bias = bias.reshape(ATT_KV_HEADS, ATT_REP // 2, 2, L, KEY_SPAN).transpose(0, 1, 3, 2, 4)
    bias = jnp.asarray(bias.reshape(ATT_KV_HEADS, ATT_REP // 2 * L, 2 * KEY_SPAN))
    n_chunks = TB // L
    in_specs = [pl.BlockSpec(memory_space=pltpu.SMEM),
                seq(ATT_DIM), seq(KV_DIM), seq(KV_DIM), prev, prev,
                _const_spec(mk.shape), _const_spec(mv.shape), _const_spec(bias.shape),
                _const_spec(w["g_att_out"].shape)]
    return pl.pallas_call(
        functools.partial(_attn_kernel, L=L, TB=TB, SB=SB, from_cache=from_cache),
        grid=(S // SB, nb),
        in_specs=in_specs,
        out_specs=seq(ATT_DIM),
        out_shape=jax.ShapeDtypeStruct((S, Ls, ATT_DIM), F32),
        scratch_shapes=[pltpu.VMEM((SB, WINDOW + TB, 2 * KV_DIM), BF16) for _ in range(4)]
                       + [pltpu.VMEM((SB, TB, ATT_DIM), F32),
                          pltpu.VMEM((SB, n_chunks * ATT_KV_HEADS, 2 * L, 2 * KEY_SPAN), F32),
                          pltpu.VMEM((SB, n_chunks * ATT_KV_HEADS, 2 * L, 2 * KEY_SPAN), BF16),
                          pltpu.VMEM((SB, n_chunks * ATT_HEADS // 2, L, LANES), F32)],
        compiler_params=pltpu.CompilerParams(
            dimension_semantics=("arbitrary", "arbitrary"), vmem_limit_bytes=VMEM_LIMIT),
        name="attention",
    )(w["attn_sinks"], q, k, v, kprev, vprev, mk, mv, bias, w["g_att_out"])


TOK_TILE = 512
SEG_ALIGN = 2 * SUBLANES
SORT_ROWS = 2 * TOK_TILE + N_EXPERTS * SEG_ALIGN
XS_COLS = D_MODEL + LANES
META_P1, META_P2, META_G1, META_G2, META_E1 = (N_EXPERTS + j for j in range(5))
PAY_G1, PAY_G2, PAY_E1 = 0, 3, 6


def _merge_kernel(h_ref, ys_ref, ya_ref, wo_ref, gffn_ref, wr_ref, br_ref, tri_ref, upper_ref,
                  h1_ref, xn_ref, meta_ref, cnt_ref, logit_buf):
    i = pl.program_id(0)

    @pl.when(i == 0)
    def _():
        logit_buf[...] = jnp.zeros(logit_buf.shape, F32)

    ycat = jnp.concatenate([ys_ref[...].astype(BF16), ya_ref[...].astype(BF16)], axis=1)
    h1 = h_ref[...] + _dot(ycat, wo_ref[...])
    h1_ref[...] = h1
    xn = _rms(h1, gffn_ref[...]).astype(BF16)
    xn_ref[...] = xn
    all_logits = logit_buf[(i + 1) % 2]
    logit_buf[i % 2] = _dot(xn, wr_ref[...]) + br_ref[...]
    tri = tri_ref[...]
    upper = upper_ref[...]
    lane = lax.broadcasted_iota(jnp.int32, (TOK_TILE, LANES), 1)
    big = jnp.int32(LANES)
    gmask = (lane >= N_EXPERTS) & (lane < N_EXPERTS + N_EXPERT_GROUPS)
    for s in range(all_logits.shape[0] // TOK_TILE):
        rows = slice(s * TOK_TILE, (s + 1) * TOK_TILE)
        logits = all_logits[rows]

        def top1(mask):
            mval = jnp.max(jnp.where(mask, logits, NEG_INF), axis=-1, keepdims=True)
            idx = jnp.min(jnp.where(mask & (logits == mval), lane, big), axis=-1, keepdims=True)
            return mval, idx

        gmax, gidx = top1(gmask)
        gate_g = 1.0 / jnp.sum(jnp.where(gmask, jnp.exp(logits - gmax), 0.0), axis=-1, keepdims=True)
        grp = gidx - N_EXPERTS
        emask = (lane // EXPERTS_PER_GROUP) == grp
        v1, i1 = top1(emask)
        v2, i2 = top1(emask & (lane != i1))
        e2 = jnp.exp(v2 - v1)
        g1 = gate_g / (1.0 + e2)
        g2 = gate_g * e2 / (1.0 + e2)
        oh1 = jnp.where(lane == i1, 1.0, 0.0)
        oh2 = jnp.where(lane == i2, 1.0, 0.0)
        oh = oh1 + oh2
        earlier = _dot(tri, oh.astype(BF16))
        cnt = jnp.sum(oh, axis=0, keepdims=True)
        units = jnp.floor((cnt + (SEG_ALIGN - 1)) * (1.0 / SEG_ALIGN))
        units = jnp.broadcast_to(units, (2 * SUBLANES, LANES)).astype(BF16)
        slot = _dot(units, upper)[0:1, :] * SEG_ALIGN + earlier
        p1 = jnp.sum(oh1 * slot, axis=-1, keepdims=True)
        p2 = jnp.sum(oh2 * slot, axis=-1, keepdims=True)
        meta = jnp.where(lane == META_P1, p1, 0.0)
        meta = jnp.where(lane == META_P2, p2, meta)
        meta = jnp.where(lane == META_G1, g1, meta)
        meta = jnp.where(lane == META_G2, g2, meta)
        meta = jnp.where(lane == META_E1, i1.astype(F32), meta)
        meta_ref[rows, :] = meta
        cnt_ref[s] = jnp.broadcast_to(cnt, (SUBLANES, LANES))


def _merge(h, y_ssd, y_att, w, tm):
    t = h.shape[0]
    assert t % tm == 0 and tm % TOK_TILE == 0
    sub = tm // TOK_TILE
    last = t // tm - 1
    row = lambda n: pl.BlockSpec((tm, n), lambda i: (jnp.minimum(i, last), 0))
    routed = lambda i: jnp.maximum(i - 1, 0)
    tri = jnp.asarray(np.tril(np.ones((TOK_TILE, TOK_TILE), np.float32), -1), BF16)
    upper = jnp.asarray(np.triu(np.ones((LANES, LANES), np.float32), 1), BF16)
    consts = [w["w_out"], w["g_ffn"], w["w_route"], w["b_route"], tri, upper]
    return pl.pallas_call(
        _merge_kernel,
        grid=(t // tm + 1,),
        in_specs=[row(D_MODEL), row(SSD_DIM), row(ATT_DIM)] + [_const_spec(c.shape) for c in consts],
        out_specs=[row(D_MODEL), row(D_MODEL),
                   pl.BlockSpec((tm, LANES), lambda i: (routed(i), 0)),
                   pl.BlockSpec((sub, SUBLANES, LANES), lambda i: (routed(i), 0, 0))],
        scratch_shapes=[pltpu.VMEM((2, tm, LANES), F32)],
        out_shape=[jax.ShapeDtypeStruct((t, D_MODEL), F32),
                   jax.ShapeDtypeStruct((t, D_MODEL), BF16),
                   jax.ShapeDtypeStruct((t, LANES), F32),
                   jax.ShapeDtypeStruct((t // TOK_TILE, SUBLANES, LANES), F32)],
        compiler_params=pltpu.CompilerParams(
            dimension_semantics=("arbitrary",), vmem_limit_bytes=VMEM_LIMIT),
        name="merge_route",
    )(h, y_ssd, y_att, *consts)


def _meta_col(meta, j):
    lane = lax.broadcasted_iota(jnp.int32, meta.shape, 1)
    return jnp.sum(jnp.where(lane == j, meta, 0.0), axis=-1, keepdims=True)


def _pair_selector(meta):
    rows = lax.broadcasted_iota(jnp.int32, (meta.shape[0], SORT_ROWS), 1)
    p1 = _meta_col(meta, META_P1).astype(jnp.int32)
    p2 = _meta_col(meta, META_P2).astype(jnp.int32)
    return jnp.where(rows == p1, 1.0, jnp.where(rows == p2, 1.0, 0.0)).astype(BF16)


def _segment_dmas(off_ref, cnt_ref, row_ref, b, tile_buf, hbm, sem, to_hbm, wait):
    for e in range(N_EXPERTS):
        k = b * N_EXPERTS + e
        n = pl.multiple_of(cnt_ref[k], SEG_ALIGN)
        v = tile_buf.at[pl.ds(pl.multiple_of(off_ref[k], SEG_ALIGN), n)]
        h = hbm.at[pl.ds(pl.multiple_of(row_ref[k], SEG_ALIGN), n)]
        cp = pltpu.make_async_copy(v, h, sem) if to_hbm else pltpu.make_async_copy(h, v, sem)
        if wait:
            cp.wait()
        else:
            cp.start()


def _token_payload(meta):
    cols = {PAY_E1: _meta_col(meta, META_E1)}
    for base, src in ((PAY_G1, META_G1), (PAY_G2, META_G2)):
        for k, part in enumerate(_split3(_meta_col(meta, src))):
            cols[base + k] = part.astype(F32)
    lane = lax.broadcasted_iota(jnp.int32, (meta.shape[0], LANES), 1)
    out = jnp.zeros((meta.shape[0], LANES), F32)
    for k, c in cols.items():
        out = jnp.where(lane == k, c, out)
    return out.astype(BF16)


def _dispatch_kernel(off_ref, cnt_ref, row_ref, tail_row_ref, tail_cnt_ref, xn_ref, meta_ref, *rest,
                     tile0, first):
    xs_ref, sort_buf, zero_buf, sem, tail_sem = rest if first else rest[1:]
    b = pl.program_id(0)
    nb = pl.num_programs(0)
    slot = b % 2
    seg = functools.partial(_segment_dmas, off_ref, cnt_ref, row_ref, hbm=xs_ref, to_hbm=True)

    def tails(wait):
        def body(e, carry):
            n = pl.multiple_of(tail_cnt_ref[e], SEG_ALIGN)

            @pl.when(n > 0)
            def _():
                cp = pltpu.make_async_copy(
                    zero_buf.at[pl.ds(0, n)],
                    xs_ref.at[pl.ds(pl.multiple_of(tail_row_ref[e], SEG_ALIGN), n)], tail_sem.at[0])
                if wait:
                    cp.wait()
                else:
                    cp.start()
            return carry

        lax.fori_loop(0, N_EXPERTS, body, 0)

    @pl.when(b >= 2)
    def _():
        seg(tile0 + b - 2, sort_buf.at[slot], sem=sem.at[slot], wait=True)

    if first:
        @pl.when(b == 0)
        def _():
            zero_buf[...] = jnp.zeros(zero_buf.shape, BF16)
            tails(wait=False)

    meta = meta_ref[...]
    payload = jnp.concatenate([xn_ref[...], _token_payload(meta)], axis=1)
    sort_buf[slot] = _dot_tn(_pair_selector(meta), payload).astype(BF16)
    seg(tile0 + b, sort_buf.at[slot], sem=sem.at[slot], wait=False)

    @pl.when(b == nb - 1)
    def _():
        @pl.when(b >= 1)
        def _():
            seg(tile0 + b - 1, sort_buf.at[1 - slot], sem=sem.at[1 - slot], wait=True)

        seg(tile0 + b, sort_buf.at[slot], sem=sem.at[slot], wait=True)
        if first:
            tails(wait=True)


def _expert_kernel(tile_expert_ref, n_active_ref, xs_ref, wg_ref, wu_ref, wd_ref, y_ref):
    @pl.when(pl.program_id(0) < n_active_ref[0])
    def _():
        x = xs_ref[:, 0:D_MODEL]
        pay = xs_ref[:, D_MODEL:XS_COLS].astype(F32)
        lane = lax.broadcasted_iota(jnp.int32, pay.shape, 1)
        pick = lambda lo, n: jnp.sum(jnp.where((lane >= lo) & (lane < lo + n), pay, 0.0), axis=-1, keepdims=True)
        expert = tile_expert_ref[pl.program_id(0)].astype(F32)
        gate = jnp.where(pick(PAY_E1, 1) == expert, pick(PAY_G1, 3), pick(PAY_G2, 3))
        gu = _dot(x, jnp.concatenate([wg_ref[...].astype(BF16), wu_ref[...].astype(BF16)], axis=1))
        hid = _silu(gu[:, 0:EXPERT_FF]) * gu[:, EXPERT_FF:2 * EXPERT_FF]
        y_ref[...] = (gate * _dot(hid.astype(BF16), wd_ref[...].astype(BF16))).astype(BF16)


def _combine_kernel(off_ref, cnt_ref, row_ref, h1_ref, meta_ref, y_ref, o_ref, y_buf, sem, *, tile0):
    b = pl.program_id(0)
    nb = pl.num_programs(0)
    slot = b % 2
    seg = functools.partial(_segment_dmas, off_ref, cnt_ref, row_ref, hbm=y_ref, to_hbm=False)

    @pl.when(b == 0)
    def _():
        y_buf[...] = jnp.zeros(y_buf.shape, BF16)
        seg(tile0 + b, y_buf.at[slot], sem=sem.at[slot], wait=False)

    @pl.when(b + 1 < nb)
    def _():
        seg(tile0 + b + 1, y_buf.at[1 - slot], sem=sem.at[1 - slot], wait=False)

    seg(tile0 + b, y_buf.at[slot], sem=sem.at[slot], wait=True)
    o_ref[...] = h1_ref[...] + _dot(_pair_selector(meta_ref[...]), y_buf[slot, 0:SORT_ROWS, :])


def _moe(parts, w, tm):
    i32 = jnp.int32
    n_tiles = [p[0].shape[0] // TOK_TILE for p in parts]
    nb = sum(n_tiles)
    t = nb * TOK_TILE
    cnt = jnp.concatenate([p[3][:, 0, :N_EXPERTS] for p in parts], axis=0).astype(i32)
    cnt_al = (cnt + (SEG_ALIGN - 1)) // SEG_ALIGN * SEG_ALIGN
    off = jnp.cumsum(cnt_al, axis=1) - cnt_al
    tot = jnp.sum(cnt_al, axis=0)
    tot_tm = (tot + (tm - 1)) // tm * tm
    start = jnp.cumsum(tot_tm) - tot_tm
    row = start[None, :] + jnp.cumsum(cnt_al, axis=0) - cnt_al
    tile_ends = jnp.cumsum(tot_tm // tm)
    max_rows = 2 * t + (SEG_ALIGN - 1) * min(N_EXPERTS * nb, 2 * t) + N_EXPERTS * (tm - SEG_ALIGN)
    nt = -(-max_rows // tm)
    tile_expert = jnp.sum((jnp.arange(nt, dtype=i32)[:, None] >= tile_ends[None, :]).astype(i32), axis=1)
    tile_expert = jnp.minimum(tile_expert, N_EXPERTS - 1)
    n_active = tile_ends[-1:].astype(i32)
    empty = cnt_al == 0
    flat = lambda a: a.reshape(-1).astype(i32)
    n_rows = flat(jnp.maximum(cnt_al, SEG_ALIGN))
    e_idx = jnp.arange(N_EXPERTS, dtype=i32)[None, :]
    spare = nt * tm + ((jnp.arange(nb, dtype=i32)[:, None] % 2) * N_EXPERTS + e_idx) * SEG_ALIGN
    dispatch_tables = [flat(off), n_rows, flat(jnp.where(empty, spare, row))]
    combine_tables = [flat(jnp.where(empty, SORT_ROWS + e_idx * SEG_ALIGN, off)), n_rows,
                      flat(jnp.where(empty, 0, row))]
    n_spare = 2 * N_EXPERTS * SEG_ALIGN
    tail_tables = [(start + tot).astype(i32), (tot_tm - tot).astype(i32)]

    tok = lambda n: pl.BlockSpec((TOK_TILE, n), lambda i, *_: (i, 0))
    hbm = pl.BlockSpec(memory_space=pl.ANY)
    n_prefetch = len(dispatch_tables) + len(tail_tables)
    xs = None
    tile0 = 0
    for (_, xn, meta, _), n in zip(parts, n_tiles):
        first = xs is None
        xs = pl.pallas_call(
            functools.partial(_dispatch_kernel, tile0=tile0, first=first),
            grid_spec=pltpu.PrefetchScalarGridSpec(
                num_scalar_prefetch=n_prefetch, grid=(n,),
                in_specs=[tok(D_MODEL), tok(LANES)] + ([] if first else [hbm]),
                out_specs=hbm,
                scratch_shapes=[pltpu.VMEM((2, SORT_ROWS, XS_COLS), BF16),
                                pltpu.VMEM((tm, XS_COLS), BF16),
                                pltpu.SemaphoreType.DMA((2,)),
                                pltpu.SemaphoreType.DMA((1,))]),
            out_shape=jax.ShapeDtypeStruct((nt * tm + n_spare, XS_COLS), BF16),
            input_output_aliases={} if first else {n_prefetch + 2: 0},
            compiler_params=pltpu.CompilerParams(
                dimension_semantics=("arbitrary",), vmem_limit_bytes=VMEM_LIMIT),
            name="dispatch",
        )(*dispatch_tables, *tail_tables, xn, meta, *([] if first else [xs]))
        tile0 += n

    act = lambda i, te, na: jnp.minimum(i, na[0] - 1)
    y = pl.pallas_call(
        _expert_kernel,
        grid_spec=pltpu.PrefetchScalarGridSpec(
            num_scalar_prefetch=2, grid=(nt,),
            in_specs=[pl.BlockSpec((tm, XS_COLS), lambda i, te, na: (act(i, te, na), 0)),
                      pl.BlockSpec((None, D_MODEL, EXPERT_FF), lambda i, te, na: (te[act(i, te, na)], 0, 0)),
                      pl.BlockSpec((None, D_MODEL, EXPERT_FF), lambda i, te, na: (te[act(i, te, na)], 0, 0)),
                      pl.BlockSpec((None, EXPERT_FF, D_MODEL), lambda i, te, na: (te[act(i, te, na)], 0, 0))],
            out_specs=pl.BlockSpec((tm, D_MODEL), lambda i, te, na: (act(i, te, na), 0))),
        out_shape=jax.ShapeDtypeStruct((nt * tm, D_MODEL), BF16),
        compiler_params=pltpu.CompilerParams(
            dimension_semantics=("arbitrary",), vmem_limit_bytes=VMEM_LIMIT),
        name="experts",
    )(tile_expert, n_active, xs, w["w_gate"], w["w_up"], w["w_down"])

    outs = []
    tile0 = 0
    for (h1, _, meta, _), n in zip(parts, n_tiles):
        outs.append(pl.pallas_call(
            functools.partial(_combine_kernel, tile0=tile0),
            grid_spec=pltpu.PrefetchScalarGridSpec(
                num_scalar_prefetch=len(combine_tables), grid=(n,),
                in_specs=[tok(D_MODEL), tok(LANES), hbm],
                out_specs=tok(D_MODEL),
                scratch_shapes=[pltpu.VMEM((2, SORT_ROWS + N_EXPERTS * SEG_ALIGN, D_MODEL), BF16),
                                pltpu.SemaphoreType.DMA((2,))]),
            out_shape=jax.ShapeDtypeStruct((n * TOK_TILE, D_MODEL), F32),
            compiler_params=pltpu.CompilerParams(
                dimension_semantics=("arbitrary",), vmem_limit_bytes=VMEM_LIMIT),
            name="combine",
        )(*combine_tables, h1, meta, y))
        tile0 += n
    return outs


def _prepare_weights(g_mix, w_in, conv_w, conv_b, dt_bias, a_log, d_skip, g_ssd_out, g_q, g_k,
                     attn_sinks, g_att_out, w_out, g_ffn, w_route_group, b_route_group,
                     w_route_expert, b_route_expert, w_gate, w_up, w_down):
    cuts = np.cumsum([0, SSD_DIM, CONV_DIM, SSD_HEADS, ATT_DIM, KV_DIM, KV_DIM])
    seg = lambda i: w_in[:, cuts[i]:cuts[i + 1]]
    pad_lanes = lambda a: jnp.pad(a, ((0, 0), (0, LANES - a.shape[1])))
    bd = np.kron(np.eye(KV_DIM // HEAD_DIM, dtype=np.float32), np.ones((HEAD_DIM, HEAD_DIM), np.float32))
    n_route = N_EXPERTS + N_EXPERT_GROUPS
    return {
        "g_mix": g_mix.reshape(1, D_MODEL),
        "w_qk": jnp.concatenate([seg(3), seg(4)], axis=1).astype(BF16),
        "w_rest": jnp.concatenate([seg(0), seg(1), seg(5), pad_lanes(jnp.tile(seg(2), (1, DT_COPIES)))],
                                  axis=1).astype(BF16),
        "g_qk": jnp.concatenate([jnp.tile(g_q, ATT_HEADS), jnp.tile(g_k, ATT_KV_HEADS)]).reshape(1, ATT_DIM + KV_DIM),
        "bd": jnp.asarray(bd, BF16),
        "conv_w": conv_w, "conv_b": conv_b.reshape(1, CONV_DIM),
        "dt_bias": pad_lanes(jnp.tile(dt_bias, DT_COPIES).reshape(1, DT_COPIES * SSD_HEADS)),
        "a_log": pad_lanes(jnp.tile(a_log, DT_COPIES).reshape(1, DT_COPIES * SSD_HEADS)),
        "d_skip": jnp.repeat(d_skip, HEAD_DIM).reshape(1, SSD_DIM),
        "g_ssd_out": g_ssd_out.reshape(1, SSD_DIM),
        "attn_sinks": attn_sinks,
        "g_att_out": g_att_out.reshape(1, ATT_DIM),
        "w_out": w_out.astype(BF16),
        "g_ffn": g_ffn.reshape(1, D_MODEL),
        "w_route": pad_lanes(jnp.concatenate([w_route_expert, w_route_group], axis=1)).astype(BF16),
        "b_route": pad_lanes(jnp.concatenate([b_route_expert, b_route_group]).reshape(1, n_route)),
        "w_gate": w_gate, "w_up": w_up, "w_down": w_down,
    }


def _segment(x3, w, mk, mv, tail, h0t, kprev, vprev, L, tb_ssd, tb_att, tm, from_cache, sb=1):
    S, Ls, _ = x3.shape
    x = x3.reshape(S * Ls, D_MODEL)
    z, xbc, q, k, v, dt = _project(x, w, tm)
    r3 = lambda a: a.reshape(S, Ls, a.shape[-1])
    xbc3, k3, v3 = r3(xbc), r3(k), r3(v)
    y_ssd, h_new = _ssd(xbc3, r3(z), r3(dt), tail, h0t, w, L, tb_ssd, sb)
    if not from_cache:
        kprev, vprev = k3, v3
    y_att = _attention(r3(q), k3, v3, kprev, vprev, mk, mv, w, L, tb_att, from_cache, sb)
    part = _merge(x, y_ssd.reshape(S * Ls, SSD_DIM), y_att.reshape(S * Ls, ATT_DIM), w, tm)
    return part, xbc3, h_new, k3, v3


def _state_to_kernel(h):
    return h.reshape(h.shape[0], SSD_DIM, D_STATE)


def _state_from_kernel(hk):
    return hk.reshape(hk.shape[0], SSD_HEADS, HEAD_DIM, D_STATE)


def kernel(x_prompt, x_sample, cache_conv, state_ssd, cache_k, cache_v, meta_tokens, g_mix, w_in, conv_w, conv_b, dt_bias, a_log, d_skip, g_ssd_out, g_q, g_k, attn_sinks, g_att_out, w_out, g_ffn, w_route_group, b_route_group, w_route_expert, b_route_expert, w_gate, w_up, w_down):
    w = _prepare_weights(g_mix[0], w_in[0], conv_w[0], conv_b[0], dt_bias[0], a_log[0], d_skip[0],
                         g_ssd_out[0], g_q[0], g_k[0], attn_sinks[0], g_att_out[0], w_out[0], g_ffn[0],
                         w_route_group[0], b_route_group[0], w_route_expert[0], b_route_expert[0],
                         w_gate[0], w_up[0], w_down[0])
    n_b = x_sample.shape[0]
    n_dec = x_sample.shape[1]

    _, m_xbc, _, mk, mv, m_dt = _project(meta_tokens, w, N_META)
    zero_tail = jnp.zeros((1, CONV_WIDTH - 1, CONV_DIM), F32)
    zero_state = jnp.zeros((1, SSD_DIM, D_STATE), F32)
    m_xbc3 = m_xbc.reshape(1, N_META, CONV_DIM)
    _, m_state = _ssd(m_xbc3, jnp.zeros((1, N_META, SSD_DIM), F32), m_dt.reshape(1, N_META, LANES),
                      zero_tail, zero_state, w, N_META, N_META)
    m_tail = m_xbc3[:, N_META - (CONV_WIDTH - 1):]

    part_p, xbc_p, st_p, k_p, v_p = _segment(
        x_prompt, w, mk, mv, m_tail, m_state, None, None,
        L=CHUNK, tb_ssd=512, tb_att=512, tm=512, from_cache=False)
    part_s, xbc_s, st_s, k_s, v_s = _segment(
        x_sample, w, mk, mv, cache_conv[0], _state_to_kernel(state_ssd[0]),
        cache_k[0].reshape(n_b, WINDOW, KV_DIM), cache_v[0].reshape(n_b, WINDOW, KV_DIM),
        L=n_dec, tb_ssd=n_dec, tb_att=n_dec, tm=512, from_cache=True, sb=8)
    yp, ys = _moe([part_p, part_s], w, tm=512)
    yp = yp.reshape(x_prompt.shape)
    ys = ys.reshape(x_sample.shape)

    heads = lambda a, rows: a.reshape(a.shape[0], rows, ATT_KV_HEADS, HEAD_DIM)[None]
    return (yp, ys,
            xbc_p[:, -(CONV_WIDTH - 1):][None],
            _state_from_kernel(st_p)[None],
            heads(k_p[:, -WINDOW:], WINDOW), heads(v_p[:, -WINDOW:], WINDOW),
            xbc_s[:, -(CONV_WIDTH - 1):][None],
            _state_from_kernel(st_s)[None],
            heads(k_s, n_dec), heads(v_s, n_dec))
```

```python
import functools
import math

import numpy as np
import jax
import jax.numpy as jnp
from jax import lax
from jax.experimental import pallas as pl
from jax.experimental.pallas import tpu as pltpu

D_MODEL = 1024
CHUNK = 64
N_META = 16
HEAD_DIM = 64
ATT_HEADS = 16
ATT_KV_HEADS = 4
ATT_REP = ATT_HEADS // ATT_KV_HEADS
ATT_DIM = ATT_HEADS * HEAD_DIM
KV_DIM = ATT_KV_HEADS * HEAD_DIM
WINDOW = 128
SSD_HEADS = 16
SSD_DIM = SSD_HEADS * HEAD_DIM
SSD_GROUPS = 2
GROUP_DIM = SSD_DIM // SSD_GROUPS
D_STATE = 128
CONV_WIDTH = 4
CONV_DIM = SSD_DIM + 2 * SSD_GROUPS * D_STATE
N_EXPERT_GROUPS = 4
EXPERTS_PER_GROUP = 8
N_EXPERTS = N_EXPERT_GROUPS * EXPERTS_PER_GROUP
EXPERT_FF = D_MODEL // 4
EPS = 1e-6

LANES = 128
SUBLANES = 8
MXU_DIM = 256
DT_COPIES = 3
KEY_SPAN = 256
VMEM_LIMIT = 56 * 1024 * 1024

F32 = jnp.float32
BF16 = jnp.bfloat16
NEG_INF = float("-inf")
LOG2E = math.log2(math.e)


def _dot(a, b):
    return jnp.dot(a, b, preferred_element_type=F32)


def _dot_nt(a, b):
    return lax.dot_general(a, b, (((1,), (1,)), ((), ())), preferred_element_type=F32)


def _dot_tn(a, b):
    return lax.dot_general(a, b, (((0,), (0,)), ((), ())), preferred_element_type=F32)


def _split3(x):
    hi = x.astype(BF16)
    r = x - hi.astype(F32)
    mid = r.astype(BF16)
    lo = (r - mid.astype(F32)).astype(BF16)
    return hi, mid, lo


def _pack3(x):
    hi, mid, lo = _split3(x)
    lane = lax.broadcasted_iota(jnp.int32, x.shape, 1)
    packed = jnp.where(lane < SSD_HEADS, hi.astype(F32),
                       jnp.where(lane < 2 * SSD_HEADS, mid.astype(F32), lo.astype(F32)))
    return packed.astype(BF16)


def _sel_dot(sel, x):
    hi, mid, lo = _split3(x)
    return _dot(sel, hi) + _dot(sel, mid) + _dot(sel, lo)


def _silu(x):
    return x * (1.0 / (1.0 + jnp.exp2(x * (-LOG2E))))


def _rms(x, g):
    ms = jnp.mean(x * x, axis=-1, keepdims=True)
    return x * lax.rsqrt(ms + EPS) * g


def _const_spec(shape):
    n = len(shape)
    return pl.BlockSpec(shape, lambda *_: (0,) * n, pipeline_mode=pl.Buffered(1))


def _proj_kernel(x_ref, gmix_ref, wqk_ref, wrest_ref, gqk_ref, bd_ref,
                 z_ref, xbc_ref, q_ref, k_ref, v_ref, dt_ref):
    tm = x_ref.shape[0]
    xn = _rms(x_ref[...], gmix_ref[...]).astype(BF16)
    qk = _dot(xn, wqk_ref[...])
    n_slices = (ATT_DIM + KV_DIM) // KV_DIM
    sq = jnp.concatenate([qk[:, j * KV_DIM:(j + 1) * KV_DIM] for j in range(n_slices)], axis=0)
    ms = _dot((sq * sq).astype(BF16), bd_ref[...]) * (1.0 / HEAD_DIM)
    inv = lax.rsqrt(ms + EPS)
    for j in range(n_slices):
        sl = slice(j * KV_DIM, (j + 1) * KV_DIM)
        normed = qk[:, sl] * inv[j * tm:(j + 1) * tm, :] * gqk_ref[:, sl]
        if j < ATT_DIM // KV_DIM:
            q_ref[:, sl] = (normed * (HEAD_DIM ** -0.5 * LOG2E)).astype(BF16)
        else:
            k_ref[...] = normed
    rest = _dot(xn, wrest_ref[...])
    cuts = np.cumsum([0, SSD_DIM, CONV_DIM, KV_DIM, LANES])
    for ref, lo_c, hi_c in zip((z_ref, xbc_ref, v_ref, dt_ref), cuts[:-1], cuts[1:]):
        ref[...] = rest[:, lo_c:hi_c]


def _project(x, w, tm):
    t = x.shape[0]
    assert t % tm == 0
    row = lambda n: pl.BlockSpec((tm, n), lambda i: (i, 0))
    ins = [x, w["g_mix"], w["w_qk"], w["w_rest"], w["g_qk"], w["bd"]]
    in_specs = [row(D_MODEL)] + [_const_spec(a.shape) for a in ins[1:]]
    out_dims = (SSD_DIM, CONV_DIM, ATT_DIM, KV_DIM, KV_DIM, LANES)
    out_dtypes = (F32, F32, BF16, F32, F32, F32)
    return pl.pallas_call(
        _proj_kernel,
        grid=(t // tm,),
        in_specs=in_specs,
        out_specs=[row(n) for n in out_dims],
        out_shape=[jax.ShapeDtypeStruct((t, n), d) for n, d in zip(out_dims, out_dtypes)],
        compiler_params=pltpu.CompilerParams(
            dimension_semantics=("arbitrary",), vmem_limit_bytes=VMEM_LIMIT),
        name="projection",
    )(*ins)


N_SSD_STREAM_INPUTS = 5
N_SSD_CONSTS = 9


def _ssd_kernel(*refs, L, TB, SB):
    streamed = lambda k: k < N_SSD_STREAM_INPUTS or k >= N_SSD_STREAM_INPUTS + N_SSD_CONSTS
    for s in range(SB):
        _ssd_stream(*[r.at[s] if streamed(k) else r for k, r in enumerate(refs)], L=L, TB=TB)


def _ssd_stream(xbc_ref, z_ref, dt_ref, tail_ref, h0_ref, convw_ref, convb_ref, dtb_ref,
                alog_ref, dskip_ref, gout_ref, ehead_ref, epos_ref, tri_ref,
                y_ref, hout_ref, buf_ref, act_ref, dts_ref, ys_ref, st_ref, *, L, TB):
    hp = LANES // L
    n_lane_tiles = SSD_HEADS // hp
    hq = MXU_DIM // L
    n_tiles = SSD_HEADS // hq
    tile_w = hq * HEAD_DIM
    b = pl.program_id(1)
    pad = SUBLANES - (CONV_WIDTH - 1)

    @pl.when(b == 0)
    def _():
        buf_ref[0:SUBLANES, :] = jnp.zeros((SUBLANES, CONV_DIM), F32)
        buf_ref[pad:SUBLANES, :] = tail_ref[...]
        st_ref[...] = h0_ref[...].T

    buf_ref[SUBLANES:SUBLANES + TB, :] = xbc_ref[...]
    rows = buf_ref[...]
    acc = convb_ref[...] + convw_ref[CONV_WIDTH - 1:CONV_WIDTH, :] * rows[SUBLANES:, :]
    for j in range(CONV_WIDTH - 1):
        shifted = pltpu.roll(rows, CONV_WIDTH - 1 - j, axis=0)[SUBLANES:, :]
        acc = acc + convw_ref[j:j + 1, :] * shifted
    act_ref[...] = _silu(acc)
    buf_ref[pad:SUBLANES, :] = buf_ref[TB + pad:TB + SUBLANES, :]

    dtx = dt_ref[...] + dtb_ref[...]
    dts_ref[...] = jnp.maximum(dtx, 0.0) + jnp.log(1.0 + jnp.exp(-jnp.abs(dtx)))

    a_row = -jnp.exp(alog_ref[...]) * LOG2E
    ehead = ehead_ref[...]
    epos = epos_ref[...]
    tri = tri_ref[...]
    dskip = dskip_ref[...]
    row_i = lax.broadcasted_iota(jnp.int32, (L, n_lane_tiles * LANES), 0)
    col_i = lax.broadcasted_iota(jnp.int32, (L, n_lane_tiles * LANES), 1)
    pos_i = col_i % L
    diag_mask = row_i == pos_i
    causal_mask = row_i >= pos_i
    colhead = lax.broadcasted_iota(jnp.int32, (L, tile_w), 1) // HEAD_DIM

    def chunk(c):
        r0 = pl.multiple_of(c * L, L)
        dtc = dts_ref[pl.ds(r0, L), :]
        cs = _sel_dot(tri, dtc * a_row)
        cs16 = _pack3(cs)
        dt_b = _dot(_pack3(dtc), ehead)
        cs_b = _dot(cs16, ehead)
        cs_last_b = cs_b[L - 1:L, :]
        ecs_b = jnp.exp2(cs_b)
        dec_end_b = jnp.exp2(cs_last_b - cs_b)
        bdec_b = jnp.exp2(cs_last_b)
        cs_col = cs_b if L == HEAD_DIM else _dot(cs16, epos)
        cs_row = jnp.sum(jnp.where(diag_mask, cs_col, 0.0), axis=0, keepdims=True)
        lmat = jnp.exp2(jnp.where(causal_mask, cs_col - cs_row, NEG_INF))

        xs = act_ref[pl.ds(r0, L), 0:SSD_DIM]
        xdt_f = xs * dt_b
        xdt = xdt_f.astype(BF16)
        wx = (xdt_f * dec_end_b).astype(BF16)
        st = st_ref[...]
        st16 = st.astype(BF16)

        y_diag = []
        y_off = []
        s_new = []
        cb = [None] * SSD_GROUPS
        for g in range(SSD_GROUPS):
            bg = act_ref[pl.ds(r0, L), SSD_DIM + g * D_STATE:SSD_DIM + (g + 1) * D_STATE].astype(BF16)
            cg = act_ref[pl.ds(r0, L), SSD_DIM + (SSD_GROUPS + g) * D_STATE:
                         SSD_DIM + (SSD_GROUPS + g + 1) * D_STATE].astype(BF16)
            cb[g] = _dot_nt(cg, jnp.concatenate([bg] * hp, axis=0))
            gsl = slice(g * GROUP_DIM, (g + 1) * GROUP_DIM)
            y_off.append(_dot(cg, st16[:, gsl]))
            s_new.append(_dot_tn(bg, wx[:, gsl]))
        lane_tiles = MXU_DIM // LANES
        for t in range(n_tiles):
            cbt = jnp.concatenate(
                [cb[((t * lane_tiles + j) * hp * HEAD_DIM) // GROUP_DIM] for j in range(lane_tiles)], axis=1)
            gmat = (lmat[:, t * MXU_DIM:(t + 1) * MXU_DIM] * cbt).astype(BF16)
            xt = xdt[:, t * tile_w:(t + 1) * tile_w]
            rhs = jnp.concatenate(
                [jnp.where(colhead == hh, xt, jnp.zeros_like(xt)) for hh in range(hq)], axis=0)
            y_diag.append(_dot(gmat, rhs))
        y = (jnp.concatenate(y_diag, axis=1) + jnp.concatenate(y_off, axis=1) * ecs_b
             + dskip * xs)
        ys_ref[pl.ds(r0, L), :] = y
        st_ref[...] = bdec_b * st + jnp.concatenate(s_new, axis=1)

    n_chunks = TB // L
    unroll = n_chunks if n_chunks in (4, 8) else 1

    def chunks(i, carry):
        for u in range(unroll):
            chunk(i * unroll + u)
        return carry

    lax.fori_loop(0, n_chunks // unroll, chunks, 0)

    yg = ys_ref[...] * _silu(z_ref[...])
    y_ref[...] = _rms(yg, gout_ref[...])

    @pl.when(b == pl.num_programs(1) - 1)
    def _():
        hout_ref[...] = st_ref[...].T


def _ssd(xbc, z, dt, tail, h0t, w, L, TB, SB=1):
    S, Ls, _ = xbc.shape
    assert Ls % TB == 0 and TB % L == 0 and LANES % L == 0 and S % SB == 0
    nb = Ls // TB
    assert SB == 1 or nb == 1
    hp = LANES // L
    n_tiles = SSD_HEADS // hp
    per_stream = lambda a: (lambda s, b: (s, 0, 0)) if a.shape[0] == S and S > 1 else (lambda s, b: (0, 0, 0))
    seq = lambda n: pl.BlockSpec((SB, TB, n), lambda s, b: (s, b, 0))
    ehead = np.zeros((LANES, SSD_DIM), np.float32)
    epos = np.zeros((LANES, n_tiles * LANES), np.float32)
    for part in range(DT_COPIES):
        for h in range(SSD_HEADS):
            ehead[part * SSD_HEADS + h, h * HEAD_DIM:(h + 1) * HEAD_DIM] = 1.0
            epos[part * SSD_HEADS + h, h * L:(h + 1) * L] = 1.0
    tri = np.tril(np.ones((L, L), np.float32))
    consts = [w["conv_w"], w["conv_b"], w["dt_bias"], w["a_log"], w["d_skip"], w["g_ssd_out"],
              jnp.asarray(ehead, BF16), jnp.asarray(epos, BF16), jnp.asarray(tri, BF16)]
    in_specs = [seq(CONV_DIM), seq(SSD_DIM), seq(LANES),
                pl.BlockSpec((SB, CONV_WIDTH - 1, CONV_DIM), per_stream(tail)),
                pl.BlockSpec((SB, SSD_DIM, D_STATE), per_stream(h0t))]
    assert len(in_specs) == N_SSD_STREAM_INPUTS and len(consts) == N_SSD_CONSTS
    in_specs += [_const_spec(c.shape) for c in consts]
    return pl.pallas_call(
        functools.partial(_ssd_kernel, L=L, TB=TB, SB=SB),
        grid=(S // SB, nb),
        in_specs=in_specs,
        out_specs=[seq(SSD_DIM), pl.BlockSpec((SB, SSD_DIM, D_STATE), lambda s, b: (s, 0, 0))],
        out_shape=[jax.ShapeDtypeStruct((S, Ls, SSD_DIM), F32),
                   jax.ShapeDtypeStruct((S, SSD_DIM, D_STATE), F32)],
        scratch_shapes=[pltpu.VMEM((SB, SUBLANES + TB, CONV_DIM), F32),
                        pltpu.VMEM((SB, TB, CONV_DIM), F32),
                        pltpu.VMEM((SB, TB, LANES), F32),
                        pltpu.VMEM((SB, TB, SSD_DIM), F32),
                        pltpu.VMEM((SB, D_STATE, SSD_DIM), F32)],
        compiler_params=pltpu.CompilerParams(
            dimension_semantics=("arbitrary", "arbitrary"), vmem_limit_bytes=VMEM_LIMIT),
        name="ssd",
    )(xbc, z, dt, tail, h0t, *consts)


def _split_heads(x):
    lane = lax.broadcasted_iota(jnp.int32, (x.shape[0], LANES), 1)
    low = lane < HEAD_DIM
    lo, hi = [], []
    for t in range(KV_DIM // LANES):
        tile = x[:, t * LANES:(t + 1) * LANES]
        swapped = pltpu.roll(tile, HEAD_DIM, axis=1)
        lo += [jnp.where(low, tile, 0.0), jnp.where(low, swapped, 0.0)]
        hi += [jnp.where(low, 0.0, swapped), jnp.where(low, 0.0, tile)]
    return jnp.concatenate(lo, axis=1).astype(BF16), jnp.concatenate(hi, axis=1).astype(BF16)


def _attn_kernel(*refs, L, TB, SB, from_cache):
    for s in range(SB):
        _attn_stream(*[r if k == 0 or 6 <= k <= 9 else r.at[s] for k, r in enumerate(refs)],
                     L=L, TB=TB, from_cache=from_cache)


def _attn_stream(sink_ref, q_ref, k_ref, v_ref, kp_ref, vp_ref, mk_ref, mv_ref, bias_ref, g_ref,
                 o_ref, k_lo, k_hi, v_lo, v_hi, ybuf, s_buf, e_buf, sink_buf, *, L, TB, from_cache):
    b = pl.program_id(1)
    bufs = (k_lo, k_hi, v_lo, v_hi)
    if from_cache:
        window = _split_heads(kp_ref[...]) + _split_heads(vp_ref[...])
        for buf, val in zip(bufs, window):
            buf[0:WINDOW, :] = val
    else:
        @pl.when(b == 0)
        def _():
            for buf in bufs:
                buf[0:WINDOW, :] = jnp.zeros((WINDOW, buf.shape[1]), BF16)

        @pl.when(b > 0)
        def _():
            for buf in bufs:
                buf[0:WINDOW, :] = buf[TB:TB + WINDOW, :]

    for buf, val in zip(bufs, _split_heads(k_ref[...]) + _split_heads(v_ref[...])):
        buf[WINDOW:WINDOW + TB, :] = val
    n_pad = KEY_SPAN - WINDOW - L - N_META
    pad_rows = jnp.zeros((n_pad, 2 * KV_DIM), BF16)
    tails = [jnp.concatenate([t, pad_rows], axis=0)
             for t in _split_heads(mk_ref[...]) + _split_heads(mv_ref[...])]
    lane = lax.broadcasted_iota(jnp.int32, (2 * L, 2 * KEY_SPAN), 1) % KEY_SPAN
    lane_h = lax.broadcasted_iota(jnp.int32, (L, LANES), 1)
    lane_v = lax.broadcasted_iota(jnp.int32, (KEY_SPAN, LANES), 1)
    ones_lo = jnp.where(lane_v < HEAD_DIM, 1.0, 0.0).astype(BF16)
    ones_hi = jnp.where(lane_v < HEAD_DIM, 0.0, 1.0).astype(BF16)
    n_chunks = TB // L
    pairs_per_group = ATT_REP // 2

    def chunk_keys(c):
        r0 = c * L
        return [jnp.concatenate([buf[r0:r0 + WINDOW, :], buf[r0 + WINDOW:r0 + WINDOW + L, :], tail], axis=0)
                for buf, tail in zip(bufs, tails)]

    for c in range(n_chunks):
        r0 = c * L
        keys = chunk_keys(c)
        for g in range(ATT_KV_HEADS):
            gl = slice(g * LANES, (g + 1) * LANES)
            q4 = jnp.concatenate([q_ref[r0:r0 + L, p * LANES:(p + 1) * LANES]
                                  for p in range(pairs_per_group * g, pairs_per_group * (g + 1))], axis=0)
            s = _dot_nt(q4, jnp.concatenate([keys[0][:, gl], keys[1][:, gl]], axis=0)) + bias_ref[g]
            if not from_cache and r0 < WINDOW:
                n_invalid = WINDOW - (b * TB + r0)
                s = jnp.where(lane < n_invalid, NEG_INF, s)
            s_buf[c * ATT_KV_HEADS + g] = s

    for c in range(n_chunks):
        for pair in range(ATT_HEADS // 2):
            slot = c * ATT_KV_HEADS + pair // pairs_per_group
            rows = slice((pair % pairs_per_group) * L, (pair % pairs_per_group + 1) * L)
            sink_terms = []
            for half in range(2):
                cols = slice(half * KEY_SPAN, (half + 1) * KEY_SPAN)
                s = s_buf[slot, rows, cols]
                sink = sink_ref[2 * pair + half] * LOG2E
                m = jnp.maximum(jnp.max(s, axis=-1, keepdims=True), sink)
                e_buf[slot, rows, cols] = jnp.exp2(s - m).astype(BF16)
                sink_terms.append(jnp.exp2(sink - m))
            sink_buf[c * (ATT_HEADS // 2) + pair] = jnp.where(lane_h < HEAD_DIM, sink_terms[0], sink_terms[1])

    for c in range(n_chunks):
        r0 = c * L
        keys = chunk_keys(c)
        for g in range(ATT_KV_HEADS):
            gl = slice(g * LANES, (g + 1) * LANES)
            vcat = jnp.concatenate([jnp.concatenate([keys[2][:, gl], ones_lo], axis=1),
                                    jnp.concatenate([keys[3][:, gl], ones_hi], axis=1)], axis=0)
            ov = _dot(e_buf[c * ATT_KV_HEADS + g], vcat)
            for j in range(pairs_per_group):
                pair = pairs_per_group * g + j
                den = ov[j * L:(j + 1) * L, LANES:2 * LANES] + sink_buf[c * (ATT_HEADS // 2) + pair]
                ybuf[r0:r0 + L, pair * LANES:(pair + 1) * LANES] = ov[j * L:(j + 1) * L, 0:LANES] * (1.0 / den)
    o_ref[...] = _rms(ybuf[...], g_ref[...])


def _attention(q, k, v, kprev, vprev, mk, mv, w, L, TB, from_cache, SB=1):
    S, Ls, _ = q.shape
    assert Ls % TB == 0 and TB % L == 0 and (from_cache or TB >= WINDOW) and S % SB == 0
    nb = Ls // TB
    assert SB == 1 or (from_cache and nb == 1)
    seq = lambda n: pl.BlockSpec((SB, TB, n), lambda s, b: (s, b, 0))
    if not from_cache:
        kprev = vprev = jnp.zeros((1, WINDOW, KV_DIM), F32)
    prev = pl.BlockSpec((SB, WINDOW, KV_DIM),
                        (lambda s, b: (s, 0, 0)) if from_cache else (lambda s, b: (0, 0, 0)))
    slopes = 2.0 ** (-8.0 * np.arange(1, ATT_HEADS + 1, dtype=np.float64) / ATT_HEADS)
    dist = np.abs(WINDOW + np.arange(L)[:, None] - np.arange(WINDOW + L)[None, :])
    bias = np.full((ATT_HEADS, L, KEY_SPAN), NEG_INF, np.float32)
    bias[:, :, :WINDOW + L] = -slopes[:, None, None] * dist[None] * LOG2E
    bias[:, :, WINDOW + L:WINDOW + L + N_META] = 0.0
    bias = bias.reshape(ATT_KV_HEADS, ATT_REP // 2, 2, L, KEY_SPAN).transpose(0, 1, 3, 2, 4)
    bias = jnp.asarray(bias.reshape(ATT_KV_HEADS, ATT_REP // 2 * L, 2 * KEY_SPAN))
    n_chunks = TB // L
    in_specs = [pl.BlockSpec(memory_space=pltpu.SMEM),
                seq(ATT_DIM), seq(KV_DIM), seq(KV_DIM), prev, prev,
                _const_spec(mk.shape), _const_spec(mv.shape), _const_spec(bias.shape),
                _const_spec(w["g_att_out"].shape)]
    return pl.pallas_call(
        functools.partial(_attn_kernel, L=L, TB=TB, SB=SB, from_cache=from_cache),
        grid=(S // SB, nb),
        in_specs=in_specs,
        out_specs=seq(ATT_DIM),
        out_shape=jax.ShapeDtypeStruct((S, Ls, ATT_DIM), F32),
        scratch_shapes=[pltpu.VMEM((SB, WINDOW + TB, 2 * KV_DIM), BF16) for _ in range(4)]
                       + [pltpu.VMEM((SB, TB, ATT_DIM), F32),
                          pltpu.VMEM((SB, n_chunks * ATT_KV_HEADS, 2 * L, 2 * KEY_SPAN), F32),
                          pltpu.VMEM((SB, n_chunks * ATT_KV_HEADS, 2 * L, 2 * KEY_SPAN), BF16),
                          pltpu.VMEM((SB, n_chunks * ATT_HEADS // 2, L, LANES), F32)],
        compiler_params=pltpu.CompilerParams(
            dimension_semantics=("arbitrary", "arbitrary"), vmem_limit_bytes=VMEM_LIMIT),
        name="attention",
    )(w["attn_sinks"], q, k, v, kprev, vprev, mk, mv, bias, w["g_att_out"])


TOK_TILE = 512
SEG_ALIGN = 2 * SUBLANES
SORT_ROWS = 2 * TOK_TILE + N_EXPERTS * SEG_ALIGN
XS_COLS = D_MODEL + LANES
META_P1, META_P2, META_G1, META_G2, META_E1 = (N_EXPERTS + j for j in range(5))
PAY_G1, PAY_G2, PAY_E1 = 0, 3, 6


def _merge_kernel(h_ref, ys_ref, ya_ref, wo_ref, gffn_ref, wr_ref, br_ref, tri_ref, upper_ref,
                  h1_ref, xn_ref, meta_ref, cnt_ref, logit_buf):
    i = pl.program_id(0)

    @pl.when(i == 0)
    def _():
        logit_buf[...] = jnp.zeros(logit_buf.shape, F32)

    ycat = jnp.concatenate([ys_ref[...].astype(BF16), ya_ref[...].astype(BF16)], axis=1)
    h1 = h_ref[...] + _dot(ycat, wo_ref[...])
    h1_ref[...] = h1
    xn = _rms(h1, gffn_ref[...]).astype(BF16)
    xn_ref[...] = xn
    all_logits = logit_buf[(i + 1) % 2]
    logit_buf[i % 2] = _dot(xn, wr_ref[...]) + br_ref[...]
    tri = tri_ref[...]
    upper = upper_ref[...]
    lane = lax.broadcasted_iota(jnp.int32, (TOK_TILE, LANES), 1)
    big = jnp.int32(LANES)
    gmask = (lane >= N_EXPERTS) & (lane < N_EXPERTS + N_EXPERT_GROUPS)
    for s in range(all_logits.shape[0] // TOK_TILE):
        rows = slice(s * TOK_TILE, (s + 1) * TOK_TILE)
        logits = all_logits[rows]

        def top1(mask):
            mval = jnp.max(jnp.where(mask, logits, NEG_INF), axis=-1, keepdims=True)
            idx = jnp.min(jnp.where(mask & (logits == mval), lane, big), axis=-1, keepdims=True)
            return mval, idx

        gmax, gidx = top1(gmask)
        gate_g = 1.0 / jnp.sum(jnp.where(gmask, jnp.exp(logits - gmax), 0.0), axis=-1, keepdims=True)
        grp = gidx - N_EXPERTS
        emask = (lane // EXPERTS_PER_GROUP) == grp
        v1, i1 = top1(emask)
        v2, i2 = top1(emask & (lane != i1))
        e2 = jnp.exp(v2 - v1)
        g1 = gate_g / (1.0 + e2)
        g2 = gate_g * e2 / (1.0 + e2)
        oh1 = jnp.where(lane == i1, 1.0, 0.0)
        oh2 = jnp.where(lane == i2, 1.0, 0.0)
        oh = oh1 + oh2
        earlier = _dot(tri, oh.astype(BF16))
        cnt = jnp.sum(oh, axis=0, keepdims=True)
        units = jnp.floor((cnt + (SEG_ALIGN - 1)) * (1.0 / SEG_ALIGN))
        units = jnp.broadcast_to(units, (2 * SUBLANES, LANES)).astype(BF16)
        slot = _dot(units, upper)[0:1, :] * SEG_ALIGN + earlier
        p1 = jnp.sum(oh1 * slot, axis=-1, keepdims=True)
        p2 = jnp.sum(oh2 * slot, axis=-1, keepdims=True)
        meta = jnp.where(lane == META_P1, p1, 0.0)
        meta = jnp.where(lane == META_P2, p2, meta)
        meta = jnp.where(lane == META_G1, g1, meta)
        meta = jnp.where(lane == META_G2, g2, meta)
        meta = jnp.where(lane == META_E1, i1.astype(F32), meta)
        meta_ref[rows, :] = meta
        cnt_ref[s] = jnp.broadcast_to(cnt, (SUBLANES, LANES))


def _merge(h, y_ssd, y_att, w, tm):
    t = h.shape[0]
    assert t % tm == 0 and tm % TOK_TILE == 0
    sub = tm // TOK_TILE
    last = t // tm - 1
    row = lambda n: pl.BlockSpec((tm, n), lambda i: (jnp.minimum(i, last), 0))
    routed = lambda i: jnp.maximum(i - 1, 0)
    tri = jnp.asarray(np.tril(np.ones((TOK_TILE, TOK_TILE), np.float32), -1), BF16)
    upper = jnp.asarray(np.triu(np.ones((LANES, LANES), np.float32), 1), BF16)
    consts = [w["w_out"], w["g_ffn"], w["w_route"], w["b_route"], tri, upper]
    return pl.pallas_call(
        _merge_kernel,
        grid=(t // tm + 1,),
        in_specs=[row(D_MODEL), row(SSD_DIM), row(ATT_DIM)] + [_const_spec(c.shape) for c in consts],
        out_specs=[row(D_MODEL), row(D_MODEL),
                   pl.BlockSpec((tm, LANES), lambda i: (routed(i), 0)),
                   pl.BlockSpec((sub, SUBLANES, LANES), lambda i: (routed(i), 0, 0))],
        scratch_shapes=[pltpu.VMEM((2, tm, LANES), F32)],
        out_shape=[jax.ShapeDtypeStruct((t, D_MODEL), F32),
                   jax.ShapeDtypeStruct((t, D_MODEL), BF16),
                   jax.ShapeDtypeStruct((t, LANES), F32),
                   jax.ShapeDtypeStruct((t // TOK_TILE, SUBLANES, LANES), F32)],
        compiler_params=pltpu.CompilerParams(
            dimension_semantics=("arbitrary",), vmem_limit_bytes=VMEM_LIMIT),
        name="merge_route",
    )(h, y_ssd, y_att, *consts)


def _meta_col(meta, j):
    lane = lax.broadcasted_iota(jnp.int32, meta.shape, 1)
    return jnp.sum(jnp.where(lane == j, meta, 0.0), axis=-1, keepdims=True)


def _pair_selector(meta):
    rows = lax.broadcasted_iota(jnp.int32, (meta.shape[0], SORT_ROWS), 1)
    p1 = _meta_col(meta, META_P1).astype(jnp.int32)
    p2 = _meta_col(meta, META_P2).astype(jnp.int32)
    return jnp.where(rows == p1, 1.0, jnp.where(rows == p2, 1.0, 0.0)).astype(BF16)


def _segment_dmas(off_ref, cnt_ref, row_ref, b, tile_buf, hbm, sem, to_hbm, wait):
    for e in range(N_EXPERTS):
        k = b * N_EXPERTS + e
        n = pl.multiple_of(cnt_ref[k], SEG_ALIGN)
        v = tile_buf.at[pl.ds(pl.multiple_of(off_ref[k], SEG_ALIGN), n)]
        h = hbm.at[pl.ds(pl.multiple_of(row_ref[k], SEG_ALIGN), n)]
        cp = pltpu.make_async_copy(v, h, sem) if to_hbm else pltpu.make_async_copy(h, v, sem)
        if wait:
            cp.wait()
        else:
            cp.start()


def _token_payload(meta):
    cols = {PAY_E1: _meta_col(meta, META_E1)}
    for base, src in ((PAY_G1, META_G1), (PAY_G2, META_G2)):
        for k, part in enumerate(_split3(_meta_col(meta, src))):
            cols[base + k] = part.astype(F32)
    lane = lax.broadcasted_iota(jnp.int32, (meta.shape[0], LANES), 1)
    out = jnp.zeros((meta.shape[0], LANES), F32)
    for k, c in cols.items():
        out = jnp.where(lane == k, c, out)
    return out.astype(BF16)


def _dispatch_kernel(off_ref, cnt_ref, row_ref, tail_row_ref, tail_cnt_ref, xn_ref, meta_ref, *rest,
                     tile0, first):
    xs_ref, sort_buf, zero_buf, sem, tail_sem = rest if first else rest[1:]
    b = pl.program_id(0)
    nb = pl.num_programs(0)
    slot = b % 2
    seg = functools.partial(_segment_dmas, off_ref, cnt_ref, row_ref, hbm=xs_ref, to_hbm=True)

    def tails(wait):
        def body(e, carry):
            n = pl.multiple_of(tail_cnt_ref[e], SEG_ALIGN)

            @pl.when(n > 0)
            def _():
                cp = pltpu.make_async_copy(
                    zero_buf.at[pl.ds(0, n)],
                    xs_ref.at[pl.ds(pl.multiple_of(tail_row_ref[e], SEG_ALIGN), n)], tail_sem.at[0])
                if wait:
                    cp.wait()
                else:
                    cp.start()
            return carry

        lax.fori_loop(0, N_EXPERTS, body, 0)

    @pl.when(b >= 2)
    def _():
        seg(tile0 + b - 2, sort_buf.at[slot], sem=sem.at[slot], wait=True)

    if first:
        @pl.when(b == 0)
        def _():
            zero_buf[...] = jnp.zeros(zero_buf.shape, BF16)
            tails(wait=False)

    meta = meta_ref[...]
    payload = jnp.concatenate([xn_ref[...], _token_payload(meta)], axis=1)
    sort_buf[slot] = _dot_tn(_pair_selector(meta), payload).astype(BF16)
    seg(tile0 + b, sort_buf.at[slot], sem=sem.at[slot], wait=False)

    @pl.when(b == nb - 1)
    def _():
        @pl.when(b >= 1)
        def _():
            seg(tile0 + b - 1, sort_buf.at[1 - slot], sem=sem.at[1 - slot], wait=True)

        seg(tile0 + b, sort_buf.at[slot], sem=sem.at[slot], wait=True)
        if first:
            tails(wait=True)


def _expert_kernel(tile_expert_ref, n_active_ref, xs_ref, wg_ref, wu_ref, wd_ref, y_ref):
    @pl.when(pl.program_id(0) < n_active_ref[0])
    def _():
        x = xs_ref[:, 0:D_MODEL]
        pay = xs_ref[:, D_MODEL:XS_COLS].astype(F32)
        lane = lax.broadcasted_iota(jnp.int32, pay.shape, 1)
        pick = lambda lo, n: jnp.sum(jnp.where((lane >= lo) & (lane < lo + n), pay, 0.0), axis=-1, keepdims=True)
        expert = tile_expert_ref[pl.program_id(0)].astype(F32)
        gate = jnp.where(pick(PAY_E1, 1) == expert, pick(PAY_G1, 3), pick(PAY_G2, 3))
        gu = _dot(x, jnp.concatenate([wg_ref[...].astype(BF16), wu_ref[...].astype(BF16)], axis=1))
        hid = _silu(gu[:, 0:EXPERT_FF]) * gu[:, EXPERT_FF:2 * EXPERT_FF]
        y_ref[...] = (gate * _dot(hid.astype(BF16), wd_ref[...].astype(BF16))).astype(BF16)


def _combine_kernel(off_ref, cnt_ref, row_ref, h1_ref, meta_ref, y_ref, o_ref, y_buf, sem, *, tile0):
    b = pl.program_id(0)
    nb = pl.num_programs(0)
    slot = b % 2
    seg = functools.partial(_segment_dmas, off_ref, cnt_ref, row_ref, hbm=y_ref, to_hbm=False)

    @pl.when(b == 0)
    def _():
        y_buf[...] = jnp.zeros(y_buf.shape, BF16)
        seg(tile0 + b, y_buf.at[slot], sem=sem.at[slot], wait=False)

    @pl.when(b + 1 < nb)
    def _():
        seg(tile0 + b + 1, y_buf.at[1 - slot], sem=sem.at[1 - slot], wait=False)

    seg(tile0 + b, y_buf.at[slot], sem=sem.at[slot], wait=True)
    o_ref[...] = h1_ref[...] + _dot(_pair_selector(meta_ref[...]), y_buf[slot, 0:SORT_ROWS, :])


def _moe(parts, w, tm):
    i32 = jnp.int32
    n_tiles = [p[0].shape[0] // TOK_TILE for p in parts]
    nb = sum(n_tiles)
    t = nb * TOK_TILE
    cnt = jnp.concatenate([p[3][:, 0, :N_EXPERTS] for p in parts], axis=0).astype(i32)
    cnt_al = (cnt + (SEG_ALIGN - 1)) // SEG_ALIGN * SEG_ALIGN
    off = jnp.cumsum(cnt_al, axis=1) - cnt_al
    tot = jnp.sum(cnt_al, axis=0)
    tot_tm = (tot + (tm - 1)) // tm * tm
    start = jnp.cumsum(tot_tm) - tot_tm
    row = start[None, :] + jnp.cumsum(cnt_al, axis=0) - cnt_al
    tile_ends = jnp.cumsum(tot_tm // tm)
    max_rows = 2 * t + (SEG_ALIGN - 1) * min(N_EXPERTS * nb, 2 * t) + N_EXPERTS * (tm - SEG_ALIGN)
    nt = -(-max_rows // tm)
    tile_expert = jnp.sum((jnp.arange(nt, dtype=i32)[:, None] >= tile_ends[None, :]).astype(i32), axis=1)
    tile_expert = jnp.minimum(tile_expert, N_EXPERTS - 1)
    n_active = tile_ends[-1:].astype(i32)
    empty = cnt_al == 0
    flat = lambda a: a.reshape(-1).astype(i32)
    n_rows = flat(jnp.maximum(cnt_al, SEG_ALIGN))
    e_idx = jnp.arange(N_EXPERTS, dtype=i32)[None, :]
    spare = nt * tm + ((jnp.arange(nb, dtype=i32)[:, None] % 2) * N_EXPERTS + e_idx) * SEG_ALIGN
    dispatch_tables = [flat(off), n_rows, flat(jnp.where(empty, spare, row))]
    combine_tables = [flat(jnp.where(empty, SORT_ROWS + e_idx * SEG_ALIGN, off)), n_rows,
                      flat(jnp.where(empty, 0, row))]
    n_spare = 2 * N_EXPERTS * SEG_ALIGN
    tail_tables = [(start + tot).astype(i32), (tot_tm - tot).astype(i32)]

    tok = lambda n: pl.BlockSpec((TOK_TILE, n), lambda i, *_: (i, 0))
    hbm = pl.BlockSpec(memory_space=pl.ANY)
    n_prefetch = len(dispatch_tables) + len(tail_tables)
    xs = None
    tile0 = 0
    for (_, xn, meta, _), n in zip(parts, n_tiles):
        first = xs is None
        xs = pl.pallas_call(
            functools.partial(_dispatch_kernel, tile0=tile0, first=first),
            grid_spec=pltpu.PrefetchScalarGridSpec(
                num_scalar_prefetch=n_prefetch, grid=(n,),
                in_specs=[tok(D_MODEL), tok(LANES)] + ([] if first else [hbm]),
                out_specs=hbm,
                scratch_shapes=[pltpu.VMEM((2, SORT_ROWS, XS_COLS), BF16),
                                pltpu.VMEM((tm, XS_COLS), BF16),
                                pltpu.SemaphoreType.DMA((2,)),
                                pltpu.SemaphoreType.DMA((1,))]),
            out_shape=jax.ShapeDtypeStruct((nt * tm + n_spare, XS_COLS), BF16),
            input_output_aliases={} if first else {n_prefetch + 2: 0},
            compiler_params=pltpu.CompilerParams(
                dimension_semantics=("arbitrary",), vmem_limit_bytes=VMEM_LIMIT),
            name="dispatch",
        )(*dispatch_tables, *tail_tables, xn, meta, *([] if first else [xs]))
        tile0 += n

    act = lambda i, te, na: jnp.minimum(i, na[0] - 1)
    y = pl.pallas_call(
        _expert_kernel,
        grid_spec=pltpu.PrefetchScalarGridSpec(
            num_scalar_prefetch=2, grid=(nt,),
            in_specs=[pl.BlockSpec((tm, XS_COLS), lambda i, te, na: (act(i, te, na), 0)),
                      pl.BlockSpec((None, D_MODEL, EXPERT_FF), lambda i, te, na: (te[act(i, te, na)], 0, 0)),
                      pl.BlockSpec((None, D_MODEL, EXPERT_FF), lambda i, te, na: (te[act(i, te, na)], 0, 0)),
                      pl.BlockSpec((None, EXPERT_FF, D_MODEL), lambda i, te, na: (te[act(i, te, na)], 0, 0))],
            out_specs=pl.BlockSpec((tm, D_MODEL), lambda i, te, na: (act(i, te, na), 0))),
        out_shape=jax.ShapeDtypeStruct((nt * tm, D_MODEL), BF16),
        compiler_params=pltpu.CompilerParams(
            dimension_semantics=("arbitrary",), vmem_limit_bytes=VMEM_LIMIT),
        name="experts",
    )(tile_expert, n_active, xs, w["w_gate"], w["w_up"], w["w_down"])

    outs = []
    tile0 = 0
    for (h1, _, meta, _), n in zip(parts, n_tiles):
        outs.append(pl.pallas_call(
            functools.partial(_combine_kernel, tile0=tile0),
            grid_spec=pltpu.PrefetchScalarGridSpec(
                num_scalar_prefetch=len(combine_tables), grid=(n,),
                in_specs=[tok(D_MODEL), tok(LANES), hbm],
                out_specs=tok(D_MODEL),
                scratch_shapes=[pltpu.VMEM((2, SORT_ROWS + N_EXPERTS * SEG_ALIGN, D_MODEL), BF16),
                                pltpu.SemaphoreType.DMA((2,))]),
            out_shape=jax.ShapeDtypeStruct((n * TOK_TILE, D_MODEL), F32),
            compiler_params=pltpu.CompilerParams(
                dimension_semantics=("arbitrary",), vmem_limit_bytes=VMEM_LIMIT),
            name="combine",
        )(*combine_tables, h1, meta, y))
        tile0 += n
    return outs


def _prepare_weights(g_mix, w_in, conv_w, conv_b, dt_bias, a_log, d_skip, g_ssd_out, g_q, g_k,
                     attn_sinks, g_att_out, w_out, g_ffn, w_route_group, b_route_group,
                     w_route_expert, b_route_expert, w_gate, w_up, w_down):
    cuts = np.cumsum([0, SSD_DIM, CONV_DIM, SSD_HEADS, ATT_DIM, KV_DIM, KV_DIM])
    seg = lambda i: w_in[:, cuts[i]:cuts[i + 1]]
    pad_lanes = lambda a: jnp.pad(a, ((0, 0), (0, LANES - a.shape[1])))
    bd = np.kron(np.eye(KV_DIM // HEAD_DIM, dtype=np.float32), np.ones((HEAD_DIM, HEAD_DIM), np.float32))
    n_route = N_EXPERTS + N_EXPERT_GROUPS
    return {
        "g_mix": g_mix.reshape(1, D_MODEL),
        "w_qk": jnp.concatenate([seg(3), seg(4)], axis=1).astype(BF16),
        "w_rest": jnp.concatenate([seg(0), seg(1), seg(5), pad_lanes(jnp.tile(seg(2), (1, DT_COPIES)))],
                                  axis=1).astype(BF16),
        "g_qk": jnp.concatenate([jnp.tile(g_q, ATT_HEADS), jnp.tile(g_k, ATT_KV_HEADS)]).reshape(1, ATT_DIM + KV_DIM),
        "bd": jnp.asarray(bd, BF16),
        "conv_w": conv_w, "conv_b": conv_b.reshape(1, CONV_DIM),
        "dt_bias": pad_lanes(jnp.tile(dt_bias, DT_COPIES).reshape(1, DT_COPIES * SSD_HEADS)),
        "a_log": pad_lanes(jnp.tile(a_log, DT_COPIES).reshape(1, DT_COPIES * SSD_HEADS)),
        "d_skip": jnp.repeat(d_skip, HEAD_DIM).reshape(1, SSD_DIM),
        "g_ssd_out": g_ssd_out.reshape(1, SSD_DIM),
        "attn_sinks": attn_sinks,
        "g_att_out": g_att_out.reshape(1, ATT_DIM),
        "w_out": w_out.astype(BF16),
        "g_ffn": g_ffn.reshape(1, D_MODEL),
        "w_route": pad_lanes(jnp.concatenate([w_route_expert, w_route_group], axis=1)).astype(BF16),
        "b_route": pad_lanes(jnp.concatenate([b_route_expert, b_route_group]).reshape(1, n_route)),
        "w_gate": w_gate, "w_up": w_up, "w_down": w_down,
    }


def _segment(x3, w, mk, mv, tail, h0t, kprev, vprev, L, tb_ssd, tb_att, tm, from_cache, sb=1):
    S, Ls, _ = x3.shape
    x = x3.reshape(S * Ls, D_MODEL)
    z, xbc, q, k, v, dt = _project(x, w, 1024 if x.shape[0] > 1024 else tm)
    r3 = lambda a: a.reshape(S, Ls, a.shape[-1])
    xbc3, k3, v3 = r3(xbc), r3(k), r3(v)
    y_ssd, h_new = _ssd(xbc3, r3(z), r3(dt), tail, h0t, w, L, tb_ssd, sb)
    if not from_cache:
        kprev, vprev = k3, v3
    y_att = _attention(r3(q), k3, v3, kprev, vprev, mk, mv, w, L, tb_att, from_cache, sb)
    part = _merge(x, y_ssd.reshape(S * Ls, SSD_DIM), y_att.reshape(S * Ls, ATT_DIM), w,
                  1024 if x.shape[0] > 1024 else tm)
    return part, xbc3, h_new, k3, v3


def _state_to_kernel(h):
    return h.reshape(h.shape[0], SSD_DIM, D_STATE)


def _state_from_kernel(hk):
    return hk.reshape(hk.shape[0], SSD_HEADS, HEAD_DIM, D_STATE)


def kernel(x_prompt, x_sample, cache_conv, state_ssd, cache_k, cache_v, meta_tokens, g_mix, w_in, conv_w, conv_b, dt_bias, a_log, d_skip, g_ssd_out, g_q, g_k, attn_sinks, g_att_out, w_out, g_ffn, w_route_group, b_route_group, w_route_expert, b_route_expert, w_gate, w_up, w_down):
    w = _prepare_weights(g_mix[0], w_in[0], conv_w[0], conv_b[0], dt_bias[0], a_log[0], d_skip[0],
                         g_ssd_out[0], g_q[0], g_k[0], attn_sinks[0], g_att_out[0], w_out[0], g_ffn[0],
                         w_route_group[0], b_route_group[0], w_route_expert[0], b_route_expert[0],
                         w_gate[0], w_up[0], w_down[0])
    n_b = x_sample.shape[0]
    n_dec = x_sample.shape[1]

    _, m_xbc, _, mk, mv, m_dt = _project(meta_tokens, w, N_META)
    zero_tail = jnp.zeros((1, CONV_WIDTH - 1, CONV_DIM), F32)
    zero_state = jnp.zeros((1, SSD_DIM, D_STATE), F32)
    m_xbc3 = m_xbc.reshape(1, N_META, CONV_DIM)
    _, m_state = _ssd(m_xbc3, jnp.zeros((1, N_META, SSD_DIM), F32), m_dt.reshape(1, N_META, LANES),
                      zero_tail, zero_state, w, N_META, N_META)
    m_tail = m_xbc3[:, N_META - (CONV_WIDTH - 1):]

    part_p, xbc_p, st_p, k_p, v_p = _segment(
        x_prompt, w, mk, mv, m_tail, m_state, None, None,
        L=CHUNK, tb_ssd=512, tb_att=512, tm=512, from_cache=False)
    part_s, xbc_s, st_s, k_s, v_s = _segment(
        x_sample, w, mk, mv, cache_conv[0], _state_to_kernel(state_ssd[0]),
        cache_k[0].reshape(n_b, WINDOW, KV_DIM), cache_v[0].reshape(n_b, WINDOW, KV_DIM),
        L=n_dec, tb_ssd=n_dec, tb_att=n_dec, tm=512, from_cache=True, sb=8)
    yp, ys = _moe([part_p, part_s], w, tm=512)
    yp = yp.reshape(x_prompt.shape)
    ys = ys.reshape(x_sample.shape)

    heads = lambda a, rows: a.reshape(a.shape[0], rows, ATT_KV_HEADS, HEAD_DIM)[None]
    return (yp, ys,
            xbc_p[:, -(CONV_WIDTH - 1):][None],
            _state_from_kernel(st_p)[None],
            heads(k_p[:, -WINDOW:], WINDOW), heads(v_p[:, -WINDOW:], WINDOW),
            xbc_s[:, -(CONV_WIDTH - 1):][None],
            _state_from_kernel(st_s)[None],
            heads(k_s, n_dec), heads(v_s, n_dec))
```

```python
import functools
import math

import numpy as np
import jax
import jax.numpy as jnp
from jax import lax
from jax.experimental import pallas as pl
from jax.experimental.pallas import tpu as pltpu

D_MODEL = 1024
CHUNK = 64
N_META = 16
HEAD_DIM = 64
ATT_HEADS = 16
ATT_KV_HEADS = 4
ATT_REP = ATT_HEADS // ATT_KV_HEADS
ATT_DIM = ATT_HEADS * HEAD_DIM
KV_DIM = ATT_KV_HEADS * HEAD_DIM
WINDOW = 128
SSD_HEADS = 16
SSD_DIM = SSD_HEADS * HEAD_DIM
SSD_GROUPS = 2
GROUP_DIM = SSD_DIM // SSD_GROUPS
D_STATE = 128
CONV_WIDTH = 4
CONV_DIM = SSD_DIM + 2 * SSD_GROUPS * D_STATE
N_EXPERT_GROUPS = 4
EXPERTS_PER_GROUP = 8
N_EXPERTS = N_EXPERT_GROUPS * EXPERTS_PER_GROUP
EXPERT_FF = D_MODEL // 4
EPS = 1e-6

LANES = 128
SUBLANES = 8
MXU_DIM = 256
DT_COPIES = 3
KEY_SPAN = 256
VMEM_LIMIT = 32 * 1024 * 1024
VMEM_LIMIT_WIDE = 56 * 1024 * 1024

F32 = jnp.float32
BF16 = jnp.bfloat16
NEG_INF = float("-inf")
LOG2E = math.log2(math.e)


def _dot(a, b):
    return jnp.dot(a, b, preferred_element_type=F32)


def _dot_nt(a, b):
    return lax.dot_general(a, b, (((1,), (1,)), ((), ())), preferred_element_type=F32)


def _dot_tn(a, b):
    return lax.dot_general(a, b, (((0,), (0,)), ((), ())), preferred_element_type=F32)


def _split3(x):
    hi = x.astype(BF16)
    r = x - hi.astype(F32)
    mid = r.astype(BF16)
    lo = (r - mid.astype(F32)).astype(BF16)
    return hi, mid, lo


def _pack3(x):
    hi, mid, lo = _split3(x)
    lane = lax.broadcasted_iota(jnp.int32, x.shape, 1)
    packed = jnp.where(lane < SSD_HEADS, hi.astype(F32),
                       jnp.where(lane < 2 * SSD_HEADS, mid.astype(F32), lo.astype(F32)))
    return packed.astype(BF16)


def _sel_dot(sel, x):
    hi, mid, lo = _split3(x)
    return _dot(sel, hi) + _dot(sel, mid) + _dot(sel, lo)


def _silu(x):
    return x * (1.0 / (1.0 + jnp.exp2(x * (-LOG2E))))


def _rms(x, g):
    ms = jnp.mean(x * x, axis=-1, keepdims=True)
    return x * lax.rsqrt(ms + EPS) * g


def _const_spec(shape):
    n = len(shape)
    return pl.BlockSpec(shape, lambda *_: (0,) * n, pipeline_mode=pl.Buffered(1))


def _proj_kernel(x_ref, gmix_ref, wqk_ref, wrest_ref, gqk_ref, bd_ref,
                 z_ref, xbc_ref, q_ref, k_ref, v_ref, dt_ref):
    tm = x_ref.shape[0]
    xn = _rms(x_ref[...], gmix_ref[...]).astype(BF16)
    qk = _dot(xn, wqk_ref[...])
    n_slices = (ATT_DIM + KV_DIM) // KV_DIM
    sq = jnp.concatenate([qk[:, j * KV_DIM:(j + 1) * KV_DIM] for j in range(n_slices)], axis=0)
    ms = _dot((sq * sq).astype(BF16), bd_ref[...]) * (1.0 / HEAD_DIM)
    inv = lax.rsqrt(ms + EPS)
    for j in range(n_slices):
        sl = slice(j * KV_DIM, (j + 1) * KV_DIM)
        normed = qk[:, sl] * inv[j * tm:(j + 1) * tm, :] * gqk_ref[:, sl]
        if j < ATT_DIM // KV_DIM:
            q_ref[:, sl] = (normed * (HEAD_DIM ** -0.5 * LOG2E)).astype(BF16)
        else:
            k_ref[...] = normed
    rest = _dot(xn, wrest_ref[...])
    cuts = np.cumsum([0, SSD_DIM, CONV_DIM, KV_DIM, LANES])
    for ref, lo_c, hi_c in zip((z_ref, xbc_ref, v_ref, dt_ref), cuts[:-1], cuts[1:]):
        ref[...] = rest[:, lo_c:hi_c]


def _project(x, w, tm):
    t = x.shape[0]
    assert t % tm == 0
    row = lambda n: pl.BlockSpec((tm, n), lambda i: (i, 0))
    ins = [x, w["g_mix"], w["w_qk"], w["w_rest"], w["g_qk"], w["bd"]]
    in_specs = [row(D_MODEL)] + [_const_spec(a.shape) for a in ins[1:]]
    out_dims = (SSD_DIM, CONV_DIM, ATT_DIM, KV_DIM, KV_DIM, LANES)
    out_dtypes = (F32, F32, BF16, F32, F32, F32)
    return pl.pallas_call(
        _proj_kernel,
        grid=(t // tm,),
        in_specs=in_specs,
        out_specs=[row(n) for n in out_dims],
        out_shape=[jax.ShapeDtypeStruct((t, n), d) for n, d in zip(out_dims, out_dtypes)],
        compiler_params=pltpu.CompilerParams(
            dimension_semantics=("arbitrary",),
            vmem_limit_bytes=VMEM_LIMIT_WIDE if tm > 512 else VMEM_LIMIT),
        name="projection",
    )(*ins)


N_SSD_STREAM_INPUTS = 5
N_SSD_CONSTS = 9


def _ssd_kernel(*refs, L, TB, SB):
    streamed = lambda k: k < N_SSD_STREAM_INPUTS or k >= N_SSD_STREAM_INPUTS + N_SSD_CONSTS
    for s in range(SB):
        _ssd_stream(*[r.at[s] if streamed(k) else r for k, r in enumerate(refs)], L=L, TB=TB)


def _ssd_stream(xbc_ref, z_ref, dt_ref, tail_ref, h0_ref, convw_ref, convb_ref, dtb_ref,
                alog_ref, dskip_ref, gout_ref, ehead_ref, epos_ref, tri_ref,
                y_ref, hout_ref, buf_ref, act_ref, dts_ref, ys_ref, st_ref, *, L, TB):
    hp = LANES // L
    n_lane_tiles = SSD_HEADS // hp
    hq = MXU_DIM // L
    n_tiles = SSD_HEADS // hq
    tile_w = hq * HEAD_DIM
    b = pl.program_id(1)
    pad = SUBLANES - (CONV_WIDTH - 1)

    @pl.when(b == 0)
    def _():
        buf_ref[0:SUBLANES, :] = jnp.zeros((SUBLANES, CONV_DIM), F32)
        buf_ref[pad:SUBLANES, :] = tail_ref[...]
        st_ref[...] = h0_ref[...].T

    buf_ref[SUBLANES:SUBLANES + TB, :] = xbc_ref[...]
    rows = buf_ref[...]
    acc = convb_ref[...] + convw_ref[CONV_WIDTH - 1:CONV_WIDTH, :] * rows[SUBLANES:, :]
    for j in range(CONV_WIDTH - 1):
        shifted = pltpu.roll(rows, CONV_WIDTH - 1 - j, axis=0)[SUBLANES:, :]
        acc = acc + convw_ref[j:j + 1, :] * shifted
    act_ref[...] = _silu(acc)
    buf_ref[pad:SUBLANES, :] = buf_ref[TB + pad:TB + SUBLANES, :]

    dtx = dt_ref[...] + dtb_ref[...]
    dts_ref[...] = jnp.maximum(dtx, 0.0) + jnp.log(1.0 + jnp.exp(-jnp.abs(dtx)))

    a_row = -jnp.exp(alog_ref[...]) * LOG2E
    ehead = ehead_ref[...]
    epos = epos_ref[...]
    tri = tri_ref[...]
    dskip = dskip_ref[...]
    row_i = lax.broadcasted_iota(jnp.int32, (L, n_lane_tiles * LANES), 0)
    col_i = lax.broadcasted_iota(jnp.int32, (L, n_lane_tiles * LANES), 1)
    pos_i = col_i % L
    diag_mask = row_i == pos_i
    causal_mask = row_i >= pos_i
    colhead = lax.broadcasted_iota(jnp.int32, (L, tile_w), 1) // HEAD_DIM

    def chunk(c):
        r0 = pl.multiple_of(c * L, L)
        dtc = dts_ref[pl.ds(r0, L), :]
        cs = _sel_dot(tri, dtc * a_row)
        cs16 = _pack3(cs)
        dt_b = _dot(_pack3(dtc), ehead)
        cs_b = _dot(cs16, ehead)
        cs_last_b = cs_b[L - 1:L, :]
        ecs_b = jnp.exp2(cs_b)
        dec_end_b = jnp.exp2(cs_last_b - cs_b)
        bdec_b = jnp.exp2(cs_last_b)
        cs_col = cs_b if L == HEAD_DIM else _dot(cs16, epos)
        cs_row = jnp.sum(jnp.where(diag_mask, cs_col, 0.0), axis=0, keepdims=True)
        lmat = jnp.exp2(jnp.where(causal_mask, cs_col - cs_row, NEG_INF))

        xs = act_ref[pl.ds(r0, L), 0:SSD_DIM]
        xdt_f = xs * dt_b
        xdt = xdt_f.astype(BF16)
        wx = (xdt_f * dec_end_b).astype(BF16)
        st = st_ref[...]
        st16 = st.astype(BF16)

        y_diag = []
        y_off = []
        s_new = []
        cb = [None] * SSD_GROUPS
        for g in range(SSD_GROUPS):
            bg = act_ref[pl.ds(r0, L), SSD_DIM + g * D_STATE:SSD_DIM + (g + 1) * D_STATE].astype(BF16)
            cg = act_ref[pl.ds(r0, L), SSD_DIM + (SSD_GROUPS + g) * D_STATE:
                         SSD_DIM + (SSD_GROUPS + g + 1) * D_STATE].astype(BF16)
            cb[g] = _dot_nt(cg, jnp.concatenate([bg] * hp, axis=0))
            gsl = slice(g * GROUP_DIM, (g + 1) * GROUP_DIM)
            y_off.append(_dot(cg, st16[:, gsl]))
            s_new.append(_dot_tn(bg, wx[:, gsl]))
        lane_tiles = MXU_DIM // LANES
        for t in range(n_tiles):
            cbt = jnp.concatenate(
                [cb[((t * lane_tiles + j) * hp * HEAD_DIM) // GROUP_DIM] for j in range(lane_tiles)], axis=1)
            gmat = (lmat[:, t * MXU_DIM:(t + 1) * MXU_DIM] * cbt).astype(BF16)
            xt = xdt[:, t * tile_w:(t + 1) * tile_w]
            rhs = jnp.concatenate(
                [jnp.where(colhead == hh, xt, jnp.zeros_like(xt)) for hh in range(hq)], axis=0)
            y_diag.append(_dot(gmat, rhs))
        y = (jnp.concatenate(y_diag, axis=1) + jnp.concatenate(y_off, axis=1) * ecs_b
             + dskip * xs)
        ys_ref[pl.ds(r0, L), :] = y
        st_ref[...] = bdec_b * st + jnp.concatenate(s_new, axis=1)

    n_chunks = TB // L
    unroll = n_chunks if n_chunks in (4, 8) else 1

    def chunks(i, carry):
        for u in range(unroll):
            chunk(i * unroll + u)
        return carry

    lax.fori_loop(0, n_chunks // unroll, chunks, 0)

    yg = ys_ref[...] * _silu(z_ref[...])
    y_ref[...] = _rms(yg, gout_ref[...])

    @pl.when(b == pl.num_programs(1) - 1)
    def _():
        hout_ref[...] = st_ref[...].T


def _ssd(xbc, z, dt, tail, h0t, w, L, TB, SB=1):
    S, Ls, _ = xbc.shape
    assert Ls % TB == 0 and TB % L == 0 and LANES % L == 0 and S % SB == 0
    nb = Ls // TB
    assert SB == 1 or nb == 1
    hp = LANES // L
    n_tiles = SSD_HEADS // hp
    per_stream = lambda a: (lambda s, b: (s, 0, 0)) if a.shape[0] == S and S > 1 else (lambda s, b: (0, 0, 0))
    seq = lambda n: pl.BlockSpec((SB, TB, n), lambda s, b: (s, b, 0))
    ehead = np.zeros((LANES, SSD_DIM), np.float32)
    epos = np.zeros((LANES, n_tiles * LANES), np.float32)
    for part in range(DT_COPIES):
        for h in range(SSD_HEADS):
            ehead[part * SSD_HEADS + h, h * HEAD_DIM:(h + 1) * HEAD_DIM] = 1.0
            epos[part * SSD_HEADS + h, h * L:(h + 1) * L] = 1.0
    tri = np.tril(np.ones((L, L), np.float32))
    consts = [w["conv_w"], w["conv_b"], w["dt_bias"], w["a_log"], w["d_skip"], w["g_ssd_out"],
              jnp.asarray(ehead, BF16), jnp.asarray(epos, BF16), jnp.asarray(tri, BF16)]
    in_specs = [seq(CONV_DIM), seq(SSD_DIM), seq(LANES),
                pl.BlockSpec((SB, CONV_WIDTH - 1, CONV_DIM), per_stream(tail)),
                pl.BlockSpec((SB, SSD_DIM, D_STATE), per_stream(h0t))]
    assert len(in_specs) == N_SSD_STREAM_INPUTS and len(consts) == N_SSD_CONSTS
    in_specs += [_const_spec(c.shape) for c in consts]
    return pl.pallas_call(
        functools.partial(_ssd_kernel, L=L, TB=TB, SB=SB),
        grid=(S // SB, nb),
        in_specs=in_specs,
        out_specs=[seq(SSD_DIM), pl.BlockSpec((SB, SSD_DIM, D_STATE), lambda s, b: (s, 0, 0))],
        out_shape=[jax.ShapeDtypeStruct((S, Ls, SSD_DIM), F32),
                   jax.ShapeDtypeStruct((S, SSD_DIM, D_STATE), F32)],
        scratch_shapes=[pltpu.VMEM((SB, SUBLANES + TB, CONV_DIM), F32),
                        pltpu.VMEM((SB, TB, CONV_DIM), F32),
                        pltpu.VMEM((SB, TB, LANES), F32),
                        pltpu.VMEM((SB, TB, SSD_DIM), F32),
                        pltpu.VMEM((SB, D_STATE, SSD_DIM), F32)],
        compiler_params=pltpu.CompilerParams(
            dimension_semantics=("arbitrary", "arbitrary"), vmem_limit_bytes=VMEM_LIMIT),
        name="ssd",
    )(xbc, z, dt, tail, h0t, *consts)


def _split_heads(x):
    lane = lax.broadcasted_iota(jnp.int32, (x.shape[0], LANES), 1)
    low = lane < HEAD_DIM
    lo, hi = [], []
    for t in range(KV_DIM // LANES):
        tile = x[:, t * LANES:(t + 1) * LANES]
        swapped = pltpu.roll(tile, HEAD_DIM, axis=1)
        lo += [jnp.where(low, tile, 0.0), jnp.where(low, swapped, 0.0)]
        hi += [jnp.where(low, 0.0, swapped), jnp.where(low, 0.0, tile)]
    return jnp.concatenate(lo, axis=1).astype(BF16), jnp.concatenate(hi, axis=1).astype(BF16)


def _attn_kernel(*refs, L, TB, SB, from_cache):
    for s in range(SB):
        _attn_stream(*[r if k == 0 or 6 <= k <= 9 else r.at[s] for k, r in enumerate(refs)],
                     L=L, TB=TB, from_cache=from_cache)


def _attn_stream(sink_ref, q_ref, k_ref, v_ref, kp_ref, vp_ref, mk_ref, mv_ref, bias_ref, g_ref,
                 o_ref, k_lo, k_hi, v_lo, v_hi, ybuf, s_buf, e_buf, sink_buf, *, L, TB, from_cache):
    b = pl.program_id(1)
    bufs = (k_lo, k_hi, v_lo, v_hi)
    if from_cache:
        window = _split_heads(kp_ref[...]) + _split_heads(vp_ref[...])
        for buf, val in zip(bufs, window):
            buf[0:WINDOW, :] = val
    else:
        @pl.when(b == 0)
        def _():
            for buf in bufs:
                buf[0:WINDOW, :] = jnp.zeros((WINDOW, buf.shape[1]), BF16)

        @pl.when(b > 0)
        def _():
            for buf in bufs:
                buf[0:WINDOW, :] = buf[TB:TB + WINDOW, :]

    for buf, val in zip(bufs, _split_heads(k_ref[...]) + _split_heads(v_ref[...])):
        buf[WINDOW:WINDOW + TB, :] = val
    n_pad = KEY_SPAN - WINDOW - L - N_META
    pad_rows = jnp.zeros((n_pad, 2 * KV_DIM), BF16)
    tails = [jnp.concatenate([t, pad_rows], axis=0)
             for t in _split_heads(mk_ref[...]) + _split_heads(mv_ref[...])]
    lane = lax.broadcasted_iota(jnp.int32, (2 * L, 2 * KEY_SPAN), 1) % KEY_SPAN
    lane_h = lax.broadcasted_iota(jnp.int32, (L, LANES), 1)
    lane_v = lax.broadcasted_iota(jnp.int32, (KEY_SPAN, LANES), 1)
    ones_lo = jnp.where(lane_v < HEAD_DIM, 1.0, 0.0).astype(BF16)
    ones_hi = jnp.where(lane_v < HEAD_DIM, 0.0, 1.0).astype(BF16)
    n_chunks = TB // L
    pairs_per_group = ATT_REP // 2

    def chunk_keys(c):
        r0 = c * L
        return [jnp.concatenate([buf[r0:r0 + WINDOW, :], buf[r0 + WINDOW:r0 + WINDOW + L, :], tail], axis=0)
                for buf, tail in zip(bufs, tails)]

    for c in range(n_chunks):
        r0 = c * L
        keys = chunk_keys(c)
        for g in range(ATT_KV_HEADS):
            gl = slice(g * LANES, (g + 1) * LANES)
            q4 = jnp.concatenate([q_ref[r0:r0 + L, p * LANES:(p + 1) * LANES]
                                  for p in range(pairs_per_group * g, pairs_per_group * (g + 1))], axis=0)
            s = _dot_nt(q4, jnp.concatenate([keys[0][:, gl], keys[1][:, gl]], axis=0)) + bias_ref[g]
            if not from_cache and r0 < WINDOW:
                n_invalid = WINDOW - (b * TB + r0)
                s = jnp.where(lane < n_invalid, NEG_INF, s)
            s_buf[c * ATT_KV_HEADS + g] = s

    for c in range(n_chunks):
        for pair in range(ATT_HEADS // 2):
            slot = c * ATT_KV_HEADS + pair // pairs_per_group
            rows = slice((pair % pairs_per_group) * L, (pair % pairs_per_group + 1) * L)
            sink_terms = []
            for half in range(2):
                cols = slice(half * KEY_SPAN, (half + 1) * KEY_SPAN)
                s = s_buf[slot, rows, cols]
                sink = sink_ref[2 * pair + half] * LOG2E
                m = jnp.maximum(jnp.max(s, axis=-1, keepdims=True), sink)
                e_buf[slot, rows, cols] = jnp.exp2(s - m).astype(BF16)
                sink_terms.append(jnp.exp2(sink - m))
            sink_buf[c * (ATT_HEADS // 2) + pair] = jnp.where(lane_h < HEAD_DIM, sink_terms[0], sink_terms[1])

    for c in range(n_chunks):
        r0 = c * L
        keys = chunk_keys(c)
        for g in range(ATT_KV_HEADS):
            gl = slice(g * LANES, (g + 1) * LANES)
            vcat = jnp.concatenate([jnp.concatenate([keys[2][:, gl], ones_lo], axis=1),
                                    jnp.concatenate([keys[3][:, gl], ones_hi], axis=1)], axis=0)
            ov = _dot(e_buf[c * ATT_KV_HEADS + g], vcat)
            for j in range(pairs_per_group):
                pair = pairs_per_group * g + j
                den = ov[j * L:(j + 1) * L, LANES:2 * LANES] + sink_buf[c * (ATT_HEADS // 2) + pair]
                ybuf[r0:r0 + L, pair * LANES:(pair + 1) * LANES] = ov[j * L:(j + 1) * L, 0:LANES] * (1.0 / den)
    o_ref[...] = _rms(ybuf[...], g_ref[...])


def _attention(q, k, v, kprev, vprev, mk, mv, w, L, TB, from_cache, SB=1):
    S, Ls, _ = q.shape
    assert Ls % TB == 0 and TB % L == 0 and (from_cache or TB >= WINDOW) and S % SB == 0
    nb = Ls // TB
    assert SB == 1 or (from_cache and nb == 1)
    seq = lambda n: pl.BlockSpec((SB, TB, n), lambda s, b: (s, b, 0))
    if not from_cache:
        kprev = vprev = jnp.zeros((1, WINDOW, KV_DIM), F32)
    prev = pl.BlockSpec((SB, WINDOW, KV_DIM),
                        (lambda s, b: (s, 0, 0)) if from_cache else (lambda s, b: (0, 0, 0)))
    slopes = 2.0 ** (-8.0 * np.arange(1, ATT_HEADS + 1, dtype=np.float64) / ATT_HEADS)
    dist = np.abs(WINDOW + np.arange(L)[:, None] - np.arange(WINDOW + L)[None, :])
    bias = np.full((ATT_HEADS, L, KEY_SPAN), NEG_INF, np.float32)
    bias[:, :, :WINDOW + L] = -slopes[:, None, None] * dist[None] * LOG2E
    bias[:, :, WINDOW + L:WINDOW + L + N_META] = 0.0
    bias = bias.reshape(ATT_KV_HEADS, ATT_REP // 2, 2, L, KEY_SPAN).transpose(0, 1, 3, 2, 4)
    bias = jnp.asarray(bias.reshape(ATT_KV_HEADS, ATT_REP // 2 * L, 2 * KEY_SPAN))
    n_chunks = TB // L
    in_specs = [pl.BlockSpec(memory_space=pltpu.SMEM),
                seq(ATT_DIM), seq(KV_DIM), seq(KV_DIM), prev, prev,
                _const_spec(mk.shape), _const_spec(mv.shape), _const_spec(bias.shape),
                _const_spec(w["g_att_out"].shape)]
    return pl.pallas_call(
        functools.partial(_attn_kernel, L=L, TB=TB, SB=SB, from_cache=from_cache),
        grid=(S // SB, nb),
        in_specs=in_specs,
        out_specs=seq(ATT_DIM),
        out_shape=jax.ShapeDtypeStruct((S, Ls, ATT_DIM), F32),
        scratch_shapes=[pltpu.VMEM((SB, WINDOW + TB, 2 * KV_DIM), BF16) for _ in range(4)]
                       + [pltpu.VMEM((SB, TB, ATT_DIM), F32),
                          pltpu.VMEM((SB, n_chunks * ATT_KV_HEADS, 2 * L, 2 * KEY_SPAN), F32),
                          pltpu.VMEM((SB, n_chunks * ATT_KV_HEADS, 2 * L, 2 * KEY_SPAN), BF16),
                          pltpu.VMEM((SB, n_chunks * ATT_HEADS // 2, L, LANES), F32)],
        compiler_params=pltpu.CompilerParams(
            dimension_semantics=("arbitrary", "arbitrary"), vmem_limit_bytes=VMEM_LIMIT),
        name="attention",
    )(w["attn_sinks"], q, k, v, kprev, vprev, mk, mv, bias, w["g_att_out"])


TOK_TILE = 512
SEG_ALIGN = 2 * SUBLANES
SORT_ROWS = 2 * TOK_TILE + N_EXPERTS * SEG_ALIGN
XS_COLS = D_MODEL + LANES
META_P1, META_P2, META_G1, META_G2, META_E1 = (N_EXPERTS + j for j in range(5))
PAY_G1, PAY_G2, PAY_E1 = 0, 3, 6


def _merge_kernel(h_ref, ys_ref, ya_ref, wo_ref, gffn_ref, wr_ref, br_ref, tri_ref, upper_ref,
                  h1_ref, xn_ref, meta_ref, cnt_ref, logit_buf):
    i = pl.program_id(0)

    @pl.when(i == 0)
    def _():
        logit_buf[...] = jnp.zeros(logit_buf.shape, F32)

    ycat = jnp.concatenate([ys_ref[...].astype(BF16), ya_ref[...].astype(BF16)], axis=1)
    h1 = h_ref[...] + _dot(ycat, wo_ref[...])
    h1_ref[...] = h1
    xn = _rms(h1, gffn_ref[...]).astype(BF16)
    xn_ref[...] = xn
    all_logits = logit_buf[(i + 1) % 2]
    logit_buf[i % 2] = _dot(xn, wr_ref[...]) + br_ref[...]
    tri = tri_ref[...]
    upper = upper_ref[...]
    lane = lax.broadcasted_iota(jnp.int32, (TOK_TILE, LANES), 1)
    big = jnp.int32(LANES)
    gmask = (lane >= N_EXPERTS) & (lane < N_EXPERTS + N_EXPERT_GROUPS)
    for s in range(all_logits.shape[0] // TOK_TILE):
        rows = slice(s * TOK_TILE, (s + 1) * TOK_TILE)
        logits = all_logits[rows]

        def top1(mask):
            mval = jnp.max(jnp.where(mask, logits, NEG_INF), axis=-1, keepdims=True)
            idx = jnp.min(jnp.where(mask & (logits == mval), lane, big), axis=-1, keepdims=True)
            return mval, idx

        gmax, gidx = top1(gmask)
        gate_g = 1.0 / jnp.sum(jnp.where(gmask, jnp.exp(logits - gmax), 0.0), axis=-1, keepdims=True)
        grp = gidx - N_EXPERTS
        emask = (lane // EXPERTS_PER_GROUP) == grp
        v1, i1 = top1(emask)
        v2, i2 = top1(emask & (lane != i1))
        e2 = jnp.exp(v2 - v1)
        g1 = gate_g / (1.0 + e2)
        g2 = gate_g * e2 / (1.0 + e2)
        oh1 = jnp.where(lane == i1, 1.0, 0.0)
        oh2 = jnp.where(lane == i2, 1.0, 0.0)
        oh = oh1 + oh2
        earlier = _dot(tri, oh.astype(BF16))
        cnt = jnp.sum(oh, axis=0, keepdims=True)
        units = jnp.floor((cnt + (SEG_ALIGN - 1)) * (1.0 / SEG_ALIGN))
        units = jnp.broadcast_to(units, (2 * SUBLANES, LANES)).astype(BF16)
        slot = _dot(units, upper)[0:1, :] * SEG_ALIGN + earlier
        p1 = jnp.sum(oh1 * slot, axis=-1, keepdims=True)
        p2 = jnp.sum(oh2 * slot, axis=-1, keepdims=True)
        meta = jnp.where(lane == META_P1, p1, 0.0)
        meta = jnp.where(lane == META_P2, p2, meta)
        meta = jnp.where(lane == META_G1, g1, meta)
        meta = jnp.where(lane == META_G2, g2, meta)
        meta = jnp.where(lane == META_E1, i1.astype(F32), meta)
        meta_ref[rows, :] = meta
        cnt_ref[s] = jnp.broadcast_to(cnt, (SUBLANES, LANES))


def _merge(h, y_ssd, y_att, w, tm):
    t = h.shape[0]
    assert t % tm == 0 and tm % TOK_TILE == 0
    sub = tm // TOK_TILE
    last = t // tm - 1
    row = lambda n: pl.BlockSpec((tm, n), lambda i: (jnp.minimum(i, last), 0))
    routed = lambda i: jnp.maximum(i - 1, 0)
    tri = jnp.asarray(np.tril(np.ones((TOK_TILE, TOK_TILE), np.float32), -1), BF16)
    upper = jnp.asarray(np.triu(np.ones((LANES, LANES), np.float32), 1), BF16)
    consts = [w["w_out"], w["g_ffn"], w["w_route"], w["b_route"], tri, upper]
    return pl.pallas_call(
        _merge_kernel,
        grid=(t // tm + 1,),
        in_specs=[row(D_MODEL), row(SSD_DIM), row(ATT_DIM)] + [_const_spec(c.shape) for c in consts],
        out_specs=[row(D_MODEL), row(D_MODEL),
                   pl.BlockSpec((tm, LANES), lambda i: (routed(i), 0)),
                   pl.BlockSpec((sub, SUBLANES, LANES), lambda i: (routed(i), 0, 0))],
        scratch_shapes=[pltpu.VMEM((2, tm, LANES), F32)],
        out_shape=[jax.ShapeDtypeStruct((t, D_MODEL), F32),
                   jax.ShapeDtypeStruct((t, D_MODEL), BF16),
                   jax.ShapeDtypeStruct((t, LANES), F32),
                   jax.ShapeDtypeStruct((t // TOK_TILE, SUBLANES, LANES), F32)],
        compiler_params=pltpu.CompilerParams(
            dimension_semantics=("arbitrary",),
            vmem_limit_bytes=VMEM_LIMIT_WIDE if tm > 512 else VMEM_LIMIT),
        name="merge_route",
    )(h, y_ssd, y_att, *consts)


def _meta_col(meta, j):
    lane = lax.broadcasted_iota(jnp.int32, meta.shape, 1)
    return jnp.sum(jnp.where(lane == j, meta, 0.0), axis=-1, keepdims=True)


def _pair_selector(meta):
    rows = lax.broadcasted_iota(jnp.int32, (meta.shape[0], SORT_ROWS), 1)
    p1 = _meta_col(meta, META_P1).astype(jnp.int32)
    p2 = _meta_col(meta, META_P2).astype(jnp.int32)
    return jnp.where(rows == p1, 1.0, jnp.where(rows == p2, 1.0, 0.0)).astype(BF16)


def _segment_dmas(off_ref, cnt_ref, row_ref, b, tile_buf, hbm, sem, to_hbm, wait):
    for e in range(N_EXPERTS):
        k = b * N_EXPERTS + e
        n = pl.multiple_of(cnt_ref[k], SEG_ALIGN)
        v = tile_buf.at[pl.ds(pl.multiple_of(off_ref[k], SEG_ALIGN), n)]
        h = hbm.at[pl.ds(pl.multiple_of(row_ref[k], SEG_ALIGN), n)]
        cp = pltpu.make_async_copy(v, h, sem) if to_hbm else pltpu.make_async_copy(h, v, sem)
        if wait:
            cp.wait()
        else:
            cp.start()


def _token_payload(meta):
    cols = {PAY_E1: _meta_col(meta, META_E1)}
    for base, src in ((PAY_G1, META_G1), (PAY_G2, META_G2)):
        for k, part in enumerate(_split3(_meta_col(meta, src))):
            cols[base + k] = part.astype(F32)
    lane = lax.broadcasted_iota(jnp.int32, (meta.shape[0], LANES), 1)
    out = jnp.zeros((meta.shape[0], LANES), F32)
    for k, c in cols.items():
        out = jnp.where(lane == k, c, out)
    return out.astype(BF16)


def _dispatch_kernel(off_ref, cnt_ref, row_ref, tail_row_ref, tail_cnt_ref, xn_ref, meta_ref, *rest,
                     tile0, first):
    xs_ref, sort_buf, zero_buf, sem, tail_sem = rest if first else rest[1:]
    b = pl.program_id(0)
    nb = pl.num_programs(0)
    slot = b % 2
    seg = functools.partial(_segment_dmas, off_ref, cnt_ref, row_ref, hbm=xs_ref, to_hbm=True)

    def tails(wait):
        def body(e, carry):
            n = pl.multiple_of(tail_cnt_ref[e], SEG_ALIGN)

            @pl.when(n > 0)
            def _():
                cp = pltpu.make_async_copy(
                    zero_buf.at[pl.ds(0, n)],
                    xs_ref.at[pl.ds(pl.multiple_of(tail_row_ref[e], SEG_ALIGN), n)], tail_sem.at[0])
                if wait:
                    cp.wait()
                else:
                    cp.start()
            return carry

        lax.fori_loop(0, N_EXPERTS, body, 0)

    @pl.when(b >= 2)
    def _():
        seg(tile0 + b - 2, sort_buf.at[slot], sem=sem.at[slot], wait=True)

    if first:
        @pl.when(b == 0)
        def _():
            zero_buf[...] = jnp.zeros(zero_buf.shape, BF16)
            tails(wait=False)

    meta = meta_ref[...]
    payload = jnp.concatenate([xn_ref[...], _token_payload(meta)], axis=1)
    sort_buf[slot] = _dot_tn(_pair_selector(meta), payload).astype(BF16)
    seg(tile0 + b, sort_buf.at[slot], sem=sem.at[slot], wait=False)

    @pl.when(b == nb - 1)
    def _():
        @pl.when(b >= 1)
        def _():
            seg(tile0 + b - 1, sort_buf.at[1 - slot], sem=sem.at[1 - slot], wait=True)

        seg(tile0 + b, sort_buf.at[slot], sem=sem.at[slot], wait=True)
        if first:
            tails(wait=True)


def _expert_kernel(tile_expert_ref, n_active_ref, xs_ref, wg_ref, wu_ref, wd_ref, y_ref):
    @pl.when(pl.program_id(0) < n_active_ref[0])
    def _():
        x = xs_ref[:, 0:D_MODEL]
        pay = xs_ref[:, D_MODEL:XS_COLS].astype(F32)
        lane = lax.broadcasted_iota(jnp.int32, pay.shape, 1)
        pick = lambda lo, n: jnp.sum(jnp.where((lane >= lo) & (lane < lo + n), pay, 0.0), axis=-1, keepdims=True)
        expert = tile_expert_ref[pl.program_id(0)].astype(F32)
        gate = jnp.where(pick(PAY_E1, 1) == expert, pick(PAY_G1, 3), pick(PAY_G2, 3))
        gu = _dot(x, jnp.concatenate([wg_ref[...].astype(BF16), wu_ref[...].astype(BF16)], axis=1))
        hid = _silu(gu[:, 0:EXPERT_FF]) * gu[:, EXPERT_FF:2 * EXPERT_FF]
        y_ref[...] = (gate * _dot(hid.astype(BF16), wd_ref[...].astype(BF16))).astype(BF16)


def _combine_kernel(off_ref, cnt_ref, row_ref, h1_ref, meta_ref, y_ref, o_ref, y_buf, sem, *, tile0):
    b = pl.program_id(0)
    nb = pl.num_programs(0)
    slot = b % 2
    seg = functools.partial(_segment_dmas, off_ref, cnt_ref, row_ref, hbm=y_ref, to_hbm=False)

    @pl.when(b == 0)
    def _():
        y_buf[...] = jnp.zeros(y_buf.shape, BF16)
        seg(tile0 + b, y_buf.at[slot], sem=sem.at[slot], wait=False)

    @pl.when(b + 1 < nb)
    def _():
        seg(tile0 + b + 1, y_buf.at[1 - slot], sem=sem.at[1 - slot], wait=False)

    seg(tile0 + b, y_buf.at[slot], sem=sem.at[slot], wait=True)
    o_ref[...] = h1_ref[...] + _dot(_pair_selector(meta_ref[...]), y_buf[slot, 0:SORT_ROWS, :])


def _moe(parts, w, tm):
    i32 = jnp.int32
    n_tiles = [p[0].shape[0] // TOK_TILE for p in parts]
    nb = sum(n_tiles)
    t = nb * TOK_TILE
    cnt = jnp.concatenate([p[3][:, 0, :N_EXPERTS] for p in parts], axis=0).astype(i32)
    cnt_al = (cnt + (SEG_ALIGN - 1)) // SEG_ALIGN * SEG_ALIGN
    off = jnp.cumsum(cnt_al, axis=1) - cnt_al
    tot = jnp.sum(cnt_al, axis=0)
    tot_tm = (tot + (tm - 1)) // tm * tm
    start = jnp.cumsum(tot_tm) - tot_tm
    row = start[None, :] + jnp.cumsum(cnt_al, axis=0) - cnt_al
    tile_ends = jnp.cumsum(tot_tm // tm)
    max_rows = 2 * t + (SEG_ALIGN - 1) * min(N_EXPERTS * nb, 2 * t) + N_EXPERTS * (tm - SEG_ALIGN)
    nt = -(-max_rows // tm)
    tile_expert = jnp.sum((jnp.arange(nt, dtype=i32)[:, None] >= tile_ends[None, :]).astype(i32), axis=1)
    tile_expert = jnp.minimum(tile_expert, N_EXPERTS - 1)
    n_active = tile_ends[-1:].astype(i32)
    empty = cnt_al == 0
    flat = lambda a: a.reshape(-1).astype(i32)
    n_rows = flat(jnp.maximum(cnt_al, SEG_ALIGN))
    e_idx = jnp.arange(N_EXPERTS, dtype=i32)[None, :]
    spare = nt * tm + ((jnp.arange(nb, dtype=i32)[:, None] % 2) * N_EXPERTS + e_idx) * SEG_ALIGN
    dispatch_tables = [flat(off), n_rows, flat(jnp.where(empty, spare, row))]
    combine_tables = [flat(jnp.where(empty, SORT_ROWS + e_idx * SEG_ALIGN, off)), n_rows,
                      flat(jnp.where(empty, 0, row))]
    n_spare = 2 * N_EXPERTS * SEG_ALIGN
    tail_tables = [(start + tot).astype(i32), (tot_tm - tot).astype(i32)]

    tok = lambda n: pl.BlockSpec((TOK_TILE, n), lambda i, *_: (i, 0))
    hbm = pl.BlockSpec(memory_space=pl.ANY)
    n_prefetch = len(dispatch_tables) + len(tail_tables)
    xs = None
    tile0 = 0
    for (_, xn, meta, _), n in zip(parts, n_tiles):
        first = xs is None
        xs = pl.pallas_call(
            functools.partial(_dispatch_kernel, tile0=tile0, first=first),
            grid_spec=pltpu.PrefetchScalarGridSpec(
                num_scalar_prefetch=n_prefetch, grid=(n,),
                in_specs=[tok(D_MODEL), tok(LANES)] + ([] if first else [hbm]),
                out_specs=hbm,
                scratch_shapes=[pltpu.VMEM((2, SORT_ROWS, XS_COLS), BF16),
                                pltpu.VMEM((tm, XS_COLS), BF16),
                                pltpu.SemaphoreType.DMA((2,)),
                                pltpu.SemaphoreType.DMA((1,))]),
            out_shape=jax.ShapeDtypeStruct((nt * tm + n_spare, XS_COLS), BF16),
            input_output_aliases={} if first else {n_prefetch + 2: 0},
            compiler_params=pltpu.CompilerParams(
                dimension_semantics=("arbitrary",), vmem_limit_bytes=VMEM_LIMIT),
            name="dispatch",
        )(*dispatch_tables, *tail_tables, xn, meta, *([] if first else [xs]))
        tile0 += n

    act = lambda i, te, na: jnp.minimum(i, na[0] - 1)
    y = pl.pallas_call(
        _expert_kernel,
        grid_spec=pltpu.PrefetchScalarGridSpec(
            num_scalar_prefetch=2, grid=(nt,),
            in_specs=[pl.BlockSpec((tm, XS_COLS), lambda i, te, na: (act(i, te, na), 0)),
                      pl.BlockSpec((None, D_MODEL, EXPERT_FF), lambda i, te, na: (te[act(i, te, na)], 0, 0)),
                      pl.BlockSpec((None, D_MODEL, EXPERT_FF), lambda i, te, na: (te[act(i, te, na)], 0, 0)),
                      pl.BlockSpec((None, EXPERT_FF, D_MODEL), lambda i, te, na: (te[act(i, te, na)], 0, 0))],
            out_specs=pl.BlockSpec((tm, D_MODEL), lambda i, te, na: (act(i, te, na), 0))),
        out_shape=jax.ShapeDtypeStruct((nt * tm, D_MODEL), BF16),
        compiler_params=pltpu.CompilerParams(
            dimension_semantics=("arbitrary",), vmem_limit_bytes=VMEM_LIMIT),
        name="experts",
    )(tile_expert, n_active, xs, w["w_gate"], w["w_up"], w["w_down"])

    outs = []
    tile0 = 0
    for (h1, _, meta, _), n in zip(parts, n_tiles):
        outs.append(pl.pallas_call(
            functools.partial(_combine_kernel, tile0=tile0),
            grid_spec=pltpu.PrefetchScalarGridSpec(
                num_scalar_prefetch=len(combine_tables), grid=(n,),
                in_specs=[tok(D_MODEL), tok(LANES), hbm],
                out_specs=tok(D_MODEL),
                scratch_shapes=[pltpu.VMEM((2, SORT_ROWS + N_EXPERTS * SEG_ALIGN, D_MODEL), BF16),
                                pltpu.SemaphoreType.DMA((2,))]),
            out_shape=jax.ShapeDtypeStruct((n * TOK_TILE, D_MODEL), F32),
            compiler_params=pltpu.CompilerParams(
                dimension_semantics=("arbitrary",), vmem_limit_bytes=VMEM_LIMIT),
            name="combine",
        )(*combine_tables, h1, meta, y))
        tile0 += n
    return outs


def _prepare_weights(g_mix, w_in, conv_w, conv_b, dt_bias, a_log, d_skip, g_ssd_out, g_q, g_k,
                     attn_sinks, g_att_out, w_out, g_ffn, w_route_group, b_route_group,
                     w_route_expert, b_route_expert, w_gate, w_up, w_down):
    cuts = np.cumsum([0, SSD_DIM, CONV_DIM, SSD_HEADS, ATT_DIM, KV_DIM, KV_DIM])
    seg = lambda i: w_in[:, cuts[i]:cuts[i + 1]]
    pad_lanes = lambda a: jnp.pad(a, ((0, 0), (0, LANES - a.shape[1])))
    bd = np.kron(np.eye(KV_DIM // HEAD_DIM, dtype=np.float32), np.ones((HEAD_DIM, HEAD_DIM), np.float32))
    n_route = N_EXPERTS + N_EXPERT_GROUPS
    return {
        "g_mix": g_mix.reshape(1, D_MODEL),
        "w_qk": jnp.concatenate([seg(3), seg(4)], axis=1).astype(BF16),
        "w_rest": jnp.concatenate([seg(0), seg(1), seg(5), pad_lanes(jnp.tile(seg(2), (1, DT_COPIES)))],
                                  axis=1).astype(BF16),
        "g_qk": jnp.concatenate([jnp.tile(g_q, ATT_HEADS), jnp.tile(g_k, ATT_KV_HEADS)]).reshape(1, ATT_DIM + KV_DIM),
        "bd": jnp.asarray(bd, BF16),
        "conv_w": conv_w, "conv_b": conv_b.reshape(1, CONV_DIM),
        "dt_bias": pad_lanes(jnp.tile(dt_bias, DT_COPIES).reshape(1, DT_COPIES * SSD_HEADS)),
        "a_log": pad_lanes(jnp.tile(a_log, DT_COPIES).reshape(1, DT_COPIES * SSD_HEADS)),
        "d_skip": jnp.repeat(d_skip, HEAD_DIM).reshape(1, SSD_DIM),
        "g_ssd_out": g_ssd_out.reshape(1, SSD_DIM),
        "attn_sinks": attn_sinks,
        "g_att_out": g_att_out.reshape(1, ATT_DIM),
        "w_out": w_out.astype(BF16),
        "g_ffn": g_ffn.reshape(1, D_MODEL),
        "w_route": pad_lanes(jnp.concatenate([w_route_expert, w_route_group], axis=1)).astype(BF16),
        "b_route": pad_lanes(jnp.concatenate([b_route_expert, b_route_group]).reshape(1, n_route)),
        "w_gate": w_gate, "w_up": w_up, "w_down": w_down,
    }


def _segment(x3, w, mk, mv, tail, h0t, kprev, vprev, L, tb_ssd, tb_att, tm, from_cache, sb=1):
    S, Ls, _ = x3.shape
    x = x3.reshape(S * Ls, D_MODEL)
    z, xbc, q, k, v, dt = _project(x, w, 1024 if x.shape[0] > 1024 else tm)
    r3 = lambda a: a.reshape(S, Ls, a.shape[-1])
    xbc3, k3, v3 = r3(xbc), r3(k), r3(v)
    y_ssd, h_new = _ssd(xbc3, r3(z), r3(dt), tail, h0t, w, L, tb_ssd, sb)
    if not from_cache:
        kprev, vprev = k3, v3
    y_att = _attention(r3(q), k3, v3, kprev, vprev, mk, mv, w, L, tb_att, from_cache, sb)
    part = _merge(x, y_ssd.reshape(S * Ls, SSD_DIM), y_att.reshape(S * Ls, ATT_DIM), w,
                  1024 if x.shape[0] > 1024 else tm)
    return part, xbc3, h_new, k3, v3


def _state_to_kernel(h):
    return h.reshape(h.shape[0], SSD_DIM, D_STATE)


def _state_from_kernel(hk):
    return hk.reshape(hk.shape[0], SSD_HEADS, HEAD_DIM, D_STATE)


def kernel(x_prompt, x_sample, cache_conv, state_ssd, cache_k, cache_v, meta_tokens, g_mix, w_in, conv_w, conv_b, dt_bias, a_log, d_skip, g_ssd_out, g_q, g_k, attn_sinks, g_att_out, w_out, g_ffn, w_route_group, b_route_group, w_route_expert, b_route_expert, w_gate, w_up, w_down):
    w = _prepare_weights(g_mix[0], w_in[0], conv_w[0], conv_b[0], dt_bias[0], a_log[0], d_skip[0],
                         g_ssd_out[0], g_q[0], g_k[0], attn_sinks[0], g_att_out[0], w_out[0], g_ffn[0],
                         w_route_group[0], b_route_group[0], w_route_expert[0], b_route_expert[0],
                         w_gate[0], w_up[0], w_down[0])
    n_b = x_sample.shape[0]
    n_dec = x_sample.shape[1]

    _, m_xbc, _, mk, mv, m_dt = _project(meta_tokens, w, N_META)
    zero_tail = jnp.zeros((1, CONV_WIDTH - 1, CONV_DIM), F32)
    zero_state = jnp.zeros((1, SSD_DIM, D_STATE), F32)
    m_xbc3 = m_xbc.reshape(1, N_META, CONV_DIM)
    _, m_state = _ssd(m_xbc3, jnp.zeros((1, N_META, SSD_DIM), F32), m_dt.reshape(1, N_META, LANES),
                      zero_tail, zero_state, w, N_META, N_META)
    m_tail = m_xbc3[:, N_META - (CONV_WIDTH - 1):]

    part_p, xbc_p, st_p, k_p, v_p = _segment(
        x_prompt, w, mk, mv, m_tail, m_state, None, None,
        L=CHUNK, tb_ssd=512, tb_att=512, tm=512, from_cache=False)
    part_s, xbc_s, st_s, k_s, v_s = _segment(
        x_sample, w, mk, mv, cache_conv[0], _state_to_kernel(state_ssd[0]),
        cache_k[0].reshape(n_b, WINDOW, KV_DIM), cache_v[0].reshape(n_b, WINDOW, KV_DIM),
        L=n_dec, tb_ssd=n_dec, tb_att=n_dec, tm=512, from_cache=True, sb=8)
    yp, ys = _moe([part_p, part_s], w, tm=512)
    yp = yp.reshape(x_prompt.shape)
    ys = ys.reshape(x_sample.shape)

    heads = lambda a, rows: a.reshape(a.shape[0], rows, ATT_KV_HEADS, HEAD_DIM)[None]
    return (yp, ys,
            xbc_p[:, -(CONV_WIDTH - 1):][None],
            _state_from_kernel(st_p)[None],
            heads(k_p[:, -WINDOW:], WINDOW), heads(v_p[:, -WINDOW:], WINDOW),
            xbc_s[:, -(CONV_WIDTH - 1):][None],
            _state_from_kernel(st_s)[None],
            heads(k_s, n_dec), heads(v_s, n_dec))
```

```python
import functools
import math

import numpy as np
import jax
import jax.numpy as jnp
from jax import lax
from jax.experimental import pallas as pl
from jax.experimental.pallas import tpu as pltpu

D_MODEL = 1024
CHUNK = 64
N_META = 16
HEAD_DIM = 64
ATT_HEADS = 16
ATT_KV_HEADS = 4
ATT_REP = ATT_HEADS // ATT_KV_HEADS
ATT_DIM = ATT_HEADS * HEAD_DIM
KV_DIM = ATT_KV_HEADS * HEAD_DIM
WINDOW = 128
SSD_HEADS = 16
SSD_DIM = SSD_HEADS * HEAD_DIM
SSD_GROUPS = 2
GROUP_DIM = SSD_DIM // SSD_GROUPS
D_STATE = 128
CONV_WIDTH = 4
CONV_DIM = SSD_DIM + 2 * SSD_GROUPS * D_STATE
N_EXPERT_GROUPS = 4
EXPERTS_PER_GROUP = 8
N_EXPERTS = N_EXPERT_GROUPS * EXPERTS_PER_GROUP
EXPERT_FF = D_MODEL // 4
EPS = 1e-6

LANES = 128
SUBLANES = 8
MXU_DIM = 256
DT_COPIES = 3
KEY_SPAN = 256
VMEM_LIMIT = 48 * 1024 * 1024
VMEM_LIMIT_WIDE = 56 * 1024 * 1024

F32 = jnp.float32
BF16 = jnp.bfloat16
NEG_INF = float("-inf")
LOG2E = math.log2(math.e)


def _dot(a, b):
    return jnp.dot(a, b, preferred_element_type=F32)


def _dot_nt(a, b):
    return lax.dot_general(a, b, (((1,), (1,)), ((), ())), preferred_element_type=F32)


def _dot_tn(a, b):
    return lax.dot_general(a, b, (((0,), (0,)), ((), ())), preferred_element_type=F32)


def _split3(x):
    hi = x.astype(BF16)
    r = x - hi.astype(F32)
    mid = r.astype(BF16)
    lo = (r - mid.astype(F32)).astype(BF16)
    return hi, mid, lo


def _pack3(x):
    hi, mid, lo = _split3(x)
    lane = lax.broadcasted_iota(jnp.int32, x.shape, 1)
    packed = jnp.where(lane < SSD_HEADS, hi.astype(F32),
                       jnp.where(lane < 2 * SSD_HEADS, mid.astype(F32), lo.astype(F32)))
    return packed.astype(BF16)


def _sel_dot(sel, x):
    hi, mid, lo = _split3(x)
    return _dot(sel, hi) + _dot(sel, mid) + _dot(sel, lo)


def _silu(x):
    return x * (1.0 / (1.0 + jnp.exp2(x * (-LOG2E))))


def _rms(x, g):
    ms = jnp.mean(x * x, axis=-1, keepdims=True)
    return x * lax.rsqrt(ms + EPS) * g


def _const_spec(shape):
    n = len(shape)
    return pl.BlockSpec(shape, lambda *_: (0,) * n, pipeline_mode=pl.Buffered(1))


def _proj_kernel(x_ref, gmix_ref, wqk_ref, wrest_ref, gqk_ref, bd_ref,
                 z_ref, xbc_ref, q_ref, k_ref, v_ref, dt_ref):
    tm = x_ref.shape[0]
    xn = _rms(x_ref[...], gmix_ref[...]).astype(BF16)
    qk = _dot(xn, wqk_ref[...])
    n_slices = (ATT_DIM + KV_DIM) // KV_DIM
    sq = jnp.concatenate([qk[:, j * KV_DIM:(j + 1) * KV_DIM] for j in range(n_slices)], axis=0)
    ms = _dot((sq * sq).astype(BF16), bd_ref[...]) * (1.0 / HEAD_DIM)
    inv = lax.rsqrt(ms + EPS)
    for j in range(n_slices):
        sl = slice(j * KV_DIM, (j + 1) * KV_DIM)
        normed = qk[:, sl] * inv[j * tm:(j + 1) * tm, :] * gqk_ref[:, sl]
        if j < ATT_DIM // KV_DIM:
            q_ref[:, sl] = (normed * (HEAD_DIM ** -0.5 * LOG2E)).astype(BF16)
        else:
            k_ref[...] = normed
    rest = _dot(xn, wrest_ref[...])
    cuts = np.cumsum([0, SSD_DIM, CONV_DIM, KV_DIM, LANES])
    for ref, lo_c, hi_c in zip((z_ref, xbc_ref, v_ref, dt_ref), cuts[:-1], cuts[1:]):
        ref[...] = rest[:, lo_c:hi_c]


def _project(x, w, tm):
    t = x.shape[0]
    assert t % tm == 0
    row = lambda n: pl.BlockSpec((tm, n), lambda i: (i, 0))
    ins = [x, w["g_mix"], w["w_qk"], w["w_rest"], w["g_qk"], w["bd"]]
    in_specs = [row(D_MODEL)] + [_const_spec(a.shape) for a in ins[1:]]
    out_dims = (SSD_DIM, CONV_DIM, ATT_DIM, KV_DIM, KV_DIM, LANES)
    out_dtypes = (F32, F32, BF16, F32, F32, F32)
    return pl.pallas_call(
        _proj_kernel,
        grid=(t // tm,),
        in_specs=in_specs,
        out_specs=[row(n) for n in out_dims],
        out_shape=[jax.ShapeDtypeStruct((t, n), d) for n, d in zip(out_dims, out_dtypes)],
        compiler_params=pltpu.CompilerParams(
            dimension_semantics=("arbitrary",),
            vmem_limit_bytes=VMEM_LIMIT_WIDE if tm > 512 else VMEM_LIMIT),
        name="projection",
    )(*ins)


N_SSD_STREAM_INPUTS = 5
N_SSD_CONSTS = 9


def _ssd_kernel(*refs, L, TB, SB):
    streamed = lambda k: k < N_SSD_STREAM_INPUTS or k >= N_SSD_STREAM_INPUTS + N_SSD_CONSTS
    for s in range(SB):
        _ssd_stream(*[r.at[s] if streamed(k) else r for k, r in enumerate(refs)], L=L, TB=TB)


def _ssd_stream(xbc_ref, z_ref, dt_ref, tail_ref, h0_ref, convw_ref, convb_ref, dtb_ref,
                alog_ref, dskip_ref, gout_ref, ehead_ref, epos_ref, tri_ref,
                y_ref, hout_ref, buf_ref, act_ref, dts_ref, ys_ref, st_ref, *, L, TB):
    hp = LANES // L
    n_lane_tiles = SSD_HEADS // hp
    hq = MXU_DIM // L
    n_tiles = SSD_HEADS // hq
    tile_w = hq * HEAD_DIM
    b = pl.program_id(1)
    pad = SUBLANES - (CONV_WIDTH - 1)

    @pl.when(b == 0)
    def _():
        buf_ref[0:SUBLANES, :] = jnp.zeros((SUBLANES, CONV_DIM), F32)
        buf_ref[pad:SUBLANES, :] = tail_ref[...]
        st_ref[...] = h0_ref[...].T

    buf_ref[SUBLANES:SUBLANES + TB, :] = xbc_ref[...]
    rows = buf_ref[...]
    acc = convb_ref[...] + convw_ref[CONV_WIDTH - 1:CONV_WIDTH, :] * rows[SUBLANES:, :]
    for j in range(CONV_WIDTH - 1):
        shifted = pltpu.roll(rows, CONV_WIDTH - 1 - j, axis=0)[SUBLANES:, :]
        acc = acc + convw_ref[j:j + 1, :] * shifted
    act_ref[...] = _silu(acc)
    buf_ref[pad:SUBLANES, :] = buf_ref[TB + pad:TB + SUBLANES, :]

    dtx = dt_ref[...] + dtb_ref[...]
    dts_ref[...] = jnp.maximum(dtx, 0.0) + jnp.log(1.0 + jnp.exp(-jnp.abs(dtx)))

    a_row = -jnp.exp(alog_ref[...]) * LOG2E
    ehead = ehead_ref[...]
    epos = epos_ref[...]
    tri = tri_ref[...]
    dskip = dskip_ref[...]
    row_i = lax.broadcasted_iota(jnp.int32, (L, n_lane_tiles * LANES), 0)
    col_i = lax.broadcasted_iota(jnp.int32, (L, n_lane_tiles * LANES), 1)
    pos_i = col_i % L
    diag_mask = row_i == pos_i
    causal_mask = row_i >= pos_i
    colhead = lax.broadcasted_iota(jnp.int32, (L, tile_w), 1) // HEAD_DIM

    def chunk(c):
        r0 = pl.multiple_of(c * L, L)
        dtc = dts_ref[pl.ds(r0, L), :]
        cs = _sel_dot(tri, dtc * a_row)
        cs16 = _pack3(cs)
        dt_b = _dot(_pack3(dtc), ehead)
        cs_b = _dot(cs16, ehead)
        cs_last_b = cs_b[L - 1:L, :]
        ecs_b = jnp.exp2(cs_b)
        dec_end_b = jnp.exp2(cs_last_b - cs_b)
        bdec_b = jnp.exp2(cs_last_b)
        cs_col = cs_b if L == HEAD_DIM else _dot(cs16, epos)
        cs_row = jnp.sum(jnp.where(diag_mask, cs_col, 0.0), axis=0, keepdims=True)
        lmat = jnp.exp2(jnp.where(causal_mask, cs_col - cs_row, NEG_INF))

        xs = act_ref[pl.ds(r0, L), 0:SSD_DIM]
        xdt_f = xs * dt_b
        xdt = xdt_f.astype(BF16)
        wx = (xdt_f * dec_end_b).astype(BF16)
        st = st_ref[...]
        st16 = st.astype(BF16)

        y_diag = []
        y_off = []
        s_new = []
        cb = [None] * SSD_GROUPS
        for g in range(SSD_GROUPS):
            bg = act_ref[pl.ds(r0, L), SSD_DIM + g * D_STATE:SSD_DIM + (g + 1) * D_STATE].astype(BF16)
            cg = act_ref[pl.ds(r0, L), SSD_DIM + (SSD_GROUPS + g) * D_STATE:
                         SSD_DIM + (SSD_GROUPS + g + 1) * D_STATE].astype(BF16)
            cb[g] = _dot_nt(cg, jnp.concatenate([bg] * hp, axis=0))
            gsl = slice(g * GROUP_DIM, (g + 1) * GROUP_DIM)
            y_off.append(_dot(cg, st16[:, gsl]))
            s_new.append(_dot_tn(bg, wx[:, gsl]))
        lane_tiles = MXU_DIM // LANES
        for t in range(n_tiles):
            cbt = jnp.concatenate(
                [cb[((t * lane_tiles + j) * hp * HEAD_DIM) // GROUP_DIM] for j in range(lane_tiles)], axis=1)
            gmat = (lmat[:, t * MXU_DIM:(t + 1) * MXU_DIM] * cbt).astype(BF16)
            xt = xdt[:, t * tile_w:(t + 1) * tile_w]
            rhs = jnp.concatenate(
                [jnp.where(colhead == hh, xt, jnp.zeros_like(xt)) for hh in range(hq)], axis=0)
            y_diag.append(_dot(gmat, rhs))
        y = (jnp.concatenate(y_diag, axis=1) + jnp.concatenate(y_off, axis=1) * ecs_b
             + dskip * xs)
        ys_ref[pl.ds(r0, L), :] = y
        st_ref[...] = bdec_b * st + jnp.concatenate(s_new, axis=1)

    n_chunks = TB // L
    unroll = n_chunks if n_chunks in (4, 8) else 1

    def chunks(i, carry):
        for u in range(unroll):
            chunk(i * unroll + u)
        return carry

    lax.fori_loop(0, n_chunks // unroll, chunks, 0)

    yg = ys_ref[...] * _silu(z_ref[...])
    y_ref[...] = _rms(yg, gout_ref[...])

    @pl.when(b == pl.num_programs(1) - 1)
    def _():
        hout_ref[...] = st_ref[...].T


def _ssd(xbc, z, dt, tail, h0t, w, L, TB, SB=1):
    S, Ls, _ = xbc.shape
    assert Ls % TB == 0 and TB % L == 0 and LANES % L == 0 and S % SB == 0
    nb = Ls // TB
    assert SB == 1 or nb == 1
    hp = LANES // L
    n_tiles = SSD_HEADS // hp
    per_stream = lambda a: (lambda s, b: (s, 0, 0)) if a.shape[0] == S and S > 1 else (lambda s, b: (0, 0, 0))
    seq = lambda n: pl.BlockSpec((SB, TB, n), lambda s, b: (s, b, 0))
    ehead = np.zeros((LANES, SSD_DIM), np.float32)
    epos = np.zeros((LANES, n_tiles * LANES), np.float32)
    for part in range(DT_COPIES):
        for h in range(SSD_HEADS):
            ehead[part * SSD_HEADS + h, h * HEAD_DIM:(h + 1) * HEAD_DIM] = 1.0
            epos[part * SSD_HEADS + h, h * L:(h + 1) * L] = 1.0
    tri = np.tril(np.ones((L, L), np.float32))
    consts = [w["conv_w"], w["conv_b"], w["dt_bias"], w["a_log"], w["d_skip"], w["g_ssd_out"],
              jnp.asarray(ehead, BF16), jnp.asarray(epos, BF16), jnp.asarray(tri, BF16)]
    in_specs = [seq(CONV_DIM), seq(SSD_DIM), seq(LANES),
                pl.BlockSpec((SB, CONV_WIDTH - 1, CONV_DIM), per_stream(tail)),
                pl.BlockSpec((SB, SSD_DIM, D_STATE), per_stream(h0t))]
    assert len(in_specs) == N_SSD_STREAM_INPUTS and len(consts) == N_SSD_CONSTS
    in_specs += [_const_spec(c.shape) for c in consts]
    return pl.pallas_call(
        functools.partial(_ssd_kernel, L=L, TB=TB, SB=SB),
        grid=(S // SB, nb),
        in_specs=in_specs,
        out_specs=[seq(SSD_DIM), pl.BlockSpec((SB, SSD_DIM, D_STATE), lambda s, b: (s, 0, 0))],
        out_shape=[jax.ShapeDtypeStruct((S, Ls, SSD_DIM), F32),
                   jax.ShapeDtypeStruct((S, SSD_DIM, D_STATE), F32)],
        scratch_shapes=[pltpu.VMEM((SB, SUBLANES + TB, CONV_DIM), F32),
                        pltpu.VMEM((SB, TB, CONV_DIM), F32),
                        pltpu.VMEM((SB, TB, LANES), F32),
                        pltpu.VMEM((SB, TB, SSD_DIM), F32),
                        pltpu.VMEM((SB, D_STATE, SSD_DIM), F32)],
        compiler_params=pltpu.CompilerParams(
            dimension_semantics=("arbitrary", "arbitrary"), vmem_limit_bytes=VMEM_LIMIT),
        name="ssd",
    )(xbc, z, dt, tail, h0t, *consts)


def _split_heads(x):
    lane = lax.broadcasted_iota(jnp.int32, (x.shape[0], LANES), 1)
    low = lane < HEAD_DIM
    lo, hi = [], []
    for t in range(KV_DIM // LANES):
        tile = x[:, t * LANES:(t + 1) * LANES]
        swapped = pltpu.roll(tile, HEAD_DIM, axis=1)
        lo += [jnp.where(low, tile, 0.0), jnp.where(low, swapped, 0.0)]
        hi += [jnp.where(low, 0.0, swapped), jnp.where(low, 0.0, tile)]
    return jnp.concatenate(lo, axis=1).astype(BF16), jnp.concatenate(hi, axis=1).astype(BF16)


def _attn_kernel(*refs, L, TB, SB, from_cache):
    for s in range(SB):
        _attn_stream(*[r if k == 0 or 6 <= k <= 9 else r.at[s] for k, r in enumerate(refs)],
                     L=L, TB=TB, from_cache=from_cache)


def _attn_stream(sink_ref, q_ref, k_ref, v_ref, kp_ref, vp_ref, mk_ref, mv_ref, bias_ref, g_ref,
                 o_ref, k_lo, k_hi, v_lo, v_hi, ybuf, s_buf, e_buf, sink_buf, *, L, TB, from_cache):
    b = pl.program_id(1)
    bufs = (k_lo, k_hi, v_lo, v_hi)
    if from_cache:
        window = _split_heads(kp_ref[...]) + _split_heads(vp_ref[...])
        for buf, val in zip(bufs, window):
            buf[0:WINDOW, :] = val
    else:
        @pl.when(b == 0)
        def _():
            for buf in bufs:
                buf[0:WINDOW, :] = jnp.zeros((WINDOW, buf.shape[1]), BF16)

        @pl.when(b > 0)
        def _():
            for buf in bufs:
                buf[0:WINDOW, :] = buf[TB:TB + WINDOW, :]

    for buf, val in zip(bufs, _split_heads(k_ref[...]) + _split_heads(v_ref[...])):
        buf[WINDOW:WINDOW + TB, :] = val
    n_pad = KEY_SPAN - WINDOW - L - N_META
    pad_rows = jnp.zeros((n_pad, 2 * KV_DIM), BF16)
    tails = [jnp.concatenate([t, pad_rows], axis=0)
             for t in _split_heads(mk_ref[...]) + _split_heads(mv_ref[...])]
    lane = lax.broadcasted_iota(jnp.int32, (2 * L, 2 * KEY_SPAN), 1) % KEY_SPAN
    lane_h = lax.broadcasted_iota(jnp.int32, (L, LANES), 1)
    lane_v = lax.broadcasted_iota(jnp.int32, (KEY_SPAN, LANES), 1)
    ones_lo = jnp.where(lane_v < HEAD_DIM, 1.0, 0.0).astype(BF16)
    ones_hi = jnp.where(lane_v < HEAD_DIM, 0.0, 1.0).astype(BF16)
    n_chunks = TB // L
    pairs_per_group = ATT_REP // 2

    def chunk_keys(c):
        r0 = c * L
        return [jnp.concatenate([buf[r0:r0 + WINDOW, :], buf[r0 + WINDOW:r0 + WINDOW + L, :], tail], axis=0)
                for buf, tail in zip(bufs, tails)]

    for c in range(n_chunks):
        r0 = c * L
        keys = chunk_keys(c)
        for g in range(ATT_KV_HEADS):
            gl = slice(g * LANES, (g + 1) * LANES)
            q4 = jnp.concatenate([q_ref[r0:r0 + L, p * LANES:(p + 1) * LANES]
                                  for p in range(pairs_per_group * g, pairs_per_group * (g + 1))], axis=0)
            s = _dot_nt(q4, jnp.concatenate([keys[0][:, gl], keys[1][:, gl]], axis=0)) + bias_ref[g]
            if not from_cache and r0 < WINDOW:
                n_invalid = WINDOW - (b * TB + r0)
                s = jnp.where(lane < n_invalid, NEG_INF, s)
            s_buf[c * ATT_KV_HEADS + g] = s

    for c in range(n_chunks):
        for pair in range(ATT_HEADS // 2):
            slot = c * ATT_KV_HEADS + pair // pairs_per_group
            rows = slice((pair % pairs_per_group) * L, (pair % pairs_per_group + 1) * L)
            sink_terms = []
            for half in range(2):
                cols = slice(half * KEY_SPAN, (half + 1) * KEY_SPAN)
                s = s_buf[slot, rows, cols]
                sink = sink_ref[2 * pair + half] * LOG2E
                m = jnp.maximum(jnp.max(s, axis=-1, keepdims=True), sink)
                e_buf[slot, rows, cols] = jnp.exp2(s - m).astype(BF16)
                sink_terms.append(jnp.exp2(sink - m))
            sink_buf[c * (ATT_HEADS // 2) + pair] = jnp.where(lane_h < HEAD_DIM, sink_terms[0], sink_terms[1])

    for c in range(n_chunks):
        r0 = c * L
        keys = chunk_keys(c)
        for g in range(ATT_KV_HEADS):
            gl = slice(g * LANES, (g + 1) * LANES)
            vcat = jnp.concatenate([jnp.concatenate([keys[2][:, gl], ones_lo], axis=1),
                                    jnp.concatenate([keys[3][:, gl], ones_hi], axis=1)], axis=0)
            ov = _dot(e_buf[c * ATT_KV_HEADS + g], vcat)
            for j in range(pairs_per_group):
                pair = pairs_per_group * g + j
                den = ov[j * L:(j + 1) * L, LANES:2 * LANES] + sink_buf[c * (ATT_HEADS // 2) + pair]
                ybuf[r0:r0 + L, pair * LANES:(pair + 1) * LANES] = ov[j * L:(j + 1) * L, 0:LANES] * (1.0 / den)
    o_ref[...] = _rms(ybuf[...], g_ref[...])


def _attention(q, k, v, kprev, vprev, mk, mv, w, L, TB, from_cache, SB=1):
    S, Ls, _ = q.shape
    assert Ls % TB == 0 and TB % L == 0 and (from_cache or TB >= WINDOW) and S % SB == 0
    nb = Ls // TB
    assert SB == 1 or (from_cache and nb == 1)
    seq = lambda n: pl.BlockSpec((SB, TB, n), lambda s, b: (s, b, 0))
    if not from_cache:
        kprev = vprev = jnp.zeros((1, WINDOW, KV_DIM), F32)
    prev = pl.BlockSpec((SB, WINDOW, KV_DIM),
                        (lambda s, b: (s, 0, 0)) if from_cache else (lambda s, b: (0, 0, 0)))
    slopes = 2.0 ** (-8.0 * np.arange(1, ATT_HEADS + 1, dtype=np.float64) / ATT_HEADS)
    dist = np.abs(WINDOW + np.arange(L)[:, None] - np.arange(WINDOW + L)[None, :])
    bias = np.full((ATT_HEADS, L, KEY_SPAN), NEG_INF, np.float32)
    bias[:, :, :WINDOW + L] = -slopes[:, None, None] * dist[None] * LOG2E
    bias[:, :, WINDOW + L:WINDOW + L + N_META] = 0.0
    bias = bias.reshape(ATT_KV_HEADS, ATT_REP // 2, 2, L, KEY_SPAN).transpose(0, 1, 3, 2, 4)
    bias = jnp.asarray(bias.reshape(ATT_KV_HEADS, ATT_REP // 2 * L, 2 * KEY_SPAN))
    n_chunks = TB // L
    in_specs = [pl.BlockSpec(memory_space=pltpu.SMEM),
                seq(ATT_DIM), seq(KV_DIM), seq(KV_DIM), prev, prev,
                _const_spec(mk.shape), _const_spec(mv.shape), _const_spec(bias.shape),
                _const_spec(w["g_att_out"].shape)]
    return pl.pallas_call(
        functools.partial(_attn_kernel, L=L, TB=TB, SB=SB, from_cache=from_cache),
        grid=(S // SB, nb),
        in_specs=in_specs,
        out_specs=seq(ATT_DIM),
        out_shape=jax.ShapeDtypeStruct((S, Ls, ATT_DIM), F32),
        scratch_shapes=[pltpu.VMEM((SB, WINDOW + TB, 2 * KV_DIM), BF16) for _ in range(4)]
                       + [pltpu.VMEM((SB, TB, ATT_DIM), F32),
                          pltpu.VMEM((SB, n_chunks * ATT_KV_HEADS, 2 * L, 2 * KEY_SPAN), F32),
                          pltpu.VMEM((SB, n_chunks * ATT_KV_HEADS, 2 * L, 2 * KEY_SPAN), BF16),
                          pltpu.VMEM((SB, n_chunks * ATT_HEADS // 2, L, LANES), F32)],
        compiler_params=pltpu.CompilerParams(
            dimension_semantics=("arbitrary", "arbitrary"), vmem_limit_bytes=VMEM_LIMIT),
        name="attention",
    )(w["attn_sinks"], q, k, v, kprev, vprev, mk, mv, bias, w["g_att_out"])


TOK_TILE = 512
SEG_ALIGN = 2 * SUBLANES
SORT_ROWS = 2 * TOK_TILE + N_EXPERTS * SEG_ALIGN
XS_COLS = D_MODEL + LANES
META_P1, META_P2, META_G1, META_G2, META_E1 = (N_EXPERTS + j for j in range(5))
PAY_G1, PAY_G2, PAY_E1 = 0, 3, 6


def _merge_kernel(h_ref, ys_ref, ya_ref, wo_ref, gffn_ref, wr_ref, br_ref, tri_ref, upper_ref,
                  h1_ref, xn_ref, meta_ref, cnt_ref, logit_buf):
    i = pl.program_id(0)

    @pl.when(i == 0)
    def _():
        logit_buf[...] = jnp.zeros(logit_buf.shape, F32)

    ycat = jnp.concatenate([ys_ref[...].astype(BF16), ya_ref[...].astype(BF16)], axis=1)
    h1 = h_ref[...] + _dot(ycat, wo_ref[...])
    h1_ref[...] = h1
    xn = _rms(h1, gffn_ref[...]).astype(BF16)
    xn_ref[...] = xn
    all_logits = logit_buf[(i + 1) % 2]
    logit_buf[i % 2] = _dot(xn, wr_ref[...]) + br_ref[...]
    tri = tri_ref[...]
    upper = upper_ref[...]
    lane = lax.broadcasted_iota(jnp.int32, (TOK_TILE, LANES), 1)
    big = jnp.int32(LANES)
    gmask = (lane >= N_EXPERTS) & (lane < N_EXPERTS + N_EXPERT_GROUPS)
    for s in range(all_logits.shape[0] // TOK_TILE):
        rows = slice(s * TOK_TILE, (s + 1) * TOK_TILE)
        logits = all_logits[rows]

        def top1(mask):
            mval = jnp.max(jnp.where(mask, logits, NEG_INF), axis=-1, keepdims=True)
            idx = jnp.min(jnp.where(mask & (logits == mval), lane, big), axis=-1, keepdims=True)
            return mval, idx

        gmax, gidx = top1(gmask)
        gate_g = 1.0 / jnp.sum(jnp.where(gmask, jnp.exp(logits - gmax), 0.0), axis=-1, keepdims=True)
        grp = gidx - N_EXPERTS
        emask = (lane // EXPERTS_PER_GROUP) == grp
        v1, i1 = top1(emask)
        v2, i2 = top1(emask & (lane != i1))
        e2 = jnp.exp(v2 - v1)
        g1 = gate_g / (1.0 + e2)
        g2 = gate_g * e2 / (1.0 + e2)
        oh1 = jnp.where(lane == i1, 1.0, 0.0)
        oh2 = jnp.where(lane == i2, 1.0, 0.0)
        oh = oh1 + oh2
        earlier = _dot(tri, oh.astype(BF16))
        cnt = jnp.sum(oh, axis=0, keepdims=True)
        units = jnp.floor((cnt + (SEG_ALIGN - 1)) * (1.0 / SEG_ALIGN))
        units = jnp.broadcast_to(units, (2 * SUBLANES, LANES)).astype(BF16)
        slot = _dot(units, upper)[0:1, :] * SEG_ALIGN + earlier
        p1 = jnp.sum(oh1 * slot, axis=-1, keepdims=True)
        p2 = jnp.sum(oh2 * slot, axis=-1, keepdims=True)
        meta = jnp.where(lane == META_P1, p1, 0.0)
        meta = jnp.where(lane == META_P2, p2, meta)
        meta = jnp.where(lane == META_G1, g1, meta)
        meta = jnp.where(lane == META_G2, g2, meta)
        meta = jnp.where(lane == META_E1, i1.astype(F32), meta)
        meta_ref[rows, :] = meta
        cnt_ref[s] = jnp.broadcast_to(cnt, (SUBLANES, LANES))


def _merge(h, y_ssd, y_att, w, tm):
    t = h.shape[0]
    assert t % tm == 0 and tm % TOK_TILE == 0
    sub = tm // TOK_TILE
    last = t // tm - 1
    row = lambda n: pl.BlockSpec((tm, n), lambda i: (jnp.minimum(i, last), 0))
    routed = lambda i: jnp.maximum(i - 1, 0)
    tri = jnp.asarray(np.tril(np.ones((TOK_TILE, TOK_TILE), np.float32), -1), BF16)
    upper = jnp.asarray(np.triu(np.ones((LANES, LANES), np.float32), 1), BF16)
    consts = [w["w_out"], w["g_ffn"], w["w_route"], w["b_route"], tri, upper]
    return pl.pallas_call(
        _merge_kernel,
        grid=(t // tm + 1,),
        in_specs=[row(D_MODEL), row(SSD_DIM), row(ATT_DIM)] + [_const_spec(c.shape) for c in consts],
        out_specs=[row(D_MODEL), row(D_MODEL),
                   pl.BlockSpec((tm, LANES), lambda i: (routed(i), 0)),
                   pl.BlockSpec((sub, SUBLANES, LANES), lambda i: (routed(i), 0, 0))],
        scratch_shapes=[pltpu.VMEM((2, tm, LANES), F32)],
        out_shape=[jax.ShapeDtypeStruct((t, D_MODEL), F32),
                   jax.ShapeDtypeStruct((t, D_MODEL), BF16),
                   jax.ShapeDtypeStruct((t, LANES), F32),
                   jax.ShapeDtypeStruct((t // TOK_TILE, SUBLANES, LANES), F32)],
        compiler_params=pltpu.CompilerParams(
            dimension_semantics=("arbitrary",),
            vmem_limit_bytes=VMEM_LIMIT_WIDE if tm > 512 else VMEM_LIMIT),
        name="merge_route",
    )(h, y_ssd, y_att, *consts)


def _meta_col(meta, j):
    lane = lax.broadcasted_iota(jnp.int32, meta.shape, 1)
    return jnp.sum(jnp.where(lane == j, meta, 0.0), axis=-1, keepdims=True)


def _pair_selector(meta):
    rows = lax.broadcasted_iota(jnp.int32, (meta.shape[0], SORT_ROWS), 1)
    p1 = _meta_col(meta, META_P1).astype(jnp.int32)
    p2 = _meta_col(meta, META_P2).astype(jnp.int32)
    return jnp.where(rows == p1, 1.0, jnp.where(rows == p2, 1.0, 0.0)).astype(BF16)


def _segment_dmas(off_ref, cnt_ref, row_ref, b, tile_buf, hbm, sem, to_hbm, wait):
    for e in range(N_EXPERTS):
        k = b * N_EXPERTS + e
        n = pl.multiple_of(cnt_ref[k], SEG_ALIGN)
        v = tile_buf.at[pl.ds(pl.multiple_of(off_ref[k], SEG_ALIGN), n)]
        h = hbm.at[pl.ds(pl.multiple_of(row_ref[k], SEG_ALIGN), n)]
        cp = pltpu.make_async_copy(v, h, sem) if to_hbm else pltpu.make_async_copy(h, v, sem)
        if wait:
            cp.wait()
        else:
            cp.start()


def _token_payload(meta):
    cols = {PAY_E1: _meta_col(meta, META_E1)}
    for base, src in ((PAY_G1, META_G1), (PAY_G2, META_G2)):
        for k, part in enumerate(_split3(_meta_col(meta, src))):
            cols[base + k] = part.astype(F32)
    lane = lax.broadcasted_iota(jnp.int32, (meta.shape[0], LANES), 1)
    out = jnp.zeros((meta.shape[0], LANES), F32)
    for k, c in cols.items():
        out = jnp.where(lane == k, c, out)
    return out.astype(BF16)


def _dispatch_kernel(off_ref, cnt_ref, row_ref, tail_row_ref, tail_cnt_ref, xn_ref, meta_ref, *rest,
                     tile0, first):
    xs_ref, sort_buf, zero_buf, sem, tail_sem = rest if first else rest[1:]
    b = pl.program_id(0)
    nb = pl.num_programs(0)
    slot = b % 2
    seg = functools.partial(_segment_dmas, off_ref, cnt_ref, row_ref, hbm=xs_ref, to_hbm=True)

    def tails(wait):
        def body(e, carry):
            n = pl.multiple_of(tail_cnt_ref[e], SEG_ALIGN)

            @pl.when(n > 0)
            def _():
                cp = pltpu.make_async_copy(
                    zero_buf.at[pl.ds(0, n)],
                    xs_ref.at[pl.ds(pl.multiple_of(tail_row_ref[e], SEG_ALIGN), n)], tail_sem.at[0])
                if wait:
                    cp.wait()
                else:
                    cp.start()
            return carry

        lax.fori_loop(0, N_EXPERTS, body, 0)

    @pl.when(b >= 2)
    def _():
        seg(tile0 + b - 2, sort_buf.at[slot], sem=sem.at[slot], wait=True)

    if first:
        @pl.when(b == 0)
        def _():
            zero_buf[...] = jnp.zeros(zero_buf.shape, BF16)
            tails(wait=False)

    meta = meta_ref[...]
    payload = jnp.concatenate([xn_ref[...], _token_payload(meta)], axis=1)
    sort_buf[slot] = _dot_tn(_pair_selector(meta), payload).astype(BF16)
    seg(tile0 + b, sort_buf.at[slot], sem=sem.at[slot], wait=False)

    @pl.when(b == nb - 1)
    def _():
        @pl.when(b >= 1)
        def _():
            seg(tile0 + b - 1, sort_buf.at[1 - slot], sem=sem.at[1 - slot], wait=True)

        seg(tile0 + b, sort_buf.at[slot], sem=sem.at[slot], wait=True)
        if first:
            tails(wait=True)


def _expert_kernel(tile_expert_ref, n_active_ref, xs_ref, wg_ref, wu_ref, wd_ref, y_ref):
    @pl.when(pl.program_id(0) < n_active_ref[0])
    def _():
        x = xs_ref[:, 0:D_MODEL]
        pay = xs_ref[:, D_MODEL:XS_COLS].astype(F32)
        lane = lax.broadcasted_iota(jnp.int32, pay.shape, 1)
        pick = lambda lo, n: jnp.sum(jnp.where((lane >= lo) & (lane < lo + n), pay, 0.0), axis=-1, keepdims=True)
        expert = tile_expert_ref[pl.program_id(0)].astype(F32)
        gate = jnp.where(pick(PAY_E1, 1) == expert, pick(PAY_G1, 3), pick(PAY_G2, 3))
        gu = _dot(x, jnp.concatenate([wg_ref[...].astype(BF16), wu_ref[...].astype(BF16)], axis=1))
        hid = _silu(gu[:, 0:EXPERT_FF]) * gu[:, EXPERT_FF:2 * EXPERT_FF]
        y_ref[...] = (gate * _dot(hid.astype(BF16), wd_ref[...].astype(BF16))).astype(BF16)


def _combine_kernel(off_ref, cnt_ref, row_ref, h1_ref, meta_ref, y_ref, o_ref, y_buf, sem, *, tile0):
    b = pl.program_id(0)
    nb = pl.num_programs(0)
    slot = b % 2
    seg = functools.partial(_segment_dmas, off_ref, cnt_ref, row_ref, hbm=y_ref, to_hbm=False)

    @pl.when(b == 0)
    def _():
        y_buf[...] = jnp.zeros(y_buf.shape, BF16)
        seg(tile0 + b, y_buf.at[slot], sem=sem.at[slot], wait=False)

    @pl.when(b + 1 < nb)
    def _():
        seg(tile0 + b + 1, y_buf.at[1 - slot], sem=sem.at[1 - slot], wait=False)

    seg(tile0 + b, y_buf.at[slot], sem=sem.at[slot], wait=True)
    o_ref[...] = h1_ref[...] + _dot(_pair_selector(meta_ref[...]), y_buf[slot, 0:SORT_ROWS, :])


def _moe(parts, w, tm):
    i32 = jnp.int32
    n_tiles = [p[0].shape[0] // TOK_TILE for p in parts]
    nb = sum(n_tiles)
    t = nb * TOK_TILE
    cnt = jnp.concatenate([p[3][:, 0, :N_EXPERTS] for p in parts], axis=0).astype(i32)
    cnt_al = (cnt + (SEG_ALIGN - 1)) // SEG_ALIGN * SEG_ALIGN
    off = jnp.cumsum(cnt_al, axis=1) - cnt_al
    tot = jnp.sum(cnt_al, axis=0)
    tot_tm = (tot + (tm - 1)) // tm * tm
    start = jnp.cumsum(tot_tm) - tot_tm
    row = start[None, :] + jnp.cumsum(cnt_al, axis=0) - cnt_al
    tile_ends = jnp.cumsum(tot_tm // tm)
    max_rows = 2 * t + (SEG_ALIGN - 1) * min(N_EXPERTS * nb, 2 * t) + N_EXPERTS * (tm - SEG_ALIGN)
    nt = -(-max_rows // tm)
    tile_expert = jnp.sum((jnp.arange(nt, dtype=i32)[:, None] >= tile_ends[None, :]).astype(i32), axis=1)
    tile_expert = jnp.minimum(tile_expert, N_EXPERTS - 1)
    n_active = tile_ends[-1:].astype(i32)
    empty = cnt_al == 0
    flat = lambda a: a.reshape(-1).astype(i32)
    n_rows = flat(jnp.maximum(cnt_al, SEG_ALIGN))
    e_idx = jnp.arange(N_EXPERTS, dtype=i32)[None, :]
    spare = nt * tm + ((jnp.arange(nb, dtype=i32)[:, None] % 2) * N_EXPERTS + e_idx) * SEG_ALIGN
    dispatch_tables = [flat(off), n_rows, flat(jnp.where(empty, spare, row))]
    combine_tables = [flat(jnp.where(empty, SORT_ROWS + e_idx * SEG_ALIGN, off)), n_rows,
                      flat(jnp.where(empty, 0, row))]
    n_spare = 2 * N_EXPERTS * SEG_ALIGN
    tail_tables = [(start + tot).astype(i32), (tot_tm - tot).astype(i32)]

    tok = lambda n: pl.BlockSpec((TOK_TILE, n), lambda i, *_: (i, 0))
    hbm = pl.BlockSpec(memory_space=pl.ANY)
    n_prefetch = len(dispatch_tables) + len(tail_tables)
    xs = None
    tile0 = 0
    for (_, xn, meta, _), n in zip(parts, n_tiles):
        first = xs is None
        xs = pl.pallas_call(
            functools.partial(_dispatch_kernel, tile0=tile0, first=first),
            grid_spec=pltpu.PrefetchScalarGridSpec(
                num_scalar_prefetch=n_prefetch, grid=(n,),
                in_specs=[tok(D_MODEL), tok(LANES)] + ([] if first else [hbm]),
                out_specs=hbm,
                scratch_shapes=[pltpu.VMEM((2, SORT_ROWS, XS_COLS), BF16),
                                pltpu.VMEM((tm, XS_COLS), BF16),
                                pltpu.SemaphoreType.DMA((2,)),
                                pltpu.SemaphoreType.DMA((1,))]),
            out_shape=jax.ShapeDtypeStruct((nt * tm + n_spare, XS_COLS), BF16),
            input_output_aliases={} if first else {n_prefetch + 2: 0},
            compiler_params=pltpu.CompilerParams(
                dimension_semantics=("arbitrary",), vmem_limit_bytes=VMEM_LIMIT),
            name="dispatch",
        )(*dispatch_tables, *tail_tables, xn, meta, *([] if first else [xs]))
        tile0 += n

    act = lambda i, te, na: jnp.minimum(i, na[0] - 1)
    y = pl.pallas_call(
        _expert_kernel,
        grid_spec=pltpu.PrefetchScalarGridSpec(
            num_scalar_prefetch=2, grid=(nt,),
            in_specs=[pl.BlockSpec((tm, XS_COLS), lambda i, te, na: (act(i, te, na), 0)),
                      pl.BlockSpec((None, D_MODEL, EXPERT_FF), lambda i, te, na: (te[act(i, te, na)], 0, 0)),
                      pl.BlockSpec((None, D_MODEL, EXPERT_FF), lambda i, te, na: (te[act(i, te, na)], 0, 0)),
                      pl.BlockSpec((None, EXPERT_FF, D_MODEL), lambda i, te, na: (te[act(i, te, na)], 0, 0))],
            out_specs=pl.BlockSpec((tm, D_MODEL), lambda i, te, na: (act(i, te, na), 0))),
        out_shape=jax.ShapeDtypeStruct((nt * tm, D_MODEL), BF16),
        compiler_params=pltpu.CompilerParams(
            dimension_semantics=("arbitrary",), vmem_limit_bytes=VMEM_LIMIT),
        name="experts",
    )(tile_expert, n_active, xs, w["w_gate"], w["w_up"], w["w_down"])

    outs = []
    tile0 = 0
    for (h1, _, meta, _), n in zip(parts, n_tiles):
        outs.append(pl.pallas_call(
            functools.partial(_combine_kernel, tile0=tile0),
            grid_spec=pltpu.PrefetchScalarGridSpec(
                num_scalar_prefetch=len(combine_tables), grid=(n,),
                in_specs=[tok(D_MODEL), tok(LANES), hbm],
                out_specs=tok(D_MODEL),
                scratch_shapes=[pltpu.VMEM((2, SORT_ROWS + N_EXPERTS * SEG_ALIGN, D_MODEL), BF16),
                                pltpu.SemaphoreType.DMA((2,))]),
            out_shape=jax.ShapeDtypeStruct((n * TOK_TILE, D_MODEL), F32),
            compiler_params=pltpu.CompilerParams(
                dimension_semantics=("arbitrary",), vmem_limit_bytes=VMEM_LIMIT),
            name="combine",
        )(*combine_tables, h1, meta, y))
        tile0 += n
    return outs


def _prepare_weights(g_mix, w_in, conv_w, conv_b, dt_bias, a_log, d_skip, g_ssd_out, g_q, g_k,
                     attn_sinks, g_att_out, w_out, g_ffn, w_route_group, b_route_group,
                     w_route_expert, b_route_expert, w_gate, w_up, w_down):
    cuts = np.cumsum([0, SSD_DIM, CONV_DIM, SSD_HEADS, ATT_DIM, KV_DIM, KV_DIM])
    seg = lambda i: w_in[:, cuts[i]:cuts[i + 1]]
    pad_lanes = lambda a: jnp.pad(a, ((0, 0), (0, LANES - a.shape[1])))
    bd = np.kron(np.eye(KV_DIM // HEAD_DIM, dtype=np.float32), np.ones((HEAD_DIM, HEAD_DIM), np.float32))
    n_route = N_EXPERTS + N_EXPERT_GROUPS
    return {
        "g_mix": g_mix.reshape(1, D_MODEL),
        "w_qk": jnp.concatenate([seg(3), seg(4)], axis=1).astype(BF16),
        "w_rest": jnp.concatenate([seg(0), seg(1), seg(5), pad_lanes(jnp.tile(seg(2), (1, DT_COPIES)))],
                                  axis=1).astype(BF16),
        "g_qk": jnp.concatenate([jnp.tile(g_q, ATT_HEADS), jnp.tile(g_k, ATT_KV_HEADS)]).reshape(1, ATT_DIM + KV_DIM),
        "bd": jnp.asarray(bd, BF16),
        "conv_w": conv_w, "conv_b": conv_b.reshape(1, CONV_DIM),
        "dt_bias": pad_lanes(jnp.tile(dt_bias, DT_COPIES).reshape(1, DT_COPIES * SSD_HEADS)),
        "a_log": pad_lanes(jnp.tile(a_log, DT_COPIES).reshape(1, DT_COPIES * SSD_HEADS)),
        "d_skip": jnp.repeat(d_skip, HEAD_DIM).reshape(1, SSD_DIM),
        "g_ssd_out": g_ssd_out.reshape(1, SSD_DIM),
        "attn_sinks": attn_sinks,
        "g_att_out": g_att_out.reshape(1, ATT_DIM),
        "w_out": w_out.astype(BF16),
        "g_ffn": g_ffn.reshape(1, D_MODEL),
        "w_route": pad_lanes(jnp.concatenate([w_route_expert, w_route_group], axis=1)).astype(BF16),
        "b_route": pad_lanes(jnp.concatenate([b_route_expert, b_route_group]).reshape(1, n_route)),
        "w_gate": w_gate, "w_up": w_up, "w_down": w_down,
    }


def _segment(x3, w, mk, mv, tail, h0t, kprev, vprev, L, tb_ssd, tb_att, tm, from_cache, sb=1):
    S, Ls, _ = x3.shape
    x = x3.reshape(S * Ls, D_MODEL)
    z, xbc, q, k, v, dt = _project(x, w, 1024 if x.shape[0] > 1024 else tm)
    r3 = lambda a: a.reshape(S, Ls, a.shape[-1])
    xbc3, k3, v3 = r3(xbc), r3(k), r3(v)
    y_ssd, h_new = _ssd(xbc3, r3(z), r3(dt), tail, h0t, w, L, tb_ssd, sb)
    if not from_cache:
        kprev, vprev = k3, v3
    y_att = _attention(r3(q), k3, v3, kprev, vprev, mk, mv, w, L, tb_att, from_cache, sb)
    part = _merge(x, y_ssd.reshape(S * Ls, SSD_DIM), y_att.reshape(S * Ls, ATT_DIM), w,
                  1024 if x.shape[0] > 1024 else tm)
    return part, xbc3, h_new, k3, v3


def _state_to_kernel(h):
    return h.reshape(h.shape[0], SSD_DIM, D_STATE)


def _state_from_kernel(hk):
    return hk.reshape(hk.shape[0], SSD_HEADS, HEAD_DIM, D_STATE)


def kernel(x_prompt, x_sample, cache_conv, state_ssd, cache_k, cache_v, meta_tokens, g_mix, w_in, conv_w, conv_b, dt_bias, a_log, d_skip, g_ssd_out, g_q, g_k, attn_sinks, g_att_out, w_out, g_ffn, w_route_group, b_route_group, w_route_expert, b_route_expert, w_gate, w_up, w_down):
    w = _prepare_weights(g_mix[0], w_in[0], conv_w[0], conv_b[0], dt_bias[0], a_log[0], d_skip[0],
                         g_ssd_out[0], g_q[0], g_k[0], attn_sinks[0], g_att_out[0], w_out[0], g_ffn[0],
                         w_route_group[0], b_route_group[0], w_route_expert[0], b_route_expert[0],
                         w_gate[0], w_up[0], w_down[0])
    n_b = x_sample.shape[0]
    n_dec = x_sample.shape[1]

    _, m_xbc, _, mk, mv, m_dt = _project(meta_tokens, w, N_META)
    zero_tail = jnp.zeros((1, CONV_WIDTH - 1, CONV_DIM), F32)
    zero_state = jnp.zeros((1, SSD_DIM, D_STATE), F32)
    m_xbc3 = m_xbc.reshape(1, N_META, CONV_DIM)
    _, m_state = _ssd(m_xbc3, jnp.zeros((1, N_META, SSD_DIM), F32), m_dt.reshape(1, N_META, LANES),
                      zero_tail, zero_state, w, N_META, N_META)
    m_tail = m_xbc3[:, N_META - (CONV_WIDTH - 1):]

    part_p, xbc_p, st_p, k_p, v_p = _segment(
        x_prompt, w, mk, mv, m_tail, m_state, None, None,
        L=CHUNK, tb_ssd=512, tb_att=512, tm=512, from_cache=False)
    part_s, xbc_s, st_s, k_s, v_s = _segment(
        x_sample, w, mk, mv, cache_conv[0], _state_to_kernel(state_ssd[0]),
        cache_k[0].reshape(n_b, WINDOW, KV_DIM), cache_v[0].reshape(n_b, WINDOW, KV_DIM),
        L=n_dec, tb_ssd=n_dec, tb_att=n_dec, tm=512, from_cache=True, sb=8)
    yp, ys = _moe([part_p, part_s], w, tm=512)
    yp = yp.reshape(x_prompt.shape)
    ys = ys.reshape(x_sample.shape)

    heads = lambda a, rows: a.reshape(a.shape[0], rows, ATT_KV_HEADS, HEAD_DIM)[None]
    return (yp, ys,
            xbc_p[:, -(CONV_WIDTH - 1):][None],
            _state_from_kernel(st_p)[None],
            heads(k_p[:, -WINDOW:], WINDOW), heads(v_p[:, -WINDOW:], WINDOW),
            xbc_s[:, -(CONV_WIDTH - 1):][None],
            _state_from_kernel(st_s)[None],
            heads(k_s, n_dec), heads(v_s, n_dec))
```

```python
import functools
import math

import numpy as np
import jax
import jax.numpy as jnp
from jax import lax
from jax.experimental import pallas as pl
from jax.experimental.pallas import tpu as pltpu

D_MODEL = 1024
CHUNK = 64
N_META = 16
HEAD_DIM = 64
ATT_HEADS = 16
ATT_KV_HEADS = 4
ATT_REP = ATT_HEADS // ATT_KV_HEADS
ATT_DIM = ATT_HEADS * HEAD_DIM
KV_DIM = ATT_KV_HEADS * HEAD_DIM
WINDOW = 128
SSD_HEADS = 16
SSD_DIM = SSD_HEADS * HEAD_DIM
SSD_GROUPS = 2
GROUP_DIM = SSD_DIM // SSD_GROUPS
D_STATE = 128
CONV_WIDTH = 4
CONV_DIM = SSD_DIM + 2 * SSD_GROUPS * D_STATE
N_EXPERT_GROUPS = 4
EXPERTS_PER_GROUP = 8
N_EXPERTS = N_EXPERT_GROUPS * EXPERTS_PER_GROUP
EXPERT_FF = D_MODEL // 4
EPS = 1e-6

LANES = 128
SUBLANES = 8
MXU_DIM = 256
DT_COPIES = 3
KEY_SPAN = 256
VMEM_LIMIT = 48 * 1024 * 1024
VMEM_LIMIT_WIDE = 56 * 1024 * 1024

F32 = jnp.float32
BF16 = jnp.bfloat16
NEG_INF = float("-inf")
LOG2E = math.log2(math.e)


def _dot(a, b):
    return jnp.dot(a, b, preferred_element_type=F32)


def _dot_nt(a, b):
    return lax.dot_general(a, b, (((1,), (1,)), ((), ())), preferred_element_type=F32)


def _dot_tn(a, b):
    return lax.dot_general(a, b, (((0,), (0,)), ((), ())), preferred_element_type=F32)


def _split3(x):
    hi = x.astype(BF16)
    r = x - hi.astype(F32)
    mid = r.astype(BF16)
    lo = (r - mid.astype(F32)).astype(BF16)
    return hi, mid, lo


def _pack3(x):
    hi, mid, lo = _split3(x)
    lane = lax.broadcasted_iota(jnp.int32, x.shape, 1)
    packed = jnp.where(lane < SSD_HEADS, hi.astype(F32),
                       jnp.where(lane < 2 * SSD_HEADS, mid.astype(F32), lo.astype(F32)))
    return packed.astype(BF16)


def _sel_dot(sel, x):
    hi, mid, lo = _split3(x)
    return _dot(sel, hi) + _dot(sel, mid) + _dot(sel, lo)


def _silu(x):
    return x * (1.0 / (1.0 + jnp.exp2(x * (-LOG2E))))


def _rms(x, g):
    ms = jnp.mean(x * x, axis=-1, keepdims=True)
    return x * lax.rsqrt(ms + EPS) * g


def _const_spec(shape):
    n = len(shape)
    return pl.BlockSpec(shape, lambda *_: (0,) * n, pipeline_mode=pl.Buffered(1))


def _proj_kernel(x_ref, gmix_ref, wqk_ref, wrest_ref, gqk_ref, bd_ref,
                 z_ref, xbc_ref, q_ref, k_ref, v_ref, dt_ref):
    tm = x_ref.shape[0]
    xn = _rms(x_ref[...], gmix_ref[...]).astype(BF16)
    qk = _dot(xn, wqk_ref[...])
    n_slices = (ATT_DIM + KV_DIM) // KV_DIM
    sq = jnp.concatenate([qk[:, j * KV_DIM:(j + 1) * KV_DIM] for j in range(n_slices)], axis=0)
    ms = _dot((sq * sq).astype(BF16), bd_ref[...]) * (1.0 / HEAD_DIM)
    inv = lax.rsqrt(ms + EPS)
    for j in range(n_slices):
        sl = slice(j * KV_DIM, (j + 1) * KV_DIM)
        normed = qk[:, sl] * inv[j * tm:(j + 1) * tm, :] * gqk_ref[:, sl]
        if j < ATT_DIM // KV_DIM:
            q_ref[:, sl] = (normed * (HEAD_DIM ** -0.5 * LOG2E)).astype(BF16)
        else:
            k_ref[...] = normed
    rest = _dot(xn, wrest_ref[...])
    cuts = np.cumsum([0, SSD_DIM, CONV_DIM, KV_DIM, LANES])
    for ref, lo_c, hi_c in zip((z_ref, xbc_ref, v_ref, dt_ref), cuts[:-1], cuts[1:]):
        ref[...] = rest[:, lo_c:hi_c]


def _project(x, w, tm):
    t = x.shape[0]
    assert t % tm == 0
    row = lambda n: pl.BlockSpec((tm, n), lambda i: (i, 0))
    ins = [x, w["g_mix"], w["w_qk"], w["w_rest"], w["g_qk"], w["bd"]]
    in_specs = [row(D_MODEL)] + [_const_spec(a.shape) for a in ins[1:]]
    out_dims = (SSD_DIM, CONV_DIM, ATT_DIM, KV_DIM, KV_DIM, LANES)
    out_dtypes = (F32, F32, BF16, F32, F32, F32)
    return pl.pallas_call(
        _proj_kernel,
        grid=(t // tm,),
        in_specs=in_specs,
        out_specs=[row(n) for n in out_dims],
        out_shape=[jax.ShapeDtypeStruct((t, n), d) for n, d in zip(out_dims, out_dtypes)],
        compiler_params=pltpu.CompilerParams(
            dimension_semantics=("arbitrary",),
            vmem_limit_bytes=VMEM_LIMIT_WIDE if tm > 512 else VMEM_LIMIT),
        name="projection",
    )(*ins)


N_SSD_STREAM_INPUTS = 5
N_SSD_CONSTS = 9


def _ssd_kernel(*refs, L, TB, SB):
    streamed = lambda k: k < N_SSD_STREAM_INPUTS or k >= N_SSD_STREAM_INPUTS + N_SSD_CONSTS
    for s in range(SB):
        _ssd_stream(*[r.at[s] if streamed(k) else r for k, r in enumerate(refs)], L=L, TB=TB)


def _ssd_stream(xbc_ref, z_ref, dt_ref, tail_ref, h0_ref, convw_ref, convb_ref, dtb_ref,
                alog_ref, dskip_ref, gout_ref, ehead_ref, epos_ref, tri_ref,
                y_ref, hout_ref, buf_ref, act_ref, dts_ref, ys_ref, st_ref, *, L, TB):
    hp = LANES // L
    n_lane_tiles = SSD_HEADS // hp
    hq = MXU_DIM // L
    n_tiles = SSD_HEADS // hq
    tile_w = hq * HEAD_DIM
    b = pl.program_id(1)
    pad = SUBLANES - (CONV_WIDTH - 1)

    @pl.when(b == 0)
    def _():
        buf_ref[0:SUBLANES, :] = jnp.zeros((SUBLANES, CONV_DIM), F32)
        buf_ref[pad:SUBLANES, :] = tail_ref[...]
        st_ref[...] = h0_ref[...].T

    buf_ref[SUBLANES:SUBLANES + TB, :] = xbc_ref[...]
    rows = buf_ref[...]
    acc = convb_ref[...] + convw_ref[CONV_WIDTH - 1:CONV_WIDTH, :] * rows[SUBLANES:, :]
    for j in range(CONV_WIDTH - 1):
        shifted = pltpu.roll(rows, CONV_WIDTH - 1 - j, axis=0)[SUBLANES:, :]
        acc = acc + convw_ref[j:j + 1, :] * shifted
    act_ref[...] = _silu(acc)
    buf_ref[pad:SUBLANES, :] = buf_ref[TB + pad:TB + SUBLANES, :]

    dtx = dt_ref[...] + dtb_ref[...]
    dts_ref[...] = jnp.maximum(dtx, 0.0) + jnp.log(1.0 + jnp.exp(-jnp.abs(dtx)))

    a_row = -jnp.exp(alog_ref[...]) * LOG2E
    ehead = ehead_ref[...]
    epos = epos_ref[...]
    tri = tri_ref[...]
    dskip = dskip_ref[...]
    row_i = lax.broadcasted_iota(jnp.int32, (L, n_lane_tiles * LANES), 0)
    col_i = lax.broadcasted_iota(jnp.int32, (L, n_lane_tiles * LANES), 1)
    pos_i = col_i % L
    diag_mask = row_i == pos_i
    causal_mask = row_i >= pos_i
    colhead = lax.broadcasted_iota(jnp.int32, (L, tile_w), 1) // HEAD_DIM

    def chunk(c):
        r0 = pl.multiple_of(c * L, L)
        dtc = dts_ref[pl.ds(r0, L), :]
        cs = _sel_dot(tri, dtc * a_row)
        cs16 = _pack3(cs)
        dt_b = _dot(_pack3(dtc), ehead)
        cs_b = _dot(cs16, ehead)
        cs_last_b = cs_b[L - 1:L, :]
        ecs_b = jnp.exp2(cs_b)
        dec_end_b = jnp.exp2(cs_last_b - cs_b)
        bdec_b = jnp.exp2(cs_last_b)
        cs_col = cs_b if L == HEAD_DIM else _dot(cs16, epos)
        cs_row = jnp.sum(jnp.where(diag_mask, cs_col, 0.0), axis=0, keepdims=True)
        lmat = jnp.exp2(jnp.where(causal_mask, cs_col - cs_row, NEG_INF))

        xs = act_ref[pl.ds(r0, L), 0:SSD_DIM]
        xdt_f = xs * dt_b
        xdt = xdt_f.astype(BF16)
        wx = (xdt_f * dec_end_b).astype(BF16)
        st = st_ref[...]
        st16 = st.astype(BF16)

        y_diag = []
        y_off = []
        s_new = []
        cb = [None] * SSD_GROUPS
        for g in range(SSD_GROUPS):
            bg = act_ref[pl.ds(r0, L), SSD_DIM + g * D_STATE:SSD_DIM + (g + 1) * D_STATE].astype(BF16)
            cg = act_ref[pl.ds(r0, L), SSD_DIM + (SSD_GROUPS + g) * D_STATE:
                         SSD_DIM + (SSD_GROUPS + g + 1) * D_STATE].astype(BF16)
            cb[g] = _dot_nt(cg, jnp.concatenate([bg] * hp, axis=0))
            gsl = slice(g * GROUP_DIM, (g + 1) * GROUP_DIM)
            y_off.append(_dot(cg, st16[:, gsl]))
            s_new.append(_dot_tn(bg, wx[:, gsl]))
        lane_tiles = MXU_DIM // LANES
        for t in range(n_tiles):
            cbt = jnp.concatenate(
                [cb[((t * lane_tiles + j) * hp * HEAD_DIM) // GROUP_DIM] for j in range(lane_tiles)], axis=1)
            gmat = (lmat[:, t * MXU_DIM:(t + 1) * MXU_DIM] * cbt).astype(BF16)
            xt = xdt[:, t * tile_w:(t + 1) * tile_w]
            rhs = jnp.concatenate(
                [jnp.where(colhead == hh, xt, jnp.zeros_like(xt)) for hh in range(hq)], axis=0)
            y_diag.append(_dot(gmat, rhs))
        y = (jnp.concatenate(y_diag, axis=1) + jnp.concatenate(y_off, axis=1) * ecs_b
             + dskip * xs)
        ys_ref[pl.ds(r0, L), :] = y
        st_ref[...] = bdec_b * st + jnp.concatenate(s_new, axis=1)

    n_chunks = TB // L
    unroll = n_chunks if n_chunks in (4, 8) else 1

    def chunks(i, carry):
        for u in range(unroll):
            chunk(i * unroll + u)
        return carry

    lax.fori_loop(0, n_chunks // unroll, chunks, 0)

    yg = ys_ref[...] * _silu(z_ref[...])
    y_ref[...] = _rms(yg, gout_ref[...])

    @pl.when(b == pl.num_programs(1) - 1)
    def _():
        hout_ref[...] = st_ref[...].T


def _ssd(xbc, z, dt, tail, h0t, w, L, TB, SB=1):
    S, Ls, _ = xbc.shape
    assert Ls % TB == 0 and TB % L == 0 and LANES % L == 0 and S % SB == 0
    nb = Ls // TB
    assert SB == 1 or nb == 1
    hp = LANES // L
    n_tiles = SSD_HEADS // hp
    per_stream = lambda a: (lambda s, b: (s, 0, 0)) if a.shape[0] == S and S > 1 else (lambda s, b: (0, 0, 0))
    seq = lambda n: pl.BlockSpec((SB, TB, n), lambda s, b: (s, b, 0))
    ehead = np.zeros((LANES, SSD_DIM), np.float32)
    epos = np.zeros((LANES, n_tiles * LANES), np.float32)
    for part in range(DT_COPIES):
        for h in range(SSD_HEADS):
            ehead[part * SSD_HEADS + h, h * HEAD_DIM:(h + 1) * HEAD_DIM] = 1.0
            epos[part * SSD_HEADS + h, h * L:(h + 1) * L] = 1.0
    tri = np.tril(np.ones((L, L), np.float32))
    consts = [w["conv_w"], w["conv_b"], w["dt_bias"], w["a_log"], w["d_skip"], w["g_ssd_out"],
              jnp.asarray(ehead, BF16), jnp.asarray(epos, BF16), jnp.asarray(tri, BF16)]
    in_specs = [seq(CONV_DIM), seq(SSD_DIM), seq(LANES),
                pl.BlockSpec((SB, CONV_WIDTH - 1, CONV_DIM), per_stream(tail)),
                pl.BlockSpec((SB, SSD_DIM, D_STATE), per_stream(h0t))]
    assert len(in_specs) == N_SSD_STREAM_INPUTS and len(consts) == N_SSD_CONSTS
    in_specs += [_const_spec(c.shape) for c in consts]
    return pl.pallas_call(
        functools.partial(_ssd_kernel, L=L, TB=TB, SB=SB),
        grid=(S // SB, nb),
        in_specs=in_specs,
        out_specs=[seq(SSD_DIM), pl.BlockSpec((SB, SSD_DIM, D_STATE), lambda s, b: (s, 0, 0))],
        out_shape=[jax.ShapeDtypeStruct((S, Ls, SSD_DIM), F32),
                   jax.ShapeDtypeStruct((S, SSD_DIM, D_STATE), F32)],
        scratch_shapes=[pltpu.VMEM((SB, SUBLANES + TB, CONV_DIM), F32),
                        pltpu.VMEM((SB, TB, CONV_DIM), F32),
                        pltpu.VMEM((SB, TB, LANES), F32),
                        pltpu.VMEM((SB, TB, SSD_DIM), F32),
                        pltpu.VMEM((SB, D_STATE, SSD_DIM), F32)],
        compiler_params=pltpu.CompilerParams(
            dimension_semantics=("arbitrary", "arbitrary"), vmem_limit_bytes=VMEM_LIMIT),
        name="ssd",
    )(xbc, z, dt, tail, h0t, *consts)


def _split_heads(x):
    lane = lax.broadcasted_iota(jnp.int32, (x.shape[0], LANES), 1)
    low = lane < HEAD_DIM
    lo, hi = [], []
    for t in range(KV_DIM // LANES):
        tile = x[:, t * LANES:(t + 1) * LANES]
        swapped = pltpu.roll(tile, HEAD_DIM, axis=1)
        lo += [jnp.where(low, tile, 0.0), jnp.where(low, swapped, 0.0)]
        hi += [jnp.where(low, 0.0, swapped), jnp.where(low, 0.0, tile)]
    return jnp.concatenate(lo, axis=1).astype(BF16), jnp.concatenate(hi, axis=1).astype(BF16)


def _attn_kernel(*refs, L, TB, SB, from_cache):
    for s in range(SB):
        _attn_stream(*[r if k == 0 or 6 <= k <= 9 else r.at[s] for k, r in enumerate(refs)],
                     L=L, TB=TB, from_cache=from_cache)


def _attn_stream(sink_ref, q_ref, k_ref, v_ref, kp_ref, vp_ref, mk_ref, mv_ref, bias_ref, g_ref,
                 o_ref, k_lo, k_hi, v_lo, v_hi, ybuf, s_buf, e_buf, sink_buf, *, L, TB, from_cache):
    b = pl.program_id(1)
    bufs = (k_lo, k_hi, v_lo, v_hi)
    if from_cache:
        window = _split_heads(kp_ref[...]) + _split_heads(vp_ref[...])
        for buf, val in zip(bufs, window):
            buf[0:WINDOW, :] = val
    else:
        @pl.when(b == 0)
        def _():
            for buf in bufs:
                buf[0:WINDOW, :] = jnp.zeros((WINDOW, buf.shape[1]), BF16)

        @pl.when(b > 0)
        def _():
            for buf in bufs:
                buf[0:WINDOW, :] = buf[TB:TB + WINDOW, :]

    for buf, val in zip(bufs, _split_heads(k_ref[...]) + _split_heads(v_ref[...])):
        buf[WINDOW:WINDOW + TB, :] = val
    n_pad = KEY_SPAN - WINDOW - L - N_META
    pad_rows = jnp.zeros((n_pad, 2 * KV_DIM), BF16)
    tails = [jnp.concatenate([t, pad_rows], axis=0)
             for t in _split_heads(mk_ref[...]) + _split_heads(mv_ref[...])]
    lane = lax.broadcasted_iota(jnp.int32, (2 * L, 2 * KEY_SPAN), 1) % KEY_SPAN
    lane_h = lax.broadcasted_iota(jnp.int32, (L, LANES), 1)
    lane_v = lax.broadcasted_iota(jnp.int32, (KEY_SPAN, LANES), 1)
    ones_lo = jnp.where(lane_v < HEAD_DIM, 1.0, 0.0).astype(BF16)
    ones_hi = jnp.where(lane_v < HEAD_DIM, 0.0, 1.0).astype(BF16)
    n_chunks = TB // L
    pairs_per_group = ATT_REP // 2

    def chunk_keys(c):
        r0 = c * L
        return [jnp.concatenate([buf[r0:r0 + WINDOW, :], buf[r0 + WINDOW:r0 + WINDOW + L, :], tail], axis=0)
                for buf, tail in zip(bufs, tails)]

    for c in range(n_chunks):
        r0 = c * L
        keys = chunk_keys(c)
        for g in range(ATT_KV_HEADS):
            gl = slice(g * LANES, (g + 1) * LANES)
            q4 = jnp.concatenate([q_ref[r0:r0 + L, p * LANES:(p + 1) * LANES]
                                  for p in range(pairs_per_group * g, pairs_per_group * (g + 1))], axis=0)
            s = _dot_nt(q4, jnp.concatenate([keys[0][:, gl], keys[1][:, gl]], axis=0)) + bias_ref[g]
            if not from_cache and r0 < WINDOW:
                n_invalid = WINDOW - (b * TB + r0)
                s = jnp.where(lane < n_invalid, NEG_INF, s)
            s_buf[c * ATT_KV_HEADS + g] = s

    for c in range(n_chunks):
        for pair in range(ATT_HEADS // 2):
            slot = c * ATT_KV_HEADS + pair // pairs_per_group
            rows = slice((pair % pairs_per_group) * L, (pair % pairs_per_group + 1) * L)
            sink_terms = []
            for half in range(2):
                cols = slice(half * KEY_SPAN, (half + 1) * KEY_SPAN)
                s = s_buf[slot, rows, cols]
                sink = sink_ref[2 * pair + half] * LOG2E
                m = jnp.maximum(jnp.max(s, axis=-1, keepdims=True), sink)
                e_buf[slot, rows, cols] = jnp.exp2(s - m).astype(BF16)
                sink_terms.append(jnp.exp2(sink - m))
            sink_buf[c * (ATT_HEADS // 2) + pair] = jnp.where(lane_h < HEAD_DIM, sink_terms[0], sink_terms[1])

    for c in range(n_chunks):
        r0 = c * L
        keys = chunk_keys(c)
        for g in range(ATT_KV_HEADS):
            gl = slice(g * LANES, (g + 1) * LANES)
            vcat = jnp.concatenate([jnp.concatenate([keys[2][:, gl], ones_lo], axis=1),
                                    jnp.concatenate([keys[3][:, gl], ones_hi], axis=1)], axis=0)
            ov = _dot(e_buf[c * ATT_KV_HEADS + g], vcat)
            for j in range(pairs_per_group):
                pair = pairs_per_group * g + j
                den = ov[j * L:(j + 1) * L, LANES:2 * LANES] + sink_buf[c * (ATT_HEADS // 2) + pair]
                ybuf[r0:r0 + L, pair * LANES:(pair + 1) * LANES] = ov[j * L:(j + 1) * L, 0:LANES] * (1.0 / den)
    o_ref[...] = _rms(ybuf[...], g_ref[...])


def _attention(q, k, v, kprev, vprev, mk, mv, w, L, TB, from_cache, SB=1):
    S, Ls, _ = q.shape
    assert Ls % TB == 0 and TB % L == 0 and (from_cache or TB >= WINDOW) and S % SB == 0
    nb = Ls // TB
    assert SB == 1 or (from_cache and nb == 1)
    seq = lambda n: pl.BlockSpec((SB, TB, n), lambda s, b: (s, b, 0))
    if not from_cache:
        kprev = vprev = jnp.zeros((1, WINDOW, KV_DIM), F32)
    prev = pl.BlockSpec((SB, WINDOW, KV_DIM),
                        (lambda s, b: (s, 0, 0)) if from_cache else (lambda s, b: (0, 0, 0)))
    slopes = 2.0 ** (-8.0 * np.arange(1, ATT_HEADS + 1, dtype=np.float64) / ATT_HEADS)
    dist = np.abs(WINDOW + np.arange(L)[:, None] - np.arange(WINDOW + L)[None, :])
    bias = np.full((ATT_HEADS, L, KEY_SPAN), NEG_INF, np.float32)
    bias[:, :, :WINDOW + L] = -slopes[:, None, None] * dist[None] * LOG2E
    bias[:, :, WINDOW + L:WINDOW + L + N_META] = 0.0
    bias = bias.reshape(ATT_KV_HEADS, ATT_REP // 2, 2, L, KEY_SPAN).transpose(0, 1, 3, 2, 4)
    bias = jnp.asarray(bias.reshape(ATT_KV_HEADS, ATT_REP // 2 * L, 2 * KEY_SPAN))
    n_chunks = TB // L
    in_specs = [pl.BlockSpec(memory_space=pltpu.SMEM),
                seq(ATT_DIM), seq(KV_DIM), seq(KV_DIM), prev, prev,
                _const_spec(mk.shape), _const_spec(mv.shape), _const_spec(bias.shape),
                _const_spec(w["g_att_out"].shape)]
    return pl.pallas_call(
        functools.partial(_attn_kernel, L=L, TB=TB, SB=SB, from_cache=from_cache),
        grid=(S // SB, nb),
        in_specs=in_specs,
        out_specs=seq(ATT_DIM),
        out_shape=jax.ShapeDtypeStruct((S, Ls, ATT_DIM), F32),
        scratch_shapes=[pltpu.VMEM((SB, WINDOW + TB, 2 * KV_DIM), BF16) for _ in range(4)]
                       + [pltpu.VMEM((SB, TB, ATT_DIM), F32),
                          pltpu.VMEM((SB, n_chunks * ATT_KV_HEADS, 2 * L, 2 * KEY_SPAN), F32),
                          pltpu.VMEM((SB, n_chunks * ATT_KV_HEADS, 2 * L, 2 * KEY_SPAN), BF16),
                          pltpu.VMEM((SB, n_chunks * ATT_HEADS // 2, L, LANES), F32)],
        compiler_params=pltpu.CompilerParams(
            dimension_semantics=("arbitrary", "arbitrary"),
            vmem_limit_bytes=VMEM_LIMIT_WIDE if nb > 1 else VMEM_LIMIT),
        name="attention",
    )(w["attn_sinks"], q, k, v, kprev, vprev, mk, mv, bias, w["g_att_out"])


TOK_TILE = 512
SEG_ALIGN = 2 * SUBLANES
SORT_ROWS = 2 * TOK_TILE + N_EXPERTS * SEG_ALIGN
XS_COLS = D_MODEL + LANES
META_P1, META_P2, META_G1, META_G2, META_E1 = (N_EXPERTS + j for j in range(5))
PAY_G1, PAY_G2, PAY_E1 = 0, 3, 6


def _merge_kernel(h_ref, ys_ref, ya_ref, wo_ref, gffn_ref, wr_ref, br_ref, tri_ref, upper_ref,
                  h1_ref, xn_ref, meta_ref, cnt_ref, logit_buf):
    i = pl.program_id(0)

    @pl.when(i == 0)
    def _():
        logit_buf[...] = jnp.zeros(logit_buf.shape, F32)

    ycat = jnp.concatenate([ys_ref[...].astype(BF16), ya_ref[...].astype(BF16)], axis=1)
    h1 = h_ref[...] + _dot(ycat, wo_ref[...])
    h1_ref[...] = h1
    xn = _rms(h1, gffn_ref[...]).astype(BF16)
    xn_ref[...] = xn
    all_logits = logit_buf[(i + 1) % 2]
    logit_buf[i % 2] = _dot(xn, wr_ref[...]) + br_ref[...]
    tri = tri_ref[...]
    upper = upper_ref[...]
    lane = lax.broadcasted_iota(jnp.int32, (TOK_TILE, LANES), 1)
    big = jnp.int32(LANES)
    gmask = (lane >= N_EXPERTS) & (lane < N_EXPERTS + N_EXPERT_GROUPS)
    for s in range(all_logits.shape[0] // TOK_TILE):
        rows = slice(s * TOK_TILE, (s + 1) * TOK_TILE)
        logits = all_logits[rows]

        def top1(mask):
            mval = jnp.max(jnp.where(mask, logits, NEG_INF), axis=-1, keepdims=True)
            idx = jnp.min(jnp.where(mask & (logits == mval), lane, big), axis=-1, keepdims=True)
            return mval, idx

        gmax, gidx = top1(gmask)
        gate_g = 1.0 / jnp.sum(jnp.where(gmask, jnp.exp(logits - gmax), 0.0), axis=-1, keepdims=True)
        grp = gidx - N_EXPERTS
        emask = (lane // EXPERTS_PER_GROUP) == grp
        v1, i1 = top1(emask)
        v2, i2 = top1(emask & (lane != i1))
        e2 = jnp.exp(v2 - v1)
        g1 = gate_g / (1.0 + e2)
        g2 = gate_g * e2 / (1.0 + e2)
        oh1 = jnp.where(lane == i1, 1.0, 0.0)
        oh2 = jnp.where(lane == i2, 1.0, 0.0)
        oh = oh1 + oh2
        earlier = _dot(tri, oh.astype(BF16))
        cnt = jnp.sum(oh, axis=0, keepdims=True)
        units = jnp.floor((cnt + (SEG_ALIGN - 1)) * (1.0 / SEG_ALIGN))
        units = jnp.broadcast_to(units, (2 * SUBLANES, LANES)).astype(BF16)
        slot = _dot(units, upper)[0:1, :] * SEG_ALIGN + earlier
        p1 = jnp.sum(oh1 * slot, axis=-1, keepdims=True)
        p2 = jnp.sum(oh2 * slot, axis=-1, keepdims=True)
        meta = jnp.where(lane == META_P1, p1, 0.0)
        meta = jnp.where(lane == META_P2, p2, meta)
        meta = jnp.where(lane == META_G1, g1, meta)
        meta = jnp.where(lane == META_G2, g2, meta)
        meta = jnp.where(lane == META_E1, i1.astype(F32), meta)
        meta_ref[rows, :] = meta
        cnt_ref[s] = jnp.broadcast_to(cnt, (SUBLANES, LANES))


def _merge(h, y_ssd, y_att, w, tm):
    t = h.shape[0]
    assert t % tm == 0 and tm % TOK_TILE == 0
    sub = tm // TOK_TILE
    last = t // tm - 1
    row = lambda n: pl.BlockSpec((tm, n), lambda i: (jnp.minimum(i, last), 0))
    routed = lambda i: jnp.maximum(i - 1, 0)
    tri = jnp.asarray(np.tril(np.ones((TOK_TILE, TOK_TILE), np.float32), -1), BF16)
    upper = jnp.asarray(np.triu(np.ones((LANES, LANES), np.float32), 1), BF16)
    consts = [w["w_out"], w["g_ffn"], w["w_route"], w["b_route"], tri, upper]
    return pl.pallas_call(
        _merge_kernel,
        grid=(t // tm + 1,),
        in_specs=[row(D_MODEL), row(SSD_DIM), row(ATT_DIM)] + [_const_spec(c.shape) for c in consts],
        out_specs=[row(D_MODEL), row(D_MODEL),
                   pl.BlockSpec((tm, LANES), lambda i: (routed(i), 0)),
                   pl.BlockSpec((sub, SUBLANES, LANES), lambda i: (routed(i), 0, 0))],
        scratch_shapes=[pltpu.VMEM((2, tm, LANES), F32)],
        out_shape=[jax.ShapeDtypeStruct((t, D_MODEL), F32),
                   jax.ShapeDtypeStruct((t, D_MODEL), BF16),
                   jax.ShapeDtypeStruct((t, LANES), F32),
                   jax.ShapeDtypeStruct((t // TOK_TILE, SUBLANES, LANES), F32)],
        compiler_params=pltpu.CompilerParams(
            dimension_semantics=("arbitrary",),
            vmem_limit_bytes=VMEM_LIMIT_WIDE if tm > 512 else VMEM_LIMIT),
        name="merge_route",
    )(h, y_ssd, y_att, *consts)


def _meta_col(meta, j):
    lane = lax.broadcasted_iota(jnp.int32, meta.shape, 1)
    return jnp.sum(jnp.where(lane == j, meta, 0.0), axis=-1, keepdims=True)


def _pair_selector(meta):
    rows = lax.broadcasted_iota(jnp.int32, (meta.shape[0], SORT_ROWS), 1)
    p1 = _meta_col(meta, META_P1).astype(jnp.int32)
    p2 = _meta_col(meta, META_P2).astype(jnp.int32)
    return jnp.where(rows == p1, 1.0, jnp.where(rows == p2, 1.0, 0.0)).astype(BF16)


def _segment_dmas(off_ref, cnt_ref, row_ref, b, tile_buf, hbm, sem, to_hbm, wait):
    for e in range(N_EXPERTS):
        k = b * N_EXPERTS + e
        n = pl.multiple_of(cnt_ref[k], SEG_ALIGN)
        v = tile_buf.at[pl.ds(pl.multiple_of(off_ref[k], SEG_ALIGN), n)]
        h = hbm.at[pl.ds(pl.multiple_of(row_ref[k], SEG_ALIGN), n)]
        cp = pltpu.make_async_copy(v, h, sem) if to_hbm else pltpu.make_async_copy(h, v, sem)
        if wait:
            cp.wait()
        else:
            cp.start()


def _token_payload(meta):
    cols = {PAY_E1: _meta_col(meta, META_E1)}
    for base, src in ((PAY_G1, META_G1), (PAY_G2, META_G2)):
        for k, part in enumerate(_split3(_meta_col(meta, src))):
            cols[base + k] = part.astype(F32)
    lane = lax.broadcasted_iota(jnp.int32, (meta.shape[0], LANES), 1)
    out = jnp.zeros((meta.shape[0], LANES), F32)
    for k, c in cols.items():
        out = jnp.where(lane == k, c, out)
    return out.astype(BF16)


def _dispatch_kernel(off_ref, cnt_ref, row_ref, tail_row_ref, tail_cnt_ref, xn_ref, meta_ref, *rest,
                     tile0, first):
    xs_ref, sort_buf, zero_buf, sem, tail_sem = rest if first else rest[1:]
    b = pl.program_id(0)
    nb = pl.num_programs(0)
    slot = b % 2
    seg = functools.partial(_segment_dmas, off_ref, cnt_ref, row_ref, hbm=xs_ref, to_hbm=True)

    def tails(wait):
        def body(e, carry):
            n = pl.multiple_of(tail_cnt_ref[e], SEG_ALIGN)

            @pl.when(n > 0)
            def _():
                cp = pltpu.make_async_copy(
                    zero_buf.at[pl.ds(0, n)],
                    xs_ref.at[pl.ds(pl.multiple_of(tail_row_ref[e], SEG_ALIGN), n)], tail_sem.at[0])
                if wait:
                    cp.wait()
                else:
                    cp.start()
            return carry

        lax.fori_loop(0, N_EXPERTS, body, 0)

    @pl.when(b >= 2)
    def _():
        seg(tile0 + b - 2, sort_buf.at[slot], sem=sem.at[slot], wait=True)

    if first:
        @pl.when(b == 0)
        def _():
            zero_buf[...] = jnp.zeros(zero_buf.shape, BF16)
            tails(wait=False)

    meta = meta_ref[...]
    payload = jnp.concatenate([xn_ref[...], _token_payload(meta)], axis=1)
    sort_buf[slot] = _dot_tn(_pair_selector(meta), payload).astype(BF16)
    seg(tile0 + b, sort_buf.at[slot], sem=sem.at[slot], wait=False)

    @pl.when(b == nb - 1)
    def _():
        @pl.when(b >= 1)
        def _():
            seg(tile0 + b - 1, sort_buf.at[1 - slot], sem=sem.at[1 - slot], wait=True)

        seg(tile0 + b, sort_buf.at[slot], sem=sem.at[slot], wait=True)
        if first:
            tails(wait=True)


def _expert_kernel(tile_expert_ref, n_active_ref, xs_ref, wg_ref, wu_ref, wd_ref, y_ref):
    @pl.when(pl.program_id(0) < n_active_ref[0])
    def _():
        x = xs_ref[:, 0:D_MODEL]
        pay = xs_ref[:, D_MODEL:XS_COLS].astype(F32)
        lane = lax.broadcasted_iota(jnp.int32, pay.shape, 1)
        pick = lambda lo, n: jnp.sum(jnp.where((lane >= lo) & (lane < lo + n), pay, 0.0), axis=-1, keepdims=True)
        expert = tile_expert_ref[pl.program_id(0)].astype(F32)
        gate = jnp.where(pick(PAY_E1, 1) == expert, pick(PAY_G1, 3), pick(PAY_G2, 3))
        gu = _dot(x, jnp.concatenate([wg_ref[...].astype(BF16), wu_ref[...].astype(BF16)], axis=1))
        hid = _silu(gu[:, 0:EXPERT_FF]) * gu[:, EXPERT_FF:2 * EXPERT_FF]
        y_ref[...] = (gate * _dot(hid.astype(BF16), wd_ref[...].astype(BF16))).astype(BF16)


def _combine_kernel(off_ref, cnt_ref, row_ref, h1_ref, meta_ref, y_ref, o_ref, y_buf, sem, *, tile0):
    b = pl.program_id(0)
    nb = pl.num_programs(0)
    slot = b % 2
    seg = functools.partial(_segment_dmas, off_ref, cnt_ref, row_ref, hbm=y_ref, to_hbm=False)

    @pl.when(b == 0)
    def _():
        y_buf[...] = jnp.zeros(y_buf.shape, BF16)
        seg(tile0 + b, y_buf.at[slot], sem=sem.at[slot], wait=False)

    @pl.when(b + 1 < nb)
    def _():
        seg(tile0 + b + 1, y_buf.at[1 - slot], sem=sem.at[1 - slot], wait=False)

    seg(tile0 + b, y_buf.at[slot], sem=sem.at[slot], wait=True)
    o_ref[...] = h1_ref[...] + _dot(_pair_selector(meta_ref[...]), y_buf[slot, 0:SORT_ROWS, :])


def _moe(parts, w, tm):
    i32 = jnp.int32
    n_tiles = [p[0].shape[0] // TOK_TILE for p in parts]
    nb = sum(n_tiles)
    t = nb * TOK_TILE
    cnt = jnp.concatenate([p[3][:, 0, :N_EXPERTS] for p in parts], axis=0).astype(i32)
    cnt_al = (cnt + (SEG_ALIGN - 1)) // SEG_ALIGN * SEG_ALIGN
    off = jnp.cumsum(cnt_al, axis=1) - cnt_al
    tot = jnp.sum(cnt_al, axis=0)
    tot_tm = (tot + (tm - 1)) // tm * tm
    start = jnp.cumsum(tot_tm) - tot_tm
    row = start[None, :] + jnp.cumsum(cnt_al, axis=0) - cnt_al
    tile_ends = jnp.cumsum(tot_tm // tm)
    max_rows = 2 * t + (SEG_ALIGN - 1) * min(N_EXPERTS * nb, 2 * t) + N_EXPERTS * (tm - SEG_ALIGN)
    nt = -(-max_rows // tm)
    tile_expert = jnp.sum((jnp.arange(nt, dtype=i32)[:, None] >= tile_ends[None, :]).astype(i32), axis=1)
    tile_expert = jnp.minimum(tile_expert, N_EXPERTS - 1)
    n_active = tile_ends[-1:].astype(i32)
    empty = cnt_al == 0
    flat = lambda a: a.reshape(-1).astype(i32)
    n_rows = flat(jnp.maximum(cnt_al, SEG_ALIGN))
    e_idx = jnp.arange(N_EXPERTS, dtype=i32)[None, :]
    spare = nt * tm + ((jnp.arange(nb, dtype=i32)[:, None] % 2) * N_EXPERTS + e_idx) * SEG_ALIGN
    dispatch_tables = [flat(off), n_rows, flat(jnp.where(empty, spare, row))]
    combine_tables = [flat(jnp.where(empty, SORT_ROWS + e_idx * SEG_ALIGN, off)), n_rows,
                      flat(jnp.where(empty, 0, row))]
    n_spare = 2 * N_EXPERTS * SEG_ALIGN
    tail_tables = [(start + tot).astype(i32), (tot_tm - tot).astype(i32)]

    tok = lambda n: pl.BlockSpec((TOK_TILE, n), lambda i, *_: (i, 0))
    hbm = pl.BlockSpec(memory_space=pl.ANY)
    n_prefetch = len(dispatch_tables) + len(tail_tables)
    xs = None
    tile0 = 0
    for (_, xn, meta, _), n in zip(parts, n_tiles):
        first = xs is None
        xs = pl.pallas_call(
            functools.partial(_dispatch_kernel, tile0=tile0, first=first),
            grid_spec=pltpu.PrefetchScalarGridSpec(
                num_scalar_prefetch=n_prefetch, grid=(n,),
                in_specs=[tok(D_MODEL), tok(LANES)] + ([] if first else [hbm]),
                out_specs=hbm,
                scratch_shapes=[pltpu.VMEM((2, SORT_ROWS, XS_COLS), BF16),
                                pltpu.VMEM((tm, XS_COLS), BF16),
                                pltpu.SemaphoreType.DMA((2,)),
                                pltpu.SemaphoreType.DMA((1,))]),
            out_shape=jax.ShapeDtypeStruct((nt * tm + n_spare, XS_COLS), BF16),
            input_output_aliases={} if first else {n_prefetch + 2: 0},
            compiler_params=pltpu.CompilerParams(
                dimension_semantics=("arbitrary",), vmem_limit_bytes=VMEM_LIMIT),
            name="dispatch",
        )(*dispatch_tables, *tail_tables, xn, meta, *([] if first else [xs]))
        tile0 += n

    act = lambda i, te, na: jnp.minimum(i, na[0] - 1)
    y = pl.pallas_call(
        _expert_kernel,
        grid_spec=pltpu.PrefetchScalarGridSpec(
            num_scalar_prefetch=2, grid=(nt,),
            in_specs=[pl.BlockSpec((tm, XS_COLS), lambda i, te, na: (act(i, te, na), 0)),
                      pl.BlockSpec((None, D_MODEL, EXPERT_FF), lambda i, te, na: (te[act(i, te, na)], 0, 0)),
                      pl.BlockSpec((None, D_MODEL, EXPERT_FF), lambda i, te, na: (te[act(i, te, na)], 0, 0)),
                      pl.BlockSpec((None, EXPERT_FF, D_MODEL), lambda i, te, na: (te[act(i, te, na)], 0, 0))],
            out_specs=pl.BlockSpec((tm, D_MODEL), lambda i, te, na: (act(i, te, na), 0))),
        out_shape=jax.ShapeDtypeStruct((nt * tm, D_MODEL), BF16),
        compiler_params=pltpu.CompilerParams(
            dimension_semantics=("arbitrary",), vmem_limit_bytes=VMEM_LIMIT),
        name="experts",
    )(tile_expert, n_active, xs, w["w_gate"], w["w_up"], w["w_down"])

    outs = []
    tile0 = 0
    for (h1, _, meta, _), n in zip(parts, n_tiles):
        outs.append(pl.pallas_call(
            functools.partial(_combine_kernel, tile0=tile0),
            grid_spec=pltpu.PrefetchScalarGridSpec(
                num_scalar_prefetch=len(combine_tables), grid=(n,),
                in_specs=[tok(D_MODEL), tok(LANES), hbm],
                out_specs=tok(D_MODEL),
                scratch_shapes=[pltpu.VMEM((2, SORT_ROWS + N_EXPERTS * SEG_ALIGN, D_MODEL), BF16),
                                pltpu.SemaphoreType.DMA((2,))]),
            out_shape=jax.ShapeDtypeStruct((n * TOK_TILE, D_MODEL), F32),
            compiler_params=pltpu.CompilerParams(
                dimension_semantics=("arbitrary",), vmem_limit_bytes=VMEM_LIMIT),
            name="combine",
        )(*combine_tables, h1, meta, y))
        tile0 += n
    return outs


def _prepare_weights(g_mix, w_in, conv_w, conv_b, dt_bias, a_log, d_skip, g_ssd_out, g_q, g_k,
                     attn_sinks, g_att_out, w_out, g_ffn, w_route_group, b_route_group,
                     w_route_expert, b_route_expert, w_gate, w_up, w_down):
    cuts = np.cumsum([0, SSD_DIM, CONV_DIM, SSD_HEADS, ATT_DIM, KV_DIM, KV_DIM])
    seg = lambda i: w_in[:, cuts[i]:cuts[i + 1]]
    pad_lanes = lambda a: jnp.pad(a, ((0, 0), (0, LANES - a.shape[1])))
    bd = np.kron(np.eye(KV_DIM // HEAD_DIM, dtype=np.float32), np.ones((HEAD_DIM, HEAD_DIM), np.float32))
    n_route = N_EXPERTS + N_EXPERT_GROUPS
    return {
        "g_mix": g_mix.reshape(1, D_MODEL),
        "w_qk": jnp.concatenate([seg(3), seg(4)], axis=1).astype(BF16),
        "w_rest": jnp.concatenate([seg(0), seg(1), seg(5), pad_lanes(jnp.tile(seg(2), (1, DT_COPIES)))],
                                  axis=1).astype(BF16),
        "g_qk": jnp.concatenate([jnp.tile(g_q, ATT_HEADS), jnp.tile(g_k, ATT_KV_HEADS)]).reshape(1, ATT_DIM + KV_DIM),
        "bd": jnp.asarray(bd, BF16),
        "conv_w": conv_w, "conv_b": conv_b.reshape(1, CONV_DIM),
        "dt_bias": pad_lanes(jnp.tile(dt_bias, DT_COPIES).reshape(1, DT_COPIES * SSD_HEADS)),
        "a_log": pad_lanes(jnp.tile(a_log, DT_COPIES).reshape(1, DT_COPIES * SSD_HEADS)),
        "d_skip": jnp.repeat(d_skip, HEAD_DIM).reshape(1, SSD_DIM),
        "g_ssd_out": g_ssd_out.reshape(1, SSD_DIM),
        "attn_sinks": attn_sinks,
        "g_att_out": g_att_out.reshape(1, ATT_DIM),
        "w_out": w_out.astype(BF16),
        "g_ffn": g_ffn.reshape(1, D_MODEL),
        "w_route": pad_lanes(jnp.concatenate([w_route_expert, w_route_group], axis=1)).astype(BF16),
        "b_route": pad_lanes(jnp.concatenate([b_route_expert, b_route_group]).reshape(1, n_route)),
        "w_gate": w_gate, "w_up": w_up, "w_down": w_down,
    }


def _segment(x3, w, mk, mv, tail, h0t, kprev, vprev, L, tb_ssd, tb_att, tm, from_cache, sb=1):
    S, Ls, _ = x3.shape
    x = x3.reshape(S * Ls, D_MODEL)
    z, xbc, q, k, v, dt = _project(x, w, 1024 if x.shape[0] > 1024 else tm)
    r3 = lambda a: a.reshape(S, Ls, a.shape[-1])
    xbc3, k3, v3 = r3(xbc), r3(k), r3(v)
    y_ssd, h_new = _ssd(xbc3, r3(z), r3(dt), tail, h0t, w, L, tb_ssd, sb)
    if not from_cache:
        kprev, vprev = k3, v3
    y_att = _attention(r3(q), k3, v3, kprev, vprev, mk, mv, w, L, tb_att, from_cache, sb)
    part = _merge(x, y_ssd.reshape(S * Ls, SSD_DIM), y_att.reshape(S * Ls, ATT_DIM), w,
                  1024 if x.shape[0] > 1024 else tm)
    return part, xbc3, h_new, k3, v3


def _state_to_kernel(h):
    return h.reshape(h.shape[0], SSD_DIM, D_STATE)


def _state_from_kernel(hk):
    return hk.reshape(hk.shape[0], SSD_HEADS, HEAD_DIM, D_STATE)


def kernel(x_prompt, x_sample, cache_conv, state_ssd, cache_k, cache_v, meta_tokens, g_mix, w_in, conv_w, conv_b, dt_bias, a_log, d_skip, g_ssd_out, g_q, g_k, attn_sinks, g_att_out, w_out, g_ffn, w_route_group, b_route_group, w_route_expert, b_route_expert, w_gate, w_up, w_down):
    w = _prepare_weights(g_mix[0], w_in[0], conv_w[0], conv_b[0], dt_bias[0], a_log[0], d_skip[0],
                         g_ssd_out[0], g_q[0], g_k[0], attn_sinks[0], g_att_out[0], w_out[0], g_ffn[0],
                         w_route_group[0], b_route_group[0], w_route_expert[0], b_route_expert[0],
                         w_gate[0], w_up[0], w_down[0])
    n_b = x_sample.shape[0]
    n_dec = x_sample.shape[1]

    _, m_xbc, _, mk, mv, m_dt = _project(meta_tokens, w, N_META)
    zero_tail = jnp.zeros((1, CONV_WIDTH - 1, CONV_DIM), F32)
    zero_state = jnp.zeros((1, SSD_DIM, D_STATE), F32)
    m_xbc3 = m_xbc.reshape(1, N_META, CONV_DIM)
    _, m_state = _ssd(m_xbc3, jnp.zeros((1, N_META, SSD_DIM), F32), m_dt.reshape(1, N_META, LANES),
                      zero_tail, zero_state, w, N_META, N_META)
    m_tail = m_xbc3[:, N_META - (CONV_WIDTH - 1):]

    part_p, xbc_p, st_p, k_p, v_p = _segment(
        x_prompt, w, mk, mv, m_tail, m_state, None, None,
        L=CHUNK, tb_ssd=512, tb_att=512, tm=512, from_cache=False)
    part_s, xbc_s, st_s, k_s, v_s = _segment(
        x_sample, w, mk, mv, cache_conv[0], _state_to_kernel(state_ssd[0]),
        cache_k[0].reshape(n_b, WINDOW, KV_DIM), cache_v[0].reshape(n_b, WINDOW, KV_DIM),
        L=n_dec, tb_ssd=n_dec, tb_att=n_dec, tm=512, from_cache=True, sb=8)
    yp, ys = _moe([part_p, part_s], w, tm=512)
    yp = yp.reshape(x_prompt.shape)
    ys = ys.reshape(x_sample.shape)

    heads = lambda a, rows: a.reshape(a.shape[0], rows, ATT_KV_HEADS, HEAD_DIM)[None]
    return (yp, ys,
            xbc_p[:, -(CONV_WIDTH - 1):][None],
            _state_from_kernel(st_p)[None],
            heads(k_p[:, -WINDOW:], WINDOW), heads(v_p[:, -WINDOW:], WINDOW),
            xbc_s[:, -(CONV_WIDTH - 1):][None],
            _state_from_kernel(st_s)[None],
            heads(k_s, n_dec), heads(v_s, n_dec))
```

```python
import functools
import math

import numpy as np
import jax
import jax.numpy as jnp
from jax import lax
from jax.experimental import pallas as pl
from jax.experimental.pallas import tpu as pltpu

D_MODEL = 1024
CHUNK = 64
N_META = 16
HEAD_DIM = 64
ATT_HEADS = 16
ATT_KV_HEADS = 4
ATT_REP = ATT_HEADS // ATT_KV_HEADS
ATT_DIM = ATT_HEADS * HEAD_DIM
KV_DIM = ATT_KV_HEADS * HEAD_DIM
WINDOW = 128
SSD_HEADS = 16
SSD_DIM = SSD_HEADS * HEAD_DIM
SSD_GROUPS = 2
GROUP_DIM = SSD_DIM // SSD_GROUPS
D_STATE = 128
CONV_WIDTH = 4
CONV_DIM = SSD_DIM + 2 * SSD_GROUPS * D_STATE
N_EXPERT_GROUPS = 4
EXPERTS_PER_GROUP = 8
N_EXPERTS = N_EXPERT_GROUPS * EXPERTS_PER_GROUP
EXPERT_FF = D_MODEL // 4
EPS = 1e-6

LANES = 128
SUBLANES = 8
MXU_DIM = 256
DT_COPIES = 3
KEY_SPAN = 256
VMEM_LIMIT = 48 * 1024 * 1024
VMEM_LIMIT_WIDE = 56 * 1024 * 1024

F32 = jnp.float32
BF16 = jnp.bfloat16
NEG_INF = float("-inf")
LOG2E = math.log2(math.e)


def _dot(a, b):
    return jnp.dot(a, b, preferred_element_type=F32)


def _dot_nt(a, b):
    return lax.dot_general(a, b, (((1,), (1,)), ((), ())), preferred_element_type=F32)


def _dot_tn(a, b):
    return lax.dot_general(a, b, (((0,), (0,)), ((), ())), preferred_element_type=F32)


def _split3(x):
    hi = x.astype(BF16)
    r = x - hi.astype(F32)
    mid = r.astype(BF16)
    lo = (r - mid.astype(F32)).astype(BF16)
    return hi, mid, lo


def _pack3(x):
    hi, mid, lo = _split3(x)
    lane = lax.broadcasted_iota(jnp.int32, x.shape, 1)
    packed = jnp.where(lane < SSD_HEADS, hi.astype(F32),
                       jnp.where(lane < 2 * SSD_HEADS, mid.astype(F32), lo.astype(F32)))
    return packed.astype(BF16)


def _sel_dot(sel, x):
    hi, mid, lo = _split3(x)
    return _dot(sel, hi) + _dot(sel, mid) + _dot(sel, lo)


def _silu(x):
    return x * (1.0 / (1.0 + jnp.exp2(x * (-LOG2E))))


def _rms(x, g):
    ms = jnp.mean(x * x, axis=-1, keepdims=True)
    return x * lax.rsqrt(ms + EPS) * g


def _const_spec(shape):
    n = len(shape)
    return pl.BlockSpec(shape, lambda *_: (0,) * n, pipeline_mode=pl.Buffered(1))


def _proj_kernel(x_ref, gmix_ref, wqk_ref, wrest_ref, gqk_ref, bd_ref,
                 z_ref, xbc_ref, q_ref, k_ref, v_ref, dt_ref):
    tm = x_ref.shape[0]
    xn = _rms(x_ref[...], gmix_ref[...]).astype(BF16)
    qk = _dot(xn, wqk_ref[...])
    n_slices = (ATT_DIM + KV_DIM) // KV_DIM
    sq = jnp.concatenate([qk[:, j * KV_DIM:(j + 1) * KV_DIM] for j in range(n_slices)], axis=0)
    ms = _dot((sq * sq).astype(BF16), bd_ref[...]) * (1.0 / HEAD_DIM)
    inv = lax.rsqrt(ms + EPS)
    for j in range(n_slices):
        sl = slice(j * KV_DIM, (j + 1) * KV_DIM)
        normed = qk[:, sl] * inv[j * tm:(j + 1) * tm, :] * gqk_ref[:, sl]
        if j < ATT_DIM // KV_DIM:
            q_ref[:, sl] = (normed * (HEAD_DIM ** -0.5 * LOG2E)).astype(BF16)
        else:
            k_ref[...] = normed
    rest = _dot(xn, wrest_ref[...])
    cuts = np.cumsum([0, SSD_DIM, CONV_DIM, KV_DIM, LANES])
    for ref, lo_c, hi_c in zip((z_ref, xbc_ref, v_ref, dt_ref), cuts[:-1], cuts[1:]):
        ref[...] = rest[:, lo_c:hi_c]


def _project(x, w, tm):
    t = x.shape[0]
    assert t % tm == 0
    row = lambda n: pl.BlockSpec((tm, n), lambda i: (i, 0))
    ins = [x, w["g_mix"], w["w_qk"], w["w_rest"], w["g_qk"], w["bd"]]
    in_specs = [row(D_MODEL)] + [_const_spec(a.shape) for a in ins[1:]]
    out_dims = (SSD_DIM, CONV_DIM, ATT_DIM, KV_DIM, KV_DIM, LANES)
    out_dtypes = (F32, F32, BF16, F32, F32, F32)
    return pl.pallas_call(
        _proj_kernel,
        grid=(t // tm,),
        in_specs=in_specs,
        out_specs=[row(n) for n in out_dims],
        out_shape=[jax.ShapeDtypeStruct((t, n), d) for n, d in zip(out_dims, out_dtypes)],
        compiler_params=pltpu.CompilerParams(
            dimension_semantics=("arbitrary",),
            vmem_limit_bytes=VMEM_LIMIT_WIDE if tm > 512 else VMEM_LIMIT),
        name="projection",
    )(*ins)


N_SSD_STREAM_INPUTS = 5
N_SSD_CONSTS = 9


def _ssd_kernel(*refs, L, TB, SB):
    streamed = lambda k: k < N_SSD_STREAM_INPUTS or k >= N_SSD_STREAM_INPUTS + N_SSD_CONSTS
    for s in range(SB):
        _ssd_stream(*[r.at[s] if streamed(k) else r for k, r in enumerate(refs)], L=L, TB=TB)


def _ssd_stream(xbc_ref, z_ref, dt_ref, tail_ref, h0_ref, convw_ref, convb_ref, dtb_ref,
                alog_ref, dskip_ref, gout_ref, ehead_ref, epos_ref, tri_ref,
                y_ref, hout_ref, buf_ref, act_ref, dts_ref, ys_ref, st_ref, *, L, TB):
    hp = LANES // L
    n_lane_tiles = SSD_HEADS // hp
    hq = MXU_DIM // L
    n_tiles = SSD_HEADS // hq
    tile_w = hq * HEAD_DIM
    b = pl.program_id(1)
    pad = SUBLANES - (CONV_WIDTH - 1)

    @pl.when(b == 0)
    def _():
        buf_ref[0:SUBLANES, :] = jnp.zeros((SUBLANES, CONV_DIM), F32)
        buf_ref[pad:SUBLANES, :] = tail_ref[...]
        st_ref[...] = h0_ref[...].T

    buf_ref[SUBLANES:SUBLANES + TB, :] = xbc_ref[...]
    rows = buf_ref[...]
    acc = convb_ref[...] + convw_ref[CONV_WIDTH - 1:CONV_WIDTH, :] * rows[SUBLANES:, :]
    for j in range(CONV_WIDTH - 1):
        shifted = pltpu.roll(rows, CONV_WIDTH - 1 - j, axis=0)[SUBLANES:, :]
        acc = acc + convw_ref[j:j + 1, :] * shifted
    act_ref[...] = _silu(acc)
    buf_ref[pad:SUBLANES, :] = buf_ref[TB + pad:TB + SUBLANES, :]

    dtx = dt_ref[...] + dtb_ref[...]
    dts_ref[...] = jnp.maximum(dtx, 0.0) + jnp.log(1.0 + jnp.exp(-jnp.abs(dtx)))

    a_row = -jnp.exp(alog_ref[...]) * LOG2E
    ehead = ehead_ref[...]
    epos = epos_ref[...]
    tri = tri_ref[...]
    dskip = dskip_ref[...]
    row_i = lax.broadcasted_iota(jnp.int32, (L, n_lane_tiles * LANES), 0)
    col_i = lax.broadcasted_iota(jnp.int32, (L, n_lane_tiles * LANES), 1)
    pos_i = col_i % L
    diag_mask = row_i == pos_i
    causal_mask = row_i >= pos_i
    colhead = lax.broadcasted_iota(jnp.int32, (L, tile_w), 1) // HEAD_DIM

    def chunk(c):
        r0 = pl.multiple_of(c * L, L)
        dtc = dts_ref[pl.ds(r0, L), :]
        cs = _sel_dot(tri, dtc * a_row)
        cs16 = _pack3(cs)
        dt_b = _dot(_pack3(dtc), ehead)
        cs_b = _dot(cs16, ehead)
        cs_last_b = cs_b[L - 1:L, :]
        ecs_b = jnp.exp2(cs_b)
        dec_end_b = jnp.exp2(cs_last_b - cs_b)
        bdec_b = jnp.exp2(cs_last_b)
        cs_col = cs_b if L == HEAD_DIM else _dot(cs16, epos)
        cs_row = jnp.sum(jnp.where(diag_mask, cs_col, 0.0), axis=0, keepdims=True)
        lmat = jnp.exp2(jnp.where(causal_mask, cs_col - cs_row, NEG_INF))

        xs = act_ref[pl.ds(r0, L), 0:SSD_DIM]
        xdt_f = xs * dt_b
        xdt = xdt_f.astype(BF16)
        wx = (xdt_f * dec_end_b).astype(BF16)
        st = st_ref[...]
        st16 = st.astype(BF16)

        y_diag = []
        y_off = []
        s_new = []
        cb = [None] * SSD_GROUPS
        for g in range(SSD_GROUPS):
            bg = act_ref[pl.ds(r0, L), SSD_DIM + g * D_STATE:SSD_DIM + (g + 1) * D_STATE].astype(BF16)
            cg = act_ref[pl.ds(r0, L), SSD_DIM + (SSD_GROUPS + g) * D_STATE:
                         SSD_DIM + (SSD_GROUPS + g + 1) * D_STATE].astype(BF16)
            cb[g] = _dot_nt(cg, jnp.concatenate([bg] * hp, axis=0))
            gsl = slice(g * GROUP_DIM, (g + 1) * GROUP_DIM)
            y_off.append(_dot(cg, st16[:, gsl]))
            s_new.append(_dot_tn(bg, wx[:, gsl]))
        lane_tiles = MXU_DIM // LANES
        for t in range(n_tiles):
            cbt = jnp.concatenate(
                [cb[((t * lane_tiles + j) * hp * HEAD_DIM) // GROUP_DIM] for j in range(lane_tiles)], axis=1)
            gmat = (lmat[:, t * MXU_DIM:(t + 1) * MXU_DIM] * cbt).astype(BF16)
            xt = xdt[:, t * tile_w:(t + 1) * tile_w]
            rhs = jnp.concatenate(
                [jnp.where(colhead == hh, xt, jnp.zeros_like(xt)) for hh in range(hq)], axis=0)
            y_diag.append(_dot(gmat, rhs))
        y = (jnp.concatenate(y_diag, axis=1) + jnp.concatenate(y_off, axis=1) * ecs_b
             + dskip * xs)
        ys_ref[pl.ds(r0, L), :] = y
        st_ref[...] = bdec_b * st + jnp.concatenate(s_new, axis=1)

    n_chunks = TB // L
    unroll = n_chunks if n_chunks in (4, 8) else 1

    def chunks(i, carry):
        for u in range(unroll):
            chunk(i * unroll + u)
        return carry

    lax.fori_loop(0, n_chunks // unroll, chunks, 0)

    yg = ys_ref[...] * _silu(z_ref[...])
    y_ref[...] = _rms(yg, gout_ref[...])

    @pl.when(b == pl.num_programs(1) - 1)
    def _():
        hout_ref[...] = st_ref[...].T


def _ssd(xbc, z, dt, tail, h0t, w, L, TB, SB=1):
    S, Ls, _ = xbc.shape
    assert Ls % TB == 0 and TB % L == 0 and LANES % L == 0 and S % SB == 0
    nb = Ls // TB
    assert SB == 1 or nb == 1
    hp = LANES // L
    n_tiles = SSD_HEADS // hp
    per_stream = lambda a: (lambda s, b: (s, 0, 0)) if a.shape[0] == S and S > 1 else (lambda s, b: (0, 0, 0))
    seq = lambda n: pl.BlockSpec((SB, TB, n), lambda s, b: (s, b, 0))
    ehead = np.zeros((LANES, SSD_DIM), np.float32)
    epos = np.zeros((LANES, n_tiles * LANES), np.float32)
    for part in range(DT_COPIES):
        for h in range(SSD_HEADS):
            ehead[part * SSD_HEADS + h, h * HEAD_DIM:(h + 1) * HEAD_DIM] = 1.0
            epos[part * SSD_HEADS + h, h * L:(h + 1) * L] = 1.0
    tri = np.tril(np.ones((L, L), np.float32))
    consts = [w["conv_w"], w["conv_b"], w["dt_bias"], w["a_log"], w["d_skip"], w["g_ssd_out"],
              jnp.asarray(ehead, BF16), jnp.asarray(epos, BF16), jnp.asarray(tri, BF16)]
    in_specs = [seq(CONV_DIM), seq(SSD_DIM), seq(LANES),
                pl.BlockSpec((SB, CONV_WIDTH - 1, CONV_DIM), per_stream(tail)),
                pl.BlockSpec((SB, SSD_DIM, D_STATE), per_stream(h0t))]
    assert len(in_specs) == N_SSD_STREAM_INPUTS and len(consts) == N_SSD_CONSTS
    in_specs += [_const_spec(c.shape) for c in consts]
    return pl.pallas_call(
        functools.partial(_ssd_kernel, L=L, TB=TB, SB=SB),
        grid=(S // SB, nb),
        in_specs=in_specs,
        out_specs=[seq(SSD_DIM), pl.BlockSpec((SB, SSD_DIM, D_STATE), lambda s, b: (s, 0, 0))],
        out_shape=[jax.ShapeDtypeStruct((S, Ls, SSD_DIM), F32),
                   jax.ShapeDtypeStruct((S, SSD_DIM, D_STATE), F32)],
        scratch_shapes=[pltpu.VMEM((SB, SUBLANES + TB, CONV_DIM), F32),
                        pltpu.VMEM((SB, TB, CONV_DIM), F32),
                        pltpu.VMEM((SB, TB, LANES), F32),
                        pltpu.VMEM((SB, TB, SSD_DIM), F32),
                        pltpu.VMEM((SB, D_STATE, SSD_DIM), F32)],
        compiler_params=pltpu.CompilerParams(
            dimension_semantics=("arbitrary", "arbitrary"),
            vmem_limit_bytes=VMEM_LIMIT_WIDE if nb > 1 else VMEM_LIMIT),
        name="ssd",
    )(xbc, z, dt, tail, h0t, *consts)


def _split_heads(x):
    lane = lax.broadcasted_iota(jnp.int32, (x.shape[0], LANES), 1)
    low = lane < HEAD_DIM
    lo, hi = [], []
    for t in range(KV_DIM // LANES):
        tile = x[:, t * LANES:(t + 1) * LANES]
        swapped = pltpu.roll(tile, HEAD_DIM, axis=1)
        lo += [jnp.where(low, tile, 0.0), jnp.where(low, swapped, 0.0)]
        hi += [jnp.where(low, 0.0, swapped), jnp.where(low, 0.0, tile)]
    return jnp.concatenate(lo, axis=1).astype(BF16), jnp.concatenate(hi, axis=1).astype(BF16)


def _attn_kernel(*refs, L, TB, SB, from_cache):
    for s in range(SB):
        _attn_stream(*[r if k == 0 or 6 <= k <= 9 else r.at[s] for k, r in enumerate(refs)],
                     L=L, TB=TB, from_cache=from_cache)


def _attn_stream(sink_ref, q_ref, k_ref, v_ref, kp_ref, vp_ref, mk_ref, mv_ref, bias_ref, g_ref,
                 o_ref, k_lo, k_hi, v_lo, v_hi, ybuf, s_buf, e_buf, sink_buf, *, L, TB, from_cache):
    b = pl.program_id(1)
    bufs = (k_lo, k_hi, v_lo, v_hi)
    if from_cache:
        window = _split_heads(kp_ref[...]) + _split_heads(vp_ref[...])
        for buf, val in zip(bufs, window):
            buf[0:WINDOW, :] = val
    else:
        @pl.when(b == 0)
        def _():
            for buf in bufs:
                buf[0:WINDOW, :] = jnp.zeros((WINDOW, buf.shape[1]), BF16)

        @pl.when(b > 0)
        def _():
            for buf in bufs:
                buf[0:WINDOW, :] = buf[TB:TB + WINDOW, :]

    for buf, val in zip(bufs, _split_heads(k_ref[...]) + _split_heads(v_ref[...])):
        buf[WINDOW:WINDOW + TB, :] = val
    n_pad = KEY_SPAN - WINDOW - L - N_META
    pad_rows = jnp.zeros((n_pad, 2 * KV_DIM), BF16)
    tails = [jnp.concatenate([t, pad_rows], axis=0)
             for t in _split_heads(mk_ref[...]) + _split_heads(mv_ref[...])]
    lane = lax.broadcasted_iota(jnp.int32, (2 * L, 2 * KEY_SPAN), 1) % KEY_SPAN
    lane_h = lax.broadcasted_iota(jnp.int32, (L, LANES), 1)
    lane_v = lax.broadcasted_iota(jnp.int32, (KEY_SPAN, LANES), 1)
    ones_lo = jnp.where(lane_v < HEAD_DIM, 1.0, 0.0).astype(BF16)
    ones_hi = jnp.where(lane_v < HEAD_DIM, 0.0, 1.0).astype(BF16)
    n_chunks = TB // L
    pairs_per_group = ATT_REP // 2

    def chunk_keys(c):
        r0 = c * L
        return [jnp.concatenate([buf[r0:r0 + WINDOW, :], buf[r0 + WINDOW:r0 + WINDOW + L, :], tail], axis=0)
                for buf, tail in zip(bufs, tails)]

    for c in range(n_chunks):
        r0 = c * L
        keys = chunk_keys(c)
        for g in range(ATT_KV_HEADS):
            gl = slice(g * LANES, (g + 1) * LANES)
            q4 = jnp.concatenate([q_ref[r0:r0 + L, p * LANES:(p + 1) * LANES]
                                  for p in range(pairs_per_group * g, pairs_per_group * (g + 1))], axis=0)
            s = _dot_nt(q4, jnp.concatenate([keys[0][:, gl], keys[1][:, gl]], axis=0)) + bias_ref[g]
            if not from_cache and r0 < WINDOW:
                n_invalid = WINDOW - (b * TB + r0)
                s = jnp.where(lane < n_invalid, NEG_INF, s)
            s_buf[c * ATT_KV_HEADS + g] = s

    for c in range(n_chunks):
        for pair in range(ATT_HEADS // 2):
            slot = c * ATT_KV_HEADS + pair // pairs_per_group
            rows = slice((pair % pairs_per_group) * L, (pair % pairs_per_group + 1) * L)
            sink_terms = []
            for half in range(2):
                cols = slice(half * KEY_SPAN, (half + 1) * KEY_SPAN)
                s = s_buf[slot, rows, cols]
                sink = sink_ref[2 * pair + half] * LOG2E
                m = jnp.maximum(jnp.max(s, axis=-1, keepdims=True), sink)
                e_buf[slot, rows, cols] = jnp.exp2(s - m).astype(BF16)
                sink_terms.append(jnp.exp2(sink - m))
            sink_buf[c * (ATT_HEADS // 2) + pair] = jnp.where(lane_h < HEAD_DIM, sink_terms[0], sink_terms[1])

    for c in range(n_chunks):
        r0 = c * L
        keys = chunk_keys(c)
        for g in range(ATT_KV_HEADS):
            gl = slice(g * LANES, (g + 1) * LANES)
            vcat = jnp.concatenate([jnp.concatenate([keys[2][:, gl], ones_lo], axis=1),
                                    jnp.concatenate([keys[3][:, gl], ones_hi], axis=1)], axis=0)
            ov = _dot(e_buf[c * ATT_KV_HEADS + g], vcat)
            for j in range(pairs_per_group):
                pair = pairs_per_group * g + j
                den = ov[j * L:(j + 1) * L, LANES:2 * LANES] + sink_buf[c * (ATT_HEADS // 2) + pair]
                ybuf[r0:r0 + L, pair * LANES:(pair + 1) * LANES] = ov[j * L:(j + 1) * L, 0:LANES] * (1.0 / den)
    o_ref[...] = _rms(ybuf[...], g_ref[...])


def _attention(q, k, v, kprev, vprev, mk, mv, w, L, TB, from_cache, SB=1):
    S, Ls, _ = q.shape
    assert Ls % TB == 0 and TB % L == 0 and (from_cache or TB >= WINDOW) and S % SB == 0
    nb = Ls // TB
    assert SB == 1 or (from_cache and nb == 1)
    seq = lambda n: pl.BlockSpec((SB, TB, n), lambda s, b: (s, b, 0))
    if not from_cache:
        kprev = vprev = jnp.zeros((1, WINDOW, KV_DIM), F32)
    prev = pl.BlockSpec((SB, WINDOW, KV_DIM),
                        (lambda s, b: (s, 0, 0)) if from_cache else (lambda s, b: (0, 0, 0)))
    slopes = 2.0 ** (-8.0 * np.arange(1, ATT_HEADS + 1, dtype=np.float64) / ATT_HEADS)
    dist = np.abs(WINDOW + np.arange(L)[:, None] - np.arange(WINDOW + L)[None, :])
    bias = np.full((ATT_HEADS, L, KEY_SPAN), NEG_INF, np.float32)
    bias[:, :, :WINDOW + L] = -slopes[:, None, None] * dist[None] * LOG2E
    bias[:, :, WINDOW + L:WINDOW + L + N_META] = 0.0
    bias = bias.reshape(ATT_KV_HEADS, ATT_REP // 2, 2, L, KEY_SPAN).transpose(0, 1, 3, 2, 4)
    bias = jnp.asarray(bias.reshape(ATT_KV_HEADS, ATT_REP // 2 * L, 2 * KEY_SPAN))
    n_chunks = TB // L
    in_specs = [pl.BlockSpec(memory_space=pltpu.SMEM),
                seq(ATT_DIM), seq(KV_DIM), seq(KV_DIM), prev, prev,
                _const_spec(mk.shape), _const_spec(mv.shape), _const_spec(bias.shape),
                _const_spec(w["g_att_out"].shape)]
    return pl.pallas_call(
        functools.partial(_attn_kernel, L=L, TB=TB, SB=SB, from_cache=from_cache),
        grid=(S // SB, nb),
        in_specs=in_specs,
        out_specs=seq(ATT_DIM),
        out_shape=jax.ShapeDtypeStruct((S, Ls, ATT_DIM), F32),
        scratch_shapes=[pltpu.VMEM((SB, WINDOW + TB, 2 * KV_DIM), BF16) for _ in range(4)]
                       + [pltpu.VMEM((SB, TB, ATT_DIM), F32),
                          pltpu.VMEM((SB, n_chunks * ATT_KV_HEADS, 2 * L, 2 * KEY_SPAN), F32),
                          pltpu.VMEM((SB, n_chunks * ATT_KV_HEADS, 2 * L, 2 * KEY_SPAN), BF16),
                          pltpu.VMEM((SB, n_chunks * ATT_HEADS // 2, L, LANES), F32)],
        compiler_params=pltpu.CompilerParams(
            dimension_semantics=("arbitrary", "arbitrary"),
            vmem_limit_bytes=VMEM_LIMIT_WIDE if nb > 1 else VMEM_LIMIT),
        name="attention",
    )(w["attn_sinks"], q, k, v, kprev, vprev, mk, mv, bias, w["g_att_out"])


TOK_TILE = 512
SEG_ALIGN = 2 * SUBLANES
SORT_ROWS = 2 * TOK_TILE + N_EXPERTS * SEG_ALIGN
XS_COLS = D_MODEL + LANES
META_P1, META_P2, META_G1, META_G2, META_E1 = (N_EXPERTS + j for j in range(5))
PAY_G1, PAY_G2, PAY_E1 = 0, 3, 6


def _merge_kernel(h_ref, ys_ref, ya_ref, wo_ref, gffn_ref, wr_ref, br_ref, tri_ref, upper_ref,
                  h1_ref, xn_ref, meta_ref, cnt_ref, logit_buf):
    i = pl.program_id(0)

    @pl.when(i == 0)
    def _():
        logit_buf[...] = jnp.zeros(logit_buf.shape, F32)

    ycat = jnp.concatenate([ys_ref[...].astype(BF16), ya_ref[...].astype(BF16)], axis=1)
    h1 = h_ref[...] + _dot(ycat, wo_ref[...])
    h1_ref[...] = h1
    xn = _rms(h1, gffn_ref[...]).astype(BF16)
    xn_ref[...] = xn
    all_logits = logit_buf[(i + 1) % 2]
    logit_buf[i % 2] = _dot(xn, wr_ref[...]) + br_ref[...]
    tri = tri_ref[...]
    upper = upper_ref[...]
    lane = lax.broadcasted_iota(jnp.int32, (TOK_TILE, LANES), 1)
    big = jnp.int32(LANES)
    gmask = (lane >= N_EXPERTS) & (lane < N_EXPERTS + N_EXPERT_GROUPS)
    for s in range(all_logits.shape[0] // TOK_TILE):
        rows = slice(s * TOK_TILE, (s + 1) * TOK_TILE)
        logits = all_logits[rows]

        def top1(mask):
            mval = jnp.max(jnp.where(mask, logits, NEG_INF), axis=-1, keepdims=True)
            idx = jnp.min(jnp.where(mask & (logits == mval), lane, big), axis=-1, keepdims=True)
            return mval, idx

        gmax, gidx = top1(gmask)
        gate_g = 1.0 / jnp.sum(jnp.where(gmask, jnp.exp(logits - gmax), 0.0), axis=-1, keepdims=True)
        grp = gidx - N_EXPERTS
        emask = (lane // EXPERTS_PER_GROUP) == grp
        v1, i1 = top1(emask)
        v2, i2 = top1(emask & (lane != i1))
        e2 = jnp.exp(v2 - v1)
        g1 = gate_g / (1.0 + e2)
        g2 = gate_g * e2 / (1.0 + e2)
        oh1 = jnp.where(lane == i1, 1.0, 0.0)
        oh2 = jnp.where(lane == i2, 1.0, 0.0)
        oh = oh1 + oh2
        earlier = _dot(tri, oh.astype(BF16))
        cnt = jnp.sum(oh, axis=0, keepdims=True)
        units = jnp.floor((cnt + (SEG_ALIGN - 1)) * (1.0 / SEG_ALIGN))
        units = jnp.broadcast_to(units, (2 * SUBLANES, LANES)).astype(BF16)
        slot = _dot(units, upper)[0:1, :] * SEG_ALIGN + earlier
        p1 = jnp.sum(oh1 * slot, axis=-1, keepdims=True)
        p2 = jnp.sum(oh2 * slot, axis=-1, keepdims=True)
        meta = jnp.where(lane == META_P1, p1, 0.0)
        meta = jnp.where(lane == META_P2, p2, meta)
        meta = jnp.where(lane == META_G1, g1, meta)
        meta = jnp.where(lane == META_G2, g2, meta)
        meta = jnp.where(lane == META_E1, i1.astype(F32), meta)
        meta_ref[rows, :] = meta
        cnt_ref[s] = jnp.broadcast_to(cnt, (SUBLANES, LANES))


def _merge(h, y_ssd, y_att, w, tm):
    t = h.shape[0]
    assert t % tm == 0 and tm % TOK_TILE == 0
    sub = tm // TOK_TILE
    last = t // tm - 1
    row = lambda n: pl.BlockSpec((tm, n), lambda i: (jnp.minimum(i, last), 0))
    routed = lambda i: jnp.maximum(i - 1, 0)
    tri = jnp.asarray(np.tril(np.ones((TOK_TILE, TOK_TILE), np.float32), -1), BF16)
    upper = jnp.asarray(np.triu(np.ones((LANES, LANES), np.float32), 1), BF16)
    consts = [w["w_out"], w["g_ffn"], w["w_route"], w["b_route"], tri, upper]
    return pl.pallas_call(
        _merge_kernel,
        grid=(t // tm + 1,),
        in_specs=[row(D_MODEL), row(SSD_DIM), row(ATT_DIM)] + [_const_spec(c.shape) for c in consts],
        out_specs=[row(D_MODEL), row(D_MODEL),
                   pl.BlockSpec((tm, LANES), lambda i: (routed(i), 0)),
                   pl.BlockSpec((sub, SUBLANES, LANES), lambda i: (routed(i), 0, 0))],
        scratch_shapes=[pltpu.VMEM((2, tm, LANES), F32)],
        out_shape=[jax.ShapeDtypeStruct((t, D_MODEL), F32),
                   jax.ShapeDtypeStruct((t, D_MODEL), BF16),
                   jax.ShapeDtypeStruct((t, LANES), F32),
                   jax.ShapeDtypeStruct((t // TOK_TILE, SUBLANES, LANES), F32)],
        compiler_params=pltpu.CompilerParams(
            dimension_semantics=("arbitrary",),
            vmem_limit_bytes=VMEM_LIMIT_WIDE if tm > 512 else VMEM_LIMIT),
        name="merge_route",
    )(h, y_ssd, y_att, *consts)


def _meta_col(meta, j):
    lane = lax.broadcasted_iota(jnp.int32, meta.shape, 1)
    return jnp.sum(jnp.where(lane == j, meta, 0.0), axis=-1, keepdims=True)


def _pair_selector(meta):
    rows = lax.broadcasted_iota(jnp.int32, (meta.shape[0], SORT_ROWS), 1)
    p1 = _meta_col(meta, META_P1).astype(jnp.int32)
    p2 = _meta_col(meta, META_P2).astype(jnp.int32)
    return jnp.where(rows == p1, 1.0, jnp.where(rows == p2, 1.0, 0.0)).astype(BF16)


def _segment_dmas(off_ref, cnt_ref, row_ref, b, tile_buf, hbm, sem, to_hbm, wait):
    for e in range(N_EXPERTS):
        k = b * N_EXPERTS + e
        n = pl.multiple_of(cnt_ref[k], SEG_ALIGN)
        v = tile_buf.at[pl.ds(pl.multiple_of(off_ref[k], SEG_ALIGN), n)]
        h = hbm.at[pl.ds(pl.multiple_of(row_ref[k], SEG_ALIGN), n)]
        cp = pltpu.make_async_copy(v, h, sem) if to_hbm else pltpu.make_async_copy(h, v, sem)
        if wait:
            cp.wait()
        else:
            cp.start()


def _token_payload(meta):
    cols = {PAY_E1: _meta_col(meta, META_E1)}
    for base, src in ((PAY_G1, META_G1), (PAY_G2, META_G2)):
        for k, part in enumerate(_split3(_meta_col(meta, src))):
            cols[base + k] = part.astype(F32)
    lane = lax.broadcasted_iota(jnp.int32, (meta.shape[0], LANES), 1)
    out = jnp.zeros((meta.shape[0], LANES), F32)
    for k, c in cols.items():
        out = jnp.where(lane == k, c, out)
    return out.astype(BF16)


def _dispatch_kernel(off_ref, cnt_ref, row_ref, tail_row_ref, tail_cnt_ref, xn_ref, meta_ref, *rest,
                     tile0, first):
    xs_ref, sort_buf, zero_buf, sem, tail_sem = rest if first else rest[1:]
    b = pl.program_id(0)
    nb = pl.num_programs(0)
    slot = b % 2
    seg = functools.partial(_segment_dmas, off_ref, cnt_ref, row_ref, hbm=xs_ref, to_hbm=True)

    def tails(wait):
        def body(e, carry):
            n = pl.multiple_of(tail_cnt_ref[e], SEG_ALIGN)

            @pl.when(n > 0)
            def _():
                cp = pltpu.make_async_copy(
                    zero_buf.at[pl.ds(0, n)],
                    xs_ref.at[pl.ds(pl.multiple_of(tail_row_ref[e], SEG_ALIGN), n)], tail_sem.at[0])
                if wait:
                    cp.wait()
                else:
                    cp.start()
            return carry

        lax.fori_loop(0, N_EXPERTS, body, 0)

    @pl.when(b >= 2)
    def _():
        seg(tile0 + b - 2, sort_buf.at[slot], sem=sem.at[slot], wait=True)

    if first:
        @pl.when(b == 0)
        def _():
            zero_buf[...] = jnp.zeros(zero_buf.shape, BF16)
            tails(wait=False)

    meta = meta_ref[...]
    payload = jnp.concatenate([xn_ref[...], _token_payload(meta)], axis=1)
    sort_buf[slot] = _dot_tn(_pair_selector(meta), payload).astype(BF16)
    seg(tile0 + b, sort_buf.at[slot], sem=sem.at[slot], wait=False)

    @pl.when(b == nb - 1)
    def _():
        @pl.when(b >= 1)
        def _():
            seg(tile0 + b - 1, sort_buf.at[1 - slot], sem=sem.at[1 - slot], wait=True)

        seg(tile0 + b, sort_buf.at[slot], sem=sem.at[slot], wait=True)
        if first:
            tails(wait=True)


def _expert_kernel(tile_expert_ref, n_active_ref, xs_ref, wg_ref, wu_ref, wd_ref, y_ref):
    @pl.when(pl.program_id(0) < n_active_ref[0])
    def _():
        x = xs_ref[:, 0:D_MODEL]
        pay = xs_ref[:, D_MODEL:XS_COLS].astype(F32)
        lane = lax.broadcasted_iota(jnp.int32, pay.shape, 1)
        pick = lambda lo, n: jnp.sum(jnp.where((lane >= lo) & (lane < lo + n), pay, 0.0), axis=-1, keepdims=True)
        expert = tile_expert_ref[pl.program_id(0)].astype(F32)
        gate = jnp.where(pick(PAY_E1, 1) == expert, pick(PAY_G1, 3), pick(PAY_G2, 3))
        gu = _dot(x, jnp.concatenate([wg_ref[...].astype(BF16), wu_ref[...].astype(BF16)], axis=1))
        hid = _silu(gu[:, 0:EXPERT_FF]) * gu[:, EXPERT_FF:2 * EXPERT_FF]
        y_ref[...] = (gate * _dot(hid.astype(BF16), wd_ref[...].astype(BF16))).astype(BF16)


def _combine_kernel(off_ref, cnt_ref, row_ref, h1_ref, meta_ref, y_ref, o_ref, y_buf, sem, *, tile0):
    b = pl.program_id(0)
    nb = pl.num_programs(0)
    slot = b % 2
    seg = functools.partial(_segment_dmas, off_ref, cnt_ref, row_ref, hbm=y_ref, to_hbm=False)

    @pl.when(b == 0)
    def _():
        y_buf[...] = jnp.zeros(y_buf.shape, BF16)
        seg(tile0 + b, y_buf.at[slot], sem=sem.at[slot], wait=False)

    @pl.when(b + 1 < nb)
    def _():
        seg(tile0 + b + 1, y_buf.at[1 - slot], sem=sem.at[1 - slot], wait=False)

    seg(tile0 + b, y_buf.at[slot], sem=sem.at[slot], wait=True)
    o_ref[...] = h1_ref[...] + _dot(_pair_selector(meta_ref[...]), y_buf[slot, 0:SORT_ROWS, :])


def _moe(parts, w, tm):
    i32 = jnp.int32
    n_tiles = [p[0].shape[0] // TOK_TILE for p in parts]
    nb = sum(n_tiles)
    t = nb * TOK_TILE
    cnt = jnp.concatenate([p[3][:, 0, :N_EXPERTS] for p in parts], axis=0).astype(i32)
    cnt_al = (cnt + (SEG_ALIGN - 1)) // SEG_ALIGN * SEG_ALIGN
    off = jnp.cumsum(cnt_al, axis=1) - cnt_al
    tot = jnp.sum(cnt_al, axis=0)
    tot_tm = (tot + (tm - 1)) // tm * tm
    start = jnp.cumsum(tot_tm) - tot_tm
    row = start[None, :] + jnp.cumsum(cnt_al, axis=0) - cnt_al
    tile_ends = jnp.cumsum(tot_tm // tm)
    max_rows = 2 * t + (SEG_ALIGN - 1) * min(N_EXPERTS * nb, 2 * t) + N_EXPERTS * (tm - SEG_ALIGN)
    nt = -(-max_rows // tm)
    tile_expert = jnp.sum((jnp.arange(nt, dtype=i32)[:, None] >= tile_ends[None, :]).astype(i32), axis=1)
    tile_expert = jnp.minimum(tile_expert, N_EXPERTS - 1)
    n_active = tile_ends[-1:].astype(i32)
    empty = cnt_al == 0
    flat = lambda a: a.reshape(-1).astype(i32)
    n_rows = flat(jnp.maximum(cnt_al, SEG_ALIGN))
    e_idx = jnp.arange(N_EXPERTS, dtype=i32)[None, :]
    spare = nt * tm + ((jnp.arange(nb, dtype=i32)[:, None] % 2) * N_EXPERTS + e_idx) * SEG_ALIGN
    dispatch_tables = [flat(off), n_rows, flat(jnp.where(empty, spare, row))]
    combine_tables = [flat(jnp.where(empty, SORT_ROWS + e_idx * SEG_ALIGN, off)), n_rows,
                      flat(jnp.where(empty, 0, row))]
    n_spare = 2 * N_EXPERTS * SEG_ALIGN
    tail_tables = [(start + tot).astype(i32), (tot_tm - tot).astype(i32)]

    tok = lambda n: pl.BlockSpec((TOK_TILE, n), lambda i, *_: (i, 0))
    hbm = pl.BlockSpec(memory_space=pl.ANY)
    n_prefetch = len(dispatch_tables) + len(tail_tables)
    xs = None
    tile0 = 0
    for (_, xn, meta, _), n in zip(parts, n_tiles):
        first = xs is None
        xs = pl.pallas_call(
            functools.partial(_dispatch_kernel, tile0=tile0, first=first),
            grid_spec=pltpu.PrefetchScalarGridSpec(
                num_scalar_prefetch=n_prefetch, grid=(n,),
                in_specs=[tok(D_MODEL), tok(LANES)] + ([] if first else [hbm]),
                out_specs=hbm,
                scratch_shapes=[pltpu.VMEM((2, SORT_ROWS, XS_COLS), BF16),
                                pltpu.VMEM((tm, XS_COLS), BF16),
                                pltpu.SemaphoreType.DMA((2,)),
                                pltpu.SemaphoreType.DMA((1,))]),
            out_shape=jax.ShapeDtypeStruct((nt * tm + n_spare, XS_COLS), BF16),
            input_output_aliases={} if first else {n_prefetch + 2: 0},
            compiler_params=pltpu.CompilerParams(
                dimension_semantics=("arbitrary",), vmem_limit_bytes=VMEM_LIMIT),
            name="dispatch",
        )(*dispatch_tables, *tail_tables, xn, meta, *([] if first else [xs]))
        tile0 += n

    act = lambda i, te, na: jnp.minimum(i, na[0] - 1)
    y = pl.pallas_call(
        _expert_kernel,
        grid_spec=pltpu.PrefetchScalarGridSpec(
            num_scalar_prefetch=2, grid=(nt,),
            in_specs=[pl.BlockSpec((tm, XS_COLS), lambda i, te, na: (act(i, te, na), 0)),
                      pl.BlockSpec((None, D_MODEL, EXPERT_FF), lambda i, te, na: (te[act(i, te, na)], 0, 0)),
                      pl.BlockSpec((None, D_MODEL, EXPERT_FF), lambda i, te, na: (te[act(i, te, na)], 0, 0)),
                      pl.BlockSpec((None, EXPERT_FF, D_MODEL), lambda i, te, na: (te[act(i, te, na)], 0, 0))],
            out_specs=pl.BlockSpec((tm, D_MODEL), lambda i, te, na: (act(i, te, na), 0))),
        out_shape=jax.ShapeDtypeStruct((nt * tm, D_MODEL), BF16),
        compiler_params=pltpu.CompilerParams(
            dimension_semantics=("arbitrary",), vmem_limit_bytes=VMEM_LIMIT),
        name="experts",
    )(tile_expert, n_active, xs, w["w_gate"], w["w_up"], w["w_down"])

    outs = []
    tile0 = 0
    for (h1, _, meta, _), n in zip(parts, n_tiles):
        outs.append(pl.pallas_call(
            functools.partial(_combine_kernel, tile0=tile0),
            grid_spec=pltpu.PrefetchScalarGridSpec(
                num_scalar_prefetch=len(combine_tables), grid=(n,),
                in_specs=[tok(D_MODEL), tok(LANES), hbm],
                out_specs=tok(D_MODEL),
                scratch_shapes=[pltpu.VMEM((2, SORT_ROWS + N_EXPERTS * SEG_ALIGN, D_MODEL), BF16),
                                pltpu.SemaphoreType.DMA((2,))]),
            out_shape=jax.ShapeDtypeStruct((n * TOK_TILE, D_MODEL), F32),
            compiler_params=pltpu.CompilerParams(
                dimension_semantics=("arbitrary",), vmem_limit_bytes=VMEM_LIMIT),
            name="combine",
        )(*combine_tables, h1, meta, y))
        tile0 += n
    return outs


def _prepare_weights(g_mix, w_in, conv_w, conv_b, dt_bias, a_log, d_skip, g_ssd_out, g_q, g_k,
                     attn_sinks, g_att_out, w_out, g_ffn, w_route_group, b_route_group,
                     w_route_expert, b_route_expert, w_gate, w_up, w_down):
    cuts = np.cumsum([0, SSD_DIM, CONV_DIM, SSD_HEADS, ATT_DIM, KV_DIM, KV_DIM])
    seg = lambda i: w_in[:, cuts[i]:cuts[i + 1]]
    pad_lanes = lambda a: jnp.pad(a, ((0, 0), (0, LANES - a.shape[1])))
    bd = np.kron(np.eye(KV_DIM // HEAD_DIM, dtype=np.float32), np.ones((HEAD_DIM, HEAD_DIM), np.float32))
    n_route = N_EXPERTS + N_EXPERT_GROUPS
    return {
        "g_mix": g_mix.reshape(1, D_MODEL),
        "w_qk": jnp.concatenate([seg(3), seg(4)], axis=1).astype(BF16),
        "w_rest": jnp.concatenate([seg(0), seg(1), seg(5), pad_lanes(jnp.tile(seg(2), (1, DT_COPIES)))],
                                  axis=1).astype(BF16),
        "g_qk": jnp.concatenate([jnp.tile(g_q, ATT_HEADS), jnp.tile(g_k, ATT_KV_HEADS)]).reshape(1, ATT_DIM + KV_DIM),
        "bd": jnp.asarray(bd, BF16),
        "conv_w": conv_w, "conv_b": conv_b.reshape(1, CONV_DIM),
        "dt_bias": pad_lanes(jnp.tile(dt_bias, DT_COPIES).reshape(1, DT_COPIES * SSD_HEADS)),
        "a_log": pad_lanes(jnp.tile(a_log, DT_COPIES).reshape(1, DT_COPIES * SSD_HEADS)),
        "d_skip": jnp.repeat(d_skip, HEAD_DIM).reshape(1, SSD_DIM),
        "g_ssd_out": g_ssd_out.reshape(1, SSD_DIM),
        "attn_sinks": attn_sinks,
        "g_att_out": g_att_out.reshape(1, ATT_DIM),
        "w_out": w_out.astype(BF16),
        "g_ffn": g_ffn.reshape(1, D_MODEL),
        "w_route": pad_lanes(jnp.concatenate([w_route_expert, w_route_group], axis=1)).astype(BF16),
        "b_route": pad_lanes(jnp.concatenate([b_route_expert, b_route_group]).reshape(1, n_route)),
        "w_gate": w_gate, "w_up": w_up, "w_down": w_down,
    }


def _segment(x3, w, mk, mv, tail, h0t, kprev, vprev, L, tb_ssd, tb_att, tm, from_cache, sb=1):
    S, Ls, _ = x3.shape
    x = x3.reshape(S * Ls, D_MODEL)
    z, xbc, q, k, v, dt = _project(x, w, 1024 if x.shape[0] > 1024 else tm)
    r3 = lambda a: a.reshape(S, Ls, a.shape[-1])
    xbc3, k3, v3 = r3(xbc), r3(k), r3(v)
    y_ssd, h_new = _ssd(xbc3, r3(z), r3(dt), tail, h0t, w, L, tb_ssd, sb)
    if not from_cache:
        kprev, vprev = k3, v3
    y_att = _attention(r3(q), k3, v3, kprev, vprev, mk, mv, w, L, tb_att, from_cache, sb)
    part = _merge(x, y_ssd.reshape(S * Ls, SSD_DIM), y_att.reshape(S * Ls, ATT_DIM), w,
                  1024 if x.shape[0] > 1024 else tm)
    return part, xbc3, h_new, k3, v3


def _state_to_kernel(h):
    return h.reshape(h.shape[0], SSD_DIM, D_STATE)


def _state_from_kernel(hk):
    return hk.reshape(hk.shape[0], SSD_HEADS, HEAD_DIM, D_STATE)


def kernel(x_prompt, x_sample, cache_conv, state_ssd, cache_k, cache_v, meta_tokens, g_mix, w_in, conv_w, conv_b, dt_bias, a_log, d_skip, g_ssd_out, g_q, g_k, attn_sinks, g_att_out, w_out, g_ffn, w_route_group, b_route_group, w_route_expert, b_route_expert, w_gate, w_up, w_down):
    w = _prepare_weights(g_mix[0], w_in[0], conv_w[0], conv_b[0], dt_bias[0], a_log[0], d_skip[0],
                         g_ssd_out[0], g_q[0], g_k[0], attn_sinks[0], g_att_out[0], w_out[0], g_ffn[0],
                         w_route_group[0], b_route_group[0], w_route_expert[0], b_route_expert[0],
                         w_gate[0], w_up[0], w_down[0])
    n_b = x_sample.shape[0]
    n_dec = x_sample.shape[1]

    _, m_xbc, _, mk, mv, m_dt = _project(meta_tokens, w, N_META)
    zero_tail = jnp.zeros((1, CONV_WIDTH - 1, CONV_DIM), F32)
    zero_state = jnp.zeros((1, SSD_DIM, D_STATE), F32)
    m_xbc3 = m_xbc.reshape(1, N_META, CONV_DIM)
    _, m_state = _ssd(m_xbc3, jnp.zeros((1, N_META, SSD_DIM), F32), m_dt.reshape(1, N_META, LANES),
                      zero_tail, zero_state, w, N_META, N_META)
    m_tail = m_xbc3[:, N_META - (CONV_WIDTH - 1):]

    part_p, xbc_p, st_p, k_p, v_p = _segment(
        x_prompt, w, mk, mv, m_tail, m_state, None, None,
        L=CHUNK, tb_ssd=512, tb_att=512, tm=512, from_cache=False)
    part_s, xbc_s, st_s, k_s, v_s = _segment(
        x_sample, w, mk, mv, cache_conv[0], _state_to_kernel(state_ssd[0]),
        cache_k[0].reshape(n_b, WINDOW, KV_DIM), cache_v[0].reshape(n_b, WINDOW, KV_DIM),
        L=n_dec, tb_ssd=n_dec, tb_att=n_dec, tm=512, from_cache=True, sb=8)
    yp, ys = _moe([part_p, part_s], w, tm=512)
    yp = yp.reshape(x_prompt.shape)
    ys = ys.reshape(x_sample.shape)

    heads = lambda a, rows: a.reshape(a.shape[0], rows, ATT_KV_HEADS, HEAD_DIM)[None]
    return (yp, ys,
            xbc_p[:, -(CONV_WIDTH - 1):][None],
            _state_from_kernel(st_p)[None],
            heads(k_p[:, -WINDOW:], WINDOW), heads(v_p[:, -WINDOW:], WINDOW),
            xbc_s[:, -(CONV_WIDTH - 1):][None],
            _state_from_kernel(st_s)[None],
            heads(k_s, n_dec), heads(v_s, n_dec))
```
